```python
import math
import jax
import jax.numpy as jnp
from jax import lax
import numpy as np

D_MODEL = 2048
BATCH = 2
SEQ = 4096
DEPTH = 4
DEC_BATCH = 32
DEC_SEQ = 1
PAST_LEN = 16384
PAGE_SIZE = 128

SWA_HEADS = 16
SWA_KV_HEADS = 4
SWA_GROUP = SWA_HEADS // SWA_KV_HEADS
SWA_HEAD_DIM = 64
WINDOW = 128
GDN_HEADS = 8
GDN_DK = 128
GDN_DV = 128
GDN_CHUNK = 64
CONV_W = 4
SSM_D_INNER = 2 * D_MODEL
SSM_HEAD_DIM = 64
SSM_HEADS = SSM_D_INNER // SSM_HEAD_DIM
SSM_GROUPS = 8
SSM_REP = SSM_HEADS // SSM_GROUPS
SSM_STATE = 128
SSM_CHUNK = 128
D_FF = 5632
EPS = 1e-6
NEG_INF = -1e30

N_EVEN = (DEPTH + 1) // 2
N_ODD = DEPTH // 2

SWA_Q_W = SWA_HEADS * SWA_HEAD_DIM
SWA_KV_W = SWA_KV_HEADS * SWA_HEAD_DIM
GDN_QK_W = GDN_HEADS * GDN_DK
GDN_V_W = GDN_HEADS * GDN_DV
GDN_CONV_CH = 2 * GDN_QK_W + GDN_V_W
EVEN_SPLITS = (SWA_Q_W, SWA_KV_W, SWA_KV_W, GDN_CONV_CH, GDN_V_W, GDN_HEADS, GDN_HEADS)
EVEN_IN = SWA_Q_W + 2 * SWA_KV_W + GDN_CONV_CH + GDN_V_W + 2 * GDN_HEADS
EVEN_OUT = SWA_Q_W + GDN_V_W
SSM_BC_W = SSM_GROUPS * SSM_STATE
SSM_CONV_CH = SSM_D_INNER + 2 * SSM_BC_W
ODD_SPLITS = (SSM_D_INNER, SSM_CONV_CH, SSM_HEADS)
ODD_IN = SSM_D_INNER + SSM_CONV_CH + SSM_HEADS

kernel_name = 'hybrid_swa_gdn_ssd_macaron_step'


def rmsnorm(x, w):
    xf = x.astype(jnp.float32)
    y = xf * lax.rsqrt(jnp.mean(xf * xf, axis=-1, keepdims=True) + EPS)
    return (y * w.astype(jnp.float32)).astype(x.dtype)


def l2norm(x):
    return x * lax.rsqrt(jnp.sum(x * x, axis=-1, keepdims=True) + EPS)


def split_cols(h, sizes):
    parts, start = [], 0
    for s in sizes:
        parts.append(h[..., start:start + s])
        start += s
    return parts


def ffn_half(x, g, w_gate, w_up, w_down):
    h = rmsnorm(x, g)
    return x + 0.5 * ((jax.nn.silu(h @ w_gate) * (h @ w_up)) @ w_down)


def causal_conv(u, prev, w, b):
    T = u.shape[1]
    full = jnp.concatenate([prev.astype(u.dtype), u], axis=1)
    out = full[:, CONV_W - 1:CONV_W - 1 + T] * w[CONV_W - 1]
    for j in range(CONV_W - 1):
        out = out + full[:, j:j + T] * w[j]
    if b is not None:
        out = out + b
    return jax.nn.silu(out), full[:, full.shape[1] - (CONV_W - 1):]


def sink_softmax(s, sink):
    m = jnp.maximum(jnp.max(s, axis=-1, keepdims=True), sink)
    p = jnp.exp(s - m)
    return p / (jnp.sum(p, axis=-1, keepdims=True) + jnp.exp(sink - m))


def swa_prompt(q, k, v, sinks):
    Bn, T = q.shape[:2]
    nb = T // WINDOW
    f32 = jnp.float32
    qb = q.astype(f32).reshape(Bn, nb, WINDOW, SWA_KV_HEADS, SWA_GROUP, SWA_HEAD_DIM)
    kb = k.astype(f32).reshape(Bn, nb, WINDOW, SWA_KV_HEADS, SWA_HEAD_DIM)
    vb = v.astype(f32).reshape(Bn, nb, WINDOW, SWA_KV_HEADS, SWA_HEAD_DIM)

    def with_prev(a):
        prev = jnp.concatenate([jnp.zeros_like(a[:, :1]), a[:, :-1]], axis=1)
        return jnp.concatenate([prev, a], axis=2)

    kk, vv = with_prev(kb), with_prev(vb)
    s = jnp.einsum('bnqkgd,bnskd->bnkgqs', qb, kk) * (SWA_HEAD_DIM ** -0.5)
    blk = jnp.arange(nb)[:, None, None] * WINDOW
    qpos = blk + jnp.arange(WINDOW)[None, :, None]
    kpos = blk - WINDOW + jnp.arange(2 * WINDOW)[None, None, :]
    diff = qpos - kpos
    valid = (diff >= 0) & (diff < WINDOW) & (kpos >= 0)
    s = jnp.where(valid[None, :, None, None], s, NEG_INF)
    p = sink_softmax(s, sinks.astype(f32).reshape(SWA_KV_HEADS, SWA_GROUP)[:, :, None, None])
    o = jnp.einsum('bnkgqs,bnskd->bnqkgd', p, vv)
    return o.reshape(Bn, T, SWA_Q_W).astype(q.dtype)


def swa_sample(q, k, v, ck, cv, sinks):
    Bn, T = q.shape[:2]
    f32 = jnp.float32
    qf = q.astype(f32).reshape(Bn, T, SWA_KV_HEADS, SWA_GROUP, SWA_HEAD_DIM)
    kk = jnp.concatenate([ck.astype(f32), k.astype(f32)], axis=1)
    vv = jnp.concatenate([cv.astype(f32), v.astype(f32)], axis=1)
    s = jnp.einsum('bqkgd,bskd->bkgqs', qf, kk) * (SWA_HEAD_DIM ** -0.5)
    qpos = PAST_LEN + jnp.arange(T)
    kpos = PAST_LEN - WINDOW + jnp.arange(WINDOW + T)
    diff = qpos[:, None] - kpos[None, :]
    valid = (diff >= 0) & (diff < WINDOW)
    s = jnp.where(valid, s, NEG_INF)
    p = sink_softmax(s, sinks.astype(f32).reshape(SWA_KV_HEADS, SWA_GROUP)[:, :, None, None])
    o = jnp.einsum('bkgqs,bskd->bqkgd', p, vv)
    return o.reshape(Bn, T, SWA_Q_W).astype(q.dtype)


def gdn_chunked(q, k, v, g, beta, S0):
    Bn, T = q.shape[:2]
    nc = T // GDN_CHUNK
    incl = jnp.tril(jnp.ones((GDN_CHUNK, GDN_CHUNK), bool))
    strict = jnp.tril(jnp.ones((GDN_CHUNK, GDN_CHUNK), bool), -1)
    eye = jnp.eye(GDN_CHUNK, dtype=jnp.float32)

    def chunks(t):
        return jnp.moveaxis(t.reshape(Bn, nc, GDN_CHUNK, *t.shape[2:]), 1, 0)

    def step(S, inp):
        qc, kc, vc, gc, bc = [jnp.swapaxes(t, 1, 2) for t in inp]
        gcum = jnp.cumsum(gc, axis=-1)
        decay = jnp.exp(jnp.where(incl, gcum[..., :, None] - gcum[..., None, :], NEG_INF))
        kb = kc * bc[..., None]
        A = jnp.where(strict, jnp.einsum('bhid,bhjd->bhij', kb, kc) * decay, 0.0)
        rhs = jnp.concatenate([vc * bc[..., None], kb * jnp.exp(gcum)[..., None]], axis=-1)
        sol = lax.linalg.triangular_solve(eye + A, rhs, left_side=True, lower=True, unit_diagonal=True)
        u, w = sol[..., :GDN_DV], sol[..., GDN_DV:]
        v_new = u - jnp.einsum('bhik,bhkv->bhiv', w, S)
        attn = jnp.einsum('bhid,bhjd->bhij', qc, kc) * decay
        o = (jnp.einsum('bhik,bhkv->bhiv', qc * jnp.exp(gcum)[..., None], S)
             + jnp.einsum('bhij,bhjv->bhiv', attn, v_new))
        glast = gcum[..., -1]
        S = (S * jnp.exp(glast)[..., None, None]
             + jnp.einsum('bhjk,bhjv->bhkv', kc * jnp.exp(glast[..., None] - gcum)[..., None], v_new))
        return S, jnp.swapaxes(o, 1, 2)

    S, o = lax.scan(step, S0, tuple(chunks(t) for t in (q, k, v, g, beta)))
    return S, jnp.moveaxis(o, 0, 1).reshape(Bn, T, GDN_HEADS, GDN_DV)


def gdn_step(S, inp):
    q, k, v, g, b = inp
    S = S * jnp.exp(g)[..., None, None]
    err = v - jnp.einsum('bhk,bhkv->bhv', k, S)
    S = S + jnp.einsum('bhk,bhv->bhkv', k, b[..., None] * err)
    return S, jnp.einsum('bhk,bhkv->bhv', q, S)


def gdn_mixer(qkv_raw, z, a, b, conv_prev, S0, conv_w, A_log, dt_bias, norm_w, chunked):
    Bn, T = qkv_raw.shape[:2]
    f32 = jnp.float32
    qkv, conv_new = causal_conv(qkv_raw, conv_prev, conv_w, None)
    q, k, v = split_cols(qkv.astype(f32), (GDN_QK_W, GDN_QK_W, GDN_V_W))
    q = l2norm(q.reshape(Bn, T, GDN_HEADS, GDN_DK)) * (GDN_DK ** -0.5)
    k = l2norm(k.reshape(Bn, T, GDN_HEADS, GDN_DK))
    v = v.reshape(Bn, T, GDN_HEADS, GDN_DV)
    g = -jnp.exp(A_log.astype(f32)) * jax.nn.softplus(a.astype(f32) + dt_bias.astype(f32))
    beta = jax.nn.sigmoid(b.astype(f32))
    S0 = S0.astype(f32)
    if chunked:
        S, o = gdn_chunked(q, k, v, g, beta, S0)
    else:
        S, o = lax.scan(gdn_step, S0, tuple(jnp.moveaxis(t, 1, 0) for t in (q, k, v, g, beta)))
        o = jnp.moveaxis(o, 0, 1)
    o = rmsnorm(o, norm_w) * jax.nn.silu(z.astype(f32).reshape(Bn, T, GDN_HEADS, GDN_DV))
    return o.reshape(Bn, T, GDN_V_W).astype(qkv_raw.dtype), S.astype(qkv_raw.dtype), conv_new


def ssd_chunked(x, dt, A, Bm, Cm, h0):
    Bn, T = x.shape[:2]
    nc = T // SSM_CHUNK
    tri = jnp.tril(jnp.ones((SSM_CHUNK, SSM_CHUNK), bool))
    Ag = A.reshape(SSM_GROUPS, SSM_REP)

    def chunks(t):
        return jnp.moveaxis(t.reshape(Bn, nc, SSM_CHUNK, *t.shape[2:]), 1, 0)

    def step(h, inp):
        xc, dtc, Bc, Cc = inp
        cum = jnp.cumsum(dtc * Ag, axis=1)
        cum_t = jnp.moveaxis(cum, 1, -1)
        dt_t = jnp.moveaxis(dtc, 1, -1)
        L = jnp.exp(jnp.where(tri, cum_t[..., :, None] - cum_t[..., None, :], NEG_INF))
        CB = jnp.einsum('bign,bjgn->bgij', Cc, Bc)
        scores = CB[:, :, None] * L * dt_t[..., None, :]
        y = jnp.einsum('bgrij,bjgrp->bigrp', scores, xc)
        y = y + jnp.einsum('bign,bgrpn->bigrp', Cc, h) * jnp.exp(cum)[..., None]
        last = cum_t[..., -1]
        wts = jnp.exp(last[..., None] - cum_t) * dt_t
        h = h * jnp.exp(last)[..., None, None] + jnp.einsum('bgrj,bjgrp,bjgn->bgrpn', wts, xc, Bc)
        return h, y

    xg = x.reshape(Bn, T, SSM_GROUPS, SSM_REP, SSM_HEAD_DIM)
    dtg = dt.reshape(Bn, T, SSM_GROUPS, SSM_REP)
    hg0 = h0.reshape(Bn, SSM_GROUPS, SSM_REP, SSM_HEAD_DIM, SSM_STATE)
    h, y = lax.scan(step, hg0, (chunks(xg), chunks(dtg), chunks(Bm), chunks(Cm)))
    y = jnp.moveaxis(y, 0, 1).reshape(Bn, T, SSM_HEADS, SSM_HEAD_DIM)
    return y, h.reshape(Bn, SSM_HEADS, SSM_HEAD_DIM, SSM_STATE)


def ssd_step(h, inp, A):
    x, dt, Bt, Ct = inp
    Bn = x.shape[0]
    xg = x.reshape(Bn, SSM_GROUPS, SSM_REP, SSM_HEAD_DIM)
    dtg = dt.reshape(Bn, SSM_GROUPS, SSM_REP)
    hg = h.reshape(Bn, SSM_GROUPS, SSM_REP, SSM_HEAD_DIM, SSM_STATE)
    hg = (hg * jnp.exp(dtg * A.reshape(SSM_GROUPS, SSM_REP))[..., None, None]
          + jnp.einsum('bgr,bgrp,bgn->bgrpn', dtg, xg, Bt))
    y = jnp.einsum('bgrpn,bgn->bgrp', hg, Ct)
    return hg.reshape(h.shape), y.reshape(x.shape)


def ssm_mixer(h_in, conv_prev, h0, conv_w, conv_b, A_log, dt_bias, D_skip, norm_w, chunked):
    Bn, T = h_in.shape[:2]
    f32 = jnp.float32
    z, xbc, dt = split_cols(h_in, ODD_SPLITS)
    xbc, conv_new = causal_conv(xbc, conv_prev, conv_w, conv_b)
    xs, Bm, Cm = split_cols(xbc.astype(f32), (SSM_D_INNER, SSM_BC_W, SSM_BC_W))
    xs = xs.reshape(Bn, T, SSM_HEADS, SSM_HEAD_DIM)
    Bm = Bm.reshape(Bn, T, SSM_GROUPS, SSM_STATE)
    Cm = Cm.reshape(Bn, T, SSM_GROUPS, SSM_STATE)
    dt = jax.nn.softplus(dt.astype(f32) + dt_bias.astype(f32))
    A = -jnp.exp(A_log.astype(f32))
    h0 = h0.astype(f32)
    if chunked:
        y, h = ssd_chunked(xs, dt, A, Bm, Cm, h0)
    else:
        h, y = lax.scan(lambda c, i: ssd_step(c, i, A), h0,
                        tuple(jnp.moveaxis(t, 1, 0) for t in (xs, dt, Bm, Cm)))
        y = jnp.moveaxis(y, 0, 1)
    y = y + D_skip.astype(f32)[:, None] * xs
    yz = y.reshape(Bn, T, SSM_D_INNER) * jax.nn.silu(z.astype(f32))
    yg = yz.reshape(Bn, T, SSM_GROUPS, SSM_D_INNER // SSM_GROUPS)
    yg = yg * lax.rsqrt(jnp.mean(yg * yg, axis=-1, keepdims=True) + EPS)
    y = yg.reshape(Bn, T, SSM_D_INNER) * norm_w.astype(f32)
    return y.astype(h_in.dtype), h.astype(h_in.dtype), conv_new


def setup_inputs(seed: int = 0) -> dict:
    key = jax.random.key(seed)
    ks = list(jax.random.split(key, 32))
    f32 = jnp.float32

    def nrm(k, shape, scale):
        return scale * jax.random.normal(k, shape, f32)

    def gain(k, shape):
        return 1.0 + 0.02 * jax.random.normal(k, shape, f32)

    def decay_log_rate(k, shape):
        return jnp.log(jax.random.uniform(k, shape, f32, 1.0, 16.0))

    def dt_bias_init(k, shape):
        dt = jnp.exp(jax.random.uniform(k, shape, f32, math.log(1e-3), math.log(1e-1)))
        return dt + jnp.log(-jnp.expm1(-dt))

    return {
        'x_prompt': nrm(ks[0], (BATCH, SEQ, D_MODEL), 1.0),
        'x_sample': nrm(ks[1], (DEC_BATCH, DEC_SEQ, D_MODEL), 1.0),
        'cache_swa_k': nrm(ks[2], (N_EVEN, DEC_BATCH, WINDOW, SWA_KV_HEADS, SWA_HEAD_DIM), 1.0),
        'cache_swa_v': nrm(ks[3], (N_EVEN, DEC_BATCH, WINDOW, SWA_KV_HEADS, SWA_HEAD_DIM), 1.0),
        'state_gdn': nrm(ks[4], (N_EVEN, DEC_BATCH, GDN_HEADS, GDN_DK, GDN_DV), 0.1),
        'state_gdn_conv': nrm(ks[5], (N_EVEN, DEC_BATCH, CONV_W - 1, GDN_CONV_CH), 1.0),
        'state_ssm': nrm(ks[6], (N_ODD, DEC_BATCH, SSM_HEADS, SSM_HEAD_DIM, SSM_STATE), 0.1),
        'state_ssm_conv': nrm(ks[7], (N_ODD, DEC_BATCH, CONV_W - 1, SSM_CONV_CH), 1.0),
        'norm_ffn1': gain(ks[8], (DEPTH, D_MODEL)),
        'norm_mix': gain(ks[9], (DEPTH, D_MODEL)),
        'norm_ffn2': gain(ks[10], (DEPTH, D_MODEL)),
        'norm_final': gain(ks[11], (D_MODEL,)),
        'w_ffn_gate': nrm(ks[12], (DEPTH, 2, D_MODEL, D_FF), D_MODEL ** -0.5),
        'w_ffn_up': nrm(ks[13], (DEPTH, 2, D_MODEL, D_FF), D_MODEL ** -0.5),
        'w_ffn_down': nrm(ks[14], (DEPTH, 2, D_FF, D_MODEL), D_FF ** -0.5),
        'w_in_even': nrm(ks[15], (N_EVEN, D_MODEL, EVEN_IN), D_MODEL ** -0.5),
        'w_out_even': nrm(ks[16], (N_EVEN, EVEN_OUT, D_MODEL), EVEN_OUT ** -0.5),
        'attn_sinks': nrm(ks[17], (N_EVEN, SWA_HEADS), 1.0),
        'gdn_conv_w': nrm(ks[18], (N_EVEN, CONV_W, GDN_CONV_CH), 0.3),
        'gdn_A_log': decay_log_rate(ks[19], (N_EVEN, GDN_HEADS)),
        'gdn_dt_bias': dt_bias_init(ks[20], (N_EVEN, GDN_HEADS)),
        'gdn_norm_w': gain(ks[21], (N_EVEN, GDN_DV)),
        'w_in_odd': nrm(ks[22], (N_ODD, D_MODEL, ODD_IN), D_MODEL ** -0.5),
        'w_out_odd': nrm(ks[23], (N_ODD, SSM_D_INNER, D_MODEL), SSM_D_INNER ** -0.5),
        'ssm_conv_w': nrm(ks[24], (N_ODD, CONV_W, SSM_CONV_CH), 0.3),
        'ssm_conv_b': nrm(ks[25], (N_ODD, SSM_CONV_CH), 0.02),
        'ssm_A_log': decay_log_rate(ks[26], (N_ODD, SSM_HEADS)),
        'ssm_dt_bias': dt_bias_init(ks[27], (N_ODD, SSM_HEADS)),
        'ssm_D': gain(ks[28], (N_ODD, SSM_HEADS)),
        'ssm_norm_w': gain(ks[29], (N_ODD, SSM_D_INNER)),
    }


def reference(x_prompt, x_sample, cache_swa_k, cache_swa_v, state_gdn, state_gdn_conv, state_ssm, state_ssm_conv,
              norm_ffn1, norm_mix, norm_ffn2, norm_final, w_ffn_gate, w_ffn_up, w_ffn_down,
              w_in_even, w_out_even, attn_sinks, gdn_conv_w, gdn_A_log, gdn_dt_bias, gdn_norm_w,
              w_in_odd, w_out_odd, ssm_conv_w, ssm_conv_b, ssm_A_log, ssm_dt_bias, ssm_D, ssm_norm_w):

    def trunk(x, prompt):
        Bn, T = x.shape[:2]
        new_k, new_v, new_S, new_gconv, new_h, new_sconv = [], [], [], [], [], []
        for l in range(DEPTH):
            x = ffn_half(x, norm_ffn1[l], w_ffn_gate[l, 0], w_ffn_up[l, 0], w_ffn_down[l, 0])
            h = rmsnorm(x, norm_mix[l])
            if l % 2 == 0:
                e = l // 2
                q, k, v, qkv_raw, z, a, b = split_cols(h @ w_in_even[e], EVEN_SPLITS)
                q = q.reshape(Bn, T, SWA_HEADS, SWA_HEAD_DIM)
                k = k.reshape(Bn, T, SWA_KV_HEADS, SWA_HEAD_DIM)
                v = v.reshape(Bn, T, SWA_KV_HEADS, SWA_HEAD_DIM)
                if prompt:
                    o_a = swa_prompt(q, k, v, attn_sinks[e])
                    new_k.append(k[:, T - WINDOW:])
                    new_v.append(v[:, T - WINDOW:])
                    conv_prev = jnp.zeros((Bn, CONV_W - 1, GDN_CONV_CH), x.dtype)
                    S0 = jnp.zeros((Bn, GDN_HEADS, GDN_DK, GDN_DV), x.dtype)
                else:
                    o_a = swa_sample(q, k, v, cache_swa_k[e], cache_swa_v[e], attn_sinks[e])
                    new_k.append(k)
                    new_v.append(v)
                    conv_prev, S0 = state_gdn_conv[e], state_gdn[e]
                o_b, S, gconv = gdn_mixer(qkv_raw, z, a, b, conv_prev, S0, gdn_conv_w[e], gdn_A_log[e],
                                          gdn_dt_bias[e], gdn_norm_w[e], prompt)
                new_S.append(S)
                new_gconv.append(gconv)
                x = x + jnp.concatenate([o_a, o_b], axis=-1) @ w_out_even[e]
            else:
                o = l // 2
                if prompt:
                    conv_prev = jnp.zeros((Bn, CONV_W - 1, SSM_CONV_CH), x.dtype)
                    h0 = jnp.zeros((Bn, SSM_HEADS, SSM_HEAD_DIM, SSM_STATE), x.dtype)
                else:
                    conv_prev, h0 = state_ssm_conv[o], state_ssm[o]
                y, hS, sconv = ssm_mixer(h @ w_in_odd[o], conv_prev, h0, ssm_conv_w[o], ssm_conv_b[o], ssm_A_log[o],
                                         ssm_dt_bias[o], ssm_D[o], ssm_norm_w[o], prompt)
                new_h.append(hS)
                new_sconv.append(sconv)
                x = x + y @ w_out_odd[o]
            x = ffn_half(x, norm_ffn2[l], w_ffn_gate[l, 1], w_ffn_up[l, 1], w_ffn_down[l, 1])
        return (rmsnorm(x, norm_final), jnp.stack(new_k), jnp.stack(new_v), jnp.stack(new_S),
                jnp.stack(new_gconv), jnp.stack(new_h), jnp.stack(new_sconv))

    y_prompt, p_swa_k, p_swa_v, p_gdn, p_gdn_conv, p_ssm, p_ssm_conv = trunk(x_prompt, True)
    y_sample, s_swa_k, s_swa_v, s_gdn, s_gdn_conv, s_ssm, s_ssm_conv = trunk(x_sample, False)
    return (y_prompt, y_sample, p_swa_k, p_swa_v, p_gdn, p_gdn_conv, p_ssm, p_ssm_conv,
            s_swa_k, s_swa_v, s_gdn, s_gdn_conv, s_ssm, s_ssm_conv)
```

```python
import functools

import jax
import jax.numpy as jnp
from jax import lax
from jax.experimental import pallas as pl
from jax.experimental.pallas import tpu as pltpu

F32 = jnp.float32
BF16 = jnp.bfloat16

D_MODEL = 2048
SWA_HEADS = 16
SWA_KV_HEADS = 4
SWA_GROUP = SWA_HEADS // SWA_KV_HEADS
SWA_HEAD_DIM = 64
WINDOW = 128
GDN_HEADS = 8
GDN_DK = 128
GDN_DV = 128
CONV_W = 4
SSM_D_INNER = 2 * D_MODEL
SSM_HEAD_DIM = 64
SSM_HEADS = SSM_D_INNER // SSM_HEAD_DIM
SSM_GROUPS = 8
SSM_REP = SSM_HEADS // SSM_GROUPS
SSM_STATE = 128
D_FF = 5632
EPS = 1e-6
NEG_INF = -1e30

SWA_Q_W = SWA_HEADS * SWA_HEAD_DIM
SWA_KV_W = SWA_KV_HEADS * SWA_HEAD_DIM
GDN_QK_W = GDN_HEADS * GDN_DK
GDN_V_W = GDN_HEADS * GDN_DV
GDN_CONV_CH = 2 * GDN_QK_W + GDN_V_W
SSM_BC_W = SSM_GROUPS * SSM_STATE
SSM_CONV_CH = SSM_D_INNER + 2 * SSM_BC_W
SSM_GROUP_W = SSM_REP * SSM_HEAD_DIM

EV_Q = 0
EV_GQKV = SWA_Q_W
EV_Z = EV_GQKV + GDN_CONV_CH
EV_K = EV_Z + GDN_V_W
EV_V = EV_K + SWA_KV_W
EV_MAIN = EV_V + SWA_KV_W
OD_Z = 0
OD_XBC = SSM_D_INNER
OD_MAIN = SSM_D_INNER + SSM_CONV_CH

LANES = 128
SUBLANES = 8
CHUNK = 128
VMEM_LIMIT = 56 * 1024 * 1024


def _params(sem):
    return pltpu.CompilerParams(dimension_semantics=sem, vmem_limit_bytes=VMEM_LIMIT)


def _sigmoid(x):
    return 1.0 / (1.0 + jnp.exp(-x))


def _silu(x):
    return x * _sigmoid(x)


def _softplus(x):
    return jnp.maximum(x, 0.0) + jnp.log1p(jnp.exp(-jnp.abs(x)))


def _dot(a, b):
    return jnp.dot(a.astype(BF16), b.astype(BF16), preferred_element_type=F32)


def _dot_nt(a, b):
    return lax.dot_general(a.astype(BF16), b.astype(BF16), (((1,), (1,)), ((), ())),
                           preferred_element_type=F32)


def _split2(a):
    hi = a.astype(BF16)
    lo = (a - hi.astype(F32)).astype(BF16)
    return hi, lo


def _split3(a):
    hi = a.astype(BF16)
    r = a - hi.astype(F32)
    mid = r.astype(BF16)
    lo = (r - mid.astype(F32)).astype(BF16)
    return hi, mid, lo


def _dot3(a, b):
    ah, al = _split2(a)
    bh, bl = _split2(b)
    out = jnp.dot(ah, bh, preferred_element_type=F32)
    out = out + jnp.dot(ah, bl, preferred_element_type=F32)
    out = out + jnp.dot(al, bh, preferred_element_type=F32)
    return out


def _dot_sel(m01, x):
    h, m, l = _split3(x)
    out = jnp.dot(m01, h, preferred_element_type=F32)
    out = out + jnp.dot(m01, m, preferred_element_type=F32)
    out = out + jnp.dot(m01, l, preferred_element_type=F32)
    return out


def _dot_sel_r(x, m01):
    h, m, l = _split3(x)
    out = jnp.dot(h, m01, preferred_element_type=F32)
    out = out + jnp.dot(m, m01, preferred_element_type=F32)
    out = out + jnp.dot(l, m01, preferred_element_type=F32)
    return out


def _rms_rows(x, gain):
    ms = jnp.mean(x * x, axis=-1, keepdims=True)
    return x * lax.rsqrt(ms + EPS) * gain


def _iota2(shape, dim):
    return lax.broadcasted_iota(jnp.int32, shape, dim)


def _ffn_kernel(x_ref, g_ref, gf_ref, wg_ref, wu_ref, wd_ref, o_ref, h_ref, *, final_norm):
    j = pl.program_id(1)

    @pl.when(j == 0)
    def _():
        h_ref[...] = _rms_rows(x_ref[...], g_ref[...]).astype(BF16)
        o_ref[...] = jnp.zeros_like(o_ref)

    h = h_ref[...]
    a = jnp.dot(h, wg_ref[...], preferred_element_type=F32)
    b = jnp.dot(h, wu_ref[...], preferred_element_type=F32)
    t = (_silu(a) * b).astype(BF16)
    o_ref[...] += jnp.dot(t, wd_ref[...], preferred_element_type=F32)

    @pl.when(j == pl.num_programs(1) - 1)
    def _():
        y = x_ref[...] + 0.5 * o_ref[...]
        if final_norm:
            y = _rms_rows(y, gf_ref[...])
        o_ref[...] = y


def _ffn(x, gain, gain_final, wg, wu, wd, *, final_norm, tm, tf):
    m, d = x.shape
    f = wg.shape[1]
    grid = (m // tm, f // tf)
    return pl.pallas_call(
        functools.partial(_ffn_kernel, final_norm=final_norm),
        grid=grid,
        in_specs=[
            pl.BlockSpec((tm, d), lambda i, j: (i, 0)),
            pl.BlockSpec((1, d), lambda i, j: (0, 0)),
            pl.BlockSpec((1, d), lambda i, j: (0, 0)),
            pl.BlockSpec((d, tf), lambda i, j: (0, j)),
            pl.BlockSpec((d, tf), lambda i, j: (0, j)),
            pl.BlockSpec((tf, d), lambda i, j: (j, 0)),
        ],
        out_specs=pl.BlockSpec((tm, d), lambda i, j: (i, 0)),
        out_shape=jax.ShapeDtypeStruct((m, d), F32),
        scratch_shapes=[pltpu.VMEM((tm, d), BF16)],
        compiler_params=_params(("parallel", "arbitrary")),
        name="ffn_half",
    )(x, gain, gain_final, wg, wu, wd)


def _proj_kernel(x_ref, g_ref, w_ref, wt_ref, o_ref, ot_ref, h_ref):
    j = pl.program_id(1)

    @pl.when(j == 0)
    def _():
        h = _rms_rows(x_ref[...], g_ref[...]).astype(BF16)
        h_ref[...] = h
        ot_ref[...] = jnp.dot(h, wt_ref[...], preferred_element_type=F32)

    o_ref[...] = jnp.dot(h_ref[...], w_ref[...], preferred_element_type=F32)


def _proj(x, gain, w_main, w_tail, *, tm, tn):
    m, d = x.shape
    n = w_main.shape[1]
    grid = (m // tm, n // tn)
    return pl.pallas_call(
        _proj_kernel,
        grid=grid,
        in_specs=[
            pl.BlockSpec((tm, d), lambda i, j: (i, 0)),
            pl.BlockSpec((1, d), lambda i, j: (0, 0)),
            pl.BlockSpec((d, tn), lambda i, j: (0, j)),
            pl.BlockSpec((d, LANES), lambda i, j: (0, 0)),
        ],
        out_specs=[
            pl.BlockSpec((tm, tn), lambda i, j: (i, j)),
            pl.BlockSpec((tm, LANES), lambda i, j: (i, 0)),
        ],
        out_shape=[jax.ShapeDtypeStruct((m, n), F32), jax.ShapeDtypeStruct((m, LANES), F32)],
        scratch_shapes=[pltpu.VMEM((tm, d), BF16)],
        compiler_params=_params(("parallel", "arbitrary")),
        name="norm_proj",
    )(x, gain, w_main, w_tail)


def _outproj_kernel(*refs, n_in):
    x_ref = refs[0]
    o_refs = refs[1:1 + n_in]
    w_refs = refs[1 + n_in:1 + 2 * n_in]
    out_ref = refs[1 + 2 * n_in]
    acc = x_ref[...]
    for o_ref, w_ref in zip(o_refs, w_refs):
        acc = acc + jnp.dot(o_ref[...], w_ref[...], preferred_element_type=F32)
    out_ref[...] = acc


def _outproj(x, outs, ws, *, tm, tn):
    m, d = x.shape
    n_in = len(outs)
    grid = (m // tm, d // tn)
    in_specs = [pl.BlockSpec((tm, tn), lambda i, j: (i, j))]
    for o in outs:
        in_specs.append(pl.BlockSpec((tm, o.shape[1]), lambda i, j: (i, 0)))
    for w in ws:
        in_specs.append(pl.BlockSpec((w.shape[0], tn), lambda i, j: (0, j)))
    return pl.pallas_call(
        functools.partial(_outproj_kernel, n_in=n_in),
        grid=grid,
        in_specs=in_specs,
        out_specs=pl.BlockSpec((tm, tn), lambda i, j: (i, j)),
        out_shape=jax.ShapeDtypeStruct((m, d), F32),
        compiler_params=_params(("parallel", "arbitrary")),
        name="out_proj",
    )(x, *outs, *ws)


def _swa_prompt_kernel(sink_ref, q_ref, kc_ref, kp_ref, vc_ref, vp_ref, o_ref):
    n = pl.program_id(1)
    w = WINDOW
    qi = _iota2((w, 2 * w), 0)
    kj = _iota2((w, 2 * w), 1)
    valid = (kj > qi) & (kj <= qi + w) & ((n > 0) | (kj >= w))
    scale = SWA_HEAD_DIM ** -0.5
    kk = jnp.concatenate([kp_ref[...], kc_ref[...]], axis=0).astype(BF16)
    vv = jnp.concatenate([vp_ref[...], vc_ref[...]], axis=0).astype(BF16)
    q = q_ref[...].astype(BF16)
    outs = []
    for h in range(SWA_HEADS):
        c = h // SWA_GROUP
        qh = q[:, h * SWA_HEAD_DIM:(h + 1) * SWA_HEAD_DIM]
        kh = kk[:, c * SWA_HEAD_DIM:(c + 1) * SWA_HEAD_DIM]
        vh = vv[:, c * SWA_HEAD_DIM:(c + 1) * SWA_HEAD_DIM]
        s = lax.dot_general(qh, kh, (((1,), (1,)), ((), ())), preferred_element_type=F32) * scale
        s = jnp.where(valid, s, NEG_INF)
        sink = sink_ref[h]
        mx = jnp.maximum(jnp.max(s, axis=-1, keepdims=True), sink)
        p = jnp.exp(s - mx)
        den = jnp.sum(p, axis=-1, keepdims=True) + jnp.exp(sink - mx)
        p = p / den
        outs.append(jnp.dot(p.astype(BF16), vh, preferred_element_type=F32))
    o_ref[...] = jnp.concatenate(outs, axis=-1).astype(o_ref.dtype)


def _swa_prompt(p_main, sinks, bsz, t):
    nb = t // WINDOW
    kblk = EV_K // SWA_KV_W
    vblk = EV_V // SWA_KV_W

    def cur(col):
        return lambda b, n: (b * nb + n, col)

    def prev(col):
        return lambda b, n: (b * nb + jnp.maximum(n - 1, 0), col)

    return pl.pallas_call(
        _swa_prompt_kernel,
        grid=(bsz, nb),
        in_specs=[
            pl.BlockSpec(memory_space=pltpu.SMEM),
            pl.BlockSpec((WINDOW, SWA_Q_W), cur(0)),
            pl.BlockSpec((WINDOW, SWA_KV_W), cur(kblk)),
            pl.BlockSpec((WINDOW, SWA_KV_W), prev(kblk)),
            pl.BlockSpec((WINDOW, SWA_KV_W), cur(vblk)),
            pl.BlockSpec((WINDOW, SWA_KV_W), prev(vblk)),
        ],
        out_specs=pl.BlockSpec((WINDOW, SWA_Q_W), cur(0)),
        out_shape=jax.ShapeDtypeStruct((bsz * t, SWA_Q_W), BF16),
        compiler_params=_params(("parallel", "arbitrary")),
        name="swa_prompt",
    )(sinks, p_main, p_main, p_main, p_main, p_main)


def _unit_lower_inverse(a, ri, ci):
    eye = (ri == ci).astype(F32)
    a0 = jnp.where((ri >> 4) == (ci >> 4), a, 0.0)
    x = eye - a0
    p = _dot3(a0, a0)
    for step in range(3):
        x = x + _dot3(x, p)
        if step < 2:
            p = _dot3(p, p)
    for sh in (4, 5, 6):
        e = jnp.where(((ri >> (sh + 1)) == (ci >> (sh + 1))) & ((ri >> sh) != (ci >> sh)), a, 0.0)
        x = x - _dot3(_dot3(x, e), x)
    return x


def _conv_section(ext_ref, carry_ref, sec, u_ref, w, bias, rows):
    ext_ref[0:SUBLANES, :] = carry_ref[sec]
    ext_ref[SUBLANES:SUBLANES + rows, :] = u_ref[...]
    acc = ext_ref[SUBLANES:SUBLANES + rows, :] * w[CONV_W - 1:CONV_W, :]
    for j in range(CONV_W - 1):
        off = SUBLANES - (CONV_W - 1) + j
        acc = acc + ext_ref[off:off + rows, :] * w[j:j + 1, :]
    if bias is not None:
        acc = acc + bias
    carry_ref[sec] = u_ref[rows - SUBLANES:rows, :]
    return _silu(acc)


def _gdn_prompt_kernel(uq_ref, uk_ref, uv_ref, z_ref, tail_ref, wq_ref, wk_ref, wv_ref,
                       pa_ref, pb_ref, nw_ref, tri_ref, ones_ref,
                       o_ref, sout_ref,
                       s_ref, carry_ref, ext_ref, cq_ref, ck_ref, cv_ref, grow_ref, *, hb_heads, rows):
    hb = pl.program_id(1)
    i = pl.program_id(2)
    nchunk = rows // CHUNK

    @pl.when(i == 0)
    def _():
        s_ref[...] = jnp.zeros_like(s_ref)
        carry_ref[...] = jnp.zeros_like(carry_ref)

    cq_ref[...] = _conv_section(ext_ref, carry_ref, 0, uq_ref, wq_ref[...], None, rows)
    ck_ref[...] = _conv_section(ext_ref, carry_ref, 1, uk_ref, wk_ref[...], None, rows)
    cv_ref[...] = _conv_section(ext_ref, carry_ref, 2, uv_ref, wv_ref[...], None, rows)

    t = tail_ref[...]
    g_all = -jnp.exp(pa_ref[...]) * _softplus(t + pb_ref[...])
    beta_all = _sigmoid(t)
    gcum_all = _dot_sel(tri_ref[...], g_all)
    gtot_all = _dot_sel(ones_ref[...], g_all)
    grow_ref[...] = jnp.transpose(gcum_all)[0:SUBLANES, :]

    lane = _iota2((rows, LANES), 1)
    ri = _iota2((CHUNK, CHUNK), 0)
    ci = _iota2((CHUNK, CHUNK), 1)
    incl = ri >= ci
    strict = ri > ci

    for h in range(hb_heads):
        hg = hb * hb_heads + h
        gcol = jnp.sum(jnp.where(lane == hg, gcum_all, 0.0), axis=1, keepdims=True)
        gtot = jnp.sum(jnp.where(lane == hg, gtot_all, 0.0), axis=1, keepdims=True)
        beta = jnp.sum(jnp.where(lane == hg + GDN_HEADS, beta_all, 0.0), axis=1, keepdims=True)
        grow = grow_ref[pl.ds(hg, 1), :]
        hs = slice(h * GDN_DK, (h + 1) * GDN_DK)
        for c in range(nchunk):
            rs = slice(c * CHUNK, (c + 1) * CHUNK)
            q = cq_ref[rs, hs]
            k = ck_ref[rs, hs]
            v = cv_ref[rs, hs]
            q = q * lax.rsqrt(jnp.sum(q * q, axis=-1, keepdims=True) + EPS) * (GDN_DK ** -0.5)
            k = k * lax.rsqrt(jnp.sum(k * k, axis=-1, keepdims=True) + EPS)
            gc = gcol[rs]
            gt = gtot[rs]
            bc = beta[rs]
            gr = grow[:, rs]
            decay = jnp.exp(jnp.where(incl, gc - gr, NEG_INF))
            kb = k * bc
            a = jnp.where(strict, _dot_nt(kb, k) * decay, 0.0)
            tinv = _unit_lower_inverse(a, ri, ci)
            eg = jnp.exp(gc)
            rhs = jnp.concatenate([v * bc, kb * eg], axis=1)
            sol = _dot3(tinv, rhs)
            u = sol[:, :GDN_DV]
            w = sol[:, GDN_DV:]
            s = s_ref[h]
            ws_qs = _dot(jnp.concatenate([w, q * eg], axis=0), s)
            v_new = u - ws_qs[:CHUNK]
            attn = _dot_nt(q, k) * decay
            o = ws_qs[CHUNK:] + _dot(attn, v_new)
            kd = k * jnp.exp(gt - gc)
            s_ref[h] = s * jnp.exp(gt) + _dot(jnp.transpose(kd), v_new)
            zc = z_ref[rs, hs]
            o = _rms_rows(o, nw_ref[...]) * _silu(zc)
            o_ref[rs, hs] = o.astype(o_ref.dtype)

    @pl.when(i == pl.num_programs(2) - 1)
    def _():
        sout_ref[0] = s_ref[...]


def _chunk_masks(rows):
    r = jnp.arange(rows)
    same = (r[:, None] // CHUNK) == (r[None, :] // CHUNK)
    tri = (same & (r[:, None] >= r[None, :])).astype(BF16)
    ones = same.astype(BF16)
    return tri, ones


def _gdn_prompt(p_main, p_tail, conv_w, pa, pb, norm_w, bsz, t, *, hb_heads, rows):
    nblk = t // rows
    wblk = hb_heads * GDN_DK
    nhb = GDN_HEADS // hb_heads
    tri, ones = _chunk_masks(rows)

    def col(base):
        return lambda b, hb, i: (b * nblk + i, base // wblk + hb)

    def wcol(base):
        return lambda b, hb, i: (0, base // wblk + hb)

    const = lambda b, hb, i: (0, 0)
    kern = functools.partial(_gdn_prompt_kernel, hb_heads=hb_heads, rows=rows)
    return pl.pallas_call(
        kern,
        grid=(bsz, nhb, nblk),
        in_specs=[
            pl.BlockSpec((rows, wblk), col(EV_GQKV)),
            pl.BlockSpec((rows, wblk), col(EV_GQKV + GDN_QK_W)),
            pl.BlockSpec((rows, wblk), col(EV_GQKV + 2 * GDN_QK_W)),
            pl.BlockSpec((rows, wblk), col(EV_Z)),
            pl.BlockSpec((rows, LANES), lambda b, hb, i: (b * nblk + i, 0)),
            pl.BlockSpec((CONV_W, wblk), wcol(0)),
            pl.BlockSpec((CONV_W, wblk), wcol(GDN_QK_W)),
            pl.BlockSpec((CONV_W, wblk), wcol(2 * GDN_QK_W)),
            pl.BlockSpec((1, LANES), const),
            pl.BlockSpec((1, LANES), const),
            pl.BlockSpec((1, GDN_DV), const),
            pl.BlockSpec((rows, rows), const),
            pl.BlockSpec((rows, rows), const),
        ],
        out_specs=[
            pl.BlockSpec((rows, wblk), lambda b, hb, i: (b * nblk + i, hb)),
            pl.BlockSpec((1, hb_heads, GDN_DK, GDN_DV), lambda b, hb, i: (b, hb, 0, 0)),
        ],
        out_shape=[
            jax.ShapeDtypeStruct((bsz * t, GDN_V_W), BF16),
            jax.ShapeDtypeStruct((bsz, GDN_HEADS, GDN_DK, GDN_DV), F32),
        ],
        scratch_shapes=[
            pltpu.VMEM((hb_heads, GDN_DK, GDN_DV), F32),
            pltpu.VMEM((3, SUBLANES, wblk), F32),
            pltpu.VMEM((rows + SUBLANES, wblk), F32),
            pltpu.VMEM((rows, wblk), F32),
            pltpu.VMEM((rows, wblk), F32),
            pltpu.VMEM((rows, wblk), F32),
            pltpu.VMEM((SUBLANES, rows), F32),
        ],
        compiler_params=_params(("parallel", "parallel", "arbitrary")),
        name="gdn_prompt",
    )(p_main, p_main, p_main, p_main, p_tail, conv_w, conv_w, conv_w, pa, pb, norm_w, tri, ones)


SSD_SEC_W = SSM_CONV_CH // 3


def _ssd_prompt_kernel(z_ref, xa_ref, xb_ref, xc3_ref, tail_ref, cw_ref, cb_ref, pa_ref, pb_ref, de_ref, nw_ref,
                       ex_ref, tri_ref,
                       y_ref, hout_ref,
                       ht_ref, carry_ref, ext_ref, xc_ref, cumt_ref, dtt_ref):
    i = pl.program_id(1)
    rows = CHUNK

    @pl.when(i == 0)
    def _():
        ht_ref[...] = jnp.zeros_like(ht_ref)
        carry_ref[...] = jnp.zeros_like(carry_ref)

    for sec, u_ref in enumerate((xa_ref, xb_ref, xc3_ref)):
        ss = slice(sec * SSD_SEC_W, (sec + 1) * SSD_SEC_W)
        xc_ref[:, ss] = _conv_section(ext_ref, carry_ref, sec, u_ref, cw_ref[:, ss], cb_ref[:, ss], rows)

    dt = _softplus(tail_ref[...] + pb_ref[...])
    da = dt * (-jnp.exp(pa_ref[...]))
    cum = _dot_sel(tri_ref[...], da)
    cumt_ref[...] = jnp.transpose(cum)
    dtt_ref[...] = jnp.transpose(dt)

    ri = _iota2((rows, rows), 0)
    ci = _iota2((rows, rows), 1)
    tri = ri >= ci
    lane = _iota2((rows, LANES), 1)
    lo_half = lane < SSM_HEAD_DIM

    for g in range(SSM_GROUPS):
        gs = slice(g * SSM_GROUP_W, (g + 1) * SSM_GROUP_W)
        bmat = xc_ref[:, SSM_D_INNER + g * SSM_STATE:SSM_D_INNER + (g + 1) * SSM_STATE]
        cmat = xc_ref[:, SSM_D_INNER + SSM_BC_W + g * SSM_STATE:SSM_D_INNER + SSM_BC_W + (g + 1) * SSM_STATE]
        xg = xc_ref[:, gs]
        cb = _dot_nt(cmat, bmat)
        exg = ex_ref[:, gs]
        cum_e = _dot_sel_r(cum, exg)
        dt_e = _dot_sel_r(dt, exg)
        htg = ht_ref[g]
        y = _dot(cmat, htg) * jnp.exp(cum_e)
        pieces = []
        for pr in range(SSM_REP // 2):
            scs = []
            for half in range(2):
                h = g * SSM_REP + 2 * pr + half
                ccol = cum[:, h:h + 1]
                crow = cumt_ref[h:h + 1, :]
                drow = dtt_ref[h:h + 1, :]
                lmat = jnp.exp(jnp.where(tri, ccol - crow, NEG_INF))
                scs.append((cb * lmat * drow).astype(BF16))
            xp = xg[:, pr * LANES:(pr + 1) * LANES]
            rhs = jnp.concatenate([jnp.where(lo_half, xp, 0.0), jnp.where(lo_half, 0.0, xp)], axis=0)
            pieces.append(jnp.dot(jnp.concatenate(scs, axis=1), rhs.astype(BF16),
                                  preferred_element_type=F32))
        y = y + jnp.concatenate(pieces, axis=1) + de_ref[:, gs] * xg
        last_e = cum_e[rows - 1:rows, :]
        wts = jnp.exp(last_e - cum_e) * dt_e
        ht_ref[g] = htg * jnp.exp(last_e) + _dot(jnp.transpose(bmat), xg * wts)
        zg = z_ref[:, gs]
        yz = y * _silu(zg)
        y_ref[:, gs] = _rms_rows(yz, nw_ref[:, gs]).astype(y_ref.dtype)

    @pl.when(i == pl.num_programs(1) - 1)
    def _():
        for g in range(SSM_GROUPS):
            hg = jnp.transpose(ht_ref[g])
            hout_ref[0, g * SSM_REP:(g + 1) * SSM_REP] = hg.reshape(SSM_REP, SSM_HEAD_DIM, SSM_STATE)


def _head_expand():
    h = jnp.arange(LANES)[:, None]
    c = jnp.arange(SSM_D_INNER)[None, :]
    return ((c // SSM_HEAD_DIM) == h).astype(BF16)


def _ssd_prompt(p_main, p_tail, conv_w, conv_b, pa, pb, d_e, norm_w, bsz, t):
    rows = CHUNK
    nblk = t // rows
    tri, _ = _chunk_masks(rows)
    ex = _head_expand()
    const = lambda b, i: (0, 0)
    row = lambda b, i: (b * nblk + i, 0)
    return pl.pallas_call(
        _ssd_prompt_kernel,
        grid=(bsz, nblk),
        in_specs=[
            pl.BlockSpec((rows, SSM_D_INNER), row),
            pl.BlockSpec((rows, SSD_SEC_W), lambda b, i: (b * nblk + i, OD_XBC // SSD_SEC_W)),
            pl.BlockSpec((rows, SSD_SEC_W), lambda b, i: (b * nblk + i, OD_XBC // SSD_SEC_W + 1)),
            pl.BlockSpec((rows, SSD_SEC_W), lambda b, i: (b * nblk + i, OD_XBC // SSD_SEC_W + 2)),
            pl.BlockSpec((rows, LANES), row),
            pl.BlockSpec((CONV_W, SSM_CONV_CH), const),
            pl.BlockSpec((1, SSM_CONV_CH), const),
            pl.BlockSpec((1, LANES), const),
            pl.BlockSpec((1, LANES), const),
            pl.BlockSpec((1, SSM_D_INNER), const),
            pl.BlockSpec((1, SSM_D_INNER), const),
            pl.BlockSpec((LANES, SSM_D_INNER), const),
            pl.BlockSpec((rows, rows), const),
        ],
        out_specs=[
            pl.BlockSpec((rows, SSM_D_INNER), row),
            pl.BlockSpec((1, SSM_HEADS, SSM_HEAD_DIM, SSM_STATE), lambda b, i: (b, 0, 0, 0)),
        ],
        out_shape=[
            jax.ShapeDtypeStruct((bsz * t, SSM_D_INNER), BF16),
            jax.ShapeDtypeStruct((bsz, SSM_HEADS, SSM_HEAD_DIM, SSM_STATE), F32),
        ],
        scratch_shapes=[
            pltpu.VMEM((SSM_GROUPS, SSM_STATE, SSM_GROUP_W), F32),
            pltpu.VMEM((3, SUBLANES, SSD_SEC_W), F32),
            pltpu.VMEM((rows + SUBLANES, SSD_SEC_W), F32),
            pltpu.VMEM((rows, SSM_CONV_CH), F32),
            pltpu.VMEM((LANES, rows), F32),
            pltpu.VMEM((LANES, rows), F32),
        ],
        compiler_params=_params(("parallel", "arbitrary")),
        name="ssd_prompt",
    )(p_main, p_main, p_main, p_main, p_tail, conv_w, conv_b, pa, pb, d_e, norm_w, ex, tri)


def _swa_sample_kernel(sink_ref, q_ref, kn_ref, vn_ref, ck_ref, cv_ref, o_ref):
    scale = SWA_HEAD_DIM ** -0.5
    q = q_ref[0].astype(BF16)
    kn = kn_ref[0].astype(BF16).astype(F32)
    vn = vn_ref[0].astype(BF16).astype(F32)
    ck = ck_ref[0].astype(BF16)
    cv = cv_ref[0].astype(BF16)
    sink = sink_ref[...]
    valid = _iota2((SWA_GROUP, WINDOW), 1) >= 1
    outs = []
    for c in range(SWA_KV_HEADS):
        cs = slice(c * SWA_HEAD_DIM, (c + 1) * SWA_HEAD_DIM)
        gs = slice(c * SWA_GROUP, (c + 1) * SWA_GROUP)
        qc = q[gs, :]
        s = lax.dot_general(qc, ck[:, cs], (((1,), (1,)), ((), ())), preferred_element_type=F32) * scale
        s = jnp.where(valid, s, NEG_INF)
        sn = jnp.sum(qc.astype(F32) * kn[:, cs], axis=-1, keepdims=True) * scale
        sk = sink[gs, :]
        mx = jnp.maximum(jnp.maximum(jnp.max(s, axis=-1, keepdims=True), sn), sk)
        p = jnp.exp(s - mx)
        pn = jnp.exp(sn - mx)
        den = jnp.sum(p, axis=-1, keepdims=True) + pn + jnp.exp(sk - mx)
        o = jnp.dot(p.astype(BF16), cv[:, cs], preferred_element_type=F32) + pn * vn[:, cs]
        outs.append(o / den)
    o_ref[0] = jnp.concatenate(outs, axis=0).astype(o_ref.dtype)


def _swa_sample(q, kn, vn, ck, cv, sinks):
    nb = q.shape[0]
    kvw = SWA_KV_W
    b3 = lambda b: (b, 0, 0)
    return pl.pallas_call(
        _swa_sample_kernel,
        grid=(nb,),
        in_specs=[
            pl.BlockSpec((SWA_HEADS, 1), lambda b: (0, 0)),
            pl.BlockSpec((1, SWA_HEADS, SWA_HEAD_DIM), b3),
            pl.BlockSpec((1, 1, kvw), b3),
            pl.BlockSpec((1, 1, kvw), b3),
            pl.BlockSpec((1, WINDOW, kvw), b3),
            pl.BlockSpec((1, WINDOW, kvw), b3),
        ],
        out_specs=pl.BlockSpec((1, SWA_HEADS, SWA_HEAD_DIM), b3),
        out_shape=jax.ShapeDtypeStruct((nb, SWA_HEADS, SWA_HEAD_DIM), F32),
        compiler_params=_params(("parallel",)),
        name="swa_sample",
    )(sinks, q, kn, vn, ck, cv)


def _gdn_sample_kernel(u_ref, prev_ref, w_ref, s_ref, z_ref, ab_ref, pa_ref, pb_ref, nw_ref,
                       o_ref, sout_ref, cnew_ref):
    hh = GDN_HEADS
    u = u_ref[0]
    pv = prev_ref[0]
    w = w_ref[...]
    x = u * w[CONV_W - 1]
    for j in range(CONV_W - 1):
        x = x + pv[j] * w[j]
        cnew_ref[0, j] = pv[j + 1] if j + 1 < CONV_W - 1 else u
    x = _silu(x)
    q = x[0:hh]
    k = x[hh:2 * hh]
    v = x[2 * hh:3 * hh]
    q = q * lax.rsqrt(jnp.sum(q * q, axis=-1, keepdims=True) + EPS) * (GDN_DK ** -0.5)
    k = k * lax.rsqrt(jnp.sum(k * k, axis=-1, keepdims=True) + EPS)
    ab = ab_ref[0]
    dec = jnp.exp(-jnp.exp(pa_ref[...]) * _softplus(ab[0:hh] + pb_ref[...]))
    beta = _sigmoid(ab[hh:2 * hh])
    qk = jnp.concatenate([q, k, jnp.zeros((LANES - 2 * hh, GDN_DK), F32)], axis=0)
    qkt = jnp.transpose(qk)
    outs = []
    for h in range(hh):
        s = s_ref[0, h] * dec[h:h + 1, :]
        kcol = qkt[:, hh + h:hh + h + 1]
        ks = jnp.sum(s * kcol, axis=0, keepdims=True)
        delta = beta[h:h + 1, :] * (v[h:h + 1, :] - ks)
        s = s + kcol * delta
        sout_ref[0, h] = s
        outs.append(jnp.sum(s * qkt[:, h:h + 1], axis=0, keepdims=True))
    o = jnp.concatenate(outs, axis=0)
    o_ref[0] = _rms_rows(o, nw_ref[...]) * _silu(z_ref[0])


def _gdn_sample(u, prev, w, s, z, ab, pa, pb, nw):
    nb = u.shape[0]
    hh = GDN_HEADS
    b3 = lambda b: (b, 0, 0)
    b4 = lambda b: (b, 0, 0, 0)
    c2 = lambda b: (0, 0)
    return pl.pallas_call(
        _gdn_sample_kernel,
        grid=(nb,),
        in_specs=[
            pl.BlockSpec((1, 3 * hh, GDN_DK), b3),
            pl.BlockSpec((1, CONV_W - 1, 3 * hh, GDN_DK), b4),
            pl.BlockSpec((CONV_W, 3 * hh, GDN_DK), lambda b: (0, 0, 0)),
            pl.BlockSpec((1, hh, GDN_DK, GDN_DV), b4),
            pl.BlockSpec((1, hh, GDN_DV), b3),
            pl.BlockSpec((1, 2 * hh, LANES), b3),
            pl.BlockSpec((hh, LANES), c2),
            pl.BlockSpec((hh, LANES), c2),
            pl.BlockSpec((1, GDN_DV), c2),
        ],
        out_specs=[
            pl.BlockSpec((1, hh, GDN_DV), b3),
            pl.BlockSpec((1, hh, GDN_DK, GDN_DV), b4),
            pl.BlockSpec((1, CONV_W - 1, 3 * hh, GDN_DK), b4),
        ],
        out_shape=[
            jax.ShapeDtypeStruct((nb, hh, GDN_DV), F32),
            jax.ShapeDtypeStruct((nb, hh, GDN_DK, GDN_DV), F32),
            jax.ShapeDtypeStruct((nb, CONV_W - 1, 3 * hh, GDN_DK), F32),
        ],
        compiler_params=_params(("parallel",)),
        name="gdn_sample",
    )(u, prev, w, s, z, ab, pa, pb, nw)


def _ssd_prep_kernel(x_ref, prev_ref, cw_ref, cb_ref, tail_ref, pa_ref, pb_ref,
                     xc_ref, dt_ref, dec_ref, cnew_ref):
    u = x_ref[...]
    w = cw_ref[...]
    acc = u * w[CONV_W - 1:CONV_W, :] + cb_ref[...]
    for j in range(CONV_W - 1):
        acc = acc + prev_ref[j] * w[j:j + 1, :]
        cnew_ref[j] = prev_ref[j + 1] if j + 1 < CONV_W - 1 else u
    xc_ref[...] = _silu(acc)
    dt = _softplus(tail_ref[...] + pb_ref[...])
    dt_ref[...] = dt
    dec_ref[...] = jnp.exp(dt * (-jnp.exp(pa_ref[...])))


def _ssd_prep_sample(p_main, prev_t, cw, cb, tail, pa, pb):
    nb = p_main.shape[0]
    sec = SSD_SEC_W
    nsec = SSM_CONV_CH // sec
    c2 = lambda s: (0, 0)
    return pl.pallas_call(
        _ssd_prep_kernel,
        grid=(nsec,),
        in_specs=[
            pl.BlockSpec((nb, sec), lambda s: (0, OD_XBC // sec + s)),
            pl.BlockSpec((CONV_W - 1, nb, sec), lambda s: (0, 0, s)),
            pl.BlockSpec((CONV_W, sec), lambda s: (0, s)),
            pl.BlockSpec((1, sec), lambda s: (0, s)),
            pl.BlockSpec((nb, LANES), c2),
            pl.BlockSpec((1, LANES), c2),
            pl.BlockSpec((1, LANES), c2),
        ],
        out_specs=[
            pl.BlockSpec((nb, sec), lambda s: (0, s)),
            pl.BlockSpec((nb, LANES), c2),
            pl.BlockSpec((nb, LANES), c2),
            pl.BlockSpec((CONV_W - 1, nb, sec), lambda s: (0, 0, s)),
        ],
        out_shape=[
            jax.ShapeDtypeStruct((nb, SSM_CONV_CH), F32),
            jax.ShapeDtypeStruct((nb, LANES), F32),
            jax.ShapeDtypeStruct((nb, LANES), F32),
            jax.ShapeDtypeStruct((CONV_W - 1, nb, SSM_CONV_CH), F32),
        ],
        compiler_params=_params(("arbitrary",)),
        name="ssd_prep_sample",
    )(p_main, prev_t, cw, cb, tail, pa, pb)


def _ssd_state_kernel(dt_ref, dec_ref, x_ref, b_ref, c_ref, z_ref, de_ref, nw_ref, h_ref,
                      y_ref, hout_ref, *, nb):
    g = pl.program_id(0)
    x = x_ref[...]
    xt = jnp.transpose(jnp.concatenate([x, jnp.zeros((LANES - nb, SSM_GROUP_W), F32)], axis=0))
    bm = b_ref[...]
    cm = c_ref[...]
    lane = _iota2((SSM_HEAD_DIM, LANES), 1)
    yts = [jnp.zeros((SSM_HEAD_DIM, LANES), F32) for _ in range(SSM_REP)]
    for b in range(nb):
        brow = bm[b:b + 1, :]
        crow = cm[b:b + 1, :]
        for r in range(SSM_REP):
            hd = g * SSM_REP + r
            xcol = xt[r * SSM_HEAD_DIM:(r + 1) * SSM_HEAD_DIM, b:b + 1]
            hn = h_ref[b, r] * dec_ref[b, hd] + (xcol * dt_ref[b, hd]) * brow
            hout_ref[b, r] = hn
            ycol = jnp.sum(hn * crow, axis=1, keepdims=True)
            yts[r] = jnp.where(lane == b, ycol, yts[r])
    y = jnp.transpose(jnp.concatenate(yts, axis=0))[0:nb, :]
    y = y + de_ref[...] * x
    zg = z_ref[...]
    y_ref[...] = _rms_rows(y * _silu(zg), nw_ref[...]).astype(y_ref.dtype)


def _ssd_state_sample(dt, dec, xc, p_main, d_e, nw, h):
    nb = xc.shape[0]
    assert nb <= LANES
    gw = SSM_GROUP_W
    return pl.pallas_call(
        functools.partial(_ssd_state_kernel, nb=nb),
        grid=(SSM_GROUPS,),
        in_specs=[
            pl.BlockSpec(memory_space=pltpu.SMEM),
            pl.BlockSpec(memory_space=pltpu.SMEM),
            pl.BlockSpec((nb, gw), lambda g: (0, g)),
            pl.BlockSpec((nb, SSM_STATE), lambda g: (0, SSM_D_INNER // SSM_STATE + g)),
            pl.BlockSpec((nb, SSM_STATE), lambda g: (0, (SSM_D_INNER + SSM_BC_W) // SSM_STATE + g)),
            pl.BlockSpec((nb, gw), lambda g: (0, g)),
            pl.BlockSpec((1, gw), lambda g: (0, g)),
            pl.BlockSpec((1, gw), lambda g: (0, g)),
            pl.BlockSpec((nb, SSM_REP, SSM_HEAD_DIM, SSM_STATE), lambda g: (0, g, 0, 0)),
        ],
        out_specs=[
            pl.BlockSpec((nb, gw), lambda g: (0, g)),
            pl.BlockSpec((nb, SSM_REP, SSM_HEAD_DIM, SSM_STATE), lambda g: (0, g, 0, 0)),
        ],
        out_shape=[
            jax.ShapeDtypeStruct((nb, SSM_D_INNER), BF16),
            jax.ShapeDtypeStruct((nb, SSM_HEADS, SSM_HEAD_DIM, SSM_STATE), F32),
        ],
        compiler_params=_params(("arbitrary",)),
        name="ssd_state_sample",
    )(dt, dec, xc, xc, xc, p_main, d_e, nw, h)


def _tile(m, pref):
    return pref if m % pref == 0 else m


def _lane_pad_row(v):
    return jnp.zeros((1, LANES), F32).at[0, :v.shape[0]].set(v.astype(F32))


def kernel(x_prompt, x_sample, cache_swa_k, cache_swa_v, state_gdn, state_gdn_conv, state_ssm, state_ssm_conv,
           norm_ffn1, norm_mix, norm_ffn2, norm_final, w_ffn_gate, w_ffn_up, w_ffn_down,
           w_in_even, w_out_even, attn_sinks, gdn_conv_w, gdn_A_log, gdn_dt_bias, gdn_norm_w,
           w_in_odd, w_out_odd, ssm_conv_w, ssm_conv_b, ssm_A_log, ssm_dt_bias, ssm_D, ssm_norm_w):
    bsz, t, d = x_prompt.shape
    nb = x_sample.shape[0]
    depth = norm_ffn1.shape[0]
    mp = bsz * t
    assert x_sample.shape[1] == 1 and d == D_MODEL and t % (2 * CHUNK) == 0
    hh = GDN_HEADS

    wg = w_ffn_gate.astype(BF16)
    wu = w_ffn_up.astype(BF16)
    wd = w_ffn_down.astype(BF16)
    o_q, o_k, o_v, o_g, o_z, o_a = 0, SWA_Q_W, SWA_Q_W + SWA_KV_W, SWA_Q_W + 2 * SWA_KV_W, \
        SWA_Q_W + 2 * SWA_KV_W + GDN_CONV_CH, SWA_Q_W + 2 * SWA_KV_W + GDN_CONV_CH + GDN_V_W
    w_even_main = jnp.concatenate(
        [w_in_even[:, :, o_q:o_k], w_in_even[:, :, o_g:o_z], w_in_even[:, :, o_z:o_a],
         w_in_even[:, :, o_k:o_v], w_in_even[:, :, o_v:o_g]], axis=-1).astype(BF16)
    w_even_tail = jnp.pad(w_in_even[:, :, o_a:], ((0, 0), (0, 0), (0, LANES - 2 * hh))).astype(BF16)
    w_odd_main = w_in_odd[:, :, :OD_MAIN].astype(BF16)
    w_odd_tail = jnp.pad(w_in_odd[:, :, OD_MAIN:], ((0, 0), (0, 0), (0, LANES - SSM_HEADS))).astype(BF16)
    w_oe = w_out_even.astype(BF16)
    w_oo = w_out_odd.astype(BF16)

    xp = x_prompt.reshape(mp, d)
    xs = x_sample.reshape(nb, d)
    tm_p = _tile(mp, 512)
    gain_f = norm_final.reshape(1, d)

    outs = {k: [] for k in ("pk", "pv", "pS", "pgc", "ph", "psc", "sk", "sv", "sS", "sgc", "sh", "ssc")}

    def ffn_pair(xp, xs, gain, l, which, final):
        gain = gain.reshape(1, d)
        xp = _ffn(xp, gain, gain_f, wg[l, which], wu[l, which], wd[l, which], final_norm=final, tm=tm_p, tf=512)
        xs = _ffn(xs, gain, gain_f, wg[l, which], wu[l, which], wd[l, which], final_norm=final, tm=nb, tf=512)
        return xp, xs

    for l in range(depth):
        xp, xs = ffn_pair(xp, xs, norm_ffn1[l], l, 0, False)
        gmix = norm_mix[l].reshape(1, d)
        if l % 2 == 0:
            e = l // 2
            pm, pt = _proj(xp, gmix, w_even_main[e], w_even_tail[e], tm=tm_p, tn=512)
            sm, st = _proj(xs, gmix, w_even_main[e], w_even_tail[e], tm=nb, tn=512)
            pa = _lane_pad_row(gdn_A_log[e])
            pb = _lane_pad_row(gdn_dt_bias[e])
            nw = gdn_norm_w[e].reshape(1, GDN_DV).astype(F32)

            o_a_p = _swa_prompt(pm, attn_sinks[e].astype(F32), bsz, t)
            o_b_p, s_p = _gdn_prompt(pm, pt, gdn_conv_w[e], pa, pb, nw, bsz, t, hb_heads=2, rows=2 * CHUNK)
            pm3 = pm.reshape(bsz, t, EV_MAIN)
            outs["pk"].append(pm3[:, t - WINDOW:, EV_K:EV_K + SWA_KV_W].reshape(bsz, WINDOW, SWA_KV_HEADS, SWA_HEAD_DIM))
            outs["pv"].append(pm3[:, t - WINDOW:, EV_V:EV_V + SWA_KV_W].reshape(bsz, WINDOW, SWA_KV_HEADS, SWA_HEAD_DIM))
            outs["pS"].append(s_p)
            outs["pgc"].append(pm3[:, t - (CONV_W - 1):, EV_GQKV:EV_GQKV + GDN_CONV_CH])
            xp = _outproj(xp, [o_a_p, o_b_p], [w_oe[e, :SWA_Q_W], w_oe[e, SWA_Q_W:]], tm=tm_p, tn=512)

            kn = sm[:, EV_K:EV_K + SWA_KV_W]
            vn = sm[:, EV_V:EV_V + SWA_KV_W]
            o_a_s = _swa_sample(sm[:, EV_Q:EV_Q + SWA_Q_W].reshape(nb, SWA_HEADS, SWA_HEAD_DIM),
                                kn.reshape(nb, 1, SWA_KV_W), vn.reshape(nb, 1, SWA_KV_W),
                                cache_swa_k[e].reshape(nb, WINDOW, SWA_KV_W), cache_swa_v[e].reshape(nb, WINDOW, SWA_KV_W),
                                attn_sinks[e].astype(F32).reshape(SWA_HEADS, 1))
            ab = jnp.broadcast_to(st[:, :2 * hh, None], (nb, 2 * hh, LANES))
            o_b_s, s_s, gc_s = _gdn_sample(
                sm[:, EV_GQKV:EV_GQKV + GDN_CONV_CH].reshape(nb, 3 * hh, GDN_DK),
                state_gdn_conv[e].reshape(nb, CONV_W - 1, 3 * hh, GDN_DK),
                gdn_conv_w[e].reshape(CONV_W, 3 * hh, GDN_DK),
                state_gdn[e], sm[:, EV_Z:EV_Z + GDN_V_W].reshape(nb, hh, GDN_DV), ab,
                jnp.broadcast_to(gdn_A_log[e].astype(F32)[:, None], (hh, LANES)),
                jnp.broadcast_to(gdn_dt_bias[e].astype(F32)[:, None], (hh, LANES)), nw)
            outs["sk"].append(kn.reshape(nb, 1, SWA_KV_HEADS, SWA_HEAD_DIM))
            outs["sv"].append(vn.reshape(nb, 1, SWA_KV_HEADS, SWA_HEAD_DIM))
            outs["sS"].append(s_s)
            outs["sgc"].append(gc_s.reshape(nb, CONV_W - 1, GDN_CONV_CH))
            xs = _outproj(xs, [o_a_s.reshape(nb, SWA_Q_W).astype(BF16), o_b_s.reshape(nb, GDN_V_W).astype(BF16)],
                          [w_oe[e, :SWA_Q_W], w_oe[e, SWA_Q_W:]], tm=nb, tn=512)
        else:
            o = l // 2
            pm, pt = _proj(xp, gmix, w_odd_main[o], w_odd_tail[o], tm=tm_p, tn=1024)
            sm, st = _proj(xs, gmix, w_odd_main[o], w_odd_tail[o], tm=nb, tn=1024)
            pa = _lane_pad_row(ssm_A_log[o])
            pb = _lane_pad_row(ssm_dt_bias[o])
            d_e = jnp.repeat(ssm_D[o].astype(F32), SSM_HEAD_DIM).reshape(1, SSM_D_INNER)
            nw = ssm_norm_w[o].reshape(1, SSM_D_INNER).astype(F32)
            cb = ssm_conv_b[o].reshape(1, SSM_CONV_CH)

            y_p, h_p = _ssd_prompt(pm, pt, ssm_conv_w[o], cb, pa, pb, d_e, nw, bsz, t)
            pm3 = pm.reshape(bsz, t, OD_MAIN)
            outs["ph"].append(h_p)
            outs["psc"].append(pm3[:, t - (CONV_W - 1):, OD_XBC:OD_XBC + SSM_CONV_CH])
            xp = _outproj(xp, [y_p], [w_oo[o]], tm=tm_p, tn=512)

            xc, dtv, dec, cnew = _ssd_prep_sample(sm, jnp.swapaxes(state_ssm_conv[o], 0, 1), ssm_conv_w[o], cb,
                                                  st, pa, pb)
            y_s, h_s = _ssd_state_sample(dtv, dec, xc, sm, d_e, nw, state_ssm[o])
            outs["sh"].append(h_s)
            outs["ssc"].append(jnp.swapaxes(cnew, 0, 1))
            xs = _outproj(xs, [y_s], [w_oo[o]], tm=nb, tn=512)
        xp, xs = ffn_pair(xp, xs, norm_ffn2[l], l, 1, l == depth - 1)

    st_ = {k: jnp.stack(v) for k, v in outs.items()}
    return (xp.reshape(bsz, t, d), xs.reshape(nb, 1, d),
            st_["pk"], st_["pv"], st_["pS"], st_["pgc"], st_["ph"], st_["psc"],
            st_["sk"], st_["sv"], st_["sS"], st_["sgc"], st_["sh"], st_["ssc"])
```

```python
import functools

import jax
import jax.numpy as jnp
from jax import lax
from jax.experimental import pallas as pl
from jax.experimental.pallas import tpu as pltpu

F32 = jnp.float32
BF16 = jnp.bfloat16

D_MODEL = 2048
SWA_HEADS = 16
SWA_KV_HEADS = 4
SWA_GROUP = SWA_HEADS // SWA_KV_HEADS
SWA_HEAD_DIM = 64
WINDOW = 128
GDN_HEADS = 8
GDN_DK = 128
GDN_DV = 128
CONV_W = 4
SSM_D_INNER = 2 * D_MODEL
SSM_HEAD_DIM = 64
SSM_HEADS = SSM_D_INNER // SSM_HEAD_DIM
SSM_GROUPS = 8
SSM_REP = SSM_HEADS // SSM_GROUPS
SSM_STATE = 128
D_FF = 5632
EPS = 1e-6
NEG_INF = -1e30

SWA_Q_W = SWA_HEADS * SWA_HEAD_DIM
SWA_KV_W = SWA_KV_HEADS * SWA_HEAD_DIM
GDN_QK_W = GDN_HEADS * GDN_DK
GDN_V_W = GDN_HEADS * GDN_DV
GDN_CONV_CH = 2 * GDN_QK_W + GDN_V_W
SSM_BC_W = SSM_GROUPS * SSM_STATE
SSM_CONV_CH = SSM_D_INNER + 2 * SSM_BC_W
SSM_GROUP_W = SSM_REP * SSM_HEAD_DIM

EV_Q = 0
EV_GQKV = SWA_Q_W
EV_Z = EV_GQKV + GDN_CONV_CH
EV_K = EV_Z + GDN_V_W
EV_V = EV_K + SWA_KV_W
EV_MAIN = EV_V + SWA_KV_W
OD_Z = 0
OD_XBC = SSM_D_INNER
OD_MAIN = SSM_D_INNER + SSM_CONV_CH

LANES = 128
SUBLANES = 8
CHUNK = 128
VMEM_LIMIT = 56 * 1024 * 1024


def _params(sem):
    return pltpu.CompilerParams(dimension_semantics=sem, vmem_limit_bytes=VMEM_LIMIT)


def _sigmoid(x):
    return 1.0 / (1.0 + jnp.exp(-x))


def _silu(x):
    return x * _sigmoid(x)


def _softplus(x):
    return jnp.maximum(x, 0.0) + jnp.log1p(jnp.exp(-jnp.abs(x)))


def _dot(a, b):
    return jnp.dot(a.astype(BF16), b.astype(BF16), preferred_element_type=F32)


def _dot_nt(a, b):
    return lax.dot_general(a.astype(BF16), b.astype(BF16), (((1,), (1,)), ((), ())),
                           preferred_element_type=F32)


def _bdot(a, b):
    return jnp.einsum("bij,bjk->bik", a.astype(BF16), b.astype(BF16), preferred_element_type=F32)


def _bdot_nt(a, b):
    return jnp.einsum("bid,bjd->bij", a.astype(BF16), b.astype(BF16), preferred_element_type=F32)


def _split2(a):
    hi = a.astype(BF16)
    lo = (a - hi.astype(F32)).astype(BF16)
    return hi, lo


def _split3(a):
    hi = a.astype(BF16)
    r = a - hi.astype(F32)
    mid = r.astype(BF16)
    lo = (r - mid.astype(F32)).astype(BF16)
    return hi, mid, lo


def _dot3(a, b):
    ah, al = _split2(a)
    bh, bl = _split2(b)
    out = jnp.dot(ah, bh, preferred_element_type=F32)
    out = out + jnp.dot(ah, bl, preferred_element_type=F32)
    out = out + jnp.dot(al, bh, preferred_element_type=F32)
    return out


def _dot_sel(m01, x):
    h, m, l = _split3(x)
    out = jnp.dot(m01, h, preferred_element_type=F32)
    out = out + jnp.dot(m01, m, preferred_element_type=F32)
    out = out + jnp.dot(m01, l, preferred_element_type=F32)
    return out


def _dot_sel_r(x, m01):
    h, m, l = _split3(x)
    out = jnp.dot(h, m01, preferred_element_type=F32)
    out = out + jnp.dot(m, m01, preferred_element_type=F32)
    out = out + jnp.dot(l, m01, preferred_element_type=F32)
    return out


def _rms_rows(x, gain):
    ms = jnp.mean(x * x, axis=-1, keepdims=True)
    return x * lax.rsqrt(ms + EPS) * gain


def _iota2(shape, dim):
    return lax.broadcasted_iota(jnp.int32, shape, dim)


def _ffn_kernel(x_ref, g_ref, gf_ref, wg_ref, wu_ref, wd_ref, o_ref, h_ref, *, final_norm):
    j = pl.program_id(1)

    @pl.when(j == 0)
    def _():
        h_ref[...] = _rms_rows(x_ref[...], g_ref[...]).astype(BF16)
        o_ref[...] = jnp.zeros_like(o_ref)

    h = h_ref[...]
    a = jnp.dot(h, wg_ref[...], preferred_element_type=F32)
    b = jnp.dot(h, wu_ref[...], preferred_element_type=F32)
    t = (_silu(a) * b).astype(BF16)
    o_ref[...] += jnp.dot(t, wd_ref[...], preferred_element_type=F32)

    @pl.when(j == pl.num_programs(1) - 1)
    def _():
        y = x_ref[...] + 0.5 * o_ref[...]
        if final_norm:
            y = _rms_rows(y, gf_ref[...])
        o_ref[...] = y


def _ffn(x, gain, gain_final, wg, wu, wd, layer, which, *, final_norm, tm, tf):
    m, d = x.shape
    f = wg.shape[-1]
    grid = (m // tm, f // tf)
    return pl.pallas_call(
        functools.partial(_ffn_kernel, final_norm=final_norm),
        grid=grid,
        in_specs=[
            pl.BlockSpec((tm, d), lambda i, j: (i, 0), pipeline_mode=pl.Buffered(1)),
            pl.BlockSpec((1, d), lambda i, j: (0, 0)),
            pl.BlockSpec((1, d), lambda i, j: (0, 0)),
            pl.BlockSpec((None, None, d, tf), lambda i, j: (layer, which, 0, j)),
            pl.BlockSpec((None, None, d, tf), lambda i, j: (layer, which, 0, j)),
            pl.BlockSpec((None, None, tf, d), lambda i, j: (layer, which, j, 0)),
        ],
        out_specs=pl.BlockSpec((tm, d), lambda i, j: (i, 0)),
        out_shape=jax.ShapeDtypeStruct((m, d), F32),
        scratch_shapes=[pltpu.VMEM((tm, d), BF16)],
        compiler_params=_params(("parallel", "arbitrary")),
        name="ffn_half",
    )(x, gain, gain_final, wg, wu, wd)


def _proj_kernel(x_ref, g_ref, w_ref, wt_ref, o_ref, ot_ref, h_ref):
    j = pl.program_id(1)

    @pl.when(j == 0)
    def _():
        h = _rms_rows(x_ref[...], g_ref[...]).astype(BF16)
        h_ref[...] = h
        ot_ref[...] = jnp.dot(h, wt_ref[...], preferred_element_type=F32)

    o_ref[...] = jnp.dot(h_ref[...], w_ref[...], preferred_element_type=F32)


def _proj(x, gain, w_main, w_tail, layer, *, tm, tn):
    m, d = x.shape
    n = w_main.shape[-1]
    grid = (m // tm, n // tn)
    return pl.pallas_call(
        _proj_kernel,
        grid=grid,
        in_specs=[
            pl.BlockSpec((tm, d), lambda i, j: (i, 0)),
            pl.BlockSpec((1, d), lambda i, j: (0, 0)),
            pl.BlockSpec((None, d, tn), lambda i, j: (layer, 0, j)),
            pl.BlockSpec((None, d, LANES), lambda i, j: (layer, 0, 0)),
        ],
        out_specs=[
            pl.BlockSpec((tm, tn), lambda i, j: (i, j)),
            pl.BlockSpec((tm, LANES), lambda i, j: (i, 0)),
        ],
        out_shape=[jax.ShapeDtypeStruct((m, n), F32), jax.ShapeDtypeStruct((m, LANES), F32)],
        scratch_shapes=[pltpu.VMEM((tm, d), BF16)],
        compiler_params=_params(("parallel", "arbitrary")),
        name="norm_proj",
    )(x, gain, w_main, w_tail)


def _outproj_kernel(*refs, n_in):
    x_ref = refs[0]
    o_refs = refs[1:1 + n_in]
    w_refs = refs[1 + n_in:1 + 2 * n_in]
    out_ref = refs[1 + 2 * n_in]
    acc = x_ref[...]
    for o_ref, w_ref in zip(o_refs, w_refs):
        acc = acc + jnp.dot(o_ref[...], w_ref[...], preferred_element_type=F32)
    out_ref[...] = acc


def _outproj(x, outs, w_all, layer, *, tm, tn):
    m, d = x.shape
    n_in = len(outs)
    grid = (m // tm, d // tn)
    in_specs = [pl.BlockSpec((tm, tn), lambda i, j: (i, j))]
    for o in outs:
        in_specs.append(pl.BlockSpec((tm, o.shape[1]), lambda i, j: (i, 0)))
    kw = outs[0].shape[1]
    assert all(o.shape[1] == kw for o in outs) and kw * n_in == w_all.shape[1]
    for kb in range(n_in):
        in_specs.append(pl.BlockSpec((None, kw, tn), lambda i, j, kb=kb: (layer, kb, j)))
    return pl.pallas_call(
        functools.partial(_outproj_kernel, n_in=n_in),
        grid=grid,
        in_specs=in_specs,
        out_specs=pl.BlockSpec((tm, tn), lambda i, j: (i, j)),
        out_shape=jax.ShapeDtypeStruct((m, d), F32),
        compiler_params=_params(("parallel", "arbitrary")),
        name="out_proj",
    )(x, *outs, *([w_all] * n_in))


def _swa_prompt_kernel(sink_ref, q_ref, kc_ref, kp_ref, vc_ref, vp_ref, o_ref):
    n = pl.program_id(1)
    w = WINDOW
    qi = _iota2((w, 2 * w), 0)
    kj = _iota2((w, 2 * w), 1)
    valid = (kj > qi) & (kj <= qi + w) & ((n > 0) | (kj >= w))
    scale = SWA_HEAD_DIM ** -0.5
    kk = jnp.concatenate([kp_ref[...], kc_ref[...]], axis=0).astype(BF16)
    vv = jnp.concatenate([vp_ref[...], vc_ref[...]], axis=0).astype(BF16)
    q = q_ref[...].astype(BF16)
    outs = []
    for h in range(SWA_HEADS):
        c = h // SWA_GROUP
        qh = q[:, h * SWA_HEAD_DIM:(h + 1) * SWA_HEAD_DIM]
        kh = kk[:, c * SWA_HEAD_DIM:(c + 1) * SWA_HEAD_DIM]
        vh = vv[:, c * SWA_HEAD_DIM:(c + 1) * SWA_HEAD_DIM]
        s = lax.dot_general(qh, kh, (((1,), (1,)), ((), ())), preferred_element_type=F32) * scale
        s = jnp.where(valid, s, NEG_INF)
        sink = sink_ref[h]
        mx = jnp.maximum(jnp.max(s, axis=-1, keepdims=True), sink)
        p = jnp.exp(s - mx)
        den = jnp.sum(p, axis=-1, keepdims=True) + jnp.exp(sink - mx)
        p = p / den
        outs.append(jnp.dot(p.astype(BF16), vh, preferred_element_type=F32))
    o_ref[...] = jnp.concatenate(outs, axis=-1).astype(o_ref.dtype)


def _swa_prompt(p_main, sinks, bsz, t):
    nb = t // WINDOW
    kblk = EV_K // SWA_KV_W
    vblk = EV_V // SWA_KV_W

    def cur(col):
        return lambda b, n: (b * nb + n, col)

    def prev(col):
        return lambda b, n: (b * nb + jnp.maximum(n - 1, 0), col)

    return pl.pallas_call(
        _swa_prompt_kernel,
        grid=(bsz, nb),
        in_specs=[
            pl.BlockSpec(memory_space=pltpu.SMEM),
            pl.BlockSpec((WINDOW, SWA_Q_W), cur(0)),
            pl.BlockSpec((WINDOW, SWA_KV_W), cur(kblk)),
            pl.BlockSpec((WINDOW, SWA_KV_W), prev(kblk)),
            pl.BlockSpec((WINDOW, SWA_KV_W), cur(vblk)),
            pl.BlockSpec((WINDOW, SWA_KV_W), prev(vblk)),
        ],
        out_specs=pl.BlockSpec((WINDOW, SWA_Q_W), cur(0)),
        out_shape=jax.ShapeDtypeStruct((bsz * t, SWA_Q_W), BF16),
        compiler_params=_params(("parallel", "arbitrary")),
        name="swa_prompt",
    )(sinks, p_main, p_main, p_main, p_main, p_main)


def _unit_lower_inverse(a, ri, ci):
    a0 = jnp.where((ri >> 4) == (ci >> 4), a, 0.0)
    y = -a0
    p = _bdot(a0, a0)
    for step in range(3):
        y = y + p + _bdot(y, p)
        if step < 2:
            p = _bdot(p, p)
    for sh in (4, 5, 6):
        e = jnp.where(((ri >> (sh + 1)) == (ci >> (sh + 1))) & ((ri >> sh) != (ci >> sh)), a, 0.0)
        t = e + _bdot(y, e)
        y = y - t - _bdot(t, y)
    return y


def _conv_section(ext_ref, carry_ref, sec, u_ref, w, bias, rows):
    ext_ref[0:SUBLANES, :] = carry_ref[sec]
    ext_ref[SUBLANES:SUBLANES + rows, :] = u_ref[...]
    acc = ext_ref[SUBLANES:SUBLANES + rows, :] * w[CONV_W - 1:CONV_W, :]
    for j in range(CONV_W - 1):
        off = SUBLANES - (CONV_W - 1) + j
        acc = acc + ext_ref[off:off + rows, :] * w[j:j + 1, :]
    if bias is not None:
        acc = acc + bias
    carry_ref[sec] = u_ref[rows - SUBLANES:rows, :]
    return _silu(acc)


def _gdn_prompt_kernel(uq_ref, uk_ref, uv_ref, z_ref, tail_ref, wq_ref, wk_ref, wv_ref,
                       pa_ref, pb_ref, nw_ref, tri_ref, ones_ref,
                       o_ref, sout_ref,
                       s_ref, carry_ref, ext_ref, cq_ref, ck_ref, cv_ref, grow_ref, *, hb_heads, rows):
    hb = pl.program_id(1)
    i = pl.program_id(2)
    nchunk = rows // CHUNK

    @pl.when(i == 0)
    def _():
        s_ref[...] = jnp.zeros_like(s_ref)
        carry_ref[...] = jnp.zeros_like(carry_ref)

    cq_ref[...] = _conv_section(ext_ref, carry_ref, 0, uq_ref, wq_ref[...], None, rows)
    ck_ref[...] = _conv_section(ext_ref, carry_ref, 1, uk_ref, wk_ref[...], None, rows)
    cv_ref[...] = _conv_section(ext_ref, carry_ref, 2, uv_ref, wv_ref[...], None, rows)

    t = tail_ref[...]
    g_all = -jnp.exp(pa_ref[...]) * _softplus(t + pb_ref[...])
    beta_all = _sigmoid(t)
    gcum_all = _dot_sel(tri_ref[...], g_all)
    gtot_all = _dot_sel(ones_ref[...], g_all)
    grow_ref[...] = jnp.transpose(gcum_all)[0:SUBLANES, :]

    lane = _iota2((rows, LANES), 1)
    ri = _iota2((CHUNK, CHUNK), 0)
    ci = _iota2((CHUNK, CHUNK), 1)
    incl = ri >= ci
    strict = ri > ci

    for h in range(hb_heads):
        hg = hb * hb_heads + h
        gcol = jnp.sum(jnp.where(lane == hg, gcum_all, 0.0), axis=1, keepdims=True)
        gtot = jnp.sum(jnp.where(lane == hg, gtot_all, 0.0), axis=1, keepdims=True)
        beta = jnp.sum(jnp.where(lane == hg + GDN_HEADS, beta_all, 0.0), axis=1, keepdims=True)
        grow = grow_ref[pl.ds(hg, 1), :]
        hs = slice(h * GDN_DK, (h + 1) * GDN_DK)
        for c in range(nchunk):
            rs = slice(c * CHUNK, (c + 1) * CHUNK)
            q = cq_ref[rs, hs]
            k = ck_ref[rs, hs]
            v = cv_ref[rs, hs]
            q = q * lax.rsqrt(jnp.sum(q * q, axis=-1, keepdims=True) + EPS) * (GDN_DK ** -0.5)
            k = k * lax.rsqrt(jnp.sum(k * k, axis=-1, keepdims=True) + EPS)
            gc = gcol[rs]
            gt = gtot[rs]
            bc = beta[rs]
            gr = grow[:, rs]
            decay = jnp.exp(jnp.where(incl, gc - gr, NEG_INF))
            kb = k * bc
            a = jnp.where(strict, _dot_nt(kb, k) * decay, 0.0)
            tinv = _unit_lower_inverse(a, ri, ci)
            eg = jnp.exp(gc)
            rhs = jnp.concatenate([v * bc, kb * eg], axis=1)
            sol = _dot3(tinv, rhs)
            u = sol[:, :GDN_DV]
            w = sol[:, GDN_DV:]
            s = s_ref[h]
            ws_qs = _dot(jnp.concatenate([w, q * eg], axis=0), s)
            v_new = u - ws_qs[:CHUNK]
            attn = _dot_nt(q, k) * decay
            o = ws_qs[CHUNK:] + _dot(attn, v_new)
            kd = k * jnp.exp(gt - gc)
            s_ref[h] = s * jnp.exp(gt) + _dot(jnp.transpose(kd), v_new)
            zc = z_ref[rs, hs]
            o = _rms_rows(o, nw_ref[...]) * _silu(zc)
            o_ref[rs, hs] = o.astype(o_ref.dtype)

    @pl.when(i == pl.num_programs(2) - 1)
    def _():
        sout_ref[0] = s_ref[...]


def _chunk_masks(rows):
    r = jnp.arange(rows)
    same = (r[:, None] // CHUNK) == (r[None, :] // CHUNK)
    tri = (same & (r[:, None] >= r[None, :])).astype(BF16)
    ones = same.astype(BF16)
    return tri, ones


def _gdn_prompt(p_main, p_tail, conv_w, pa, pb, norm_w, bsz, t, *, hb_heads, rows):
    nblk = t // rows
    wblk = hb_heads * GDN_DK
    nhb = GDN_HEADS // hb_heads
    tri, ones = _chunk_masks(rows)

    def col(base):
        return lambda b, hb, i: (b * nblk + i, base // wblk + hb)

    def wcol(base):
        return lambda b, hb, i: (0, base // wblk + hb)

    const = lambda b, hb, i: (0, 0)
    kern = functools.partial(_gdn_prompt_kernel, hb_heads=hb_heads, rows=rows)
    return pl.pallas_call(
        kern,
        grid=(bsz, nhb, nblk),
        in_specs=[
            pl.BlockSpec((rows, wblk), col(EV_GQKV)),
            pl.BlockSpec((rows, wblk), col(EV_GQKV + GDN_QK_W)),
            pl.BlockSpec((rows, wblk), col(EV_GQKV + 2 * GDN_QK_W)),
            pl.BlockSpec((rows, wblk), col(EV_Z)),
            pl.BlockSpec((rows, LANES), lambda b, hb, i: (b * nblk + i, 0)),
            pl.BlockSpec((CONV_W, wblk), wcol(0)),
            pl.BlockSpec((CONV_W, wblk), wcol(GDN_QK_W)),
            pl.BlockSpec((CONV_W, wblk), wcol(2 * GDN_QK_W)),
            pl.BlockSpec((1, LANES), const),
            pl.BlockSpec((1, LANES), const),
            pl.BlockSpec((1, GDN_DV), const),
            pl.BlockSpec((rows, rows), const),
            pl.BlockSpec((rows, rows), const),
        ],
        out_specs=[
            pl.BlockSpec((rows, wblk), lambda b, hb, i: (b * nblk + i, hb)),
            pl.BlockSpec((1, hb_heads, GDN_DK, GDN_DV), lambda b, hb, i: (b, hb, 0, 0)),
        ],
        out_shape=[
            jax.ShapeDtypeStruct((bsz * t, GDN_V_W), BF16),
            jax.ShapeDtypeStruct((bsz, GDN_HEADS, GDN_DK, GDN_DV), F32),
        ],
        scratch_shapes=[
            pltpu.VMEM((hb_heads, GDN_DK, GDN_DV), F32),
            pltpu.VMEM((3, SUBLANES, wblk), F32),
            pltpu.VMEM((rows + SUBLANES, wblk), F32),
            pltpu.VMEM((rows, wblk), F32),
            pltpu.VMEM((rows, wblk), F32),
            pltpu.VMEM((rows, wblk), F32),
            pltpu.VMEM((SUBLANES, rows), F32),
        ],
        compiler_params=_params(("parallel", "parallel", "arbitrary")),
        name="gdn_prompt",
    )(p_main, p_main, p_main, p_main, p_tail, conv_w, conv_w, conv_w, pa, pb, norm_w, tri, ones)


def _gdn_prompt_kernel(*refs, rows, has_prev):
    (uq_ref, uk_ref, uv_ref, z_ref, tail_ref, cw_ref, pa_ref, pb_ref, nw_ref, tri_ref, ones_ref) = refs[:11]
    rest = refs[12:] if has_prev else refs[11:]
    o_ref, sout_ref, s_ref, carry_ref, ext_ref, cq_ref, ck_ref, cv_ref = rest
    i = pl.program_id(1)
    nchunk = rows // CHUNK
    hh = GDN_HEADS

    @pl.when(i == 0)
    def _():
        s_ref[...] = jnp.zeros_like(s_ref)
        carry_ref[...] = jnp.zeros_like(carry_ref)

    for sec, (u_ref, c_ref) in enumerate(((uq_ref, cq_ref), (uk_ref, ck_ref), (uv_ref, cv_ref))):
        c_ref[...] = _conv_section(ext_ref, carry_ref, sec, u_ref,
                                   cw_ref[:, sec * GDN_QK_W:(sec + 1) * GDN_QK_W], None, rows)

    t = tail_ref[...]
    g_all = -jnp.exp(pa_ref[...]) * _softplus(t + pb_ref[...])
    beta_all = _sigmoid(t)
    gcum_all = _dot_sel(tri_ref[...], g_all)
    gtot_all = _dot_sel(ones_ref[...], g_all)
    gcum_t = jnp.transpose(gcum_all)

    ri = _iota2((1, CHUNK, CHUNK), 1)
    ci = _iota2((1, CHUNK, CHUNK), 2)
    incl = ri >= ci
    strict = ri > ci

    def rs(c):
        return slice(c * CHUNK, (c + 1) * CHUNK)

    def hs(h):
        return slice(h * GDN_DK, (h + 1) * GDN_DK)

    def stack(fn):
        return jnp.stack([fn(c, h) for c in range(nchunk) for h in range(hh)], axis=0)

    q = stack(lambda c, h: cq_ref[rs(c), hs(h)])
    k = stack(lambda c, h: ck_ref[rs(c), hs(h)])
    v = stack(lambda c, h: cv_ref[rs(c), hs(h)])
    gc = stack(lambda c, h: gcum_all[rs(c), h:h + 1])
    gt = stack(lambda c, h: gtot_all[rs(c), h:h + 1])
    bc = stack(lambda c, h: beta_all[rs(c), hh + h:hh + h + 1])
    gr = stack(lambda c, h: gcum_t[h:h + 1, rs(c)])

    q = q * lax.rsqrt(jnp.sum(q * q, axis=-1, keepdims=True) + EPS) * (GDN_DK ** -0.5)
    k = k * lax.rsqrt(jnp.sum(k * k, axis=-1, keepdims=True) + EPS)
    decay = jnp.exp(jnp.where(incl, gc - gr, NEG_INF))
    kb = k * bc
    a = jnp.where(strict, _bdot_nt(kb, k) * decay, 0.0)
    y = _unit_lower_inverse(a, ri, ci)
    eg = jnp.exp(gc)
    rhs = jnp.concatenate([v * bc, kb * eg], axis=2)
    sol = rhs + _bdot(y, rhs)
    u = sol[:, :, :GDN_DV]
    w = sol[:, :, GDN_DV:]
    wq = jnp.concatenate([w, q * eg], axis=1)
    attn = _bdot_nt(q, k) * decay
    kdt = jnp.swapaxes(k * jnp.exp(gt - gc), 1, 2)
    egt = jnp.exp(gt)

    for c in range(nchunk):
        bs = slice(c * hh, (c + 1) * hh)
        s = s_ref[...]
        ws_qs = _bdot(wq[bs], s)
        v_new = u[bs] - ws_qs[:, :CHUNK]
        o = ws_qs[:, CHUNK:] + _bdot(attn[bs], v_new)
        s_ref[...] = s * egt[bs] + _bdot(kdt[bs], v_new)
        for h in range(hh):
            og = _rms_rows(o[h], nw_ref[...]) * _silu(z_ref[rs(c), hs(h)])
            o_ref[rs(c), hs(h)] = og.astype(o_ref.dtype)

    @pl.when(i == pl.num_programs(1) - 1)
    def _():
        sout_ref[0] = s_ref[...]


def _gdn_prompt(p_main, p_tail, conv_w, pa, pb, norm_w, layer, n_layers, prev, bsz, t, *, rows):
    nblk = t // rows
    wblk = GDN_QK_W
    hh = GDN_HEADS
    tri, ones = _chunk_masks(rows)

    def col(base):
        return lambda b, i: (b * nblk + i, base // wblk)

    const = lambda b, i: (0, 0)
    in_specs = [
        pl.BlockSpec((rows, wblk), col(EV_GQKV)),
        pl.BlockSpec((rows, wblk), col(EV_GQKV + GDN_QK_W)),
        pl.BlockSpec((rows, wblk), col(EV_GQKV + 2 * GDN_QK_W)),
        pl.BlockSpec((rows, wblk), col(EV_Z)),
        pl.BlockSpec((rows, LANES), lambda b, i: (b * nblk + i, 0)),
        pl.BlockSpec((None, CONV_W, GDN_CONV_CH), lambda b, i: (layer, 0, 0)),
        pl.BlockSpec((1, LANES), const),
        pl.BlockSpec((1, LANES), const),
        pl.BlockSpec((1, GDN_DV), const),
        pl.BlockSpec((rows, rows), const),
        pl.BlockSpec((rows, rows), const),
    ]
    args = [p_main, p_main, p_main, p_main, p_tail, conv_w, pa, pb, norm_w, tri, ones]
    aliases = {}
    if prev is not None:
        in_specs.append(pl.BlockSpec(memory_space=pl.ANY))
        args.append(prev)
        aliases = {len(args) - 1: 1}
    return pl.pallas_call(
        functools.partial(_gdn_prompt_kernel, rows=rows, has_prev=prev is not None),
        grid=(bsz, nblk),
        in_specs=in_specs,
        out_specs=[
            pl.BlockSpec((rows, wblk), lambda b, i: (b * nblk + i, 0)),
            pl.BlockSpec((None, 1, hh, GDN_DK, GDN_DV), lambda b, i: (layer, b, 0, 0, 0)),
        ],
        out_shape=[
            jax.ShapeDtypeStruct((bsz * t, GDN_V_W), BF16),
            jax.ShapeDtypeStruct((n_layers, bsz, hh, GDN_DK, GDN_DV), F32),
        ],
        scratch_shapes=[
            pltpu.VMEM((hh, GDN_DK, GDN_DV), F32),
            pltpu.VMEM((3, SUBLANES, wblk), F32),
            pltpu.VMEM((rows + SUBLANES, wblk), F32),
            pltpu.VMEM((rows, wblk), F32),
            pltpu.VMEM((rows, wblk), F32),
            pltpu.VMEM((rows, wblk), F32),
        ],
        input_output_aliases=aliases,
        compiler_params=_params(("parallel", "arbitrary")),
        name="gdn_prompt",
    )(*args)


SSD_SEC_W = SSM_CONV_CH // 3


def _ssd_prompt_kernel(z_ref, xa_ref, xb_ref, xc3_ref, tail_ref, cw_ref, cb_ref, pa_ref, pb_ref, de_ref, nw_ref,
                       ex_ref, tri_ref,
                       y_ref, hout_ref,
                       ht_ref, carry_ref, ext_ref, xc_ref, cumt_ref, dtt_ref):
    i = pl.program_id(1)
    rows = CHUNK

    @pl.when(i == 0)
    def _():
        ht_ref[...] = jnp.zeros_like(ht_ref)
        carry_ref[...] = jnp.zeros_like(carry_ref)

    for sec, u_ref in enumerate((xa_ref, xb_ref, xc3_ref)):
        ss = slice(sec * SSD_SEC_W, (sec + 1) * SSD_SEC_W)
        xc_ref[:, ss] = _conv_section(ext_ref, carry_ref, sec, u_ref, cw_ref[:, ss], cb_ref[:, ss], rows)

    dt = _softplus(tail_ref[...] + pb_ref[...])
    da = dt * (-jnp.exp(pa_ref[...]))
    cum = _dot_sel(tri_ref[...], da)
    cumt_ref[...] = jnp.transpose(cum)
    dtt_ref[...] = jnp.transpose(dt)

    ri = _iota2((rows, rows), 0)
    ci = _iota2((rows, rows), 1)
    tri = ri >= ci
    lane = _iota2((rows, LANES), 1)
    lo_half = lane < SSM_HEAD_DIM

    for g in range(SSM_GROUPS):
        gs = slice(g * SSM_GROUP_W, (g + 1) * SSM_GROUP_W)
        bmat = xc_ref[:, SSM_D_INNER + g * SSM_STATE:SSM_D_INNER + (g + 1) * SSM_STATE]
        cmat = xc_ref[:, SSM_D_INNER + SSM_BC_W + g * SSM_STATE:SSM_D_INNER + SSM_BC_W + (g + 1) * SSM_STATE]
        xg = xc_ref[:, gs]
        cb = _dot_nt(cmat, bmat)
        exg = ex_ref[:, gs]
        cum_e = _dot_sel_r(cum, exg)
        dt_e = _dot_sel_r(dt, exg)
        htg = ht_ref[g]
        y = _dot(cmat, htg) * jnp.exp(cum_e)
        pieces = []
        for pr in range(SSM_REP // 2):
            scs = []
            for half in range(2):
                h = g * SSM_REP + 2 * pr + half
                ccol = cum[:, h:h + 1]
                crow = cumt_ref[h:h + 1, :]
                drow = dtt_ref[h:h + 1, :]
                lmat = jnp.exp(jnp.where(tri, ccol - crow, NEG_INF))
                scs.append((cb * lmat * drow).astype(BF16))
            xp = xg[:, pr * LANES:(pr + 1) * LANES]
            rhs = jnp.concatenate([jnp.where(lo_half, xp, 0.0), jnp.where(lo_half, 0.0, xp)], axis=0)
            pieces.append(jnp.dot(jnp.concatenate(scs, axis=1), rhs.astype(BF16),
                                  preferred_element_type=F32))
        y = y + jnp.concatenate(pieces, axis=1) + de_ref[:, gs] * xg
        last_e = cum_e[rows - 1:rows, :]
        wts = jnp.exp(last_e - cum_e) * dt_e
        ht_ref[g] = htg * jnp.exp(last_e) + _dot(jnp.transpose(bmat), xg * wts)
        zg = z_ref[:, gs]
        yz = y * _silu(zg)
        y_ref[:, gs] = _rms_rows(yz, nw_ref[:, gs]).astype(y_ref.dtype)

    @pl.when(i == pl.num_programs(1) - 1)
    def _():
        for g in range(SSM_GROUPS):
            hg = jnp.transpose(ht_ref[g])
            hout_ref[0, g * SSM_REP:(g + 1) * SSM_REP] = hg.reshape(SSM_REP, SSM_HEAD_DIM, SSM_STATE)


def _head_expand():
    h = jnp.arange(LANES)[:, None]
    c = jnp.arange(SSM_D_INNER)[None, :]
    return ((c // SSM_HEAD_DIM) == h).astype(BF16)


def _drop_prev(kern, n_in):
    def entry(*refs):
        return kern(*refs[:n_in], *refs[n_in + 1:])
    return entry


def _ssd_prompt(p_main, p_tail, conv_w, conv_b, pa, pb, d_e, norm_w, ex, layer, n_layers, prev, bsz, t):
    rows = CHUNK
    nblk = t // rows
    tri, _ = _chunk_masks(rows)
    const = lambda b, i: (0, 0)
    row = lambda b, i: (b * nblk + i, 0)
    extra_specs, extra_args, aliases, kern = [], [], {}, _ssd_prompt_kernel
    if prev is not None:
        extra_specs, extra_args, aliases = [pl.BlockSpec(memory_space=pl.ANY)], [prev], {13: 1}
        kern = _drop_prev(_ssd_prompt_kernel, 13)
    return pl.pallas_call(
        kern,
        grid=(bsz, nblk),
        input_output_aliases=aliases,
        in_specs=extra_specs[:0] + [
            pl.BlockSpec((rows, SSM_D_INNER), row),
            pl.BlockSpec((rows, SSD_SEC_W), lambda b, i: (b * nblk + i, OD_XBC // SSD_SEC_W)),
            pl.BlockSpec((rows, SSD_SEC_W), lambda b, i: (b * nblk + i, OD_XBC // SSD_SEC_W + 1)),
            pl.BlockSpec((rows, SSD_SEC_W), lambda b, i: (b * nblk + i, OD_XBC // SSD_SEC_W + 2)),
            pl.BlockSpec((rows, LANES), row),
            pl.BlockSpec((CONV_W, SSM_CONV_CH), const),
            pl.BlockSpec((1, SSM_CONV_CH), const),
            pl.BlockSpec((1, LANES), const),
            pl.BlockSpec((1, LANES), const),
            pl.BlockSpec((1, SSM_D_INNER), const),
            pl.BlockSpec((1, SSM_D_INNER), const),
            pl.BlockSpec((LANES, SSM_D_INNER), const),
            pl.BlockSpec((rows, rows), const),
        ] + extra_specs,
        out_specs=[
            pl.BlockSpec((rows, SSM_D_INNER), row),
            pl.BlockSpec((None, 1, SSM_HEADS, SSM_HEAD_DIM, SSM_STATE), lambda b, i: (layer, b, 0, 0, 0)),
        ],
        out_shape=[
            jax.ShapeDtypeStruct((bsz * t, SSM_D_INNER), BF16),
            jax.ShapeDtypeStruct((n_layers, bsz, SSM_HEADS, SSM_HEAD_DIM, SSM_STATE), F32),
        ],
        scratch_shapes=[
            pltpu.VMEM((SSM_GROUPS, SSM_STATE, SSM_GROUP_W), F32),
            pltpu.VMEM((3, SUBLANES, SSD_SEC_W), F32),
            pltpu.VMEM((rows + SUBLANES, SSD_SEC_W), F32),
            pltpu.VMEM((rows, SSM_CONV_CH), F32),
            pltpu.VMEM((LANES, rows), F32),
            pltpu.VMEM((LANES, rows), F32),
        ],
        compiler_params=_params(("parallel", "arbitrary")),
        name="ssd_prompt",
    )(p_main, p_main, p_main, p_main, p_tail, conv_w, conv_b, pa, pb, d_e, norm_w, ex, tri, *extra_args)


def _swa_sample_kernel(sink_ref, q_ref, kn_ref, vn_ref, ck_ref, cv_ref, o_ref):
    scale = SWA_HEAD_DIM ** -0.5
    q = q_ref[0].astype(BF16)
    kn = kn_ref[0].astype(BF16).astype(F32)
    vn = vn_ref[0].astype(BF16).astype(F32)
    ck = ck_ref[0].astype(BF16)
    cv = cv_ref[0].astype(BF16)
    sink = sink_ref[...]
    valid = _iota2((SWA_GROUP, WINDOW), 1) >= 1
    outs = []
    for c in range(SWA_KV_HEADS):
        cs = slice(c * SWA_HEAD_DIM, (c + 1) * SWA_HEAD_DIM)
        gs = slice(c * SWA_GROUP, (c + 1) * SWA_GROUP)
        qc = q[gs, :]
        s = lax.dot_general(qc, ck[:, cs], (((1,), (1,)), ((), ())), preferred_element_type=F32) * scale
        s = jnp.where(valid, s, NEG_INF)
        sn = jnp.sum(qc.astype(F32) * kn[:, cs], axis=-1, keepdims=True) * scale
        sk = sink[gs, :]
        mx = jnp.maximum(jnp.maximum(jnp.max(s, axis=-1, keepdims=True), sn), sk)
        p = jnp.exp(s - mx)
        pn = jnp.exp(sn - mx)
        den = jnp.sum(p, axis=-1, keepdims=True) + pn + jnp.exp(sk - mx)
        o = jnp.dot(p.astype(BF16), cv[:, cs], preferred_element_type=F32) + pn * vn[:, cs]
        outs.append(o / den)
    o_ref[0] = jnp.concatenate(outs, axis=0).astype(o_ref.dtype)


def _swa_sample(q, kn, vn, ck, cv, sinks):
    nb = q.shape[0]
    kvw = SWA_KV_W
    b3 = lambda b: (b, 0, 0)
    return pl.pallas_call(
        _swa_sample_kernel,
        grid=(nb,),
        in_specs=[
            pl.BlockSpec((SWA_HEADS, 1), lambda b: (0, 0)),
            pl.BlockSpec((1, SWA_HEADS, SWA_HEAD_DIM), b3),
            pl.BlockSpec((1, 1, kvw), b3),
            pl.BlockSpec((1, 1, kvw), b3),
            pl.BlockSpec((1, WINDOW, kvw), b3),
            pl.BlockSpec((1, WINDOW, kvw), b3),
        ],
        out_specs=pl.BlockSpec((1, SWA_HEADS, SWA_HEAD_DIM), b3),
        out_shape=jax.ShapeDtypeStruct((nb, SWA_HEADS, SWA_HEAD_DIM), F32),
        compiler_params=_params(("parallel",)),
        name="swa_sample",
    )(sinks, q, kn, vn, ck, cv)


def _gdn_sample_kernel(u_ref, prev_ref, w_ref, s_ref, z_ref, ab_ref, pa_ref, pb_ref, nw_ref,
                       o_ref, sout_ref, cnew_ref):
    hh = GDN_HEADS
    u = u_ref[0]
    pv = prev_ref[0]
    w = w_ref[...]
    x = u * w[CONV_W - 1]
    for j in range(CONV_W - 1):
        x = x + pv[j] * w[j]
        cnew_ref[0, j] = pv[j + 1] if j + 1 < CONV_W - 1 else u
    x = _silu(x)
    q = x[0:hh]
    k = x[hh:2 * hh]
    v = x[2 * hh:3 * hh]
    q = q * lax.rsqrt(jnp.sum(q * q, axis=-1, keepdims=True) + EPS) * (GDN_DK ** -0.5)
    k = k * lax.rsqrt(jnp.sum(k * k, axis=-1, keepdims=True) + EPS)
    ab = ab_ref[0]
    dec = jnp.exp(-jnp.exp(pa_ref[...]) * _softplus(ab[0:hh] + pb_ref[...]))
    beta = _sigmoid(ab[hh:2 * hh])
    qk = jnp.concatenate([q, k, jnp.zeros((LANES - 2 * hh, GDN_DK), F32)], axis=0)
    qkt = jnp.transpose(qk)
    outs = []
    for h in range(hh):
        s = s_ref[0, h] * dec[h:h + 1, :]
        kcol = qkt[:, hh + h:hh + h + 1]
        ks = jnp.sum(s * kcol, axis=0, keepdims=True)
        delta = beta[h:h + 1, :] * (v[h:h + 1, :] - ks)
        s = s + kcol * delta
        sout_ref[0, h] = s
        outs.append(jnp.sum(s * qkt[:, h:h + 1], axis=0, keepdims=True))
    o = jnp.concatenate(outs, axis=0)
    o_ref[0] = _rms_rows(o, nw_ref[...]) * _silu(z_ref[0])


def _gdn_sample(u, prev, w, s_all, z, ab, pa, pb, nw, layer, prev_out):
    nb = u.shape[0]
    hh = GDN_HEADS
    n_layers = s_all.shape[0]
    b3 = lambda b: (b, 0, 0)
    b4 = lambda b: (b, 0, 0, 0)
    c2 = lambda b: (0, 0)
    state_spec = pl.BlockSpec((None, 1, hh, GDN_DK, GDN_DV), lambda b: (layer, b, 0, 0, 0))
    extra_specs, extra_args, aliases, kern = [], [], {}, _gdn_sample_kernel
    if prev_out is not None:
        extra_specs, extra_args, aliases = [pl.BlockSpec(memory_space=pl.ANY)], [prev_out], {9: 1}
        kern = _drop_prev(_gdn_sample_kernel, 9)
    return pl.pallas_call(
        kern,
        grid=(nb,),
        input_output_aliases=aliases,
        in_specs=[
            pl.BlockSpec((1, 3 * hh, GDN_DK), b3),
            pl.BlockSpec((1, CONV_W - 1, 3 * hh, GDN_DK), b4),
            pl.BlockSpec((CONV_W, 3 * hh, GDN_DK), lambda b: (0, 0, 0)),
            state_spec,
            pl.BlockSpec((1, hh, GDN_DV), b3),
            pl.BlockSpec((1, 2 * hh, LANES), b3),
            pl.BlockSpec((hh, LANES), c2),
            pl.BlockSpec((hh, LANES), c2),
            pl.BlockSpec((1, GDN_DV), c2),
        ] + extra_specs,
        out_specs=[
            pl.BlockSpec((1, hh, GDN_DV), b3),
            state_spec,
            pl.BlockSpec((1, CONV_W - 1, 3 * hh, GDN_DK), b4),
        ],
        out_shape=[
            jax.ShapeDtypeStruct((nb, hh, GDN_DV), F32),
            jax.ShapeDtypeStruct((n_layers, nb, hh, GDN_DK, GDN_DV), F32),
            jax.ShapeDtypeStruct((nb, CONV_W - 1, 3 * hh, GDN_DK), F32),
        ],
        compiler_params=_params(("parallel",)),
        name="gdn_sample",
    )(u, prev, w, s_all, z, ab, pa, pb, nw, *extra_args)


def _ssd_prep_kernel(x_ref, prev_ref, cw_ref, cb_ref, tail_ref, pa_ref, pb_ref,
                     xc_ref, dt_ref, dec_ref, cnew_ref):
    u = x_ref[...]
    w = cw_ref[...]
    acc = u * w[CONV_W - 1:CONV_W, :] + cb_ref[...]
    for j in range(CONV_W - 1):
        acc = acc + prev_ref[j] * w[j:j + 1, :]
        cnew_ref[j] = prev_ref[j + 1] if j + 1 < CONV_W - 1 else u
    xc_ref[...] = _silu(acc)
    dt = _softplus(tail_ref[...] + pb_ref[...])
    dt_ref[...] = dt
    dec_ref[...] = jnp.exp(dt * (-jnp.exp(pa_ref[...])))


def _ssd_prep_sample(p_main, prev_t, cw, cb, tail, pa, pb):
    nb = p_main.shape[0]
    sec = SSD_SEC_W
    nsec = SSM_CONV_CH // sec
    c2 = lambda s: (0, 0)
    return pl.pallas_call(
        _ssd_prep_kernel,
        grid=(nsec,),
        in_specs=[
            pl.BlockSpec((nb, sec), lambda s: (0, OD_XBC // sec + s)),
            pl.BlockSpec((CONV_W - 1, nb, sec), lambda s: (0, 0, s)),
            pl.BlockSpec((CONV_W, sec), lambda s: (0, s)),
            pl.BlockSpec((1, sec), lambda s: (0, s)),
            pl.BlockSpec((nb, LANES), c2),
            pl.BlockSpec((1, LANES), c2),
            pl.BlockSpec((1, LANES), c2),
        ],
        out_specs=[
            pl.BlockSpec((nb, sec), lambda s: (0, s)),
            pl.BlockSpec((nb, LANES), c2),
            pl.BlockSpec((nb, LANES), c2),
            pl.BlockSpec((CONV_W - 1, nb, sec), lambda s: (0, 0, s)),
        ],
        out_shape=[
            jax.ShapeDtypeStruct((nb, SSM_CONV_CH), F32),
            jax.ShapeDtypeStruct((nb, LANES), F32),
            jax.ShapeDtypeStruct((nb, LANES), F32),
            jax.ShapeDtypeStruct((CONV_W - 1, nb, SSM_CONV_CH), F32),
        ],
        compiler_params=_params(("arbitrary",)),
        name="ssd_prep_sample",
    )(p_main, prev_t, cw, cb, tail, pa, pb)


def _ssd_state_kernel(dt_ref, dec_ref, x_ref, b_ref, c_ref, z_ref, de_ref, nw_ref, h_ref,
                      y_ref, hout_ref, *, nb):
    g = pl.program_id(0)
    x = x_ref[...]
    xt = jnp.transpose(jnp.concatenate([x, jnp.zeros((LANES - nb, SSM_GROUP_W), F32)], axis=0))
    bm = b_ref[...]
    cm = c_ref[...]
    lane = _iota2((SSM_HEAD_DIM, LANES), 1)
    yts = [jnp.zeros((SSM_HEAD_DIM, LANES), F32) for _ in range(SSM_REP)]
    for b in range(nb):
        brow = bm[b:b + 1, :]
        crow = cm[b:b + 1, :]
        for r in range(SSM_REP):
            hd = g * SSM_REP + r
            xcol = xt[r * SSM_HEAD_DIM:(r + 1) * SSM_HEAD_DIM, b:b + 1]
            hn = h_ref[b, r] * dec_ref[b, hd] + (xcol * dt_ref[b, hd]) * brow
            hout_ref[b, r] = hn
            ycol = jnp.sum(hn * crow, axis=1, keepdims=True)
            yts[r] = jnp.where(lane == b, ycol, yts[r])
    y = jnp.transpose(jnp.concatenate(yts, axis=0))[0:nb, :]
    y = y + de_ref[...] * x
    zg = z_ref[...]
    y_ref[...] = _rms_rows(y * _silu(zg), nw_ref[...]).astype(y_ref.dtype)


def _ssd_state_sample(dt, dec, xc, p_main, d_e, nw, h):
    nb = xc.shape[0]
    assert nb <= LANES
    gw = SSM_GROUP_W
    return pl.pallas_call(
        functools.partial(_ssd_state_kernel, nb=nb),
        grid=(SSM_GROUPS,),
        in_specs=[
            pl.BlockSpec(memory_space=pltpu.SMEM),
            pl.BlockSpec(memory_space=pltpu.SMEM),
            pl.BlockSpec((nb, gw), lambda g: (0, g)),
            pl.BlockSpec((nb, SSM_STATE), lambda g: (0, SSM_D_INNER // SSM_STATE + g)),
            pl.BlockSpec((nb, SSM_STATE), lambda g: (0, (SSM_D_INNER + SSM_BC_W) // SSM_STATE + g)),
            pl.BlockSpec((nb, gw), lambda g: (0, g)),
            pl.BlockSpec((1, gw), lambda g: (0, g)),
            pl.BlockSpec((1, gw), lambda g: (0, g)),
            pl.BlockSpec((nb, SSM_REP, SSM_HEAD_DIM, SSM_STATE), lambda g: (0, g, 0, 0)),
        ],
        out_specs=[
            pl.BlockSpec((nb, gw), lambda g: (0, g)),
            pl.BlockSpec((nb, SSM_REP, SSM_HEAD_DIM, SSM_STATE), lambda g: (0, g, 0, 0)),
        ],
        out_shape=[
            jax.ShapeDtypeStruct((nb, SSM_D_INNER), BF16),
            jax.ShapeDtypeStruct((nb, SSM_HEADS, SSM_HEAD_DIM, SSM_STATE), F32),
        ],
        compiler_params=_params(("arbitrary",)),
        name="ssd_state_sample",
    )(dt, dec, xc, xc, xc, p_main, d_e, nw, h)


def _ssd_state_kernel(dec_ref, x_ref, dt_ref, b_ref, c_ref, z_ref, de_ref, nw_ref, ex_ref, h_ref,
                      y_ref, hout_ref, *, nb):
    g = pl.program_id(0)
    x = x_ref[...]
    dtx = x * _dot_sel_r(dt_ref[...], ex_ref[...])
    zrows = LANES - nb
    xt = jnp.transpose(jnp.concatenate([dtx, jnp.zeros((zrows, SSM_GROUP_W), F32)], axis=0))
    xh, xl = _split2(xt)
    bpad = jnp.concatenate([b_ref[...], jnp.zeros((zrows, SSM_STATE), F32)], axis=0)
    cpad_t = jnp.transpose(jnp.concatenate([c_ref[...], jnp.zeros((zrows, SSM_STATE), F32)], axis=0))
    rowi = _iota2((LANES, SSM_STATE), 0)
    coli = _iota2((SSM_STATE, LANES), 1)
    yt = jnp.zeros((SSM_GROUP_W, LANES), F32)
    for b in range(nb):
        bh, bl = _split2(jnp.where(rowi == b, bpad, 0.0))
        upd = (jnp.dot(xh, bh, preferred_element_type=F32) + jnp.dot(xh, bl, preferred_element_type=F32)
               + jnp.dot(xl, bh, preferred_element_type=F32))
        parts = []
        for r in range(SSM_REP):
            hn = h_ref[b, r] * dec_ref[b, g * SSM_REP + r] + upd[r * SSM_HEAD_DIM:(r + 1) * SSM_HEAD_DIM]
            hout_ref[b, r] = hn
            parts.append(hn)
        hh_, hl_ = _split2(jnp.concatenate(parts, axis=0))
        ch, cl = _split2(jnp.where(coli == b, cpad_t, 0.0))
        yt = yt + (jnp.dot(hh_, ch, preferred_element_type=F32) + jnp.dot(hh_, cl, preferred_element_type=F32)
                   + jnp.dot(hl_, ch, preferred_element_type=F32))
    y = jnp.transpose(yt)[0:nb, :] + de_ref[...] * x
    y_ref[...] = _rms_rows(y * _silu(z_ref[...]), nw_ref[...]).astype(y_ref.dtype)


def _ssd_state_sample(dtv, dec, xc, p_main, d_e, nw, ex, h_all, layer, prev_out):
    nb = xc.shape[0]
    assert nb <= LANES
    gw = SSM_GROUP_W
    n_layers = h_all.shape[0]
    state_spec = pl.BlockSpec((None, nb, SSM_REP, SSM_HEAD_DIM, SSM_STATE), lambda g: (layer, 0, g, 0, 0))
    kern = functools.partial(_ssd_state_kernel, nb=nb)
    extra_specs, extra_args, aliases = [], [], {}
    if prev_out is not None:
        extra_specs, extra_args, aliases = [pl.BlockSpec(memory_space=pl.ANY)], [prev_out], {10: 1}
        kern = _drop_prev(kern, 10)
    return pl.pallas_call(
        kern,
        grid=(SSM_GROUPS,),
        input_output_aliases=aliases,
        in_specs=[
            pl.BlockSpec(memory_space=pltpu.SMEM),
            pl.BlockSpec((nb, gw), lambda g: (0, g)),
            pl.BlockSpec((nb, LANES), lambda g: (0, 0)),
            pl.BlockSpec((nb, SSM_STATE), lambda g: (0, SSM_D_INNER // SSM_STATE + g)),
            pl.BlockSpec((nb, SSM_STATE), lambda g: (0, (SSM_D_INNER + SSM_BC_W) // SSM_STATE + g)),
            pl.BlockSpec((nb, gw), lambda g: (0, g)),
            pl.BlockSpec((1, gw), lambda g: (0, g)),
            pl.BlockSpec((1, gw), lambda g: (0, g)),
            pl.BlockSpec((LANES, gw), lambda g: (0, g)),
            state_spec,
        ] + extra_specs,
        out_specs=[
            pl.BlockSpec((nb, gw), lambda g: (0, g)),
            state_spec,
        ],
        out_shape=[
            jax.ShapeDtypeStruct((nb, SSM_D_INNER), BF16),
            jax.ShapeDtypeStruct((n_layers, nb, SSM_HEADS, SSM_HEAD_DIM, SSM_STATE), F32),
        ],
        compiler_params=_params(("arbitrary",)),
        name="ssd_state_sample",
    )(dec, xc, dtv, xc, xc, p_main, d_e, nw, ex, h_all, *extra_args)


def _tile(m, pref):
    return pref if m % pref == 0 else m


def _lane_pad_row(v):
    return jnp.zeros((1, LANES), F32).at[0, :v.shape[0]].set(v.astype(F32))


def _kernel_old(x_prompt, x_sample, cache_swa_k, cache_swa_v, state_gdn, state_gdn_conv, state_ssm, state_ssm_conv,
           norm_ffn1, norm_mix, norm_ffn2, norm_final, w_ffn_gate, w_ffn_up, w_ffn_down,
           w_in_even, w_out_even, attn_sinks, gdn_conv_w, gdn_A_log, gdn_dt_bias, gdn_norm_w,
           w_in_odd, w_out_odd, ssm_conv_w, ssm_conv_b, ssm_A_log, ssm_dt_bias, ssm_D, ssm_norm_w):
    bsz, t, d = x_prompt.shape
    nb = x_sample.shape[0]
    depth = norm_ffn1.shape[0]
    mp = bsz * t
    assert x_sample.shape[1] == 1 and d == D_MODEL and t % (2 * CHUNK) == 0
    hh = GDN_HEADS

    wg = w_ffn_gate.astype(BF16)
    wu = w_ffn_up.astype(BF16)
    wd = w_ffn_down.astype(BF16)
    o_q, o_k, o_v, o_g, o_z, o_a = 0, SWA_Q_W, SWA_Q_W + SWA_KV_W, SWA_Q_W + 2 * SWA_KV_W, \
        SWA_Q_W + 2 * SWA_KV_W + GDN_CONV_CH, SWA_Q_W + 2 * SWA_KV_W + GDN_CONV_CH + GDN_V_W
    w_even_main = jnp.concatenate(
        [w_in_even[:, :, o_q:o_k], w_in_even[:, :, o_g:o_z], w_in_even[:, :, o_z:o_a],
         w_in_even[:, :, o_k:o_v], w_in_even[:, :, o_v:o_g]], axis=-1).astype(BF16)
    w_even_tail = jnp.pad(w_in_even[:, :, o_a:], ((0, 0), (0, 0), (0, LANES - 2 * hh))).astype(BF16)
    w_odd_main = w_in_odd[:, :, :OD_MAIN].astype(BF16)
    w_odd_tail = jnp.pad(w_in_odd[:, :, OD_MAIN:], ((0, 0), (0, 0), (0, LANES - SSM_HEADS))).astype(BF16)
    w_oe = w_out_even.astype(BF16)
    w_oo = w_out_odd.astype(BF16)

    xp = x_prompt.reshape(mp, d)
    xs = x_sample.reshape(nb, d)
    tm_p = _tile(mp, 512)
    gain_f = norm_final.reshape(1, d)

    outs = {k: [] for k in ("pk", "pv", "pS", "pgc", "ph", "psc", "sk", "sv", "sS", "sgc", "sh", "ssc")}

    def ffn_pair(xp, xs, gain, l, which, final):
        gain = gain.reshape(1, d)
        xp = _ffn(xp, gain, gain_f, wg[l, which], wu[l, which], wd[l, which], final_norm=final, tm=tm_p, tf=512)
        xs = _ffn(xs, gain, gain_f, wg[l, which], wu[l, which], wd[l, which], final_norm=final, tm=nb, tf=512)
        return xp, xs

    for l in range(depth):
        xp, xs = ffn_pair(xp, xs, norm_ffn1[l], l, 0, False)
        gmix = norm_mix[l].reshape(1, d)
        if l % 2 == 0:
            e = l // 2
            pm, pt = _proj(xp, gmix, w_even_main[e], w_even_tail[e], tm=tm_p, tn=512)
            sm, st = _proj(xs, gmix, w_even_main[e], w_even_tail[e], tm=nb, tn=512)
            pa = _lane_pad_row(gdn_A_log[e])
            pb = _lane_pad_row(gdn_dt_bias[e])
            nw = gdn_norm_w[e].reshape(1, GDN_DV).astype(F32)

            o_a_p = _swa_prompt(pm, attn_sinks[e].astype(F32), bsz, t)
            o_b_p, s_p = _gdn_prompt(pm, pt, gdn_conv_w[e], pa, pb, nw, bsz, t, hb_heads=2, rows=2 * CHUNK)
            pm3 = pm.reshape(bsz, t, EV_MAIN)
            outs["pk"].append(pm3[:, t - WINDOW:, EV_K:EV_K + SWA_KV_W].reshape(bsz, WINDOW, SWA_KV_HEADS, SWA_HEAD_DIM))
            outs["pv"].append(pm3[:, t - WINDOW:, EV_V:EV_V + SWA_KV_W].reshape(bsz, WINDOW, SWA_KV_HEADS, SWA_HEAD_DIM))
            outs["pS"].append(s_p)
            outs["pgc"].append(pm3[:, t - (CONV_W - 1):, EV_GQKV:EV_GQKV + GDN_CONV_CH])
            xp = _outproj(xp, [o_a_p, o_b_p], [w_oe[e, :SWA_Q_W], w_oe[e, SWA_Q_W:]], tm=tm_p, tn=512)

            kn = sm[:, EV_K:EV_K + SWA_KV_W]
            vn = sm[:, EV_V:EV_V + SWA_KV_W]
            o_a_s = _swa_sample(sm[:, EV_Q:EV_Q + SWA_Q_W].reshape(nb, SWA_HEADS, SWA_HEAD_DIM),
                                kn.reshape(nb, 1, SWA_KV_W), vn.reshape(nb, 1, SWA_KV_W),
                                cache_swa_k[e].reshape(nb, WINDOW, SWA_KV_W), cache_swa_v[e].reshape(nb, WINDOW, SWA_KV_W),
                                attn_sinks[e].astype(F32).reshape(SWA_HEADS, 1))
            ab = jnp.broadcast_to(st[:, :2 * hh, None], (nb, 2 * hh, LANES))
            o_b_s, s_s, gc_s = _gdn_sample(
                sm[:, EV_GQKV:EV_GQKV + GDN_CONV_CH].reshape(nb, 3 * hh, GDN_DK),
                state_gdn_conv[e].reshape(nb, CONV_W - 1, 3 * hh, GDN_DK),
                gdn_conv_w[e].reshape(CONV_W, 3 * hh, GDN_DK),
                state_gdn[e], sm[:, EV_Z:EV_Z + GDN_V_W].reshape(nb, hh, GDN_DV), ab,
                jnp.broadcast_to(gdn_A_log[e].astype(F32)[:, None], (hh, LANES)),
                jnp.broadcast_to(gdn_dt_bias[e].astype(F32)[:, None], (hh, LANES)), nw)
            outs["sk"].append(kn.reshape(nb, 1, SWA_KV_HEADS, SWA_HEAD_DIM))
            outs["sv"].append(vn.reshape(nb, 1, SWA_KV_HEADS, SWA_HEAD_DIM))
            outs["sS"].append(s_s)
            outs["sgc"].append(gc_s.reshape(nb, CONV_W - 1, GDN_CONV_CH))
            xs = _outproj(xs, [o_a_s.reshape(nb, SWA_Q_W).astype(BF16), o_b_s.reshape(nb, GDN_V_W).astype(BF16)],
                          [w_oe[e, :SWA_Q_W], w_oe[e, SWA_Q_W:]], tm=nb, tn=512)
        else:
            o = l // 2
            pm, pt = _proj(xp, gmix, w_odd_main[o], w_odd_tail[o], tm=tm_p, tn=1024)
            sm, st = _proj(xs, gmix, w_odd_main[o], w_odd_tail[o], tm=nb, tn=1024)
            pa = _lane_pad_row(ssm_A_log[o])
            pb = _lane_pad_row(ssm_dt_bias[o])
            d_e = jnp.repeat(ssm_D[o].astype(F32), SSM_HEAD_DIM).reshape(1, SSM_D_INNER)
            nw = ssm_norm_w[o].reshape(1, SSM_D_INNER).astype(F32)
            cb = ssm_conv_b[o].reshape(1, SSM_CONV_CH)

            y_p, h_p = _ssd_prompt(pm, pt, ssm_conv_w[o], cb, pa, pb, d_e, nw, bsz, t)
            pm3 = pm.reshape(bsz, t, OD_MAIN)
            outs["ph"].append(h_p)
            outs["psc"].append(pm3[:, t - (CONV_W - 1):, OD_XBC:OD_XBC + SSM_CONV_CH])
            xp = _outproj(xp, [y_p], [w_oo[o]], tm=tm_p, tn=512)

            xc, dtv, dec, cnew = _ssd_prep_sample(sm, jnp.swapaxes(state_ssm_conv[o], 0, 1), ssm_conv_w[o], cb,
                                                  st, pa, pb)
            y_s, h_s = _ssd_state_sample(dtv, dec, xc, sm, d_e, nw, state_ssm[o])
            outs["sh"].append(h_s)
            outs["ssc"].append(jnp.swapaxes(cnew, 0, 1))
            xs = _outproj(xs, [y_s], [w_oo[o]], tm=nb, tn=512)
        xp, xs = ffn_pair(xp, xs, norm_ffn2[l], l, 1, l == depth - 1)

    st_ = {k: jnp.stack(v) for k, v in outs.items()}
    return (xp.reshape(bsz, t, d), xs.reshape(nb, 1, d),
            st_["pk"], st_["pv"], st_["pS"], st_["pgc"], st_["ph"], st_["psc"],
            st_["sk"], st_["sv"], st_["sS"], st_["sgc"], st_["sh"], st_["ssc"])


def kernel(x_prompt, x_sample, cache_swa_k, cache_swa_v, state_gdn, state_gdn_conv, state_ssm, state_ssm_conv,
           norm_ffn1, norm_mix, norm_ffn2, norm_final, w_ffn_gate, w_ffn_up, w_ffn_down,
           w_in_even, w_out_even, attn_sinks, gdn_conv_w, gdn_A_log, gdn_dt_bias, gdn_norm_w,
           w_in_odd, w_out_odd, ssm_conv_w, ssm_conv_b, ssm_A_log, ssm_dt_bias, ssm_D, ssm_norm_w):
    bsz, t, d = x_prompt.shape
    nb = x_sample.shape[0]
    depth = norm_ffn1.shape[0]
    n_even, n_odd = (depth + 1) // 2, depth // 2
    mp = bsz * t
    assert x_sample.shape[1] == 1 and d == D_MODEL and t % (2 * CHUNK) == 0
    hh = GDN_HEADS

    wg = w_ffn_gate.astype(BF16)
    wu = w_ffn_up.astype(BF16)
    wd = w_ffn_down.astype(BF16)
    o_q, o_k, o_v, o_g, o_z, o_a = 0, SWA_Q_W, SWA_Q_W + SWA_KV_W, SWA_Q_W + 2 * SWA_KV_W, \
        SWA_Q_W + 2 * SWA_KV_W + GDN_CONV_CH, SWA_Q_W + 2 * SWA_KV_W + GDN_CONV_CH + GDN_V_W
    w_even_main = jnp.concatenate(
        [w_in_even[:, :, o_q:o_k], w_in_even[:, :, o_g:o_z], w_in_even[:, :, o_z:o_a],
         w_in_even[:, :, o_k:o_v], w_in_even[:, :, o_v:o_g]], axis=-1).astype(BF16)
    w_even_tail = jnp.pad(w_in_even[:, :, o_a:], ((0, 0), (0, 0), (0, LANES - 2 * hh))).astype(BF16)
    w_odd_main = w_in_odd[:, :, :OD_MAIN].astype(BF16)
    w_odd_tail = jnp.pad(w_in_odd[:, :, OD_MAIN:], ((0, 0), (0, 0), (0, LANES - SSM_HEADS))).astype(BF16)
    w_oe = w_out_even.astype(BF16)
    w_oo = w_out_odd.astype(BF16)
    ex = _head_expand()

    xp = x_prompt.reshape(mp, d)
    xs = x_sample.reshape(nb, d)
    tm_p = _tile(mp, 1024)
    gain_f = norm_final.reshape(1, d)

    small = {k: [] for k in ("pk", "pv", "pgc", "psc", "sk", "sv", "sgc", "ssc")}
    p_gdn = p_ssm = s_gdn = s_ssm = None

    def ffn_pair(xp, xs, gain, l, which, final):
        gain = gain.reshape(1, d)
        xp = _ffn(xp, gain, gain_f, wg, wu, wd, l, which, final_norm=final, tm=tm_p, tf=512)
        xs = _ffn(xs, gain, gain_f, wg, wu, wd, l, which, final_norm=final, tm=nb, tf=512)
        return xp, xs

    for l in range(depth):
        xp, xs = ffn_pair(xp, xs, norm_ffn1[l], l, 0, False)
        gmix = norm_mix[l].reshape(1, d)
        if l % 2 == 0:
            e = l // 2
            pm, pt = _proj(xp, gmix, w_even_main, w_even_tail, e, tm=tm_p, tn=512)
            sm, st = _proj(xs, gmix, w_even_main, w_even_tail, e, tm=nb, tn=512)
            pa = _lane_pad_row(gdn_A_log[e])
            pb = _lane_pad_row(gdn_dt_bias[e])
            nw = gdn_norm_w[e].reshape(1, GDN_DV).astype(F32)

            o_a_p = _swa_prompt(pm, attn_sinks[e].astype(F32), bsz, t)
            o_b_p, p_gdn = _gdn_prompt(pm, pt, gdn_conv_w, pa, pb, nw, e, n_even, p_gdn, bsz, t, rows=2 * CHUNK)
            pm3 = pm.reshape(bsz, t, EV_MAIN)
            small["pk"].append(pm3[:, t - WINDOW:, EV_K:EV_K + SWA_KV_W].reshape(bsz, WINDOW, SWA_KV_HEADS, SWA_HEAD_DIM))
            small["pv"].append(pm3[:, t - WINDOW:, EV_V:EV_V + SWA_KV_W].reshape(bsz, WINDOW, SWA_KV_HEADS, SWA_HEAD_DIM))
            small["pgc"].append(pm3[:, t - (CONV_W - 1):, EV_GQKV:EV_GQKV + GDN_CONV_CH])
            xp = _outproj(xp, [o_a_p, o_b_p], w_oe, e, tm=tm_p, tn=512)

            kn = sm[:, EV_K:EV_K + SWA_KV_W]
            vn = sm[:, EV_V:EV_V + SWA_KV_W]
            o_a_s = _swa_sample(sm[:, EV_Q:EV_Q + SWA_Q_W].reshape(nb, SWA_HEADS, SWA_HEAD_DIM),
                                kn.reshape(nb, 1, SWA_KV_W), vn.reshape(nb, 1, SWA_KV_W),
                                cache_swa_k[e].reshape(nb, WINDOW, SWA_KV_W), cache_swa_v[e].reshape(nb, WINDOW, SWA_KV_W),
                                attn_sinks[e].astype(F32).reshape(SWA_HEADS, 1))
            ab = jnp.broadcast_to(st[:, :2 * hh, None], (nb, 2 * hh, LANES))
            o_b_s, s_gdn, gc_s = _gdn_sample(
                sm[:, EV_GQKV:EV_GQKV + GDN_CONV_CH].reshape(nb, 3 * hh, GDN_DK),
                state_gdn_conv[e].reshape(nb, CONV_W - 1, 3 * hh, GDN_DK),
                gdn_conv_w[e].reshape(CONV_W, 3 * hh, GDN_DK),
                state_gdn, sm[:, EV_Z:EV_Z + GDN_V_W].reshape(nb, hh, GDN_DV), ab,
                jnp.broadcast_to(gdn_A_log[e].astype(F32)[:, None], (hh, LANES)),
                jnp.broadcast_to(gdn_dt_bias[e].astype(F32)[:, None], (hh, LANES)), nw, e, s_gdn)
            small["sk"].append(kn.reshape(nb, 1, SWA_KV_HEADS, SWA_HEAD_DIM))
            small["sv"].append(vn.reshape(nb, 1, SWA_KV_HEADS, SWA_HEAD_DIM))
            small["sgc"].append(gc_s.reshape(nb, CONV_W - 1, GDN_CONV_CH))
            xs = _outproj(xs, [o_a_s.reshape(nb, SWA_Q_W).astype(BF16), o_b_s.reshape(nb, GDN_V_W).astype(BF16)],
                          w_oe, e, tm=nb, tn=512)
        else:
            o = l // 2
            pm, pt = _proj(xp, gmix, w_odd_main, w_odd_tail, o, tm=tm_p, tn=1024)
            sm, st = _proj(xs, gmix, w_odd_main, w_odd_tail, o, tm=nb, tn=1024)
            pa = _lane_pad_row(ssm_A_log[o])
            pb = _lane_pad_row(ssm_dt_bias[o])
            d_e = jnp.repeat(ssm_D[o].astype(F32), SSM_HEAD_DIM).reshape(1, SSM_D_INNER)
            nw = ssm_norm_w[o].reshape(1, SSM_D_INNER).astype(F32)
            cb = ssm_conv_b[o].reshape(1, SSM_CONV_CH)

            y_p, p_ssm = _ssd_prompt(pm, pt, ssm_conv_w[o], cb, pa, pb, d_e, nw, ex, o, n_odd, p_ssm, bsz, t)
            pm3 = pm.reshape(bsz, t, OD_MAIN)
            small["psc"].append(pm3[:, t - (CONV_W - 1):, OD_XBC:OD_XBC + SSM_CONV_CH])
            xp = _outproj(xp, [y_p], w_oo, o, tm=tm_p, tn=512)

            xc, dtv, dec, cnew = _ssd_prep_sample(sm, jnp.swapaxes(state_ssm_conv[o], 0, 1), ssm_conv_w[o], cb,
                                                  st, pa, pb)
            y_s, s_ssm = _ssd_state_sample(dtv, dec, xc, sm, d_e, nw, ex, state_ssm, o, s_ssm)
            small["ssc"].append(jnp.swapaxes(cnew, 0, 1))
            xs = _outproj(xs, [y_s], w_oo, o, tm=nb, tn=512)
        xp, xs = ffn_pair(xp, xs, norm_ffn2[l], l, 1, l == depth - 1)

    st_ = {k: jnp.stack(v) for k, v in small.items()}
    return (xp.reshape(bsz, t, d), xs.reshape(nb, 1, d),
            st_["pk"], st_["pv"], p_gdn, st_["pgc"], p_ssm, st_["psc"],
            st_["sk"], st_["sv"], s_gdn, st_["sgc"], s_ssm, st_["ssc"])
```

```python
import functools

import jax
import jax.numpy as jnp
from jax import lax
from jax.experimental import pallas as pl
from jax.experimental.pallas import tpu as pltpu

F32 = jnp.float32
BF16 = jnp.bfloat16

D_MODEL = 2048
SWA_HEADS = 16
SWA_KV_HEADS = 4
SWA_GROUP = SWA_HEADS // SWA_KV_HEADS
SWA_HEAD_DIM = 64
WINDOW = 128
GDN_HEADS = 8
GDN_DK = 128
GDN_DV = 128
CONV_W = 4
SSM_D_INNER = 2 * D_MODEL
SSM_HEAD_DIM = 64
SSM_HEADS = SSM_D_INNER // SSM_HEAD_DIM
SSM_GROUPS = 8
SSM_REP = SSM_HEADS // SSM_GROUPS
SSM_STATE = 128
EPS = 1e-6
NEG_INF = -1e30

SWA_Q_W = SWA_HEADS * SWA_HEAD_DIM
SWA_KV_W = SWA_KV_HEADS * SWA_HEAD_DIM
GDN_QK_W = GDN_HEADS * GDN_DK
GDN_V_W = GDN_HEADS * GDN_DV
GDN_CONV_CH = 2 * GDN_QK_W + GDN_V_W
GDN_HALF_W = GDN_QK_W // 2
SSM_BC_W = SSM_GROUPS * SSM_STATE
SSM_CONV_CH = SSM_D_INNER + 2 * SSM_BC_W
SSM_GROUP_W = SSM_REP * SSM_HEAD_DIM
SSD_SEC_W = SSM_CONV_CH // 3

EV_Q = 0
EV_K = EV_Q + SWA_Q_W
EV_V = EV_K + SWA_KV_W
EV_GQKV = EV_V + SWA_KV_W
EV_Z = EV_GQKV + GDN_CONV_CH
EV_MAIN = EV_Z + GDN_V_W
OD_XBC = SSM_D_INNER
OD_MAIN = SSM_D_INNER + SSM_CONV_CH

LANES = 128
SUBLANES = 8
CHUNK = 128
VMEM_LIMIT = 56 * 1024 * 1024

FFN_TM, FFN_TF = 512, 512
PROJ_TM, PROJ_TN_EVEN, PROJ_TN_ODD = 1024, 512, 1024
OUT_TM, OUT_TN = 1024, 512


def _params(sem):
    return pltpu.CompilerParams(dimension_semantics=sem, vmem_limit_bytes=VMEM_LIMIT)


def _tile(m, pref):
    return pref if m % pref == 0 else m


def _sigmoid(x):
    return 1.0 / (1.0 + jnp.exp(-x))


def _silu(x):
    return x * _sigmoid(x)


def _softplus(x):
    return jnp.maximum(x, 0.0) + jnp.log1p(jnp.exp(-jnp.abs(x)))


def _dot(a, b):
    return jnp.dot(a.astype(BF16), b.astype(BF16), preferred_element_type=F32)


def _dot_nt(a, b):
    return lax.dot_general(a.astype(BF16), b.astype(BF16), (((1,), (1,)), ((), ())),
                           preferred_element_type=F32)


def _bdot(a, b):
    return jnp.einsum("bij,bjk->bik", a.astype(BF16), b.astype(BF16), preferred_element_type=F32)


def _bdot_nt(a, b):
    return jnp.einsum("bid,bjd->bij", a.astype(BF16), b.astype(BF16), preferred_element_type=F32)


def _split2(a):
    hi = a.astype(BF16)
    lo = (a - hi.astype(F32)).astype(BF16)
    return hi, lo


def _split3(a):
    hi = a.astype(BF16)
    r = a - hi.astype(F32)
    mid = r.astype(BF16)
    lo = (r - mid.astype(F32)).astype(BF16)
    return hi, mid, lo


def _sel_dot(m01, parts):
    out = jnp.dot(m01, parts[0], preferred_element_type=F32)
    for p in parts[1:]:
        out = out + jnp.dot(m01, p, preferred_element_type=F32)
    return out


def _dot_sel(parts, m01):
    out = jnp.dot(parts[0], m01, preferred_element_type=F32)
    for p in parts[1:]:
        out = out + jnp.dot(p, m01, preferred_element_type=F32)
    return out


def _rms_rows(x, gain):
    ms = jnp.mean(x * x, axis=-1, keepdims=True)
    return x * lax.rsqrt(ms + EPS) * gain


def _iota2(shape, dim):
    return lax.broadcasted_iota(jnp.int32, shape, dim)


def _drop_ref(kern, pos):
    def entry(*refs):
        return kern(*refs[:pos], *refs[pos + 1:])
    return entry


def _alias_prev(prev, n_in, out_idx):
    if prev is None:
        return [], [], {}
    return [pl.BlockSpec(memory_space=pl.ANY)], [prev], {n_in: out_idx}


def _ffn_kernel(x_ref, g_ref, gf_ref, wg_ref, wu_ref, wd_ref, *rest, final_norm, cast):
    if cast:
        o_ref, wgb_ref, wub_ref, wdb_ref, h_ref = rest
    else:
        o_ref, h_ref = rest
    j = pl.program_id(1)

    @pl.when(j == 0)
    def _():
        h_ref[...] = _rms_rows(x_ref[...], g_ref[...]).astype(BF16)
        o_ref[...] = jnp.zeros_like(o_ref)

    wg, wu, wd = wg_ref[...], wu_ref[...], wd_ref[...]
    if cast:
        wg, wu, wd = wg.astype(BF16), wu.astype(BF16), wd.astype(BF16)
        wgb_ref[...] = wg
        wub_ref[...] = wu
        wdb_ref[...] = wd
    h = h_ref[...]
    a = jnp.dot(h, wg, preferred_element_type=F32)
    b = jnp.dot(h, wu, preferred_element_type=F32)
    t = (_silu(a) * b).astype(BF16)
    o_ref[...] += jnp.dot(t, wd, preferred_element_type=F32)

    @pl.when(j == pl.num_programs(1) - 1)
    def _():
        y = x_ref[...] + 0.5 * o_ref[...]
        if final_norm:
            y = _rms_rows(y, gf_ref[...])
        o_ref[...] = y


def _ffn(x, gain, gain_final, wg, wu, wd, sel, *, final_norm, tm, tf):
    m, d = x.shape
    f = wg.shape[-1]
    cast = sel is not None
    if cast:
        assert m == tm
        layer, which = sel
        w_specs = [pl.BlockSpec((None, None, d, tf), lambda i, j: (layer, which, 0, j)),
                   pl.BlockSpec((None, None, d, tf), lambda i, j: (layer, which, 0, j)),
                   pl.BlockSpec((None, None, tf, d), lambda i, j: (layer, which, j, 0))]
    else:
        w_specs = [pl.BlockSpec((d, tf), lambda i, j: (0, j)),
                   pl.BlockSpec((d, tf), lambda i, j: (0, j)),
                   pl.BlockSpec((tf, d), lambda i, j: (j, 0))]
    out_specs = [pl.BlockSpec((tm, d), lambda i, j: (i, 0))]
    out_shape = [jax.ShapeDtypeStruct((m, d), F32)]
    if cast:
        out_specs += [pl.BlockSpec((d, tf), lambda i, j: (0, j)),
                      pl.BlockSpec((d, tf), lambda i, j: (0, j)),
                      pl.BlockSpec((tf, d), lambda i, j: (j, 0))]
        out_shape += [jax.ShapeDtypeStruct((d, f), BF16), jax.ShapeDtypeStruct((d, f), BF16),
                      jax.ShapeDtypeStruct((f, d), BF16)]
    return pl.pallas_call(
        functools.partial(_ffn_kernel, final_norm=final_norm, cast=cast),
        grid=(m // tm, f // tf),
        in_specs=[
            pl.BlockSpec((tm, d), lambda i, j: (i, 0)),
            pl.BlockSpec((1, d), lambda i, j: (0, 0)),
            pl.BlockSpec((1, d), lambda i, j: (0, 0)),
        ] + w_specs,
        out_specs=out_specs,
        out_shape=out_shape,
        scratch_shapes=[pltpu.VMEM((tm, d), BF16)],
        compiler_params=_params(("parallel", "arbitrary")),
        name="ffn_half",
    )(x, gain, gain_final, wg, wu, wd)


def _proj_kernel(x_ref, g_ref, w_ref, wt_ref, *rest, cast):
    if cast:
        o_ref, ot_ref, wb_ref, wtb_ref, h_ref = rest
    else:
        o_ref, ot_ref, h_ref = rest
    j = pl.program_id(1)

    @pl.when(j == 0)
    def _():
        h = _rms_rows(x_ref[...], g_ref[...]).astype(BF16)
        h_ref[...] = h
        wt = wt_ref[...]
        if cast:
            wt = wt.astype(BF16)
            wtb_ref[...] = wt
        ot_ref[...] = jnp.dot(h, wt, preferred_element_type=F32)

    w = w_ref[...]
    if cast:
        w = w.astype(BF16)
        wb_ref[...] = w
    o_ref[...] = jnp.dot(h_ref[...], w, preferred_element_type=F32)


def _proj(x, gain, w_main, w_tail, layer, n, *, tm, tn):
    m, d = x.shape
    cast = layer is not None
    if cast:
        assert m == tm
        w_specs = [pl.BlockSpec((None, d, tn), lambda i, j: (layer, 0, j)),
                   pl.BlockSpec((None, d, LANES), lambda i, j: (layer, 0, 0))]
    else:
        w_specs = [pl.BlockSpec((d, tn), lambda i, j: (0, j)),
                   pl.BlockSpec((d, LANES), lambda i, j: (0, 0))]
    out_specs = [pl.BlockSpec((tm, tn), lambda i, j: (i, j)),
                 pl.BlockSpec((tm, LANES), lambda i, j: (i, 0))]
    out_shape = [jax.ShapeDtypeStruct((m, n), F32), jax.ShapeDtypeStruct((m, LANES), F32)]
    if cast:
        out_specs += [pl.BlockSpec((d, tn), lambda i, j: (0, j)),
                      pl.BlockSpec((d, LANES), lambda i, j: (0, 0))]
        out_shape += [jax.ShapeDtypeStruct((d, n), BF16), jax.ShapeDtypeStruct((d, LANES), BF16)]
    return pl.pallas_call(
        functools.partial(_proj_kernel, cast=cast),
        grid=(m // tm, n // tn),
        in_specs=[
            pl.BlockSpec((tm, d), lambda i, j: (i, 0)),
            pl.BlockSpec((1, d), lambda i, j: (0, 0)),
        ] + w_specs,
        out_specs=out_specs,
        out_shape=out_shape,
        scratch_shapes=[pltpu.VMEM((tm, d), BF16)],
        compiler_params=_params(("parallel", "arbitrary")),
        name="norm_proj",
    )(x, gain, w_main, w_tail)


def _outproj_kernel(*refs, n_in, cast):
    x_ref = refs[0]
    o_refs = refs[1:1 + n_in]
    w_refs = refs[1 + n_in:1 + 2 * n_in]
    out_ref = refs[1 + 2 * n_in]
    acc = x_ref[...]
    for kb, (o_ref, w_ref) in enumerate(zip(o_refs, w_refs)):
        w = w_ref[...]
        if cast:
            w = w.astype(BF16)
            kw = w.shape[0]
            refs[2 + 2 * n_in][kb * kw:(kb + 1) * kw, :] = w
        acc = acc + jnp.dot(o_ref[...], w, preferred_element_type=F32)
    out_ref[...] = acc


def _outproj(x, outs, w, layer, *, tm, tn):
    m, d = x.shape
    n_in = len(outs)
    kw = outs[0].shape[1]
    k_all = kw * n_in
    assert all(o.shape[1] == kw for o in outs) and k_all == w.shape[-2]
    cast = layer is not None
    in_specs = [pl.BlockSpec((tm, tn), lambda i, j: (i, j))]
    for o in outs:
        in_specs.append(pl.BlockSpec((tm, kw), lambda i, j: (i, 0)))
    for kb in range(n_in):
        if cast:
            in_specs.append(pl.BlockSpec((None, kw, tn), lambda i, j, kb=kb: (layer, kb, j)))
        else:
            in_specs.append(pl.BlockSpec((kw, tn), lambda i, j, kb=kb: (kb, j)))
    out_specs = [pl.BlockSpec((tm, tn), lambda i, j: (i, j))]
    out_shape = [jax.ShapeDtypeStruct((m, d), F32)]
    if cast:
        assert m == tm
        out_specs.append(pl.BlockSpec((k_all, tn), lambda i, j: (0, j)))
        out_shape.append(jax.ShapeDtypeStruct((k_all, d), BF16))
    return pl.pallas_call(
        functools.partial(_outproj_kernel, n_in=n_in, cast=cast),
        grid=(m // tm, d // tn),
        in_specs=in_specs,
        out_specs=out_specs,
        out_shape=out_shape,
        compiler_params=_params(("parallel", "arbitrary")),
        name="out_proj",
    )(x, *outs, *([w] * n_in))


def _swa_prompt_kernel(sink_ref, q_ref, kc_ref, kp_ref, vc_ref, vp_ref, o_ref):
    n = pl.program_id(1)
    w = WINDOW
    qi = _iota2((w, 2 * w), 0)
    kj = _iota2((w, 2 * w), 1)
    valid = (kj > qi) & (kj <= qi + w) & ((n > 0) | (kj >= w))
    scale = SWA_HEAD_DIM ** -0.5
    kk = jnp.concatenate([kp_ref[...], kc_ref[...]], axis=0).astype(BF16)
    vv = jnp.concatenate([vp_ref[...], vc_ref[...]], axis=0).astype(BF16)
    q = q_ref[...].astype(BF16)
    outs = []
    for h in range(SWA_HEADS):
        c = h // SWA_GROUP
        qh = q[:, h * SWA_HEAD_DIM:(h + 1) * SWA_HEAD_DIM]
        kh = kk[:, c * SWA_HEAD_DIM:(c + 1) * SWA_HEAD_DIM]
        vh = vv[:, c * SWA_HEAD_DIM:(c + 1) * SWA_HEAD_DIM]
        s = lax.dot_general(qh, kh, (((1,), (1,)), ((), ())), preferred_element_type=F32) * scale
        s = jnp.where(valid, s, NEG_INF)
        sink = sink_ref[h]
        mx = jnp.maximum(jnp.max(s, axis=-1, keepdims=True), sink)
        p = jnp.exp(s - mx)
        den = jnp.sum(p, axis=-1, keepdims=True) + jnp.exp(sink - mx)
        p = p / den
        outs.append(jnp.dot(p.astype(BF16), vh, preferred_element_type=F32))
    o_ref[...] = jnp.concatenate(outs, axis=-1).astype(o_ref.dtype)


def _swa_prompt(p_main, sinks, bsz, t):
    nb = t // WINDOW
    kblk = EV_K // SWA_KV_W
    vblk = EV_V // SWA_KV_W

    def cur(col):
        return lambda b, n: (b * nb + n, col)

    def prev(col):
        return lambda b, n: (b * nb + jnp.maximum(n - 1, 0), col)

    return pl.pallas_call(
        _swa_prompt_kernel,
        grid=(bsz, nb),
        in_specs=[
            pl.BlockSpec(memory_space=pltpu.SMEM),
            pl.BlockSpec((WINDOW, SWA_Q_W), cur(EV_Q // SWA_Q_W)),
            pl.BlockSpec((WINDOW, SWA_KV_W), cur(kblk)),
            pl.BlockSpec((WINDOW, SWA_KV_W), prev(kblk)),
            pl.BlockSpec((WINDOW, SWA_KV_W), cur(vblk)),
            pl.BlockSpec((WINDOW, SWA_KV_W), prev(vblk)),
        ],
        out_specs=pl.BlockSpec((WINDOW, SWA_Q_W), cur(0)),
        out_shape=jax.ShapeDtypeStruct((bsz * t, SWA_Q_W), BF16),
        compiler_params=_params(("parallel", "arbitrary")),
        name="swa_prompt",
    )(sinks, p_main, p_main, p_main, p_main, p_main)


def _unit_lower_inverse(a, ri, ci):
    a0 = jnp.where((ri >> 4) == (ci >> 4), a, 0.0)
    y = -a0
    p = _bdot(a0, a0)
    for step in range(3):
        y = y + p + _bdot(y, p)
        if step < 2:
            p = _bdot(p, p)
    for sh in (4, 5, 6):
        e = jnp.where(((ri >> (sh + 1)) == (ci >> (sh + 1))) & ((ri >> sh) != (ci >> sh)), a, 0.0)
        t = e + _bdot(y, e)
        y = y - t - _bdot(t, y)
    return y


def _conv_section(ext_ref, carry_ref, sec, u_ref, w, bias, rows):
    ext_ref[0:SUBLANES, :] = carry_ref[sec]
    ext_ref[SUBLANES:SUBLANES + rows, :] = u_ref[...]
    acc = ext_ref[SUBLANES:SUBLANES + rows, :] * w[CONV_W - 1:CONV_W, :]
    for j in range(CONV_W - 1):
        off = SUBLANES - (CONV_W - 1) + j
        acc = acc + ext_ref[off:off + rows, :] * w[j:j + 1, :]
    if bias is not None:
        acc = acc + bias
    carry_ref[sec] = u_ref[rows - SUBLANES:rows, :]
    return _silu(acc)


def _gdn_prompt_kernel(uq0, uq1, uk0, uk1, uv0, uv1, z0_ref, z1_ref, tail_ref, cw_ref, pa_ref, pb_ref, nw_ref,
                       tri_ref, ones_ref,
                       o_ref, sout_ref,
                       s_ref, carry_ref, ext_ref, cq_ref, ck_ref, cv_ref, *, rows):
    i = pl.program_id(1)
    nchunk = rows // CHUNK
    hh = GDN_HEADS
    hw = GDN_HALF_W

    @pl.when(i == 0)
    def _():
        s_ref[...] = jnp.zeros_like(s_ref)
        carry_ref[...] = jnp.zeros_like(carry_ref)

    for sec, (halves, c_ref) in enumerate((((uq0, uq1), cq_ref), ((uk0, uk1), ck_ref), ((uv0, uv1), cv_ref))):
        for half, u_ref in enumerate(halves):
            ws = slice(sec * GDN_QK_W + half * hw, sec * GDN_QK_W + (half + 1) * hw)
            c_ref[:, half * hw:(half + 1) * hw] = _conv_section(ext_ref, carry_ref, 2 * sec + half, u_ref,
                                                                 cw_ref[:, ws], None, rows)

    t = tail_ref[...]
    g_all = -jnp.exp(pa_ref[...]) * _softplus(t + pb_ref[...])
    beta_all = _sigmoid(t)
    g_parts = _split3(g_all)
    gcum_all = _sel_dot(tri_ref[...], g_parts)
    gtot_all = _sel_dot(ones_ref[...], g_parts)
    gcum_t = jnp.transpose(gcum_all)

    ri = _iota2((1, CHUNK, CHUNK), 1)
    ci = _iota2((1, CHUNK, CHUNK), 2)
    incl = ri >= ci
    strict = ri > ci

    def rs(c):
        return slice(c * CHUNK, (c + 1) * CHUNK)

    def hs(h):
        return slice(h * GDN_DK, (h + 1) * GDN_DK)

    def stack(fn):
        return jnp.stack([fn(c, h) for c in range(nchunk) for h in range(hh)], axis=0)

    q = stack(lambda c, h: cq_ref[rs(c), hs(h)])
    k = stack(lambda c, h: ck_ref[rs(c), hs(h)])
    v = stack(lambda c, h: cv_ref[rs(c), hs(h)])
    gc = stack(lambda c, h: gcum_all[rs(c), h:h + 1])
    gt = stack(lambda c, h: gtot_all[rs(c), h:h + 1])
    bc = stack(lambda c, h: beta_all[rs(c), hh + h:hh + h + 1])
    gr = stack(lambda c, h: gcum_t[h:h + 1, rs(c)])

    q = q * lax.rsqrt(jnp.sum(q * q, axis=-1, keepdims=True) + EPS) * (GDN_DK ** -0.5)
    k = k * lax.rsqrt(jnp.sum(k * k, axis=-1, keepdims=True) + EPS)
    decay = jnp.exp(jnp.where(incl, gc - gr, NEG_INF))
    kb = k * bc
    a = jnp.where(strict, _bdot_nt(kb, k) * decay, 0.0)
    y = _unit_lower_inverse(a, ri, ci)
    eg = jnp.exp(gc)
    rhs = jnp.concatenate([v * bc, kb * eg], axis=2)
    sol = rhs + _bdot(y, rhs)
    u = sol[:, :, :GDN_DV]
    w = sol[:, :, GDN_DV:]
    wq = jnp.concatenate([w, q * eg], axis=1)
    attn = _bdot_nt(q, k) * decay
    kdt = jnp.swapaxes(k * jnp.exp(gt - gc), 1, 2)
    egt = jnp.exp(gt)

    for c in range(nchunk):
        bs = slice(c * hh, (c + 1) * hh)
        s = s_ref[...]
        ws_qs = _bdot(wq[bs], s)
        v_new = u[bs] - ws_qs[:, :CHUNK]
        o = ws_qs[:, CHUNK:] + _bdot(attn[bs], v_new)
        s_ref[...] = s * egt[bs] + _bdot(kdt[bs], v_new)
        for h in range(hh):
            z_ref = z0_ref if h < hh // 2 else z1_ref
            zc = z_ref[rs(c), hs(h % (hh // 2))]
            og = _rms_rows(o[h], nw_ref[...]) * _silu(zc)
            o_ref[rs(c), hs(h)] = og.astype(o_ref.dtype)

    @pl.when(i == pl.num_programs(1) - 1)
    def _():
        sout_ref[0] = s_ref[...]


def _chunk_masks(rows):
    r = jnp.arange(rows)
    same = (r[:, None] // CHUNK) == (r[None, :] // CHUNK)
    tri = (same & (r[:, None] >= r[None, :])).astype(BF16)
    ones = same.astype(BF16)
    return tri, ones


def _gdn_prompt(p_main, p_tail, conv_w, pa, pb, norm_w, layer, n_layers, prev, bsz, t, *, rows):
    nblk = t // rows
    hw = GDN_HALF_W
    hh = GDN_HEADS
    tri, ones = _chunk_masks(rows)

    def col(base):
        return lambda b, i: (b * nblk + i, base // hw)

    const = lambda b, i: (0, 0)
    in_specs = [pl.BlockSpec((rows, hw), col(EV_GQKV + k * hw)) for k in range(6)]
    in_specs += [pl.BlockSpec((rows, hw), col(EV_Z)), pl.BlockSpec((rows, hw), col(EV_Z + hw))]
    in_specs += [
        pl.BlockSpec((rows, LANES), lambda b, i: (b * nblk + i, 0)),
        pl.BlockSpec((None, CONV_W, GDN_CONV_CH), lambda b, i: (layer, 0, 0)),
        pl.BlockSpec((1, LANES), const),
        pl.BlockSpec((1, LANES), const),
        pl.BlockSpec((1, GDN_DV), const),
        pl.BlockSpec((rows, rows), const),
        pl.BlockSpec((rows, rows), const),
    ]
    args = [p_main] * 8 + [p_tail, conv_w, pa, pb, norm_w, tri, ones]
    extra_specs, extra_args, aliases = _alias_prev(prev, len(args), 1)
    kern = functools.partial(_gdn_prompt_kernel, rows=rows)
    if prev is not None:
        kern = _drop_ref(kern, len(args))
    return pl.pallas_call(
        kern,
        grid=(bsz, nblk),
        in_specs=in_specs + extra_specs,
        out_specs=[
            pl.BlockSpec((rows, GDN_V_W), lambda b, i: (b * nblk + i, 0)),
            pl.BlockSpec((None, 1, hh, GDN_DK, GDN_DV), lambda b, i: (layer, b, 0, 0, 0)),
        ],
        out_shape=[
            jax.ShapeDtypeStruct((bsz * t, GDN_V_W), BF16),
            jax.ShapeDtypeStruct((n_layers, bsz, hh, GDN_DK, GDN_DV), F32),
        ],
        scratch_shapes=[
            pltpu.VMEM((hh, GDN_DK, GDN_DV), F32),
            pltpu.VMEM((6, SUBLANES, hw), F32),
            pltpu.VMEM((rows + SUBLANES, hw), F32),
            pltpu.VMEM((rows, GDN_QK_W), F32),
            pltpu.VMEM((rows, GDN_QK_W), F32),
            pltpu.VMEM((rows, GDN_V_W), F32),
        ],
        input_output_aliases=aliases,
        compiler_params=_params(("parallel", "arbitrary")),
        name="gdn_prompt",
    )(*args, *extra_args)


def _ssd_prompt_kernel(z_ref, xa_ref, xb_ref, xc3_ref, tail_ref, cw_ref, cb_ref, pa_ref, pb_ref, de_ref, nw_ref,
                       ex_ref, tri_ref,
                       y_ref, hout_ref,
                       ht_ref, carry_ref, ext_ref, xc_ref, cumt_ref, dtt_ref):
    i = pl.program_id(1)
    rows = CHUNK

    @pl.when(i == 0)
    def _():
        ht_ref[...] = jnp.zeros_like(ht_ref)
        carry_ref[...] = jnp.zeros_like(carry_ref)

    for sec, u_ref in enumerate((xa_ref, xb_ref, xc3_ref)):
        ss = slice(sec * SSD_SEC_W, (sec + 1) * SSD_SEC_W)
        xc_ref[:, ss] = _conv_section(ext_ref, carry_ref, sec, u_ref, cw_ref[:, ss], cb_ref[:, ss], rows)

    dt = _softplus(tail_ref[...] + pb_ref[...])
    da = dt * (-jnp.exp(pa_ref[...]))
    cum = _sel_dot(tri_ref[...], _split3(da))
    cumt_ref[...] = jnp.transpose(cum)
    dtt_ref[...] = jnp.transpose(dt)
    cum_parts = _split3(cum)
    dt_parts = _split3(dt)

    ri = _iota2((rows, rows), 0)
    ci = _iota2((rows, rows), 1)
    tri = ri >= ci
    lane = _iota2((rows, LANES), 1)
    lo_half = lane < SSM_HEAD_DIM

    for g in range(SSM_GROUPS):
        gs = slice(g * SSM_GROUP_W, (g + 1) * SSM_GROUP_W)
        bmat = xc_ref[:, SSM_D_INNER + g * SSM_STATE:SSM_D_INNER + (g + 1) * SSM_STATE]
        cmat = xc_ref[:, SSM_D_INNER + SSM_BC_W + g * SSM_STATE:SSM_D_INNER + SSM_BC_W + (g + 1) * SSM_STATE]
        xg = xc_ref[:, gs]
        cb = _dot_nt(cmat, bmat)
        exg = ex_ref[:, gs]
        cum_e = _dot_sel(cum_parts, exg)
        dt_e = _dot_sel(dt_parts, exg)
        htg = ht_ref[g]
        y = _dot(cmat, htg) * jnp.exp(cum_e)
        pieces = []
        for pr in range(SSM_REP // 2):
            scs = []
            for half in range(2):
                h = g * SSM_REP + 2 * pr + half
                ccol = cum[:, h:h + 1]
                crow = cumt_ref[h:h + 1, :]
                drow = dtt_ref[h:h + 1, :]
                lmat = jnp.exp(jnp.where(tri, ccol - crow, NEG_INF))
                scs.append((cb * lmat * drow).astype(BF16))
            xp = xg[:, pr * LANES:(pr + 1) * LANES]
            rhs = jnp.concatenate([jnp.where(lo_half, xp, 0.0), jnp.where(lo_half, 0.0, xp)], axis=0)
            pieces.append(jnp.dot(jnp.concatenate(scs, axis=1), rhs.astype(BF16),
                                  preferred_element_type=F32))
        y = y + jnp.concatenate(pieces, axis=1) + de_ref[:, gs] * xg
        last_e = cum_e[rows - 1:rows, :]
        wts = jnp.exp(last_e - cum_e) * dt_e
        ht_ref[g] = htg * jnp.exp(last_e) + _dot(jnp.transpose(bmat), xg * wts)
        zg = z_ref[:, gs]
        yz = y * _silu(zg)
        y_ref[:, gs] = _rms_rows(yz, nw_ref[:, gs]).astype(y_ref.dtype)

    @pl.when(i == pl.num_programs(1) - 1)
    def _():
        for g in range(SSM_GROUPS):
            hg = jnp.transpose(ht_ref[g])
            hout_ref[0, g * SSM_REP:(g + 1) * SSM_REP] = hg.reshape(SSM_REP, SSM_HEAD_DIM, SSM_STATE)


def _head_expand():
    h = jnp.arange(LANES)[:, None]
    c = jnp.arange(SSM_D_INNER)[None, :]
    return ((c // SSM_HEAD_DIM) == h).astype(BF16)


def _ssd_prompt(p_main, p_tail, conv_w, conv_b, pa, pb, d_e, norm_w, ex, layer, n_layers, prev, bsz, t):
    rows = CHUNK
    nblk = t // rows
    tri, _ = _chunk_masks(rows)
    const = lambda b, i: (0, 0)
    row = lambda b, i: (b * nblk + i, 0)
    args = [p_main, p_main, p_main, p_main, p_tail, conv_w, conv_b, pa, pb, d_e, norm_w, ex, tri]
    extra_specs, extra_args, aliases = _alias_prev(prev, len(args), 1)
    kern = _ssd_prompt_kernel if prev is None else _drop_ref(_ssd_prompt_kernel, len(args))
    return pl.pallas_call(
        kern,
        grid=(bsz, nblk),
        input_output_aliases=aliases,
        in_specs=[
            pl.BlockSpec((rows, SSM_D_INNER), row),
            pl.BlockSpec((rows, SSD_SEC_W), lambda b, i: (b * nblk + i, OD_XBC // SSD_SEC_W)),
            pl.BlockSpec((rows, SSD_SEC_W), lambda b, i: (b * nblk + i, OD_XBC // SSD_SEC_W + 1)),
            pl.BlockSpec((rows, SSD_SEC_W), lambda b, i: (b * nblk + i, OD_XBC // SSD_SEC_W + 2)),
            pl.BlockSpec((rows, LANES), row),
            pl.BlockSpec((CONV_W, SSM_CONV_CH), const),
            pl.BlockSpec((1, SSM_CONV_CH), const),
            pl.BlockSpec((1, LANES), const),
            pl.BlockSpec((1, LANES), const),
            pl.BlockSpec((1, SSM_D_INNER), const),
            pl.BlockSpec((1, SSM_D_INNER), const),
            pl.BlockSpec((LANES, SSM_D_INNER), const),
            pl.BlockSpec((rows, rows), const),
        ] + extra_specs,
        out_specs=[
            pl.BlockSpec((rows, SSM_D_INNER), row),
            pl.BlockSpec((None, 1, SSM_HEADS, SSM_HEAD_DIM, SSM_STATE), lambda b, i: (layer, b, 0, 0, 0)),
        ],
        out_shape=[
            jax.ShapeDtypeStruct((bsz * t, SSM_D_INNER), BF16),
            jax.ShapeDtypeStruct((n_layers, bsz, SSM_HEADS, SSM_HEAD_DIM, SSM_STATE), F32),
        ],
        scratch_shapes=[
            pltpu.VMEM((SSM_GROUPS, SSM_STATE, SSM_GROUP_W), F32),
            pltpu.VMEM((3, SUBLANES, SSD_SEC_W), F32),
            pltpu.VMEM((rows + SUBLANES, SSD_SEC_W), F32),
            pltpu.VMEM((rows, SSM_CONV_CH), F32),
            pltpu.VMEM((LANES, rows), F32),
            pltpu.VMEM((LANES, rows), F32),
        ],
        compiler_params=_params(("parallel", "arbitrary")),
        name="ssd_prompt",
    )(*args, *extra_args)


def _swa_sample_kernel(sink_ref, q_ref, kn_ref, vn_ref, ck_ref, cv_ref, o_ref):
    scale = SWA_HEAD_DIM ** -0.5
    q = q_ref[0].astype(BF16)
    kn = kn_ref[0].astype(BF16).astype(F32)
    vn = vn_ref[0].astype(BF16).astype(F32)
    ck = ck_ref[0].astype(BF16)
    cv = cv_ref[0].astype(BF16)
    sink = sink_ref[...]
    valid = _iota2((SWA_GROUP, WINDOW), 1) >= 1
    outs = []
    for c in range(SWA_KV_HEADS):
        cs = slice(c * SWA_HEAD_DIM, (c + 1) * SWA_HEAD_DIM)
        gs = slice(c * SWA_GROUP, (c + 1) * SWA_GROUP)
        qc = q[gs, :]
        s = lax.dot_general(qc, ck[:, cs], (((1,), (1,)), ((), ())), preferred_element_type=F32) * scale
        s = jnp.where(valid, s, NEG_INF)
        sn = jnp.sum(qc.astype(F32) * kn[:, cs], axis=-1, keepdims=True) * scale
        sk = sink[gs, :]
        mx = jnp.maximum(jnp.maximum(jnp.max(s, axis=-1, keepdims=True), sn), sk)
        p = jnp.exp(s - mx)
        pn = jnp.exp(sn - mx)
        den = jnp.sum(p, axis=-1, keepdims=True) + pn + jnp.exp(sk - mx)
        o = jnp.dot(p.astype(BF16), cv[:, cs], preferred_element_type=F32) + pn * vn[:, cs]
        outs.append(o / den)
    o_ref[0] = jnp.concatenate(outs, axis=0).astype(o_ref.dtype)


def _swa_sample(q, kn, vn, ck_all, cv_all, sinks, layer):
    nb = q.shape[0]
    kvw = SWA_KV_W
    b3 = lambda b: (b, 0, 0)
    cache = pl.BlockSpec((None, 1, WINDOW, kvw), lambda b: (layer, b, 0, 0))
    return pl.pallas_call(
        _swa_sample_kernel,
        grid=(nb,),
        in_specs=[
            pl.BlockSpec((SWA_HEADS, 1), lambda b: (0, 0)),
            pl.BlockSpec((1, SWA_HEADS, SWA_HEAD_DIM), b3),
            pl.BlockSpec((1, 1, kvw), b3),
            pl.BlockSpec((1, 1, kvw), b3),
            cache,
            cache,
        ],
        out_specs=pl.BlockSpec((1, SWA_HEADS, SWA_HEAD_DIM), b3),
        out_shape=jax.ShapeDtypeStruct((nb, SWA_HEADS, SWA_HEAD_DIM), F32),
        compiler_params=_params(("parallel",)),
        name="swa_sample",
    )(sinks, q, kn, vn, ck_all, cv_all)


def _gdn_sample_kernel(u_ref, prev_ref, w_ref, s_ref, z_ref, ab_ref, pa_ref, pb_ref, nw_ref,
                       o_ref, sout_ref, cnew_ref):
    hh = GDN_HEADS
    u = u_ref[0]
    pv = prev_ref[0]
    w = w_ref[...]
    x = u * w[CONV_W - 1]
    for j in range(CONV_W - 1):
        x = x + pv[j] * w[j]
        cnew_ref[0, j] = pv[j + 1] if j + 1 < CONV_W - 1 else u
    x = _silu(x)
    q = x[0:hh]
    k = x[hh:2 * hh]
    v = x[2 * hh:3 * hh]
    q = q * lax.rsqrt(jnp.sum(q * q, axis=-1, keepdims=True) + EPS) * (GDN_DK ** -0.5)
    k = k * lax.rsqrt(jnp.sum(k * k, axis=-1, keepdims=True) + EPS)
    ab = ab_ref[0]
    dec = jnp.exp(-jnp.exp(pa_ref[...]) * _softplus(ab[0:hh] + pb_ref[...]))
    beta = _sigmoid(ab[hh:2 * hh])
    qk = jnp.concatenate([q, k, jnp.zeros((LANES - 2 * hh, GDN_DK), F32)], axis=0)
    qkt = jnp.transpose(qk)
    outs = []
    for h in range(hh):
        s = s_ref[0, h] * dec[h:h + 1, :]
        kcol = qkt[:, hh + h:hh + h + 1]
        ks = jnp.sum(s * kcol, axis=0, keepdims=True)
        delta = beta[h:h + 1, :] * (v[h:h + 1, :] - ks)
        s = s + kcol * delta
        sout_ref[0, h] = s
        outs.append(jnp.sum(s * qkt[:, h:h + 1], axis=0, keepdims=True))
    o = jnp.concatenate(outs, axis=0)
    o_ref[0] = _rms_rows(o, nw_ref[...]) * _silu(z_ref[0])


def _gdn_sample(u, prev, w, s_all, z, ab, pa, pb, nw, layer, prev_out):
    nb = u.shape[0]
    hh = GDN_HEADS
    n_layers = s_all.shape[0]
    b3 = lambda b: (b, 0, 0)
    b4 = lambda b: (b, 0, 0, 0)
    c2 = lambda b: (0, 0)
    state_spec = pl.BlockSpec((None, 1, hh, GDN_DK, GDN_DV), lambda b: (layer, b, 0, 0, 0))
    args = [u, prev, w, s_all, z, ab, pa, pb, nw]
    extra_specs, extra_args, aliases = _alias_prev(prev_out, len(args), 1)
    kern = _gdn_sample_kernel if prev_out is None else _drop_ref(_gdn_sample_kernel, len(args))
    return pl.pallas_call(
        kern,
        grid=(nb,),
        input_output_aliases=aliases,
        in_specs=[
            pl.BlockSpec((1, 3 * hh, GDN_DK), b3),
            pl.BlockSpec((1, CONV_W - 1, 3 * hh, GDN_DK), b4),
            pl.BlockSpec((CONV_W, 3 * hh, GDN_DK), lambda b: (0, 0, 0)),
            state_spec,
            pl.BlockSpec((1, hh, GDN_DV), b3),
            pl.BlockSpec((1, 2 * hh, LANES), b3),
            pl.BlockSpec((hh, LANES), c2),
            pl.BlockSpec((hh, LANES), c2),
            pl.BlockSpec((1, GDN_DV), c2),
        ] + extra_specs,
        out_specs=[
            pl.BlockSpec((1, hh, GDN_DV), b3),
            state_spec,
            pl.BlockSpec((1, CONV_W - 1, 3 * hh, GDN_DK), b4),
        ],
        out_shape=[
            jax.ShapeDtypeStruct((nb, hh, GDN_DV), F32),
            jax.ShapeDtypeStruct((n_layers, nb, hh, GDN_DK, GDN_DV), F32),
            jax.ShapeDtypeStruct((nb, CONV_W - 1, 3 * hh, GDN_DK), F32),
        ],
        compiler_params=_params(("parallel",)),
        name="gdn_sample",
    )(*args, *extra_args)


def _ssd_prep_kernel(x_ref, prev_ref, cw_ref, cb_ref, tail_ref, pa_ref, pb_ref,
                     xc_ref, dt_ref, dec_ref, cnew_ref):
    u = x_ref[...]
    w = cw_ref[...]
    acc = u * w[CONV_W - 1:CONV_W, :] + cb_ref[...]
    for j in range(CONV_W - 1):
        acc = acc + prev_ref[j] * w[j:j + 1, :]
        cnew_ref[j] = prev_ref[j + 1] if j + 1 < CONV_W - 1 else u
    xc_ref[...] = _silu(acc)
    dt = _softplus(tail_ref[...] + pb_ref[...])
    dt_ref[...] = dt
    dec_ref[...] = jnp.exp(dt * (-jnp.exp(pa_ref[...])))


def _ssd_prep_sample(p_main, prev_t, cw, cb, tail, pa, pb):
    nb = p_main.shape[0]
    sec = SSD_SEC_W
    nsec = SSM_CONV_CH // sec
    c2 = lambda s: (0, 0)
    return pl.pallas_call(
        _ssd_prep_kernel,
        grid=(nsec,),
        in_specs=[
            pl.BlockSpec((nb, sec), lambda s: (0, OD_XBC // sec + s)),
            pl.BlockSpec((CONV_W - 1, nb, sec), lambda s: (0, 0, s)),
            pl.BlockSpec((CONV_W, sec), lambda s: (0, s)),
            pl.BlockSpec((1, sec), lambda s: (0, s)),
            pl.BlockSpec((nb, LANES), c2),
            pl.BlockSpec((1, LANES), c2),
            pl.BlockSpec((1, LANES), c2),
        ],
        out_specs=[
            pl.BlockSpec((nb, sec), lambda s: (0, s)),
            pl.BlockSpec((nb, LANES), c2),
            pl.BlockSpec((nb, LANES), c2),
            pl.BlockSpec((CONV_W - 1, nb, sec), lambda s: (0, 0, s)),
        ],
        out_shape=[
            jax.ShapeDtypeStruct((nb, SSM_CONV_CH), F32),
            jax.ShapeDtypeStruct((nb, LANES), F32),
            jax.ShapeDtypeStruct((nb, LANES), F32),
            jax.ShapeDtypeStruct((CONV_W - 1, nb, SSM_CONV_CH), F32),
        ],
        compiler_params=_params(("arbitrary",)),
        name="ssd_prep_sample",
    )(p_main, prev_t, cw, cb, tail, pa, pb)


def _ssd_state_kernel(dec_ref, x_ref, dt_ref, b_ref, c_ref, z_ref, de_ref, nw_ref, ex_ref, h_ref,
                      y_ref, hout_ref, *, nb):
    g = pl.program_id(0)
    x = x_ref[...]
    dtx = x * _dot_sel(_split3(dt_ref[...]), ex_ref[...])
    zrows = LANES - nb
    xt = jnp.transpose(jnp.concatenate([dtx, jnp.zeros((zrows, SSM_GROUP_W), F32)], axis=0))
    xh, xl = _split2(xt)
    bpad = jnp.concatenate([b_ref[...], jnp.zeros((zrows, SSM_STATE), F32)], axis=0)
    cpad_t = jnp.transpose(jnp.concatenate([c_ref[...], jnp.zeros((zrows, SSM_STATE), F32)], axis=0))
    rowi = _iota2((LANES, SSM_STATE), 0)
    coli = _iota2((SSM_STATE, LANES), 1)
    yt = jnp.zeros((SSM_GROUP_W, LANES), F32)
    for b in range(nb):
        bh, bl = _split2(jnp.where(rowi == b, bpad, 0.0))
        upd = (jnp.dot(xh, bh, preferred_element_type=F32) + jnp.dot(xh, bl, preferred_element_type=F32)
               + jnp.dot(xl, bh, preferred_element_type=F32))
        parts = []
        for r in range(SSM_REP):
            hn = h_ref[b, r] * dec_ref[b, g * SSM_REP + r] + upd[r * SSM_HEAD_DIM:(r + 1) * SSM_HEAD_DIM]
            hout_ref[b, r] = hn
            parts.append(hn)
        hh_, hl_ = _split2(jnp.concatenate(parts, axis=0))
        ch, cl = _split2(jnp.where(coli == b, cpad_t, 0.0))
        yt = yt + (jnp.dot(hh_, ch, preferred_element_type=F32) + jnp.dot(hh_, cl, preferred_element_type=F32)
                   + jnp.dot(hl_, ch, preferred_element_type=F32))
    y = jnp.transpose(yt)[0:nb, :] + de_ref[...] * x
    y_ref[...] = _rms_rows(y * _silu(z_ref[...]), nw_ref[...]).astype(y_ref.dtype)


def _ssd_state_sample(dtv, dec, xc, p_main, d_e, nw, ex, h_all, layer, prev_out):
    nb = xc.shape[0]
    assert nb <= LANES
    gw = SSM_GROUP_W
    n_layers = h_all.shape[0]
    state_spec = pl.BlockSpec((None, nb, SSM_REP, SSM_HEAD_DIM, SSM_STATE), lambda g: (layer, 0, g, 0, 0))
    args = [dec, xc, dtv, xc, xc, p_main, d_e, nw, ex, h_all]
    extra_specs, extra_args, aliases = _alias_prev(prev_out, len(args), 1)
    kern = functools.partial(_ssd_state_kernel, nb=nb)
    if prev_out is not None:
        kern = _drop_ref(kern, len(args))
    return pl.pallas_call(
        kern,
        grid=(SSM_GROUPS,),
        input_output_aliases=aliases,
        in_specs=[
            pl.BlockSpec(memory_space=pltpu.SMEM),
            pl.BlockSpec((nb, gw), lambda g: (0, g)),
            pl.BlockSpec((nb, LANES), lambda g: (0, 0)),
            pl.BlockSpec((nb, SSM_STATE), lambda g: (0, SSM_D_INNER // SSM_STATE + g)),
            pl.BlockSpec((nb, SSM_STATE), lambda g: (0, (SSM_D_INNER + SSM_BC_W) // SSM_STATE + g)),
            pl.BlockSpec((nb, gw), lambda g: (0, g)),
            pl.BlockSpec((1, gw), lambda g: (0, g)),
            pl.BlockSpec((1, gw), lambda g: (0, g)),
            pl.BlockSpec((LANES, gw), lambda g: (0, g)),
            state_spec,
        ] + extra_specs,
        out_specs=[
            pl.BlockSpec((nb, gw), lambda g: (0, g)),
            state_spec,
        ],
        out_shape=[
            jax.ShapeDtypeStruct((nb, SSM_D_INNER), BF16),
            jax.ShapeDtypeStruct((n_layers, nb, SSM_HEADS, SSM_HEAD_DIM, SSM_STATE), F32),
        ],
        compiler_params=_params(("arbitrary",)),
        name="ssd_state_sample",
    )(*args, *extra_args)


def _lane_pad_row(v):
    return jnp.zeros((1, LANES), F32).at[0, :v.shape[0]].set(v.astype(F32))


def kernel(x_prompt, x_sample, cache_swa_k, cache_swa_v, state_gdn, state_gdn_conv, state_ssm, state_ssm_conv,
           norm_ffn1, norm_mix, norm_ffn2, norm_final, w_ffn_gate, w_ffn_up, w_ffn_down,
           w_in_even, w_out_even, attn_sinks, gdn_conv_w, gdn_A_log, gdn_dt_bias, gdn_norm_w,
           w_in_odd, w_out_odd, ssm_conv_w, ssm_conv_b, ssm_A_log, ssm_dt_bias, ssm_D, ssm_norm_w):
    bsz, t, d = x_prompt.shape
    nb = x_sample.shape[0]
    depth = norm_ffn1.shape[0]
    n_even, n_odd = (depth + 1) // 2, depth // 2
    mp = bsz * t
    assert x_sample.shape[1] == 1 and d == D_MODEL and t % (2 * CHUNK) == 0
    hh = GDN_HEADS

    w_even_tail = jnp.pad(w_in_even[:, :, EV_MAIN:], ((0, 0), (0, 0), (0, LANES - 2 * hh)))
    w_odd_tail = jnp.pad(w_in_odd[:, :, OD_MAIN:], ((0, 0), (0, 0), (0, LANES - SSM_HEADS)))
    ex = _head_expand()
    cache_k = cache_swa_k.reshape(n_even, nb, WINDOW, SWA_KV_W)
    cache_v = cache_swa_v.reshape(n_even, nb, WINDOW, SWA_KV_W)

    xp = x_prompt.reshape(mp, d)
    xs = x_sample.reshape(nb, d)
    gain_f = norm_final.reshape(1, d)
    ffn_tm, proj_tm, out_tm = _tile(mp, FFN_TM), _tile(mp, PROJ_TM), _tile(mp, OUT_TM)

    small = {k: [] for k in ("pk", "pv", "pgc", "psc", "sk", "sv", "sgc", "ssc")}
    p_gdn = p_ssm = s_gdn = s_ssm = None

    def ffn_pair(xp, xs, gain, l, which, final):
        gain = gain.reshape(1, d)
        xs, wg, wu, wd = _ffn(xs, gain, gain_f, w_ffn_gate, w_ffn_up, w_ffn_down, (l, which),
                              final_norm=final, tm=nb, tf=FFN_TF)
        xp, = _ffn(xp, gain, gain_f, wg, wu, wd, None, final_norm=final, tm=ffn_tm, tf=FFN_TF)
        return xp, xs

    for l in range(depth):
        xp, xs = ffn_pair(xp, xs, norm_ffn1[l], l, 0, False)
        gmix = norm_mix[l].reshape(1, d)
        if l % 2 == 0:
            e = l // 2
            sm, st, w_main, w_tail = _proj(xs, gmix, w_in_even, w_even_tail, e, EV_MAIN, tm=nb, tn=PROJ_TN_EVEN)
            pm, pt = _proj(xp, gmix, w_main, w_tail, None, EV_MAIN, tm=proj_tm, tn=PROJ_TN_EVEN)
            pa = _lane_pad_row(gdn_A_log[e])
            pb = _lane_pad_row(gdn_dt_bias[e])
            nw = gdn_norm_w[e].reshape(1, GDN_DV).astype(F32)

            kn = sm[:, EV_K:EV_K + SWA_KV_W]
            vn = sm[:, EV_V:EV_V + SWA_KV_W]
            o_a_s = _swa_sample(sm[:, EV_Q:EV_Q + SWA_Q_W].reshape(nb, SWA_HEADS, SWA_HEAD_DIM),
                                kn.reshape(nb, 1, SWA_KV_W), vn.reshape(nb, 1, SWA_KV_W), cache_k, cache_v,
                                attn_sinks[e].astype(F32).reshape(SWA_HEADS, 1), e)
            ab = jnp.broadcast_to(st[:, :2 * hh, None], (nb, 2 * hh, LANES))
            o_b_s, s_gdn, gc_s = _gdn_sample(
                sm[:, EV_GQKV:EV_GQKV + GDN_CONV_CH].reshape(nb, 3 * hh, GDN_DK),
                state_gdn_conv[e].reshape(nb, CONV_W - 1, 3 * hh, GDN_DK),
                gdn_conv_w[e].reshape(CONV_W, 3 * hh, GDN_DK),
                state_gdn, sm[:, EV_Z:EV_Z + GDN_V_W].reshape(nb, hh, GDN_DV), ab,
                jnp.broadcast_to(gdn_A_log[e].astype(F32)[:, None], (hh, LANES)),
                jnp.broadcast_to(gdn_dt_bias[e].astype(F32)[:, None], (hh, LANES)), nw, e, s_gdn)
            small["sk"].append(kn.reshape(nb, 1, SWA_KV_HEADS, SWA_HEAD_DIM))
            small["sv"].append(vn.reshape(nb, 1, SWA_KV_HEADS, SWA_HEAD_DIM))
            small["sgc"].append(gc_s.reshape(nb, CONV_W - 1, GDN_CONV_CH))
            xs, w_out = _outproj(xs, [o_a_s.reshape(nb, SWA_Q_W).astype(BF16),
                                      o_b_s.reshape(nb, GDN_V_W).astype(BF16)], w_out_even, e, tm=nb, tn=OUT_TN)

            o_a_p = _swa_prompt(pm, attn_sinks[e].astype(F32), bsz, t)
            o_b_p, p_gdn = _gdn_prompt(pm, pt, gdn_conv_w, pa, pb, nw, e, n_even, p_gdn, bsz, t, rows=2 * CHUNK)
            pm3 = pm.reshape(bsz, t, EV_MAIN)
            small["pk"].append(pm3[:, t - WINDOW:, EV_K:EV_K + SWA_KV_W].reshape(bsz, WINDOW, SWA_KV_HEADS, SWA_HEAD_DIM))
            small["pv"].append(pm3[:, t - WINDOW:, EV_V:EV_V + SWA_KV_W].reshape(bsz, WINDOW, SWA_KV_HEADS, SWA_HEAD_DIM))
            small["pgc"].append(pm3[:, t - (CONV_W - 1):, EV_GQKV:EV_GQKV + GDN_CONV_CH])
            xp, = _outproj(xp, [o_a_p, o_b_p], w_out, None, tm=out_tm, tn=OUT_TN)
        else:
            o = l // 2
            sm, st, w_main, w_tail = _proj(xs, gmix, w_in_odd, w_odd_tail, o, OD_MAIN, tm=nb, tn=PROJ_TN_ODD)
            pm, pt = _proj(xp, gmix, w_main, w_tail, None, OD_MAIN, tm=proj_tm, tn=PROJ_TN_ODD)
            pa = _lane_pad_row(ssm_A_log[o])
            pb = _lane_pad_row(ssm_dt_bias[o])
            d_e = jnp.repeat(ssm_D[o].astype(F32), SSM_HEAD_DIM).reshape(1, SSM_D_INNER)
            nw = ssm_norm_w[o].reshape(1, SSM_D_INNER).astype(F32)
            cb = ssm_conv_b[o].reshape(1, SSM_CONV_CH)

            xc, dtv, dec, cnew = _ssd_prep_sample(sm, jnp.swapaxes(state_ssm_conv[o], 0, 1), ssm_conv_w[o], cb,
                                                  st, pa, pb)
            y_s, s_ssm = _ssd_state_sample(dtv, dec, xc, sm, d_e, nw, ex, state_ssm, o, s_ssm)
            small["ssc"].append(jnp.swapaxes(cnew, 0, 1))
            xs, w_out = _outproj(xs, [y_s], w_out_odd, o, tm=nb, tn=OUT_TN)

            y_p, p_ssm = _ssd_prompt(pm, pt, ssm_conv_w[o], cb, pa, pb, d_e, nw, ex, o, n_odd, p_ssm, bsz, t)
            pm3 = pm.reshape(bsz, t, OD_MAIN)
            small["psc"].append(pm3[:, t - (CONV_W - 1):, OD_XBC:OD_XBC + SSM_CONV_CH])
            xp, = _outproj(xp, [y_p], w_out, None, tm=out_tm, tn=OUT_TN)
        xp, xs = ffn_pair(xp, xs, norm_ffn2[l], l, 1, l == depth - 1)

    st_ = {k: jnp.stack(v) for k, v in small.items()}
    return (xp.reshape(bsz, t, d), xs.reshape(nb, 1, d),
            st_["pk"], st_["pv"], p_gdn, st_["pgc"], p_ssm, st_["psc"],
            st_["sk"], st_["sv"], s_gdn, st_["sgc"], s_ssm, st_["ssc"])
```

```python
import functools

import jax
import jax.numpy as jnp
from jax import lax
from jax.experimental import pallas as pl
from jax.experimental.pallas import tpu as pltpu

F32 = jnp.float32
BF16 = jnp.bfloat16

D_MODEL = 2048
SWA_HEADS = 16
SWA_KV_HEADS = 4
SWA_GROUP = SWA_HEADS // SWA_KV_HEADS
SWA_HEAD_DIM = 64
WINDOW = 128
GDN_HEADS = 8
GDN_DK = 128
GDN_DV = 128
CONV_W = 4
SSM_D_INNER = 2 * D_MODEL
SSM_HEAD_DIM = 64
SSM_HEADS = SSM_D_INNER // SSM_HEAD_DIM
SSM_GROUPS = 8
SSM_REP = SSM_HEADS // SSM_GROUPS
SSM_STATE = 128
EPS = 1e-6
NEG_INF = -1e30

SWA_Q_W = SWA_HEADS * SWA_HEAD_DIM
SWA_KV_W = SWA_KV_HEADS * SWA_HEAD_DIM
GDN_QK_W = GDN_HEADS * GDN_DK
GDN_V_W = GDN_HEADS * GDN_DV
GDN_CONV_CH = 2 * GDN_QK_W + GDN_V_W
GDN_HALF_W = GDN_QK_W // 2
SSM_BC_W = SSM_GROUPS * SSM_STATE
SSM_CONV_CH = SSM_D_INNER + 2 * SSM_BC_W
SSM_GROUP_W = SSM_REP * SSM_HEAD_DIM
SSD_SEC_W = SSM_CONV_CH // 3

EV_Q = 0
EV_K = EV_Q + SWA_Q_W
EV_V = EV_K + SWA_KV_W
EV_GQKV = EV_V + SWA_KV_W
EV_Z = EV_GQKV + GDN_CONV_CH
EV_MAIN = EV_Z + GDN_V_W
OD_XBC = SSM_D_INNER
OD_MAIN = SSM_D_INNER + SSM_CONV_CH

LANES = 128
SUBLANES = 8
CHUNK = 128
VMEM_LIMIT = 56 * 1024 * 1024

FFN_TM, FFN_TF = 512, 512
PROJ_EVEN_TM, PROJ_EVEN_TN = 512, EV_MAIN // 2
PROJ_ODD_TM, PROJ_ODD_TN = 1024, 1024
OUT_EVEN_TM, OUT_EVEN_TN = 512, D_MODEL
OUT_ODD_TM, OUT_ODD_TN = 1024, 1024
SAMPLE_TN_EVEN, SAMPLE_TN_ODD, SAMPLE_OUT_TN = 512, 1024, 512


def _params(sem):
    return pltpu.CompilerParams(dimension_semantics=sem, vmem_limit_bytes=VMEM_LIMIT)


def _tile(m, pref):
    return pref if m % pref == 0 else m


def _sigmoid(x):
    return 1.0 / (1.0 + jnp.exp(-x))


def _silu(x):
    return x * _sigmoid(x)


def _softplus(x):
    return jnp.maximum(x, 0.0) + jnp.log1p(jnp.exp(-jnp.abs(x)))


def _dot(a, b):
    return jnp.dot(a.astype(BF16), b.astype(BF16), preferred_element_type=F32)


def _dot_nt(a, b):
    return lax.dot_general(a.astype(BF16), b.astype(BF16), (((1,), (1,)), ((), ())),
                           preferred_element_type=F32)


def _bdot(a, b):
    return jnp.einsum("bij,bjk->bik", a.astype(BF16), b.astype(BF16), preferred_element_type=F32)


def _bdot_nt(a, b):
    return jnp.einsum("bid,bjd->bij", a.astype(BF16), b.astype(BF16), preferred_element_type=F32)


def _split2(a):
    hi = a.astype(BF16)
    lo = (a - hi.astype(F32)).astype(BF16)
    return hi, lo


def _split3(a):
    hi = a.astype(BF16)
    r = a - hi.astype(F32)
    mid = r.astype(BF16)
    lo = (r - mid.astype(F32)).astype(BF16)
    return hi, mid, lo


def _sel_dot(m01, parts):
    out = jnp.dot(m01, parts[0], preferred_element_type=F32)
    for p in parts[1:]:
        out = out + jnp.dot(m01, p, preferred_element_type=F32)
    return out


def _dot_sel(parts, m01):
    out = jnp.dot(parts[0], m01, preferred_element_type=F32)
    for p in parts[1:]:
        out = out + jnp.dot(p, m01, preferred_element_type=F32)
    return out


def _rms_rows(x, gain):
    ms = jnp.mean(x * x, axis=-1, keepdims=True)
    return x * lax.rsqrt(ms + EPS) * gain


def _iota2(shape, dim):
    return lax.broadcasted_iota(jnp.int32, shape, dim)


def _drop_ref(kern, pos):
    def entry(*refs):
        return kern(*refs[:pos], *refs[pos + 1:])
    return entry


def _alias_prev(prev, n_in, out_idx):
    if prev is None:
        return [], [], {}
    return [pl.BlockSpec(memory_space=pl.ANY)], [prev], {n_in: out_idx}


def _ffn_kernel(x_ref, g_ref, gf_ref, wg_ref, wu_ref, wd_ref, *rest, final_norm, cast):
    if cast:
        o_ref, wgb_ref, wub_ref, wdb_ref, h_ref = rest
    else:
        o_ref, h_ref = rest
    j = pl.program_id(1)

    @pl.when(j == 0)
    def _():
        h_ref[...] = _rms_rows(x_ref[...], g_ref[...]).astype(BF16)
        o_ref[...] = jnp.zeros_like(o_ref)

    wg, wu, wd = wg_ref[...], wu_ref[...], wd_ref[...]
    if cast:
        wg, wu, wd = wg.astype(BF16), wu.astype(BF16), wd.astype(BF16)
        wgb_ref[...] = wg
        wub_ref[...] = wu
        wdb_ref[...] = wd
    h = h_ref[...]
    a = jnp.dot(h, wg, preferred_element_type=F32)
    b = jnp.dot(h, wu, preferred_element_type=F32)
    t = (_silu(a) * b).astype(BF16)
    o_ref[...] += jnp.dot(t, wd, preferred_element_type=F32)

    @pl.when(j == pl.num_programs(1) - 1)
    def _():
        y = x_ref[...] + 0.5 * o_ref[...]
        if final_norm:
            y = _rms_rows(y, gf_ref[...])
        o_ref[...] = y


def _ffn(x, gain, gain_final, wg, wu, wd, sel, *, final_norm, tm, tf):
    m, d = x.shape
    f = wg.shape[-1]
    cast = sel is not None
    if cast:
        assert m == tm
        layer, which = sel
        w_specs = [pl.BlockSpec((None, None, d, tf), lambda i, j: (layer, which, 0, j)),
                   pl.BlockSpec((None, None, d, tf), lambda i, j: (layer, which, 0, j)),
                   pl.BlockSpec((None, None, tf, d), lambda i, j: (layer, which, j, 0))]
    else:
        w_specs = [pl.BlockSpec((d, tf), lambda i, j: (0, j)),
                   pl.BlockSpec((d, tf), lambda i, j: (0, j)),
                   pl.BlockSpec((tf, d), lambda i, j: (j, 0))]
    out_specs = [pl.BlockSpec((tm, d), lambda i, j: (i, 0))]
    out_shape = [jax.ShapeDtypeStruct((m, d), F32)]
    if cast:
        out_specs += [pl.BlockSpec((d, tf), lambda i, j: (0, j)),
                      pl.BlockSpec((d, tf), lambda i, j: (0, j)),
                      pl.BlockSpec((tf, d), lambda i, j: (j, 0))]
        out_shape += [jax.ShapeDtypeStruct((d, f), BF16), jax.ShapeDtypeStruct((d, f), BF16),
                      jax.ShapeDtypeStruct((f, d), BF16)]
    return pl.pallas_call(
        functools.partial(_ffn_kernel, final_norm=final_norm, cast=cast),
        grid=(m // tm, f // tf),
        in_specs=[
            pl.BlockSpec((tm, d), lambda i, j: (i, 0)),
            pl.BlockSpec((1, d), lambda i, j: (0, 0)),
            pl.BlockSpec((1, d), lambda i, j: (0, 0)),
        ] + w_specs,
        out_specs=out_specs,
        out_shape=out_shape,
        scratch_shapes=[pltpu.VMEM((tm, d), BF16)],
        compiler_params=_params(("parallel", "arbitrary")),
        name="ffn_half",
    )(x, gain, gain_final, wg, wu, wd)


def _proj_kernel(x_ref, g_ref, w_ref, wt_ref, *rest, cast):
    if cast:
        o_ref, ot_ref, wb_ref, wtb_ref, h_ref = rest
    else:
        o_ref, ot_ref, h_ref = rest
    j = pl.program_id(1)

    @pl.when(j == 0)
    def _():
        h = _rms_rows(x_ref[...], g_ref[...]).astype(BF16)
        h_ref[...] = h
        wt = wt_ref[...]
        if cast:
            pad = jnp.zeros((LANES - wt.shape[0], wt.shape[1]), F32)
            wt = jnp.transpose(jnp.concatenate([wt, pad], axis=0)).astype(BF16)
            wtb_ref[...] = wt
        ot_ref[...] = jnp.dot(h, wt, preferred_element_type=F32)

    w = w_ref[...]
    if cast:
        w = jnp.transpose(w).astype(BF16)
        wb_ref[...] = w
    o_ref[...] = jnp.dot(h_ref[...], w, preferred_element_type=F32)


def _proj(x, gain, w_main, w_tail, layer, n, *, tm, tn):
    m, d = x.shape
    cast = layer is not None
    if cast:
        n_tail = w_main.shape[1] - n
        assert m == tm and n % n_tail == 0 and n_tail % SUBLANES == 0 and n_tail <= LANES
        w_specs = [pl.BlockSpec((None, tn, d), lambda i, j: (layer, j, 0)),
                   pl.BlockSpec((None, n_tail, d), lambda i, j: (layer, n // n_tail, 0))]
    else:
        w_specs = [pl.BlockSpec((d, tn), lambda i, j: (0, j)),
                   pl.BlockSpec((d, LANES), lambda i, j: (0, 0))]
    out_specs = [pl.BlockSpec((tm, tn), lambda i, j: (i, j)),
                 pl.BlockSpec((tm, LANES), lambda i, j: (i, 0))]
    out_shape = [jax.ShapeDtypeStruct((m, n), F32), jax.ShapeDtypeStruct((m, LANES), F32)]
    if cast:
        out_specs += [pl.BlockSpec((d, tn), lambda i, j: (0, j)),
                      pl.BlockSpec((d, LANES), lambda i, j: (0, 0))]
        out_shape += [jax.ShapeDtypeStruct((d, n), BF16), jax.ShapeDtypeStruct((d, LANES), BF16)]
    return pl.pallas_call(
        functools.partial(_proj_kernel, cast=cast),
        grid=(m // tm, n // tn),
        in_specs=[
            pl.BlockSpec((tm, d), lambda i, j: (i, 0)),
            pl.BlockSpec((1, d), lambda i, j: (0, 0)),
        ] + w_specs,
        out_specs=out_specs,
        out_shape=out_shape,
        scratch_shapes=[pltpu.VMEM((tm, d), BF16)],
        compiler_params=_params(("parallel", "arbitrary")),
        name="norm_proj",
    )(x, gain, w_main, w_tail)


def _outproj_kernel(*refs, n_in, cast):
    x_ref = refs[0]
    o_refs = refs[1:1 + n_in]
    w_refs = refs[1 + n_in:1 + 2 * n_in]
    out_ref = refs[1 + 2 * n_in]
    acc = x_ref[...]
    for kb, (o_ref, w_ref) in enumerate(zip(o_refs, w_refs)):
        w = w_ref[...]
        if cast:
            w = w.astype(BF16)
            kw = w.shape[0]
            refs[2 + 2 * n_in][kb * kw:(kb + 1) * kw, :] = w
        acc = acc + jnp.dot(o_ref[...], w, preferred_element_type=F32)
    out_ref[...] = acc


def _outproj(x, outs, w, layer, *, tm, tn):
    m, d = x.shape
    n_in = len(outs)
    kw = outs[0].shape[1]
    k_all = kw * n_in
    assert all(o.shape[1] == kw for o in outs) and k_all == w.shape[-2]
    cast = layer is not None
    in_specs = [pl.BlockSpec((tm, tn), lambda i, j: (i, j))]
    for o in outs:
        in_specs.append(pl.BlockSpec((tm, kw), lambda i, j: (i, 0)))
    for kb in range(n_in):
        if cast:
            in_specs.append(pl.BlockSpec((None, kw, tn), lambda i, j, kb=kb: (layer, kb, j)))
        else:
            in_specs.append(pl.BlockSpec((kw, tn), lambda i, j, kb=kb: (kb, j)))
    out_specs = [pl.BlockSpec((tm, tn), lambda i, j: (i, j))]
    out_shape = [jax.ShapeDtypeStruct((m, d), F32)]
    if cast:
        assert m == tm
        out_specs.append(pl.BlockSpec((k_all, tn), lambda i, j: (0, j)))
        out_shape.append(jax.ShapeDtypeStruct((k_all, d), BF16))
    return pl.pallas_call(
        functools.partial(_outproj_kernel, n_in=n_in, cast=cast),
        grid=(m // tm, d // tn),
        in_specs=in_specs,
        out_specs=out_specs,
        out_shape=out_shape,
        compiler_params=_params(("parallel", "arbitrary")),
        name="out_proj",
    )(x, *outs, *([w] * n_in))


def _swa_prompt_kernel(sink_ref, q_ref, kc_ref, kp_ref, vc_ref, vp_ref, o_ref):
    n = pl.program_id(1)
    w = WINDOW
    qi = _iota2((w, 2 * w), 0)
    kj = _iota2((w, 2 * w), 1)
    valid = (kj > qi) & (kj <= qi + w) & ((n > 0) | (kj >= w))
    scale = SWA_HEAD_DIM ** -0.5
    kk = jnp.concatenate([kp_ref[...], kc_ref[...]], axis=0).astype(BF16)
    vv = jnp.concatenate([vp_ref[...], vc_ref[...]], axis=0).astype(BF16)
    q = q_ref[...].astype(BF16)
    outs = []
    for h in range(SWA_HEADS):
        c = h // SWA_GROUP
        qh = q[:, h * SWA_HEAD_DIM:(h + 1) * SWA_HEAD_DIM]
        kh = kk[:, c * SWA_HEAD_DIM:(c + 1) * SWA_HEAD_DIM]
        vh = vv[:, c * SWA_HEAD_DIM:(c + 1) * SWA_HEAD_DIM]
        s = lax.dot_general(qh, kh, (((1,), (1,)), ((), ())), preferred_element_type=F32) * scale
        s = jnp.where(valid, s, NEG_INF)
        sink = sink_ref[h]
        mx = jnp.maximum(jnp.max(s, axis=-1, keepdims=True), sink)
        p = jnp.exp(s - mx)
        den = jnp.sum(p, axis=-1, keepdims=True) + jnp.exp(sink - mx)
        p = p / den
        outs.append(jnp.dot(p.astype(BF16), vh, preferred_element_type=F32))
    o_ref[...] = jnp.concatenate(outs, axis=-1).astype(o_ref.dtype)


def _swa_prompt(p_main, sinks, bsz, t):
    nb = t // WINDOW
    kblk = EV_K // SWA_KV_W
    vblk = EV_V // SWA_KV_W

    def cur(col):
        return lambda b, n: (b * nb + n, col)

    def prev(col):
        return lambda b, n: (b * nb + jnp.maximum(n - 1, 0), col)

    return pl.pallas_call(
        _swa_prompt_kernel,
        grid=(bsz, nb),
        in_specs=[
            pl.BlockSpec(memory_space=pltpu.SMEM),
            pl.BlockSpec((WINDOW, SWA_Q_W), cur(EV_Q // SWA_Q_W)),
            pl.BlockSpec((WINDOW, SWA_KV_W), cur(kblk)),
            pl.BlockSpec((WINDOW, SWA_KV_W), prev(kblk)),
            pl.BlockSpec((WINDOW, SWA_KV_W), cur(vblk)),
            pl.BlockSpec((WINDOW, SWA_KV_W), prev(vblk)),
        ],
        out_specs=pl.BlockSpec((WINDOW, SWA_Q_W), cur(0)),
        out_shape=jax.ShapeDtypeStruct((bsz * t, SWA_Q_W), BF16),
        compiler_params=_params(("parallel", "arbitrary")),
        name="swa_prompt",
    )(sinks, p_main, p_main, p_main, p_main, p_main)


def _unit_lower_inverse(a, ri, ci):
    a0 = jnp.where((ri >> 4) == (ci >> 4), a, 0.0)
    y = -a0
    p = _bdot(a0, a0)
    for step in range(3):
        y = y + p + _bdot(y, p)
        if step < 2:
            p = _bdot(p, p)
    for sh in (4, 5, 6):
        e = jnp.where(((ri >> (sh + 1)) == (ci >> (sh + 1))) & ((ri >> sh) != (ci >> sh)), a, 0.0)
        t = e + _bdot(y, e)
        y = y - t - _bdot(t, y)
    return y


def _conv_section(ext_ref, carry_ref, sec, u_ref, w, bias, rows):
    ext_ref[0:SUBLANES, :] = carry_ref[sec]
    ext_ref[SUBLANES:SUBLANES + rows, :] = u_ref[...]
    acc = ext_ref[SUBLANES:SUBLANES + rows, :] * w[CONV_W - 1:CONV_W, :]
    for j in range(CONV_W - 1):
        off = SUBLANES - (CONV_W - 1) + j
        acc = acc + ext_ref[off:off + rows, :] * w[j:j + 1, :]
    if bias is not None:
        acc = acc + bias
    carry_ref[sec] = u_ref[rows - SUBLANES:rows, :]
    return _silu(acc)


def _gdn_prompt_kernel(uq0, uq1, uk0, uk1, uv0, uv1, z0_ref, z1_ref, tail_ref, cw_ref, pa_ref, pb_ref, nw_ref,
                       tri_ref, ones_ref,
                       o_ref, sout_ref,
                       s_ref, carry_ref, ext_ref, cq_ref, ck_ref, cv_ref, *, rows):
    i = pl.program_id(1)
    nchunk = rows // CHUNK
    hh = GDN_HEADS
    hw = GDN_HALF_W

    @pl.when(i == 0)
    def _():
        s_ref[...] = jnp.zeros_like(s_ref)
        carry_ref[...] = jnp.zeros_like(carry_ref)

    for sec, (halves, c_ref) in enumerate((((uq0, uq1), cq_ref), ((uk0, uk1), ck_ref), ((uv0, uv1), cv_ref))):
        for half, u_ref in enumerate(halves):
            ws = slice(sec * GDN_QK_W + half * hw, sec * GDN_QK_W + (half + 1) * hw)
            c_ref[:, half * hw:(half + 1) * hw] = _conv_section(ext_ref, carry_ref, 2 * sec + half, u_ref,
                                                                 cw_ref[:, ws], None, rows)

    t = tail_ref[...]
    g_all = -jnp.exp(pa_ref[...]) * _softplus(t + pb_ref[...])
    beta_all = _sigmoid(t)
    g_parts = _split3(g_all)
    gcum_all = _sel_dot(tri_ref[...], g_parts)
    gtot_all = _sel_dot(ones_ref[...], g_parts)
    gcum_t = jnp.transpose(gcum_all)

    ri = _iota2((1, CHUNK, CHUNK), 1)
    ci = _iota2((1, CHUNK, CHUNK), 2)
    incl = ri >= ci
    strict = ri > ci

    def rs(c):
        return slice(c * CHUNK, (c + 1) * CHUNK)

    def hs(h):
        return slice(h * GDN_DK, (h + 1) * GDN_DK)

    def stack(fn):
        return jnp.stack([fn(c, h) for c in range(nchunk) for h in range(hh)], axis=0)

    q = stack(lambda c, h: cq_ref[rs(c), hs(h)])
    k = stack(lambda c, h: ck_ref[rs(c), hs(h)])
    v = stack(lambda c, h: cv_ref[rs(c), hs(h)])
    gc = stack(lambda c, h: gcum_all[rs(c), h:h + 1])
    gt = stack(lambda c, h: gtot_all[rs(c), h:h + 1])
    bc = stack(lambda c, h: beta_all[rs(c), hh + h:hh + h + 1])
    gr = stack(lambda c, h: gcum_t[h:h + 1, rs(c)])

    q = q * lax.rsqrt(jnp.sum(q * q, axis=-1, keepdims=True) + EPS) * (GDN_DK ** -0.5)
    k = k * lax.rsqrt(jnp.sum(k * k, axis=-1, keepdims=True) + EPS)
    decay = jnp.exp(jnp.where(incl, gc - gr, NEG_INF))
    kb = k * bc
    a = jnp.where(strict, _bdot_nt(kb, k) * decay, 0.0)
    y = _unit_lower_inverse(a, ri, ci)
    eg = jnp.exp(gc)
    rhs = jnp.concatenate([v * bc, kb * eg], axis=2)
    sol = rhs + _bdot(y, rhs)
    u = sol[:, :, :GDN_DV]
    w = sol[:, :, GDN_DV:]
    wq = jnp.concatenate([w, q * eg], axis=1)
    attn = _bdot_nt(q, k) * decay
    kdt = jnp.swapaxes(k * jnp.exp(gt - gc), 1, 2)
    egt = jnp.exp(gt)

    for c in range(nchunk):
        bs = slice(c * hh, (c + 1) * hh)
        s = s_ref[...]
        ws_qs = _bdot(wq[bs], s)
        v_new = u[bs] - ws_qs[:, :CHUNK]
        o = ws_qs[:, CHUNK:] + _bdot(attn[bs], v_new)
        s_ref[...] = s * egt[bs] + _bdot(kdt[bs], v_new)
        for h in range(hh):
            z_ref = z0_ref if h < hh // 2 else z1_ref
            zc = z_ref[rs(c), hs(h % (hh // 2))]
            og = _rms_rows(o[h], nw_ref[...]) * _silu(zc)
            o_ref[rs(c), hs(h)] = og.astype(o_ref.dtype)

    @pl.when(i == pl.num_programs(1) - 1)
    def _():
        sout_ref[0] = s_ref[...]


def _chunk_masks(rows):
    r = jnp.arange(rows)
    same = (r[:, None] // CHUNK) == (r[None, :] // CHUNK)
    tri = (same & (r[:, None] >= r[None, :])).astype(BF16)
    ones = same.astype(BF16)
    return tri, ones


def _gdn_prompt(p_main, p_tail, conv_w, pa, pb, norm_w, layer, n_layers, prev, bsz, t, *, rows):
    nblk = t // rows
    hw = GDN_HALF_W
    hh = GDN_HEADS
    tri, ones = _chunk_masks(rows)

    def col(base):
        return lambda b, i: (b * nblk + i, base // hw)

    const = lambda b, i: (0, 0)
    in_specs = [pl.BlockSpec((rows, hw), col(EV_GQKV + k * hw)) for k in range(6)]
    in_specs += [pl.BlockSpec((rows, hw), col(EV_Z)), pl.BlockSpec((rows, hw), col(EV_Z + hw))]
    in_specs += [
        pl.BlockSpec((rows, LANES), lambda b, i: (b * nblk + i, 0)),
        pl.BlockSpec((None, CONV_W, GDN_CONV_CH), lambda b, i: (layer, 0, 0)),
        pl.BlockSpec((1, LANES), const),
        pl.BlockSpec((1, LANES), const),
        pl.BlockSpec((1, GDN_DV), const),
        pl.BlockSpec((rows, rows), const),
        pl.BlockSpec((rows, rows), const),
    ]
    args = [p_main] * 8 + [p_tail, conv_w, pa, pb, norm_w, tri, ones]
    extra_specs, extra_args, aliases = _alias_prev(prev, len(args), 1)
    kern = functools.partial(_gdn_prompt_kernel, rows=rows)
    if prev is not None:
        kern = _drop_ref(kern, len(args))
    return pl.pallas_call(
        kern,
        grid=(bsz, nblk),
        in_specs=in_specs + extra_specs,
        out_specs=[
            pl.BlockSpec((rows, GDN_V_W), lambda b, i: (b * nblk + i, 0)),
            pl.BlockSpec((None, 1, hh, GDN_DK, GDN_DV), lambda b, i: (layer, b, 0, 0, 0)),
        ],
        out_shape=[
            jax.ShapeDtypeStruct((bsz * t, GDN_V_W), BF16),
            jax.ShapeDtypeStruct((n_layers, bsz, hh, GDN_DK, GDN_DV), F32),
        ],
        scratch_shapes=[
            pltpu.VMEM((hh, GDN_DK, GDN_DV), F32),
            pltpu.VMEM((6, SUBLANES, hw), F32),
            pltpu.VMEM((rows + SUBLANES, hw), F32),
            pltpu.VMEM((rows, GDN_QK_W), F32),
            pltpu.VMEM((rows, GDN_QK_W), F32),
            pltpu.VMEM((rows, GDN_V_W), F32),
        ],
        input_output_aliases=aliases,
        compiler_params=_params(("parallel", "arbitrary")),
        name="gdn_prompt",
    )(*args, *extra_args)


def _ssd_prompt_kernel(z_ref, xa_ref, xb_ref, xc3_ref, tail_ref, cw_ref, cb_ref, pa_ref, pb_ref, de_ref, nw_ref,
                       ex_ref, tri_ref,
                       y_ref, hout_ref,
                       ht_ref, carry_ref, ext_ref, xc_ref, cumt_ref, dtt_ref):
    i = pl.program_id(1)
    rows = CHUNK

    @pl.when(i == 0)
    def _():
        ht_ref[...] = jnp.zeros_like(ht_ref)
        carry_ref[...] = jnp.zeros_like(carry_ref)

    for sec, u_ref in enumerate((xa_ref, xb_ref, xc3_ref)):
        ss = slice(sec * SSD_SEC_W, (sec + 1) * SSD_SEC_W)
        xc_ref[:, ss] = _conv_section(ext_ref, carry_ref, sec, u_ref, cw_ref[:, ss], cb_ref[:, ss], rows)

    dt = _softplus(tail_ref[...] + pb_ref[...])
    da = dt * (-jnp.exp(pa_ref[...]))
    cum = _sel_dot(tri_ref[...], _split3(da))
    cumt_ref[...] = jnp.transpose(cum)
    dtt_ref[...] = jnp.transpose(dt)
    cum_parts = _split3(cum)
    dt_parts = _split3(dt)

    ri = _iota2((rows, rows), 0)
    ci = _iota2((rows, rows), 1)
    tri = ri >= ci
    lane = _iota2((rows, LANES), 1)
    lo_half = lane < SSM_HEAD_DIM

    for g in range(SSM_GROUPS):
        gs = slice(g * SSM_GROUP_W, (g + 1) * SSM_GROUP_W)
        bmat = xc_ref[:, SSM_D_INNER + g * SSM_STATE:SSM_D_INNER + (g + 1) * SSM_STATE]
        cmat = xc_ref[:, SSM_D_INNER + SSM_BC_W + g * SSM_STATE:SSM_D_INNER + SSM_BC_W + (g + 1) * SSM_STATE]
        xg = xc_ref[:, gs]
        cb = _dot_nt(cmat, bmat)
        exg = ex_ref[:, gs]
        cum_e = _dot_sel(cum_parts, exg)
        dt_e = _dot_sel(dt_parts, exg)
        htg = ht_ref[g]
        y = _dot(cmat, htg) * jnp.exp(cum_e)
        pieces = []
        for pr in range(SSM_REP // 2):
            scs = []
            for half in range(2):
                h = g * SSM_REP + 2 * pr + half
                ccol = cum[:, h:h + 1]
                crow = cumt_ref[h:h + 1, :]
                drow = dtt_ref[h:h + 1, :]
                lmat = jnp.exp(jnp.where(tri, ccol - crow, NEG_INF))
                scs.append((cb * lmat * drow).astype(BF16))
            xp = xg[:, pr * LANES:(pr + 1) * LANES]
            rhs = jnp.concatenate([jnp.where(lo_half, xp, 0.0), jnp.where(lo_half, 0.0, xp)], axis=0)
            pieces.append(jnp.dot(jnp.concatenate(scs, axis=1), rhs.astype(BF16),
                                  preferred_element_type=F32))
        y = y + jnp.concatenate(pieces, axis=1) + de_ref[:, gs] * xg
        last_e = cum_e[rows - 1:rows, :]
        wts = jnp.exp(last_e - cum_e) * dt_e
        ht_ref[g] = htg * jnp.exp(last_e) + _dot(jnp.transpose(bmat), xg * wts)
        zg = z_ref[:, gs]
        yz = y * _silu(zg)
        y_ref[:, gs] = _rms_rows(yz, nw_ref[:, gs]).astype(y_ref.dtype)

    @pl.when(i == pl.num_programs(1) - 1)
    def _():
        for g in range(SSM_GROUPS):
            hg = jnp.transpose(ht_ref[g])
            hout_ref[0, g * SSM_REP:(g + 1) * SSM_REP] = hg.reshape(SSM_REP, SSM_HEAD_DIM, SSM_STATE)


def _head_expand():
    h = jnp.arange(LANES)[:, None]
    c = jnp.arange(SSM_D_INNER)[None, :]
    return ((c // SSM_HEAD_DIM) == h).astype(BF16)


def _ssd_prompt(p_main, p_tail, conv_w, conv_b, pa, pb, d_e, norm_w, ex, layer, n_layers, prev, bsz, t):
    rows = CHUNK
    nblk = t // rows
    tri, _ = _chunk_masks(rows)
    const = lambda b, i: (0, 0)
    row = lambda b, i: (b * nblk + i, 0)
    args = [p_main, p_main, p_main, p_main, p_tail, conv_w, conv_b, pa, pb, d_e, norm_w, ex, tri]
    extra_specs, extra_args, aliases = _alias_prev(prev, len(args), 1)
    kern = _ssd_prompt_kernel if prev is None else _drop_ref(_ssd_prompt_kernel, len(args))
    return pl.pallas_call(
        kern,
        grid=(bsz, nblk),
        input_output_aliases=aliases,
        in_specs=[
            pl.BlockSpec((rows, SSM_D_INNER), row),
            pl.BlockSpec((rows, SSD_SEC_W), lambda b, i: (b * nblk + i, OD_XBC // SSD_SEC_W)),
            pl.BlockSpec((rows, SSD_SEC_W), lambda b, i: (b * nblk + i, OD_XBC // SSD_SEC_W + 1)),
            pl.BlockSpec((rows, SSD_SEC_W), lambda b, i: (b * nblk + i, OD_XBC // SSD_SEC_W + 2)),
            pl.BlockSpec((rows, LANES), row),
            pl.BlockSpec((CONV_W, SSM_CONV_CH), const),
            pl.BlockSpec((1, SSM_CONV_CH), const),
            pl.BlockSpec((1, LANES), const),
            pl.BlockSpec((1, LANES), const),
            pl.BlockSpec((1, SSM_D_INNER), const),
            pl.BlockSpec((1, SSM_D_INNER), const),
            pl.BlockSpec((LANES, SSM_D_INNER), const),
            pl.BlockSpec((rows, rows), const),
        ] + extra_specs,
        out_specs=[
            pl.BlockSpec((rows, SSM_D_INNER), row),
            pl.BlockSpec((None, 1, SSM_HEADS, SSM_HEAD_DIM, SSM_STATE), lambda b, i: (layer, b, 0, 0, 0)),
        ],
        out_shape=[
            jax.ShapeDtypeStruct((bsz * t, SSM_D_INNER), BF16),
            jax.ShapeDtypeStruct((n_layers, bsz, SSM_HEADS, SSM_HEAD_DIM, SSM_STATE), F32),
        ],
        scratch_shapes=[
            pltpu.VMEM((SSM_GROUPS, SSM_STATE, SSM_GROUP_W), F32),
            pltpu.VMEM((3, SUBLANES, SSD_SEC_W), F32),
            pltpu.VMEM((rows + SUBLANES, SSD_SEC_W), F32),
            pltpu.VMEM((rows, SSM_CONV_CH), F32),
            pltpu.VMEM((LANES, rows), F32),
            pltpu.VMEM((LANES, rows), F32),
        ],
        compiler_params=_params(("parallel", "arbitrary")),
        name="ssd_prompt",
    )(*args, *extra_args)


def _swa_sample_kernel(sink_ref, q_ref, kn_ref, vn_ref, ck_ref, cv_ref, o_ref):
    scale = SWA_HEAD_DIM ** -0.5
    q = q_ref[0].astype(BF16)
    kn = kn_ref[0].astype(BF16).astype(F32)
    vn = vn_ref[0].astype(BF16).astype(F32)
    ck = ck_ref[0].astype(BF16)
    cv = cv_ref[0].astype(BF16)
    sink = sink_ref[...]
    valid = _iota2((SWA_GROUP, WINDOW), 1) >= 1
    outs = []
    for c in range(SWA_KV_HEADS):
        cs = slice(c * SWA_HEAD_DIM, (c + 1) * SWA_HEAD_DIM)
        gs = slice(c * SWA_GROUP, (c + 1) * SWA_GROUP)
        qc = q[gs, :]
        s = lax.dot_general(qc, ck[:, cs], (((1,), (1,)), ((), ())), preferred_element_type=F32) * scale
        s = jnp.where(valid, s, NEG_INF)
        sn = jnp.sum(qc.astype(F32) * kn[:, cs], axis=-1, keepdims=True) * scale
        sk = sink[gs, :]
        mx = jnp.maximum(jnp.maximum(jnp.max(s, axis=-1, keepdims=True), sn), sk)
        p = jnp.exp(s - mx)
        pn = jnp.exp(sn - mx)
        den = jnp.sum(p, axis=-1, keepdims=True) + pn + jnp.exp(sk - mx)
        o = jnp.dot(p.astype(BF16), cv[:, cs], preferred_element_type=F32) + pn * vn[:, cs]
        outs.append(o / den)
    o_ref[0] = jnp.concatenate(outs, axis=0).astype(o_ref.dtype)


def _swa_sample(q, kn, vn, ck_all, cv_all, sinks, layer):
    nb = q.shape[0]
    kvw = SWA_KV_W
    b3 = lambda b: (b, 0, 0)
    cache = pl.BlockSpec((None, 1, WINDOW, kvw), lambda b: (layer, b, 0, 0))
    return pl.pallas_call(
        _swa_sample_kernel,
        grid=(nb,),
        in_specs=[
            pl.BlockSpec((SWA_HEADS, 1), lambda b: (0, 0)),
            pl.BlockSpec((1, SWA_HEADS, SWA_HEAD_DIM), b3),
            pl.BlockSpec((1, 1, kvw), b3),
            pl.BlockSpec((1, 1, kvw), b3),
            cache,
            cache,
        ],
        out_specs=pl.BlockSpec((1, SWA_HEADS, SWA_HEAD_DIM), b3),
        out_shape=jax.ShapeDtypeStruct((nb, SWA_HEADS, SWA_HEAD_DIM), F32),
        compiler_params=_params(("parallel",)),
        name="swa_sample",
    )(sinks, q, kn, vn, ck_all, cv_all)


def _gdn_sample_kernel(u_ref, prev_ref, w_ref, s_ref, z_ref, ab_ref, pa_ref, pb_ref, nw_ref,
                       o_ref, sout_ref, cnew_ref):
    hh = GDN_HEADS
    u = u_ref[0]
    pv = prev_ref[0]
    w = w_ref[...]
    x = u * w[CONV_W - 1]
    for j in range(CONV_W - 1):
        x = x + pv[j] * w[j]
        cnew_ref[0, j] = pv[j + 1] if j + 1 < CONV_W - 1 else u
    x = _silu(x)
    q = x[0:hh]
    k = x[hh:2 * hh]
    v = x[2 * hh:3 * hh]
    q = q * lax.rsqrt(jnp.sum(q * q, axis=-1, keepdims=True) + EPS) * (GDN_DK ** -0.5)
    k = k * lax.rsqrt(jnp.sum(k * k, axis=-1, keepdims=True) + EPS)
    ab = ab_ref[0]
    dec = jnp.exp(-jnp.exp(pa_ref[...]) * _softplus(ab[0:hh] + pb_ref[...]))
    beta = _sigmoid(ab[hh:2 * hh])
    qk = jnp.concatenate([q, k, jnp.zeros((LANES - 2 * hh, GDN_DK), F32)], axis=0)
    qkt = jnp.transpose(qk)
    outs = []
    for h in range(hh):
        s = s_ref[0, h] * dec[h:h + 1, :]
        kcol = qkt[:, hh + h:hh + h + 1]
        ks = jnp.sum(s * kcol, axis=0, keepdims=True)
        delta = beta[h:h + 1, :] * (v[h:h + 1, :] - ks)
        s = s + kcol * delta
        sout_ref[0, h] = s
        outs.append(jnp.sum(s * qkt[:, h:h + 1], axis=0, keepdims=True))
    o = jnp.concatenate(outs, axis=0)
    o_ref[0] = _rms_rows(o, nw_ref[...]) * _silu(z_ref[0])


def _gdn_sample(u, prev, w, s_all, z, ab, pa, pb, nw, layer, prev_out):
    nb = u.shape[0]
    hh = GDN_HEADS
    n_layers = s_all.shape[0]
    b3 = lambda b: (b, 0, 0)
    b4 = lambda b: (b, 0, 0, 0)
    c2 = lambda b: (0, 0)
    state_spec = pl.BlockSpec((None, 1, hh, GDN_DK, GDN_DV), lambda b: (layer, b, 0, 0, 0))
    args = [u, prev, w, s_all, z, ab, pa, pb, nw]
    extra_specs, extra_args, aliases = _alias_prev(prev_out, len(args), 1)
    kern = _gdn_sample_kernel if prev_out is None else _drop_ref(_gdn_sample_kernel, len(args))
    return pl.pallas_call(
        kern,
        grid=(nb,),
        input_output_aliases=aliases,
        in_specs=[
            pl.BlockSpec((1, 3 * hh, GDN_DK), b3),
            pl.BlockSpec((1, CONV_W - 1, 3 * hh, GDN_DK), b4),
            pl.BlockSpec((CONV_W, 3 * hh, GDN_DK), lambda b: (0, 0, 0)),
            state_spec,
            pl.BlockSpec((1, hh, GDN_DV), b3),
            pl.BlockSpec((1, 2 * hh, LANES), b3),
            pl.BlockSpec((hh, LANES), c2),
            pl.BlockSpec((hh, LANES), c2),
            pl.BlockSpec((1, GDN_DV), c2),
        ] + extra_specs,
        out_specs=[
            pl.BlockSpec((1, hh, GDN_DV), b3),
            state_spec,
            pl.BlockSpec((1, CONV_W - 1, 3 * hh, GDN_DK), b4),
        ],
        out_shape=[
            jax.ShapeDtypeStruct((nb, hh, GDN_DV), F32),
            jax.ShapeDtypeStruct((n_layers, nb, hh, GDN_DK, GDN_DV), F32),
            jax.ShapeDtypeStruct((nb, CONV_W - 1, 3 * hh, GDN_DK), F32),
        ],
        compiler_params=_params(("parallel",)),
        name="gdn_sample",
    )(*args, *extra_args)


def _ssd_prep_kernel(x_ref, prev_ref, cw_ref, cb_ref, tail_ref, pa_ref, pb_ref,
                     xc_ref, dt_ref, dec_ref, cnew_ref):
    u = x_ref[...]
    w = cw_ref[...]
    acc = u * w[CONV_W - 1:CONV_W, :] + cb_ref[...]
    for j in range(CONV_W - 1):
        acc = acc + prev_ref[j] * w[j:j + 1, :]
        cnew_ref[j] = prev_ref[j + 1] if j + 1 < CONV_W - 1 else u
    xc_ref[...] = _silu(acc)
    dt = _softplus(tail_ref[...] + pb_ref[...])
    dt_ref[...] = dt
    dec_ref[...] = jnp.exp(dt * (-jnp.exp(pa_ref[...])))


def _ssd_prep_sample(p_main, prev_t, cw, cb, tail, pa, pb):
    nb = p_main.shape[0]
    sec = SSD_SEC_W
    nsec = SSM_CONV_CH // sec
    c2 = lambda s: (0, 0)
    return pl.pallas_call(
        _ssd_prep_kernel,
        grid=(nsec,),
        in_specs=[
            pl.BlockSpec((nb, sec), lambda s: (0, OD_XBC // sec + s)),
            pl.BlockSpec((CONV_W - 1, nb, sec), lambda s: (0, 0, s)),
            pl.BlockSpec((CONV_W, sec), lambda s: (0, s)),
            pl.BlockSpec((1, sec), lambda s: (0, s)),
            pl.BlockSpec((nb, LANES), c2),
            pl.BlockSpec((1, LANES), c2),
            pl.BlockSpec((1, LANES), c2),
        ],
        out_specs=[
            pl.BlockSpec((nb, sec), lambda s: (0, s)),
            pl.BlockSpec((nb, LANES), c2),
            pl.BlockSpec((nb, LANES), c2),
            pl.BlockSpec((CONV_W - 1, nb, sec), lambda s: (0, 0, s)),
        ],
        out_shape=[
            jax.ShapeDtypeStruct((nb, SSM_CONV_CH), F32),
            jax.ShapeDtypeStruct((nb, LANES), F32),
            jax.ShapeDtypeStruct((nb, LANES), F32),
            jax.ShapeDtypeStruct((CONV_W - 1, nb, SSM_CONV_CH), F32),
        ],
        compiler_params=_params(("arbitrary",)),
        name="ssd_prep_sample",
    )(p_main, prev_t, cw, cb, tail, pa, pb)


def _ssd_state_kernel(dec_ref, x_ref, dt_ref, b_ref, c_ref, z_ref, de_ref, nw_ref, ex_ref, h_ref,
                      y_ref, hout_ref, *, nb):
    g = pl.program_id(0)
    x = x_ref[...]
    dtx = x * _dot_sel(_split3(dt_ref[...]), ex_ref[...])
    zrows = LANES - nb
    xt = jnp.transpose(jnp.concatenate([dtx, jnp.zeros((zrows, SSM_GROUP_W), F32)], axis=0))
    xh, xl = _split2(xt)
    bpad = jnp.concatenate([b_ref[...], jnp.zeros((zrows, SSM_STATE), F32)], axis=0)
    cpad_t = jnp.transpose(jnp.concatenate([c_ref[...], jnp.zeros((zrows, SSM_STATE), F32)], axis=0))
    rowi = _iota2((LANES, SSM_STATE), 0)
    coli = _iota2((SSM_STATE, LANES), 1)
    yt = jnp.zeros((SSM_GROUP_W, LANES), F32)
    for b in range(nb):
        bh, bl = _split2(jnp.where(rowi == b, bpad, 0.0))
        upd = (jnp.dot(xh, bh, preferred_element_type=F32) + jnp.dot(xh, bl, preferred_element_type=F32)
               + jnp.dot(xl, bh, preferred_element_type=F32))
        parts = []
        for r in range(SSM_REP):
            hn = h_ref[b, r] * dec_ref[b, g * SSM_REP + r] + upd[r * SSM_HEAD_DIM:(r + 1) * SSM_HEAD_DIM]
            hout_ref[b, r] = hn
            parts.append(hn)
        hh_, hl_ = _split2(jnp.concatenate(parts, axis=0))
        ch, cl = _split2(jnp.where(coli == b, cpad_t, 0.0))
        yt = yt + (jnp.dot(hh_, ch, preferred_element_type=F32) + jnp.dot(hh_, cl, preferred_element_type=F32)
                   + jnp.dot(hl_, ch, preferred_element_type=F32))
    y = jnp.transpose(yt)[0:nb, :] + de_ref[...] * x
    y_ref[...] = _rms_rows(y * _silu(z_ref[...]), nw_ref[...]).astype(y_ref.dtype)


def _ssd_state_sample(dtv, dec, xc, p_main, d_e, nw, ex, h_all, layer, prev_out):
    nb = xc.shape[0]
    assert nb <= LANES
    gw = SSM_GROUP_W
    n_layers = h_all.shape[0]
    state_spec = pl.BlockSpec((None, nb, SSM_REP, SSM_HEAD_DIM, SSM_STATE), lambda g: (layer, 0, g, 0, 0))
    args = [dec, xc, dtv, xc, xc, p_main, d_e, nw, ex, h_all]
    extra_specs, extra_args, aliases = _alias_prev(prev_out, len(args), 1)
    kern = functools.partial(_ssd_state_kernel, nb=nb)
    if prev_out is not None:
        kern = _drop_ref(kern, len(args))
    return pl.pallas_call(
        kern,
        grid=(SSM_GROUPS,),
        input_output_aliases=aliases,
        in_specs=[
            pl.BlockSpec(memory_space=pltpu.SMEM),
            pl.BlockSpec((nb, gw), lambda g: (0, g)),
            pl.BlockSpec((nb, LANES), lambda g: (0, 0)),
            pl.BlockSpec((nb, SSM_STATE), lambda g: (0, SSM_D_INNER // SSM_STATE + g)),
            pl.BlockSpec((nb, SSM_STATE), lambda g: (0, (SSM_D_INNER + SSM_BC_W) // SSM_STATE + g)),
            pl.BlockSpec((nb, gw), lambda g: (0, g)),
            pl.BlockSpec((1, gw), lambda g: (0, g)),
            pl.BlockSpec((1, gw), lambda g: (0, g)),
            pl.BlockSpec((LANES, gw), lambda g: (0, g)),
            state_spec,
        ] + extra_specs,
        out_specs=[
            pl.BlockSpec((nb, gw), lambda g: (0, g)),
            state_spec,
        ],
        out_shape=[
            jax.ShapeDtypeStruct((nb, SSM_D_INNER), BF16),
            jax.ShapeDtypeStruct((n_layers, nb, SSM_HEADS, SSM_HEAD_DIM, SSM_STATE), F32),
        ],
        compiler_params=_params(("arbitrary",)),
        name="ssd_state_sample",
    )(*args, *extra_args)


def _lane_pad_row(v):
    return jnp.zeros((1, LANES), F32).at[0, :v.shape[0]].set(v.astype(F32))


def kernel(x_prompt, x_sample, cache_swa_k, cache_swa_v, state_gdn, state_gdn_conv, state_ssm, state_ssm_conv,
           norm_ffn1, norm_mix, norm_ffn2, norm_final, w_ffn_gate, w_ffn_up, w_ffn_down,
           w_in_even, w_out_even, attn_sinks, gdn_conv_w, gdn_A_log, gdn_dt_bias, gdn_norm_w,
           w_in_odd, w_out_odd, ssm_conv_w, ssm_conv_b, ssm_A_log, ssm_dt_bias, ssm_D, ssm_norm_w):
    bsz, t, d = x_prompt.shape
    nb = x_sample.shape[0]
    depth = norm_ffn1.shape[0]
    n_even, n_odd = (depth + 1) // 2, depth // 2
    mp = bsz * t
    assert x_sample.shape[1] == 1 and d == D_MODEL and t % (2 * CHUNK) == 0
    hh = GDN_HEADS

    w_even_t = jnp.swapaxes(w_in_even, 1, 2)
    w_odd_t = jnp.swapaxes(w_in_odd, 1, 2)
    ex = _head_expand()
    cache_k = cache_swa_k.reshape(n_even, nb, WINDOW, SWA_KV_W)
    cache_v = cache_swa_v.reshape(n_even, nb, WINDOW, SWA_KV_W)

    xp = x_prompt.reshape(mp, d)
    xs = x_sample.reshape(nb, d)
    gain_f = norm_final.reshape(1, d)
    ffn_tm = _tile(mp, FFN_TM)

    small = {k: [] for k in ("pk", "pv", "pgc", "psc", "sk", "sv", "sgc", "ssc")}
    p_gdn = p_ssm = s_gdn = s_ssm = None

    def ffn_pair(xp, xs, gain, l, which, final):
        gain = gain.reshape(1, d)
        xs, wg, wu, wd = _ffn(xs, gain, gain_f, w_ffn_gate, w_ffn_up, w_ffn_down, (l, which),
                              final_norm=final, tm=nb, tf=FFN_TF)
        xp, = _ffn(xp, gain, gain_f, wg, wu, wd, None, final_norm=final, tm=ffn_tm, tf=FFN_TF)
        return xp, xs

    for l in range(depth):
        xp, xs = ffn_pair(xp, xs, norm_ffn1[l], l, 0, False)
        gmix = norm_mix[l].reshape(1, d)
        if l % 2 == 0:
            e = l // 2
            sm, st, w_main, w_tail = _proj(xs, gmix, w_even_t, w_even_t, e, EV_MAIN, tm=nb, tn=SAMPLE_TN_EVEN)
            pm, pt = _proj(xp, gmix, w_main, w_tail, None, EV_MAIN, tm=_tile(mp, PROJ_EVEN_TM), tn=PROJ_EVEN_TN)
            pa = _lane_pad_row(gdn_A_log[e])
            pb = _lane_pad_row(gdn_dt_bias[e])
            nw = gdn_norm_w[e].reshape(1, GDN_DV).astype(F32)

            kn = sm[:, EV_K:EV_K + SWA_KV_W]
            vn = sm[:, EV_V:EV_V + SWA_KV_W]
            o_a_s = _swa_sample(sm[:, EV_Q:EV_Q + SWA_Q_W].reshape(nb, SWA_HEADS, SWA_HEAD_DIM),
                                kn.reshape(nb, 1, SWA_KV_W), vn.reshape(nb, 1, SWA_KV_W), cache_k, cache_v,
                                attn_sinks[e].astype(F32).reshape(SWA_HEADS, 1), e)
            ab = jnp.broadcast_to(st[:, :2 * hh, None], (nb, 2 * hh, LANES))
            o_b_s, s_gdn, gc_s = _gdn_sample(
                sm[:, EV_GQKV:EV_GQKV + GDN_CONV_CH].reshape(nb, 3 * hh, GDN_DK),
                state_gdn_conv[e].reshape(nb, CONV_W - 1, 3 * hh, GDN_DK),
                gdn_conv_w[e].reshape(CONV_W, 3 * hh, GDN_DK),
                state_gdn, sm[:, EV_Z:EV_Z + GDN_V_W].reshape(nb, hh, GDN_DV), ab,
                jnp.broadcast_to(gdn_A_log[e].astype(F32)[:, None], (hh, LANES)),
                jnp.broadcast_to(gdn_dt_bias[e].astype(F32)[:, None], (hh, LANES)), nw, e, s_gdn)
            small["sk"].append(kn.reshape(nb, 1, SWA_KV_HEADS, SWA_HEAD_DIM))
            small["sv"].append(vn.reshape(nb, 1, SWA_KV_HEADS, SWA_HEAD_DIM))
            small["sgc"].append(gc_s.reshape(nb, CONV_W - 1, GDN_CONV_CH))
            xs, w_out = _outproj(xs, [o_a_s.reshape(nb, SWA_Q_W).astype(BF16),
                                      o_b_s.reshape(nb, GDN_V_W).astype(BF16)], w_out_even, e, tm=nb, tn=SAMPLE_OUT_TN)

            o_a_p = _swa_prompt(pm, attn_sinks[e].astype(F32), bsz, t)
            o_b_p, p_gdn = _gdn_prompt(pm, pt, gdn_conv_w, pa, pb, nw, e, n_even, p_gdn, bsz, t, rows=2 * CHUNK)
            pm3 = pm.reshape(bsz, t, EV_MAIN)
            small["pk"].append(pm3[:, t - WINDOW:, EV_K:EV_K + SWA_KV_W].reshape(bsz, WINDOW, SWA_KV_HEADS, SWA_HEAD_DIM))
            small["pv"].append(pm3[:, t - WINDOW:, EV_V:EV_V + SWA_KV_W].reshape(bsz, WINDOW, SWA_KV_HEADS, SWA_HEAD_DIM))
            small["pgc"].append(pm3[:, t - (CONV_W - 1):, EV_GQKV:EV_GQKV + GDN_CONV_CH])
            xp, = _outproj(xp, [o_a_p, o_b_p], w_out, None, tm=_tile(mp, OUT_EVEN_TM), tn=OUT_EVEN_TN)
        else:
            o = l // 2
            sm, st, w_main, w_tail = _proj(xs, gmix, w_odd_t, w_odd_t, o, OD_MAIN, tm=nb, tn=SAMPLE_TN_ODD)
            pm, pt = _proj(xp, gmix, w_main, w_tail, None, OD_MAIN, tm=_tile(mp, PROJ_ODD_TM), tn=PROJ_ODD_TN)
            pa = _lane_pad_row(ssm_A_log[o])
            pb = _lane_pad_row(ssm_dt_bias[o])
            d_e = jnp.repeat(ssm_D[o].astype(F32), SSM_HEAD_DIM).reshape(1, SSM_D_INNER)
            nw = ssm_norm_w[o].reshape(1, SSM_D_INNER).astype(F32)
            cb = ssm_conv_b[o].reshape(1, SSM_CONV_CH)

            xc, dtv, dec, cnew = _ssd_prep_sample(sm, jnp.swapaxes(state_ssm_conv[o], 0, 1), ssm_conv_w[o], cb,
                                                  st, pa, pb)
            y_s, s_ssm = _ssd_state_sample(dtv, dec, xc, sm, d_e, nw, ex, state_ssm, o, s_ssm)
            small["ssc"].append(jnp.swapaxes(cnew, 0, 1))
            xs, w_out = _outproj(xs, [y_s], w_out_odd, o, tm=nb, tn=SAMPLE_OUT_TN)

            y_p, p_ssm = _ssd_prompt(pm, pt, ssm_conv_w[o], cb, pa, pb, d_e, nw, ex, o, n_odd, p_ssm, bsz, t)
            pm3 = pm.reshape(bsz, t, OD_MAIN)
            small["psc"].append(pm3[:, t - (CONV_W - 1):, OD_XBC:OD_XBC + SSM_CONV_CH])
            xp, = _outproj(xp, [y_p], w_out, None, tm=_tile(mp, OUT_ODD_TM), tn=OUT_ODD_TN)
        xp, xs = ffn_pair(xp, xs, norm_ffn2[l], l, 1, l == depth - 1)

    st_ = {k: jnp.stack(v) for k, v in small.items()}
    return (xp.reshape(bsz, t, d), xs.reshape(nb, 1, d),
            st_["pk"], st_["pv"], p_gdn, st_["pgc"], p_ssm, st_["psc"],
            st_["sk"], st_["sv"], s_gdn, st_["sgc"], s_ssm, st_["ssc"])
```

```python
import functools

import jax
import jax.numpy as jnp
from jax import lax
from jax.experimental import pallas as pl
from jax.experimental.pallas import tpu as pltpu

F32 = jnp.float32
BF16 = jnp.bfloat16

D_MODEL = 2048
SWA_HEADS = 16
SWA_KV_HEADS = 4
SWA_GROUP = SWA_HEADS // SWA_KV_HEADS
SWA_HEAD_DIM = 64
WINDOW = 128
GDN_HEADS = 8
GDN_DK = 128
GDN_DV = 128
CONV_W = 4
SSM_D_INNER = 2 * D_MODEL
SSM_HEAD_DIM = 64
SSM_HEADS = SSM_D_INNER // SSM_HEAD_DIM
SSM_GROUPS = 8
SSM_REP = SSM_HEADS // SSM_GROUPS
SSM_STATE = 128
EPS = 1e-6
NEG_INF = -1e30

SWA_Q_W = SWA_HEADS * SWA_HEAD_DIM
SWA_KV_W = SWA_KV_HEADS * SWA_HEAD_DIM
GDN_QK_W = GDN_HEADS * GDN_DK
GDN_V_W = GDN_HEADS * GDN_DV
GDN_CONV_CH = 2 * GDN_QK_W + GDN_V_W
GDN_HALF_W = GDN_QK_W // 2
SSM_BC_W = SSM_GROUPS * SSM_STATE
SSM_CONV_CH = SSM_D_INNER + 2 * SSM_BC_W
SSM_GROUP_W = SSM_REP * SSM_HEAD_DIM
SSD_SEC_W = SSM_CONV_CH // 3

EV_Q = 0
EV_K = EV_Q + SWA_Q_W
EV_V = EV_K + SWA_KV_W
EV_GQKV = EV_V + SWA_KV_W
EV_Z = EV_GQKV + GDN_CONV_CH
EV_MAIN = EV_Z + GDN_V_W
OD_XBC = SSM_D_INNER
OD_MAIN = SSM_D_INNER + SSM_CONV_CH

LANES = 128
SUBLANES = 8
CHUNK = 128
VMEM_LIMIT = 56 * 1024 * 1024

FFN_TM, FFN_TF = 512, 512
PROJ_EVEN_TM, PROJ_EVEN_TN = 512, EV_MAIN // 2
PROJ_ODD_TM, PROJ_ODD_TN = 1024, 1024
OUT_EVEN_TM, OUT_EVEN_TN = 512, D_MODEL
OUT_ODD_TM, OUT_ODD_TN = 1024, 1024
FFN_FIRST_TF, FIRST_TN = 256, 512


def _params(sem):
    return pltpu.CompilerParams(dimension_semantics=sem, vmem_limit_bytes=VMEM_LIMIT)


def _tile(m, pref):
    return pref if m % pref == 0 else m


def _sigmoid(x):
    return 1.0 / (1.0 + jnp.exp(-x))


def _silu(x):
    return x * _sigmoid(x)


def _softplus(x):
    return jnp.maximum(x, 0.0) + jnp.log1p(jnp.exp(-jnp.abs(x)))


def _dot(a, b):
    return jnp.dot(a.astype(BF16), b.astype(BF16), preferred_element_type=F32)


def _dot_nt(a, b):
    return lax.dot_general(a.astype(BF16), b.astype(BF16), (((1,), (1,)), ((), ())),
                           preferred_element_type=F32)


def _bdot(a, b):
    return jnp.einsum("bij,bjk->bik", a.astype(BF16), b.astype(BF16), preferred_element_type=F32)


def _bdot_nt(a, b):
    return jnp.einsum("bid,bjd->bij", a.astype(BF16), b.astype(BF16), preferred_element_type=F32)


def _split2(a):
    hi = a.astype(BF16)
    lo = (a - hi.astype(F32)).astype(BF16)
    return hi, lo


def _split3(a):
    hi = a.astype(BF16)
    r = a - hi.astype(F32)
    mid = r.astype(BF16)
    lo = (r - mid.astype(F32)).astype(BF16)
    return hi, mid, lo


def _sel_dot(m01, parts):
    out = jnp.dot(m01, parts[0], preferred_element_type=F32)
    for p in parts[1:]:
        out = out + jnp.dot(m01, p, preferred_element_type=F32)
    return out


def _dot_sel(parts, m01):
    out = jnp.dot(parts[0], m01, preferred_element_type=F32)
    for p in parts[1:]:
        out = out + jnp.dot(p, m01, preferred_element_type=F32)
    return out


def _rms_rows(x, gain):
    ms = jnp.mean(x * x, axis=-1, keepdims=True)
    return x * lax.rsqrt(ms + EPS) * gain


def _iota2(shape, dim):
    return lax.broadcasted_iota(jnp.int32, shape, dim)


def _drop_ref(kern, pos):
    def entry(*refs):
        return kern(*refs[:pos], *refs[pos + 1:])
    return entry


def _alias_prev(prev, n_in, out_idx):
    if prev is None:
        return [], [], {}
    return [pl.BlockSpec(memory_space=pl.ANY)], [prev], {n_in: out_idx}


def _ffn_kernel(x_ref, g_ref, gf_ref, wg_ref, wu_ref, wd_ref, *rest, final_norm, cast):
    if cast:
        o_ref, wgb_ref, wub_ref, wdb_ref, h_ref = rest
    else:
        o_ref, h_ref = rest
    j = pl.program_id(1)

    @pl.when(j == 0)
    def _():
        h_ref[...] = _rms_rows(x_ref[...], g_ref[...]).astype(BF16)
        o_ref[...] = jnp.zeros_like(o_ref)

    wg, wu, wd = wg_ref[...], wu_ref[...], wd_ref[...]
    if cast:
        wg, wu, wd = wg.astype(BF16), wu.astype(BF16), wd.astype(BF16)
        wgb_ref[...] = wg
        wub_ref[...] = wu
        wdb_ref[...] = wd
    h = h_ref[...]
    a = jnp.dot(h, wg, preferred_element_type=F32)
    b = jnp.dot(h, wu, preferred_element_type=F32)
    t = (_silu(a) * b).astype(BF16)
    o_ref[...] += jnp.dot(t, wd, preferred_element_type=F32)

    @pl.when(j == pl.num_programs(1) - 1)
    def _():
        y = x_ref[...] + 0.5 * o_ref[...]
        if final_norm:
            y = _rms_rows(y, gf_ref[...])
        o_ref[...] = y


def _ffn(x, gain, gain_final, wg, wu, wd, sel, *, final_norm, tm, tf):
    m, d = x.shape
    f = wg.shape[-1]
    cast = sel is not None
    if cast:
        assert m == tm
        layer, which = sel
        w_specs = [pl.BlockSpec((None, None, d, tf), lambda i, j: (layer, which, 0, j)),
                   pl.BlockSpec((None, None, d, tf), lambda i, j: (layer, which, 0, j)),
                   pl.BlockSpec((None, None, tf, d), lambda i, j: (layer, which, j, 0))]
    else:
        w_specs = [pl.BlockSpec((d, tf), lambda i, j: (0, j)),
                   pl.BlockSpec((d, tf), lambda i, j: (0, j)),
                   pl.BlockSpec((tf, d), lambda i, j: (j, 0))]
    out_specs = [pl.BlockSpec((tm, d), lambda i, j: (i, 0))]
    out_shape = [jax.ShapeDtypeStruct((m, d), F32)]
    if cast:
        out_specs += [pl.BlockSpec((d, tf), lambda i, j: (0, j)),
                      pl.BlockSpec((d, tf), lambda i, j: (0, j)),
                      pl.BlockSpec((tf, d), lambda i, j: (j, 0))]
        out_shape += [jax.ShapeDtypeStruct((d, f), BF16), jax.ShapeDtypeStruct((d, f), BF16),
                      jax.ShapeDtypeStruct((f, d), BF16)]
    return pl.pallas_call(
        functools.partial(_ffn_kernel, final_norm=final_norm, cast=cast),
        grid=(m // tm, f // tf),
        in_specs=[
            pl.BlockSpec((tm, d), lambda i, j: (i, 0)),
            pl.BlockSpec((1, d), lambda i, j: (0, 0)),
            pl.BlockSpec((1, d), lambda i, j: (0, 0)),
        ] + w_specs,
        out_specs=out_specs,
        out_shape=out_shape,
        scratch_shapes=[pltpu.VMEM((tm, d), BF16)],
        compiler_params=_params(("parallel", "arbitrary")),
        name="ffn_half",
    )(x, gain, gain_final, wg, wu, wd)


def _proj_kernel(x_ref, g_ref, w_ref, wt_ref, *rest, cast):
    if cast:
        o_ref, ot_ref, wb_ref, wtb_ref, h_ref = rest
    else:
        o_ref, ot_ref, h_ref = rest
    j = pl.program_id(1)

    @pl.when(j == 0)
    def _():
        h = _rms_rows(x_ref[...], g_ref[...]).astype(BF16)
        h_ref[...] = h
        wt = wt_ref[...]
        if cast:
            pad = jnp.zeros((LANES - wt.shape[0], wt.shape[1]), F32)
            wt = jnp.transpose(jnp.concatenate([wt, pad], axis=0)).astype(BF16)
            wtb_ref[...] = wt
        ot_ref[...] = jnp.dot(h, wt, preferred_element_type=F32)

    w = w_ref[...]
    if cast:
        w = jnp.transpose(w).astype(BF16)
        wb_ref[...] = w
    o_ref[...] = jnp.dot(h_ref[...], w, preferred_element_type=F32)


def _proj(x, gain, w_main, w_tail, layer, n, *, tm, tn):
    m, d = x.shape
    cast = layer is not None
    if cast:
        n_tail = w_main.shape[1] - n
        assert m == tm and n % n_tail == 0 and n_tail % SUBLANES == 0 and n_tail <= LANES
        w_specs = [pl.BlockSpec((None, tn, d), lambda i, j: (layer, j, 0)),
                   pl.BlockSpec((None, n_tail, d), lambda i, j: (layer, n // n_tail, 0))]
    else:
        w_specs = [pl.BlockSpec((d, tn), lambda i, j: (0, j)),
                   pl.BlockSpec((d, LANES), lambda i, j: (0, 0))]
    out_specs = [pl.BlockSpec((tm, tn), lambda i, j: (i, j)),
                 pl.BlockSpec((tm, LANES), lambda i, j: (i, 0))]
    out_shape = [jax.ShapeDtypeStruct((m, n), F32), jax.ShapeDtypeStruct((m, LANES), F32)]
    if cast:
        out_specs += [pl.BlockSpec((d, tn), lambda i, j: (0, j)),
                      pl.BlockSpec((d, LANES), lambda i, j: (0, 0))]
        out_shape += [jax.ShapeDtypeStruct((d, n), BF16), jax.ShapeDtypeStruct((d, LANES), BF16)]
    return pl.pallas_call(
        functools.partial(_proj_kernel, cast=cast),
        grid=(m // tm, n // tn),
        in_specs=[
            pl.BlockSpec((tm, d), lambda i, j: (i, 0)),
            pl.BlockSpec((1, d), lambda i, j: (0, 0)),
        ] + w_specs,
        out_specs=out_specs,
        out_shape=out_shape,
        scratch_shapes=[pltpu.VMEM((tm, d), BF16)],
        compiler_params=_params(("parallel", "arbitrary")),
        name="norm_proj",
    )(x, gain, w_main, w_tail)


def _outproj_kernel(*refs, n_in, cast):
    x_ref = refs[0]
    o_refs = refs[1:1 + n_in]
    w_refs = refs[1 + n_in:1 + 2 * n_in]
    out_ref = refs[1 + 2 * n_in]
    acc = x_ref[...]
    for kb, (o_ref, w_ref) in enumerate(zip(o_refs, w_refs)):
        w = w_ref[...]
        if cast:
            w = w.astype(BF16)
            kw = w.shape[0]
            refs[2 + 2 * n_in][kb * kw:(kb + 1) * kw, :] = w
        acc = acc + jnp.dot(o_ref[...], w, preferred_element_type=F32)
    out_ref[...] = acc


def _outproj(x, outs, w, layer, *, tm, tn):
    m, d = x.shape
    n_in = len(outs)
    kw = outs[0].shape[1]
    k_all = kw * n_in
    assert all(o.shape[1] == kw for o in outs) and k_all == w.shape[-2]
    cast = layer is not None
    in_specs = [pl.BlockSpec((tm, tn), lambda i, j: (i, j))]
    for o in outs:
        in_specs.append(pl.BlockSpec((tm, kw), lambda i, j: (i, 0)))
    for kb in range(n_in):
        if cast:
            in_specs.append(pl.BlockSpec((None, kw, tn), lambda i, j, kb=kb: (layer, kb, j)))
        else:
            in_specs.append(pl.BlockSpec((kw, tn), lambda i, j, kb=kb: (kb, j)))
    out_specs = [pl.BlockSpec((tm, tn), lambda i, j: (i, j))]
    out_shape = [jax.ShapeDtypeStruct((m, d), F32)]
    if cast:
        assert m == tm
        out_specs.append(pl.BlockSpec((k_all, tn), lambda i, j: (0, j)))
        out_shape.append(jax.ShapeDtypeStruct((k_all, d), BF16))
    return pl.pallas_call(
        functools.partial(_outproj_kernel, n_in=n_in, cast=cast),
        grid=(m // tm, d // tn),
        in_specs=in_specs,
        out_specs=out_specs,
        out_shape=out_shape,
        compiler_params=_params(("parallel", "arbitrary")),
        name="out_proj",
    )(x, *outs, *([w] * n_in))


def _ffn_rows(h, wg, wu, wd):
    a = jnp.dot(h, wg, preferred_element_type=F32)
    b = jnp.dot(h, wu, preferred_element_type=F32)
    return jnp.dot((_silu(a) * b).astype(BF16), wd, preferred_element_type=F32)


def _ffn_first_kernel(xp_ref, xs_ref, g_ref, gf_ref, wg_ref, wu_ref, wd_ref,
                      op_ref, os_ref, wgb_ref, wub_ref, wdb_ref, hp_ref, hs_ref, *, final_norm):
    j = pl.program_id(0)
    streams = ((xp_ref, hp_ref, op_ref), (xs_ref, hs_ref, os_ref))

    @pl.when(j == 0)
    def _():
        for x_ref, h_ref, o_ref in streams:
            h_ref[...] = _rms_rows(x_ref[...], g_ref[...]).astype(BF16)
            o_ref[...] = jnp.zeros_like(o_ref)

    wg, wu, wd = wg_ref[...].astype(BF16), wu_ref[...].astype(BF16), wd_ref[...].astype(BF16)
    wgb_ref[...] = wg
    wub_ref[...] = wu
    wdb_ref[...] = wd
    for x_ref, h_ref, o_ref in streams:
        o_ref[...] += _ffn_rows(h_ref[...], wg, wu, wd)

    @pl.when(j == pl.num_programs(0) - 1)
    def _():
        for x_ref, h_ref, o_ref in streams:
            y = x_ref[...] + 0.5 * o_ref[...]
            if final_norm:
                y = _rms_rows(y, gf_ref[...])
            o_ref[...] = y


def _ffn_first(xp, xs, gain, gain_final, wg, wu, wd, layer, which, *, final_norm, tm, tf):
    mp, d = xp.shape
    nb = xs.shape[0]
    f = wg.shape[-1]
    c2 = lambda j: (0, 0)
    return pl.pallas_call(
        functools.partial(_ffn_first_kernel, final_norm=final_norm),
        grid=(f // tf,),
        in_specs=[
            pl.BlockSpec((tm, d), c2),
            pl.BlockSpec((nb, d), c2),
            pl.BlockSpec((1, d), c2),
            pl.BlockSpec((1, d), c2),
            pl.BlockSpec((None, None, d, tf), lambda j: (layer, which, 0, j)),
            pl.BlockSpec((None, None, d, tf), lambda j: (layer, which, 0, j)),
            pl.BlockSpec((None, None, tf, d), lambda j: (layer, which, j, 0)),
        ],
        out_specs=[
            pl.BlockSpec((tm, d), c2),
            pl.BlockSpec((nb, d), c2),
            pl.BlockSpec((d, tf), lambda j: (0, j)),
            pl.BlockSpec((d, tf), lambda j: (0, j)),
            pl.BlockSpec((tf, d), lambda j: (j, 0)),
        ],
        out_shape=[
            jax.ShapeDtypeStruct((mp, d), F32), jax.ShapeDtypeStruct((nb, d), F32),
            jax.ShapeDtypeStruct((d, f), BF16), jax.ShapeDtypeStruct((d, f), BF16),
            jax.ShapeDtypeStruct((f, d), BF16),
        ],
        scratch_shapes=[pltpu.VMEM((tm, d), BF16), pltpu.VMEM((nb, d), BF16)],
        compiler_params=_params(("arbitrary",)),
        name="ffn_first",
    )(xp, xs, gain, gain_final, wg, wu, wd)


def _ffn_rest_kernel(x_ref, g_ref, gf_ref, wg_ref, wu_ref, wd_ref, prev_ref, o_ref, h_ref, *, final_norm):
    del prev_ref
    j = pl.program_id(1)

    @pl.when(j == 0)
    def _():
        h_ref[...] = _rms_rows(x_ref[...], g_ref[...]).astype(BF16)
        o_ref[...] = jnp.zeros_like(o_ref)

    o_ref[...] += _ffn_rows(h_ref[...], wg_ref[...], wu_ref[...], wd_ref[...])

    @pl.when(j == pl.num_programs(1) - 1)
    def _():
        y = x_ref[...] + 0.5 * o_ref[...]
        if final_norm:
            y = _rms_rows(y, gf_ref[...])
        o_ref[...] = y


def _ffn_rest(x, gain, gain_final, wg, wu, wd, prev, *, final_norm, tm, tf):
    m, d = x.shape
    f = wg.shape[-1]
    if m == tm:
        return prev
    return pl.pallas_call(
        functools.partial(_ffn_rest_kernel, final_norm=final_norm),
        grid=(m // tm - 1, f // tf),
        in_specs=[
            pl.BlockSpec((tm, d), lambda i, j: (i + 1, 0)),
            pl.BlockSpec((1, d), lambda i, j: (0, 0)),
            pl.BlockSpec((1, d), lambda i, j: (0, 0)),
            pl.BlockSpec((d, tf), lambda i, j: (0, j)),
            pl.BlockSpec((d, tf), lambda i, j: (0, j)),
            pl.BlockSpec((tf, d), lambda i, j: (j, 0)),
            pl.BlockSpec(memory_space=pl.ANY),
        ],
        out_specs=pl.BlockSpec((tm, d), lambda i, j: (i + 1, 0)),
        out_shape=jax.ShapeDtypeStruct((m, d), F32),
        scratch_shapes=[pltpu.VMEM((tm, d), BF16)],
        input_output_aliases={6: 0},
        compiler_params=_params(("parallel", "arbitrary")),
        name="ffn_rest",
    )(x, gain, gain_final, wg, wu, wd, prev)


def _proj_first_kernel(xp_ref, xs_ref, g_ref, w_ref, wt_ref,
                       op_ref, otp_ref, os_ref, ots_ref, wb_ref, wtb_ref, hp_ref, hs_ref):
    j = pl.program_id(0)
    streams = ((xp_ref, hp_ref, op_ref, otp_ref), (xs_ref, hs_ref, os_ref, ots_ref))

    @pl.when(j == 0)
    def _():
        wt = wt_ref[...]
        pad = jnp.zeros((LANES - wt.shape[0], wt.shape[1]), F32)
        wt = jnp.transpose(jnp.concatenate([wt, pad], axis=0)).astype(BF16)
        wtb_ref[...] = wt
        for x_ref, h_ref, _, ot_ref in streams:
            h = _rms_rows(x_ref[...], g_ref[...]).astype(BF16)
            h_ref[...] = h
            ot_ref[...] = jnp.dot(h, wt, preferred_element_type=F32)

    w = jnp.transpose(w_ref[...]).astype(BF16)
    wb_ref[...] = w
    for _, h_ref, o_ref, _ in streams:
        o_ref[...] = jnp.dot(h_ref[...], w, preferred_element_type=F32)


def _proj_first(xp, xs, gain, w_t, layer, n, *, tm, tn):
    mp, d = xp.shape
    nb = xs.shape[0]
    n_tail = w_t.shape[1] - n
    assert n % n_tail == 0 and n_tail % SUBLANES == 0 and n_tail <= LANES
    c2 = lambda j: (0, 0)
    return pl.pallas_call(
        _proj_first_kernel,
        grid=(n // tn,),
        in_specs=[
            pl.BlockSpec((tm, d), c2),
            pl.BlockSpec((nb, d), c2),
            pl.BlockSpec((1, d), c2),
            pl.BlockSpec((None, tn, d), lambda j: (layer, j, 0)),
            pl.BlockSpec((None, n_tail, d), lambda j: (layer, n // n_tail, 0)),
        ],
        out_specs=[
            pl.BlockSpec((tm, tn), lambda j: (0, j)),
            pl.BlockSpec((tm, LANES), c2),
            pl.BlockSpec((nb, tn), lambda j: (0, j)),
            pl.BlockSpec((nb, LANES), c2),
            pl.BlockSpec((d, tn), lambda j: (0, j)),
            pl.BlockSpec((d, LANES), c2),
        ],
        out_shape=[
            jax.ShapeDtypeStruct((mp, n), F32), jax.ShapeDtypeStruct((mp, LANES), F32),
            jax.ShapeDtypeStruct((nb, n), F32), jax.ShapeDtypeStruct((nb, LANES), F32),
            jax.ShapeDtypeStruct((d, n), BF16), jax.ShapeDtypeStruct((d, LANES), BF16),
        ],
        scratch_shapes=[pltpu.VMEM((tm, d), BF16), pltpu.VMEM((nb, d), BF16)],
        compiler_params=_params(("arbitrary",)),
        name="proj_first",
    )(xp, xs, gain, w_t, w_t)


def _proj_rest_kernel(x_ref, g_ref, w_ref, wt_ref, prev_ref, prevt_ref, o_ref, ot_ref, h_ref):
    del prev_ref, prevt_ref
    j = pl.program_id(1)

    @pl.when(j == 0)
    def _():
        h = _rms_rows(x_ref[...], g_ref[...]).astype(BF16)
        h_ref[...] = h
        ot_ref[...] = jnp.dot(h, wt_ref[...], preferred_element_type=F32)

    o_ref[...] = jnp.dot(h_ref[...], w_ref[...], preferred_element_type=F32)


def _proj_rest(x, gain, w_main, w_tail, prev_main, prev_tail, *, tm, tn):
    m, d = x.shape
    n = w_main.shape[1]
    if m == tm:
        return prev_main, prev_tail
    return pl.pallas_call(
        _proj_rest_kernel,
        grid=(m // tm - 1, n // tn),
        in_specs=[
            pl.BlockSpec((tm, d), lambda i, j: (i + 1, 0)),
            pl.BlockSpec((1, d), lambda i, j: (0, 0)),
            pl.BlockSpec((d, tn), lambda i, j: (0, j)),
            pl.BlockSpec((d, LANES), lambda i, j: (0, 0)),
            pl.BlockSpec(memory_space=pl.ANY),
            pl.BlockSpec(memory_space=pl.ANY),
        ],
        out_specs=[
            pl.BlockSpec((tm, tn), lambda i, j: (i + 1, j)),
            pl.BlockSpec((tm, LANES), lambda i, j: (i + 1, 0)),
        ],
        out_shape=[jax.ShapeDtypeStruct((m, n), F32), jax.ShapeDtypeStruct((m, LANES), F32)],
        scratch_shapes=[pltpu.VMEM((tm, d), BF16)],
        input_output_aliases={4: 0, 5: 1},
        compiler_params=_params(("parallel", "arbitrary")),
        name="proj_rest",
    )(x, gain, w_main, w_tail, prev_main, prev_tail)


def _outproj_first_kernel(*refs, n_in):
    xp_ref, xs_ref = refs[0], refs[1]
    op_refs = refs[2:2 + n_in]
    os_refs = refs[2 + n_in:2 + 2 * n_in]
    w_refs = refs[2 + 2 * n_in:2 + 3 * n_in]
    yp_ref, ys_ref, wb_ref = refs[2 + 3 * n_in:]
    accp = xp_ref[...]
    accs = xs_ref[...]
    for kb in range(n_in):
        w = w_refs[kb][...].astype(BF16)
        kw = w.shape[0]
        wb_ref[kb * kw:(kb + 1) * kw, :] = w
        accp = accp + jnp.dot(op_refs[kb][...], w, preferred_element_type=F32)
        accs = accs + jnp.dot(os_refs[kb][...], w, preferred_element_type=F32)
    yp_ref[...] = accp
    ys_ref[...] = accs


def _outproj_first(xp, xs, outs_p, outs_s, w_all, layer, *, tm, tn):
    mp, d = xp.shape
    nb = xs.shape[0]
    n_in = len(outs_p)
    kw = outs_p[0].shape[1]
    k_all = kw * n_in
    assert all(o.shape[1] == kw for o in outs_p + outs_s) and k_all == w_all.shape[1]
    in_specs = [pl.BlockSpec((tm, tn), lambda j: (0, j)), pl.BlockSpec((nb, tn), lambda j: (0, j))]
    in_specs += [pl.BlockSpec((tm, kw), lambda j: (0, 0)) for _ in range(n_in)]
    in_specs += [pl.BlockSpec((nb, kw), lambda j: (0, 0)) for _ in range(n_in)]
    in_specs += [pl.BlockSpec((None, kw, tn), lambda j, kb=kb: (layer, kb, j)) for kb in range(n_in)]
    return pl.pallas_call(
        functools.partial(_outproj_first_kernel, n_in=n_in),
        grid=(d // tn,),
        in_specs=in_specs,
        out_specs=[
            pl.BlockSpec((tm, tn), lambda j: (0, j)),
            pl.BlockSpec((nb, tn), lambda j: (0, j)),
            pl.BlockSpec((k_all, tn), lambda j: (0, j)),
        ],
        out_shape=[
            jax.ShapeDtypeStruct((mp, d), F32), jax.ShapeDtypeStruct((nb, d), F32),
            jax.ShapeDtypeStruct((k_all, d), BF16),
        ],
        compiler_params=_params(("arbitrary",)),
        name="outproj_first",
    )(xp, xs, *outs_p, *outs_s, *([w_all] * n_in))


def _outproj_rest_kernel(*refs, n_in):
    x_ref = refs[0]
    o_refs = refs[1:1 + n_in]
    w_refs = refs[1 + n_in:1 + 2 * n_in]
    out_ref = refs[2 + 2 * n_in]
    acc = x_ref[...]
    for o_ref, w_ref in zip(o_refs, w_refs):
        acc = acc + jnp.dot(o_ref[...], w_ref[...], preferred_element_type=F32)
    out_ref[...] = acc


def _outproj_rest(x, outs, w, prev, *, tm, tn):
    m, d = x.shape
    n_in = len(outs)
    kw = outs[0].shape[1]
    if m == tm:
        return prev
    in_specs = [pl.BlockSpec((tm, tn), lambda i, j: (i + 1, j))]
    in_specs += [pl.BlockSpec((tm, kw), lambda i, j: (i + 1, 0)) for _ in range(n_in)]
    in_specs += [pl.BlockSpec((kw, tn), lambda i, j, kb=kb: (kb, j)) for kb in range(n_in)]
    in_specs.append(pl.BlockSpec(memory_space=pl.ANY))
    return pl.pallas_call(
        functools.partial(_outproj_rest_kernel, n_in=n_in),
        grid=(m // tm - 1, d // tn),
        in_specs=in_specs,
        out_specs=pl.BlockSpec((tm, tn), lambda i, j: (i + 1, j)),
        out_shape=jax.ShapeDtypeStruct((m, d), F32),
        input_output_aliases={1 + 2 * n_in: 0},
        compiler_params=_params(("parallel", "arbitrary")),
        name="outproj_rest",
    )(x, *outs, *([w] * n_in), prev)


def _swa_prompt_kernel(sink_ref, q_ref, kc_ref, kp_ref, vc_ref, vp_ref, o_ref):
    n = pl.program_id(1)
    w = WINDOW
    qi = _iota2((w, 2 * w), 0)
    kj = _iota2((w, 2 * w), 1)
    valid = (kj > qi) & (kj <= qi + w) & ((n > 0) | (kj >= w))
    scale = SWA_HEAD_DIM ** -0.5
    kk = jnp.concatenate([kp_ref[...], kc_ref[...]], axis=0).astype(BF16)
    vv = jnp.concatenate([vp_ref[...], vc_ref[...]], axis=0).astype(BF16)
    q = q_ref[...].astype(BF16)
    outs = []
    for h in range(SWA_HEADS):
        c = h // SWA_GROUP
        qh = q[:, h * SWA_HEAD_DIM:(h + 1) * SWA_HEAD_DIM]
        kh = kk[:, c * SWA_HEAD_DIM:(c + 1) * SWA_HEAD_DIM]
        vh = vv[:, c * SWA_HEAD_DIM:(c + 1) * SWA_HEAD_DIM]
        s = lax.dot_general(qh, kh, (((1,), (1,)), ((), ())), preferred_element_type=F32) * scale
        s = jnp.where(valid, s, NEG_INF)
        sink = sink_ref[h]
        mx = jnp.maximum(jnp.max(s, axis=-1, keepdims=True), sink)
        p = jnp.exp(s - mx)
        den = jnp.sum(p, axis=-1, keepdims=True) + jnp.exp(sink - mx)
        p = p / den
        outs.append(jnp.dot(p.astype(BF16), vh, preferred_element_type=F32))
    o_ref[...] = jnp.concatenate(outs, axis=-1).astype(o_ref.dtype)


def _swa_prompt(p_main, sinks, bsz, t):
    nb = t // WINDOW
    kblk = EV_K // SWA_KV_W
    vblk = EV_V // SWA_KV_W

    def cur(col):
        return lambda b, n: (b * nb + n, col)

    def prev(col):
        return lambda b, n: (b * nb + jnp.maximum(n - 1, 0), col)

    return pl.pallas_call(
        _swa_prompt_kernel,
        grid=(bsz, nb),
        in_specs=[
            pl.BlockSpec(memory_space=pltpu.SMEM),
            pl.BlockSpec((WINDOW, SWA_Q_W), cur(EV_Q // SWA_Q_W)),
            pl.BlockSpec((WINDOW, SWA_KV_W), cur(kblk)),
            pl.BlockSpec((WINDOW, SWA_KV_W), prev(kblk)),
            pl.BlockSpec((WINDOW, SWA_KV_W), cur(vblk)),
            pl.BlockSpec((WINDOW, SWA_KV_W), prev(vblk)),
        ],
        out_specs=pl.BlockSpec((WINDOW, SWA_Q_W), cur(0)),
        out_shape=jax.ShapeDtypeStruct((bsz * t, SWA_Q_W), BF16),
        compiler_params=_params(("parallel", "arbitrary")),
        name="swa_prompt",
    )(sinks, p_main, p_main, p_main, p_main, p_main)


def _unit_lower_inverse(a, ri, ci):
    a0 = jnp.where((ri >> 4) == (ci >> 4), a, 0.0)
    y = -a0
    p = _bdot(a0, a0)
    for step in range(3):
        y = y + p + _bdot(y, p)
        if step < 2:
            p = _bdot(p, p)
    for sh in (4, 5, 6):
        e = jnp.where(((ri >> (sh + 1)) == (ci >> (sh + 1))) & ((ri >> sh) != (ci >> sh)), a, 0.0)
        t = e + _bdot(y, e)
        y = y - t - _bdot(t, y)
    return y


def _conv_section(ext_ref, carry_ref, sec, u_ref, w, bias, rows):
    ext_ref[0:SUBLANES, :] = carry_ref[sec]
    ext_ref[SUBLANES:SUBLANES + rows, :] = u_ref[...]
    acc = ext_ref[SUBLANES:SUBLANES + rows, :] * w[CONV_W - 1:CONV_W, :]
    for j in range(CONV_W - 1):
        off = SUBLANES - (CONV_W - 1) + j
        acc = acc + ext_ref[off:off + rows, :] * w[j:j + 1, :]
    if bias is not None:
        acc = acc + bias
    carry_ref[sec] = u_ref[rows - SUBLANES:rows, :]
    return _silu(acc)


def _gdn_prompt_kernel(uq0, uq1, uk0, uk1, uv0, uv1, z0_ref, z1_ref, tail_ref, cw_ref, pa_ref, pb_ref, nw_ref,
                       tri_ref, ones_ref,
                       o_ref, sout_ref,
                       s_ref, carry_ref, ext_ref, cq_ref, ck_ref, cv_ref, *, rows):
    i = pl.program_id(1)
    nchunk = rows // CHUNK
    hh = GDN_HEADS
    hw = GDN_HALF_W

    @pl.when(i == 0)
    def _():
        s_ref[...] = jnp.zeros_like(s_ref)
        carry_ref[...] = jnp.zeros_like(carry_ref)

    for sec, (halves, c_ref) in enumerate((((uq0, uq1), cq_ref), ((uk0, uk1), ck_ref), ((uv0, uv1), cv_ref))):
        for half, u_ref in enumerate(halves):
            ws = slice(sec * GDN_QK_W + half * hw, sec * GDN_QK_W + (half + 1) * hw)
            c_ref[:, half * hw:(half + 1) * hw] = _conv_section(ext_ref, carry_ref, 2 * sec + half, u_ref,
                                                                 cw_ref[:, ws], None, rows)

    t = tail_ref[...]
    g_all = -jnp.exp(pa_ref[...]) * _softplus(t + pb_ref[...])
    beta_all = _sigmoid(t)
    g_parts = _split3(g_all)
    gcum_all = _sel_dot(tri_ref[...], g_parts)
    gtot_all = _sel_dot(ones_ref[...], g_parts)
    gcum_t = jnp.transpose(gcum_all)

    ri = _iota2((1, CHUNK, CHUNK), 1)
    ci = _iota2((1, CHUNK, CHUNK), 2)
    incl = ri >= ci
    strict = ri > ci

    def rs(c):
        return slice(c * CHUNK, (c + 1) * CHUNK)

    def hs(h):
        return slice(h * GDN_DK, (h + 1) * GDN_DK)

    def stack(fn):
        return jnp.stack([fn(c, h) for c in range(nchunk) for h in range(hh)], axis=0)

    q = stack(lambda c, h: cq_ref[rs(c), hs(h)])
    k = stack(lambda c, h: ck_ref[rs(c), hs(h)])
    v = stack(lambda c, h: cv_ref[rs(c), hs(h)])
    gc = stack(lambda c, h: gcum_all[rs(c), h:h + 1])
    gt = stack(lambda c, h: gtot_all[rs(c), h:h + 1])
    bc = stack(lambda c, h: beta_all[rs(c), hh + h:hh + h + 1])
    gr = stack(lambda c, h: gcum_t[h:h + 1, rs(c)])

    q = q * lax.rsqrt(jnp.sum(q * q, axis=-1, keepdims=True) + EPS) * (GDN_DK ** -0.5)
    k = k * lax.rsqrt(jnp.sum(k * k, axis=-1, keepdims=True) + EPS)
    decay = jnp.exp(jnp.where(incl, gc - gr, NEG_INF))
    kb = k * bc
    a = jnp.where(strict, _bdot_nt(kb, k) * decay, 0.0)
    y = _unit_lower_inverse(a, ri, ci)
    eg = jnp.exp(gc)
    rhs = jnp.concatenate([v * bc, kb * eg], axis=2)
    sol = rhs + _bdot(y, rhs)
    u = sol[:, :, :GDN_DV]
    w = sol[:, :, GDN_DV:]
    wq = jnp.concatenate([w, q * eg], axis=1)
    attn = _bdot_nt(q, k) * decay
    kdt = jnp.swapaxes(k * jnp.exp(gt - gc), 1, 2)
    egt = jnp.exp(gt)

    for c in range(nchunk):
        bs = slice(c * hh, (c + 1) * hh)
        s = s_ref[...]
        ws_qs = _bdot(wq[bs], s)
        v_new = u[bs] - ws_qs[:, :CHUNK]
        o = ws_qs[:, CHUNK:] + _bdot(attn[bs], v_new)
        s_ref[...] = s * egt[bs] + _bdot(kdt[bs], v_new)
        for h in range(hh):
            z_ref = z0_ref if h < hh // 2 else z1_ref
            zc = z_ref[rs(c), hs(h % (hh // 2))]
            og = _rms_rows(o[h], nw_ref[...]) * _silu(zc)
            o_ref[rs(c), hs(h)] = og.astype(o_ref.dtype)

    @pl.when(i == pl.num_programs(1) - 1)
    def _():
        sout_ref[0] = s_ref[...]


def _chunk_masks(rows):
    r = jnp.arange(rows)
    same = (r[:, None] // CHUNK) == (r[None, :] // CHUNK)
    tri = (same & (r[:, None] >= r[None, :])).astype(BF16)
    ones = same.astype(BF16)
    return tri, ones


def _gdn_prompt(p_main, p_tail, conv_w, pa, pb, norm_w, layer, n_layers, prev, bsz, t, *, rows):
    nblk = t // rows
    hw = GDN_HALF_W
    hh = GDN_HEADS
    tri, ones = _chunk_masks(rows)

    def col(base):
        return lambda b, i: (b * nblk + i, base // hw)

    const = lambda b, i: (0, 0)
    in_specs = [pl.BlockSpec((rows, hw), col(EV_GQKV + k * hw)) for k in range(6)]
    in_specs += [pl.BlockSpec((rows, hw), col(EV_Z)), pl.BlockSpec((rows, hw), col(EV_Z + hw))]
    in_specs += [
        pl.BlockSpec((rows, LANES), lambda b, i: (b * nblk + i, 0)),
        pl.BlockSpec((None, CONV_W, GDN_CONV_CH), lambda b, i: (layer, 0, 0)),
        pl.BlockSpec((1, LANES), const),
        pl.BlockSpec((1, LANES), const),
        pl.BlockSpec((1, GDN_DV), const),
        pl.BlockSpec((rows, rows), const),
        pl.BlockSpec((rows, rows), const),
    ]
    args = [p_main] * 8 + [p_tail, conv_w, pa, pb, norm_w, tri, ones]
    extra_specs, extra_args, aliases = _alias_prev(prev, len(args), 1)
    kern = functools.partial(_gdn_prompt_kernel, rows=rows)
    if prev is not None:
        kern = _drop_ref(kern, len(args))
    return pl.pallas_call(
        kern,
        grid=(bsz, nblk),
        in_specs=in_specs + extra_specs,
        out_specs=[
            pl.BlockSpec((rows, GDN_V_W), lambda b, i: (b * nblk + i, 0)),
            pl.BlockSpec((None, 1, hh, GDN_DK, GDN_DV), lambda b, i: (layer, b, 0, 0, 0)),
        ],
        out_shape=[
            jax.ShapeDtypeStruct((bsz * t, GDN_V_W), BF16),
            jax.ShapeDtypeStruct((n_layers, bsz, hh, GDN_DK, GDN_DV), F32),
        ],
        scratch_shapes=[
            pltpu.VMEM((hh, GDN_DK, GDN_DV), F32),
            pltpu.VMEM((6, SUBLANES, hw), F32),
            pltpu.VMEM((rows + SUBLANES, hw), F32),
            pltpu.VMEM((rows, GDN_QK_W), F32),
            pltpu.VMEM((rows, GDN_QK_W), F32),
            pltpu.VMEM((rows, GDN_V_W), F32),
        ],
        input_output_aliases=aliases,
        compiler_params=_params(("parallel", "arbitrary")),
        name="gdn_prompt",
    )(*args, *extra_args)


def _ssd_prompt_kernel(z_ref, xa_ref, xb_ref, xc3_ref, tail_ref, cw_ref, cb_ref, pa_ref, pb_ref, de_ref, nw_ref,
                       ex_ref, tri_ref,
                       y_ref, hout_ref,
                       ht_ref, carry_ref, ext_ref, xc_ref, cumt_ref, dtt_ref):
    i = pl.program_id(1)
    rows = CHUNK

    @pl.when(i == 0)
    def _():
        ht_ref[...] = jnp.zeros_like(ht_ref)
        carry_ref[...] = jnp.zeros_like(carry_ref)

    for sec, u_ref in enumerate((xa_ref, xb_ref, xc3_ref)):
        ss = slice(sec * SSD_SEC_W, (sec + 1) * SSD_SEC_W)
        xc_ref[:, ss] = _conv_section(ext_ref, carry_ref, sec, u_ref, cw_ref[:, ss], cb_ref[:, ss], rows)

    dt = _softplus(tail_ref[...] + pb_ref[...])
    da = dt * (-jnp.exp(pa_ref[...]))
    cum = _sel_dot(tri_ref[...], _split3(da))
    cumt_ref[...] = jnp.transpose(cum)
    dtt_ref[...] = jnp.transpose(dt)
    cum_parts = _split3(cum)
    dt_parts = _split3(dt)

    ri = _iota2((rows, rows), 0)
    ci = _iota2((rows, rows), 1)
    tri = ri >= ci
    lane = _iota2((rows, LANES), 1)
    lo_half = lane < SSM_HEAD_DIM

    for g in range(SSM_GROUPS):
        gs = slice(g * SSM_GROUP_W, (g + 1) * SSM_GROUP_W)
        bmat = xc_ref[:, SSM_D_INNER + g * SSM_STATE:SSM_D_INNER + (g + 1) * SSM_STATE]
        cmat = xc_ref[:, SSM_D_INNER + SSM_BC_W + g * SSM_STATE:SSM_D_INNER + SSM_BC_W + (g + 1) * SSM_STATE]
        xg = xc_ref[:, gs]
        cb = _dot_nt(cmat, bmat)
        exg = ex_ref[:, gs]
        cum_e = _dot_sel(cum_parts, exg)
        dt_e = _dot_sel(dt_parts, exg)
        htg = ht_ref[g]
        y = _dot(cmat, htg) * jnp.exp(cum_e)
        pieces = []
        for pr in range(SSM_REP // 2):
            scs = []
            for half in range(2):
                h = g * SSM_REP + 2 * pr + half
                ccol = cum[:, h:h + 1]
                crow = cumt_ref[h:h + 1, :]
                drow = dtt_ref[h:h + 1, :]
                lmat = jnp.exp(jnp.where(tri, ccol - crow, NEG_INF))
                scs.append((cb * lmat * drow).astype(BF16))
            xp = xg[:, pr * LANES:(pr + 1) * LANES]
            rhs = jnp.concatenate([jnp.where(lo_half, xp, 0.0), jnp.where(lo_half, 0.0, xp)], axis=0)
            pieces.append(jnp.dot(jnp.concatenate(scs, axis=1), rhs.astype(BF16),
                                  preferred_element_type=F32))
        y = y + jnp.concatenate(pieces, axis=1) + de_ref[:, gs] * xg
        last_e = cum_e[rows - 1:rows, :]
        wts = jnp.exp(last_e - cum_e) * dt_e
        ht_ref[g] = htg * jnp.exp(last_e) + _dot(jnp.transpose(bmat), xg * wts)
        zg = z_ref[:, gs]
        yz = y * _silu(zg)
        y_ref[:, gs] = _rms_rows(yz, nw_ref[:, gs]).astype(y_ref.dtype)

    @pl.when(i == pl.num_programs(1) - 1)
    def _():
        for g in range(SSM_GROUPS):
            hg = jnp.transpose(ht_ref[g])
            hout_ref[0, g * SSM_REP:(g + 1) * SSM_REP] = hg.reshape(SSM_REP, SSM_HEAD_DIM, SSM_STATE)


def _head_expand():
    h = jnp.arange(LANES)[:, None]
    c = jnp.arange(SSM_D_INNER)[None, :]
    return ((c // SSM_HEAD_DIM) == h).astype(BF16)


def _ssd_prompt(p_main, p_tail, conv_w, conv_b, pa, pb, d_e, norm_w, ex, layer, n_layers, prev, bsz, t):
    rows = CHUNK
    nblk = t // rows
    tri, _ = _chunk_masks(rows)
    const = lambda b, i: (0, 0)
    row = lambda b, i: (b * nblk + i, 0)
    args = [p_main, p_main, p_main, p_main, p_tail, conv_w, conv_b, pa, pb, d_e, norm_w, ex, tri]
    extra_specs, extra_args, aliases = _alias_prev(prev, len(args), 1)
    kern = _ssd_prompt_kernel if prev is None else _drop_ref(_ssd_prompt_kernel, len(args))
    return pl.pallas_call(
        kern,
        grid=(bsz, nblk),
        input_output_aliases=aliases,
        in_specs=[
            pl.BlockSpec((rows, SSM_D_INNER), row),
            pl.BlockSpec((rows, SSD_SEC_W), lambda b, i: (b * nblk + i, OD_XBC // SSD_SEC_W)),
            pl.BlockSpec((rows, SSD_SEC_W), lambda b, i: (b * nblk + i, OD_XBC // SSD_SEC_W + 1)),
            pl.BlockSpec((rows, SSD_SEC_W), lambda b, i: (b * nblk + i, OD_XBC // SSD_SEC_W + 2)),
            pl.BlockSpec((rows, LANES), row),
            pl.BlockSpec((CONV_W, SSM_CONV_CH), const),
            pl.BlockSpec((1, SSM_CONV_CH), const),
            pl.BlockSpec((1, LANES), const),
            pl.BlockSpec((1, LANES), const),
            pl.BlockSpec((1, SSM_D_INNER), const),
            pl.BlockSpec((1, SSM_D_INNER), const),
            pl.BlockSpec((LANES, SSM_D_INNER), const),
            pl.BlockSpec((rows, rows), const),
        ] + extra_specs,
        out_specs=[
            pl.BlockSpec((rows, SSM_D_INNER), row),
            pl.BlockSpec((None, 1, SSM_HEADS, SSM_HEAD_DIM, SSM_STATE), lambda b, i: (layer, b, 0, 0, 0)),
        ],
        out_shape=[
            jax.ShapeDtypeStruct((bsz * t, SSM_D_INNER), BF16),
            jax.ShapeDtypeStruct((n_layers, bsz, SSM_HEADS, SSM_HEAD_DIM, SSM_STATE), F32),
        ],
        scratch_shapes=[
            pltpu.VMEM((SSM_GROUPS, SSM_STATE, SSM_GROUP_W), F32),
            pltpu.VMEM((3, SUBLANES, SSD_SEC_W), F32),
            pltpu.VMEM((rows + SUBLANES, SSD_SEC_W), F32),
            pltpu.VMEM((rows, SSM_CONV_CH), F32),
            pltpu.VMEM((LANES, rows), F32),
            pltpu.VMEM((LANES, rows), F32),
        ],
        compiler_params=_params(("parallel", "arbitrary")),
        name="ssd_prompt",
    )(*args, *extra_args)


def _swa_sample_kernel(sink_ref, q_ref, kn_ref, vn_ref, ck_ref, cv_ref, o_ref):
    for b in range(q_ref.shape[0]):
        _swa_sample_row(sink_ref, q_ref, kn_ref, vn_ref, ck_ref, cv_ref, o_ref, b)


def _swa_sample_row(sink_ref, q_ref, kn_ref, vn_ref, ck_ref, cv_ref, o_ref, b):
    scale = SWA_HEAD_DIM ** -0.5
    q = q_ref[b].astype(BF16)
    kn = kn_ref[b].astype(BF16).astype(F32)
    vn = vn_ref[b].astype(BF16).astype(F32)
    ck = ck_ref[b].astype(BF16)
    cv = cv_ref[b].astype(BF16)
    sink = sink_ref[...]
    valid = _iota2((SWA_GROUP, WINDOW), 1) >= 1
    outs = []
    for c in range(SWA_KV_HEADS):
        cs = slice(c * SWA_HEAD_DIM, (c + 1) * SWA_HEAD_DIM)
        gs = slice(c * SWA_GROUP, (c + 1) * SWA_GROUP)
        qc = q[gs, :]
        s = lax.dot_general(qc, ck[:, cs], (((1,), (1,)), ((), ())), preferred_element_type=F32) * scale
        s = jnp.where(valid, s, NEG_INF)
        sn = jnp.sum(qc.astype(F32) * kn[:, cs], axis=-1, keepdims=True) * scale
        sk = sink[gs, :]
        mx = jnp.maximum(jnp.maximum(jnp.max(s, axis=-1, keepdims=True), sn), sk)
        p = jnp.exp(s - mx)
        pn = jnp.exp(sn - mx)
        den = jnp.sum(p, axis=-1, keepdims=True) + pn + jnp.exp(sk - mx)
        o = jnp.dot(p.astype(BF16), cv[:, cs], preferred_element_type=F32) + pn * vn[:, cs]
        outs.append(o / den)
    o_ref[b] = jnp.concatenate(outs, axis=0).astype(o_ref.dtype)


def _swa_sample(q, kn, vn, ck_all, cv_all, sinks, layer):
    nb = q.shape[0]
    kvw = SWA_KV_W
    bb = SUBLANES if nb % SUBLANES == 0 else 1
    b3 = lambda b: (b, 0, 0)
    cache = pl.BlockSpec((None, bb, WINDOW, kvw), lambda b: (layer, b, 0, 0))
    return pl.pallas_call(
        _swa_sample_kernel,
        grid=(nb // bb,),
        in_specs=[
            pl.BlockSpec((SWA_HEADS, 1), lambda b: (0, 0)),
            pl.BlockSpec((bb, SWA_HEADS, SWA_HEAD_DIM), b3),
            pl.BlockSpec((bb, 1, kvw), b3),
            pl.BlockSpec((bb, 1, kvw), b3),
            cache,
            cache,
        ],
        out_specs=pl.BlockSpec((bb, SWA_HEADS, SWA_HEAD_DIM), b3),
        out_shape=jax.ShapeDtypeStruct((nb, SWA_HEADS, SWA_HEAD_DIM), F32),
        compiler_params=_params(("parallel",)),
        name="swa_sample",
    )(sinks, q, kn, vn, ck_all, cv_all)


def _gdn_sample_kernel(u_ref, prev_ref, w_ref, s_ref, z_ref, ab_ref, pa_ref, pb_ref, nw_ref,
                       o_ref, sout_ref, cnew_ref):
    hh = GDN_HEADS
    u = u_ref[0]
    pv = prev_ref[0]
    w = w_ref[...]
    x = u * w[CONV_W - 1]
    for j in range(CONV_W - 1):
        x = x + pv[j] * w[j]
        cnew_ref[0, j] = pv[j + 1] if j + 1 < CONV_W - 1 else u
    x = _silu(x)
    q = x[0:hh]
    k = x[hh:2 * hh]
    v = x[2 * hh:3 * hh]
    q = q * lax.rsqrt(jnp.sum(q * q, axis=-1, keepdims=True) + EPS) * (GDN_DK ** -0.5)
    k = k * lax.rsqrt(jnp.sum(k * k, axis=-1, keepdims=True) + EPS)
    ab = ab_ref[0]
    dec = jnp.exp(-jnp.exp(pa_ref[...]) * _softplus(ab[0:hh] + pb_ref[...]))
    beta = _sigmoid(ab[hh:2 * hh])
    qk = jnp.concatenate([q, k, jnp.zeros((LANES - 2 * hh, GDN_DK), F32)], axis=0)
    qkt = jnp.transpose(qk)
    outs = []
    for h in range(hh):
        s = s_ref[0, h] * dec[h:h + 1, :]
        kcol = qkt[:, hh + h:hh + h + 1]
        ks = jnp.sum(s * kcol, axis=0, keepdims=True)
        delta = beta[h:h + 1, :] * (v[h:h + 1, :] - ks)
        s = s + kcol * delta
        sout_ref[0, h] = s
        outs.append(jnp.sum(s * qkt[:, h:h + 1], axis=0, keepdims=True))
    o = jnp.concatenate(outs, axis=0)
    o_ref[0] = _rms_rows(o, nw_ref[...]) * _silu(z_ref[0])


def _gdn_sample(u, prev, w, s_all, z, ab, pa, pb, nw, layer, prev_out):
    nb = u.shape[0]
    hh = GDN_HEADS
    n_layers = s_all.shape[0]
    b3 = lambda b: (b, 0, 0)
    b4 = lambda b: (b, 0, 0, 0)
    c2 = lambda b: (0, 0)
    state_spec = pl.BlockSpec((None, 1, hh, GDN_DK, GDN_DV), lambda b: (layer, b, 0, 0, 0))
    args = [u, prev, w, s_all, z, ab, pa, pb, nw]
    extra_specs, extra_args, aliases = _alias_prev(prev_out, len(args), 1)
    kern = _gdn_sample_kernel if prev_out is None else _drop_ref(_gdn_sample_kernel, len(args))
    return pl.pallas_call(
        kern,
        grid=(nb,),
        input_output_aliases=aliases,
        in_specs=[
            pl.BlockSpec((1, 3 * hh, GDN_DK), b3),
            pl.BlockSpec((1, CONV_W - 1, 3 * hh, GDN_DK), b4),
            pl.BlockSpec((CONV_W, 3 * hh, GDN_DK), lambda b: (0, 0, 0)),
            state_spec,
            pl.BlockSpec((1, hh, GDN_DV), b3),
            pl.BlockSpec((1, 2 * hh, LANES), b3),
            pl.BlockSpec((hh, LANES), c2),
            pl.BlockSpec((hh, LANES), c2),
            pl.BlockSpec((1, GDN_DV), c2),
        ] + extra_specs,
        out_specs=[
            pl.BlockSpec((1, hh, GDN_DV), b3),
            state_spec,
            pl.BlockSpec((1, CONV_W - 1, 3 * hh, GDN_DK), b4),
        ],
        out_shape=[
            jax.ShapeDtypeStruct((nb, hh, GDN_DV), F32),
            jax.ShapeDtypeStruct((n_layers, nb, hh, GDN_DK, GDN_DV), F32),
            jax.ShapeDtypeStruct((nb, CONV_W - 1, 3 * hh, GDN_DK), F32),
        ],
        compiler_params=_params(("parallel",)),
        name="gdn_sample",
    )(*args, *extra_args)


def _ssd_prep_kernel(x_ref, prev_ref, cw_ref, cb_ref, tail_ref, pa_ref, pb_ref,
                     xc_ref, dt_ref, dec_ref, cnew_ref):
    u = x_ref[...]
    w = cw_ref[...]
    acc = u * w[CONV_W - 1:CONV_W, :] + cb_ref[...]
    for j in range(CONV_W - 1):
        acc = acc + prev_ref[j] * w[j:j + 1, :]
        cnew_ref[j] = prev_ref[j + 1] if j + 1 < CONV_W - 1 else u
    xc_ref[...] = _silu(acc)
    dt = _softplus(tail_ref[...] + pb_ref[...])
    dt_ref[...] = dt
    dec_ref[...] = jnp.exp(dt * (-jnp.exp(pa_ref[...])))


def _ssd_prep_sample(p_main, prev_t, cw, cb, tail, pa, pb):
    nb = p_main.shape[0]
    sec = SSD_SEC_W
    nsec = SSM_CONV_CH // sec
    c2 = lambda s: (0, 0)
    return pl.pallas_call(
        _ssd_prep_kernel,
        grid=(nsec,),
        in_specs=[
            pl.BlockSpec((nb, sec), lambda s: (0, OD_XBC // sec + s)),
            pl.BlockSpec((CONV_W - 1, nb, sec), lambda s: (0, 0, s)),
            pl.BlockSpec((CONV_W, sec), lambda s: (0, s)),
            pl.BlockSpec((1, sec), lambda s: (0, s)),
            pl.BlockSpec((nb, LANES), c2),
            pl.BlockSpec((1, LANES), c2),
            pl.BlockSpec((1, LANES), c2),
        ],
        out_specs=[
            pl.BlockSpec((nb, sec), lambda s: (0, s)),
            pl.BlockSpec((nb, LANES), c2),
            pl.BlockSpec((nb, LANES), c2),
            pl.BlockSpec((CONV_W - 1, nb, sec), lambda s: (0, 0, s)),
        ],
        out_shape=[
            jax.ShapeDtypeStruct((nb, SSM_CONV_CH), F32),
            jax.ShapeDtypeStruct((nb, LANES), F32),
            jax.ShapeDtypeStruct((nb, LANES), F32),
            jax.ShapeDtypeStruct((CONV_W - 1, nb, SSM_CONV_CH), F32),
        ],
        compiler_params=_params(("arbitrary",)),
        name="ssd_prep_sample",
    )(p_main, prev_t, cw, cb, tail, pa, pb)


def _ssd_state_kernel(dec_ref, x_ref, dt_ref, b_ref, c_ref, z_ref, de_ref, nw_ref, ex_ref, h_ref,
                      y_ref, hout_ref, *, nb):
    g = pl.program_id(0)
    x = x_ref[...]
    dtx = x * _dot_sel(_split3(dt_ref[...]), ex_ref[...])
    zrows = LANES - nb
    xt = jnp.transpose(jnp.concatenate([dtx, jnp.zeros((zrows, SSM_GROUP_W), F32)], axis=0))
    xh, xl = _split2(xt)
    bpad = jnp.concatenate([b_ref[...], jnp.zeros((zrows, SSM_STATE), F32)], axis=0)
    cpad_t = jnp.transpose(jnp.concatenate([c_ref[...], jnp.zeros((zrows, SSM_STATE), F32)], axis=0))
    rowi = _iota2((LANES, SSM_STATE), 0)
    coli = _iota2((SSM_STATE, LANES), 1)
    yt = jnp.zeros((SSM_GROUP_W, LANES), F32)
    for b in range(nb):
        bh, bl = _split2(jnp.where(rowi == b, bpad, 0.0))
        upd = (jnp.dot(xh, bh, preferred_element_type=F32) + jnp.dot(xh, bl, preferred_element_type=F32)
               + jnp.dot(xl, bh, preferred_element_type=F32))
        parts = []
        for r in range(SSM_REP):
            hn = h_ref[b, r] * dec_ref[b, g * SSM_REP + r] + upd[r * SSM_HEAD_DIM:(r + 1) * SSM_HEAD_DIM]
            hout_ref[b, r] = hn
            parts.append(hn)
        hh_, hl_ = _split2(jnp.concatenate(parts, axis=0))
        ch, cl = _split2(jnp.where(coli == b, cpad_t, 0.0))
        yt = yt + (jnp.dot(hh_, ch, preferred_element_type=F32) + jnp.dot(hh_, cl, preferred_element_type=F32)
                   + jnp.dot(hl_, ch, preferred_element_type=F32))
    y = jnp.transpose(yt)[0:nb, :] + de_ref[...] * x
    y_ref[...] = _rms_rows(y * _silu(z_ref[...]), nw_ref[...]).astype(y_ref.dtype)


def _ssd_state_sample(dtv, dec, xc, p_main, d_e, nw, ex, h_all, layer, prev_out):
    nb = xc.shape[0]
    assert nb <= LANES
    gw = SSM_GROUP_W
    n_layers = h_all.shape[0]
    state_spec = pl.BlockSpec((None, nb, SSM_REP, SSM_HEAD_DIM, SSM_STATE), lambda g: (layer, 0, g, 0, 0))
    args = [dec, xc, dtv, xc, xc, p_main, d_e, nw, ex, h_all]
    extra_specs, extra_args, aliases = _alias_prev(prev_out, len(args), 1)
    kern = functools.partial(_ssd_state_kernel, nb=nb)
    if prev_out is not None:
        kern = _drop_ref(kern, len(args))
    return pl.pallas_call(
        kern,
        grid=(SSM_GROUPS,),
        input_output_aliases=aliases,
        in_specs=[
            pl.BlockSpec(memory_space=pltpu.SMEM),
            pl.BlockSpec((nb, gw), lambda g: (0, g)),
            pl.BlockSpec((nb, LANES), lambda g: (0, 0)),
            pl.BlockSpec((nb, SSM_STATE), lambda g: (0, SSM_D_INNER // SSM_STATE + g)),
            pl.BlockSpec((nb, SSM_STATE), lambda g: (0, (SSM_D_INNER + SSM_BC_W) // SSM_STATE + g)),
            pl.BlockSpec((nb, gw), lambda g: (0, g)),
            pl.BlockSpec((1, gw), lambda g: (0, g)),
            pl.BlockSpec((1, gw), lambda g: (0, g)),
            pl.BlockSpec((LANES, gw), lambda g: (0, g)),
            state_spec,
        ] + extra_specs,
        out_specs=[
            pl.BlockSpec((nb, gw), lambda g: (0, g)),
            state_spec,
        ],
        out_shape=[
            jax.ShapeDtypeStruct((nb, SSM_D_INNER), BF16),
            jax.ShapeDtypeStruct((n_layers, nb, SSM_HEADS, SSM_HEAD_DIM, SSM_STATE), F32),
        ],
        compiler_params=_params(("arbitrary",)),
        name="ssd_state_sample",
    )(*args, *extra_args)


def _lane_pad_row(v):
    return jnp.zeros((1, LANES), F32).at[0, :v.shape[0]].set(v.astype(F32))


def kernel(x_prompt, x_sample, cache_swa_k, cache_swa_v, state_gdn, state_gdn_conv, state_ssm, state_ssm_conv,
           norm_ffn1, norm_mix, norm_ffn2, norm_final, w_ffn_gate, w_ffn_up, w_ffn_down,
           w_in_even, w_out_even, attn_sinks, gdn_conv_w, gdn_A_log, gdn_dt_bias, gdn_norm_w,
           w_in_odd, w_out_odd, ssm_conv_w, ssm_conv_b, ssm_A_log, ssm_dt_bias, ssm_D, ssm_norm_w):
    bsz, t, d = x_prompt.shape
    nb = x_sample.shape[0]
    depth = norm_ffn1.shape[0]
    n_even, n_odd = (depth + 1) // 2, depth // 2
    mp = bsz * t
    assert x_sample.shape[1] == 1 and d == D_MODEL and t % (2 * CHUNK) == 0
    hh = GDN_HEADS

    w_even_t = jnp.swapaxes(w_in_even, 1, 2)
    w_odd_t = jnp.swapaxes(w_in_odd, 1, 2)
    ex = _head_expand()
    cache_k = cache_swa_k.reshape(n_even, nb, WINDOW, SWA_KV_W)
    cache_v = cache_swa_v.reshape(n_even, nb, WINDOW, SWA_KV_W)

    xp = x_prompt.reshape(mp, d)
    xs = x_sample.reshape(nb, d)
    gain_f = norm_final.reshape(1, d)
    ffn_tm = _tile(mp, FFN_TM)

    small = {k: [] for k in ("pk", "pv", "pgc", "psc", "sk", "sv", "sgc", "ssc")}
    p_gdn = p_ssm = s_gdn = s_ssm = None

    def ffn_pair(xp, xs, gain, l, which, final):
        gain = gain.reshape(1, d)
        yp, ys, wg, wu, wd = _ffn_first(xp, xs, gain, gain_f, w_ffn_gate, w_ffn_up, w_ffn_down, l, which,
                                        final_norm=final, tm=ffn_tm, tf=FFN_FIRST_TF)
        yp = _ffn_rest(xp, gain, gain_f, wg, wu, wd, yp, final_norm=final, tm=ffn_tm, tf=FFN_TF)
        return yp, ys

    for l in range(depth):
        xp, xs = ffn_pair(xp, xs, norm_ffn1[l], l, 0, False)
        gmix = norm_mix[l].reshape(1, d)
        if l % 2 == 0:
            e = l // 2
            ptm, otm = _tile(mp, PROJ_EVEN_TM), _tile(mp, OUT_EVEN_TM)
            pm, pt, sm, st, w_main, w_tail = _proj_first(xp, xs, gmix, w_even_t, e, EV_MAIN, tm=ptm, tn=FIRST_TN)
            pm, pt = _proj_rest(xp, gmix, w_main, w_tail, pm, pt, tm=ptm, tn=PROJ_EVEN_TN)
            pa = _lane_pad_row(gdn_A_log[e])
            pb = _lane_pad_row(gdn_dt_bias[e])
            nw = gdn_norm_w[e].reshape(1, GDN_DV).astype(F32)

            kn = sm[:, EV_K:EV_K + SWA_KV_W]
            vn = sm[:, EV_V:EV_V + SWA_KV_W]
            o_a_s = _swa_sample(sm[:, EV_Q:EV_Q + SWA_Q_W].reshape(nb, SWA_HEADS, SWA_HEAD_DIM),
                                kn.reshape(nb, 1, SWA_KV_W), vn.reshape(nb, 1, SWA_KV_W), cache_k, cache_v,
                                attn_sinks[e].astype(F32).reshape(SWA_HEADS, 1), e)
            ab = jnp.broadcast_to(st[:, :2 * hh, None], (nb, 2 * hh, LANES))
            o_b_s, s_gdn, gc_s = _gdn_sample(
                sm[:, EV_GQKV:EV_GQKV + GDN_CONV_CH].reshape(nb, 3 * hh, GDN_DK),
                state_gdn_conv[e].reshape(nb, CONV_W - 1, 3 * hh, GDN_DK),
                gdn_conv_w[e].reshape(CONV_W, 3 * hh, GDN_DK),
                state_gdn, sm[:, EV_Z:EV_Z + GDN_V_W].reshape(nb, hh, GDN_DV), ab,
                jnp.broadcast_to(gdn_A_log[e].astype(F32)[:, None], (hh, LANES)),
                jnp.broadcast_to(gdn_dt_bias[e].astype(F32)[:, None], (hh, LANES)), nw, e, s_gdn)
            small["sk"].append(kn.reshape(nb, 1, SWA_KV_HEADS, SWA_HEAD_DIM))
            small["sv"].append(vn.reshape(nb, 1, SWA_KV_HEADS, SWA_HEAD_DIM))
            small["sgc"].append(gc_s.reshape(nb, CONV_W - 1, GDN_CONV_CH))
            outs_s = [o_a_s.reshape(nb, SWA_Q_W).astype(BF16), o_b_s.reshape(nb, GDN_V_W).astype(BF16)]

            o_a_p = _swa_prompt(pm, attn_sinks[e].astype(F32), bsz, t)
            o_b_p, p_gdn = _gdn_prompt(pm, pt, gdn_conv_w, pa, pb, nw, e, n_even, p_gdn, bsz, t, rows=2 * CHUNK)
            pm3 = pm.reshape(bsz, t, EV_MAIN)
            small["pk"].append(pm3[:, t - WINDOW:, EV_K:EV_K + SWA_KV_W].reshape(bsz, WINDOW, SWA_KV_HEADS, SWA_HEAD_DIM))
            small["pv"].append(pm3[:, t - WINDOW:, EV_V:EV_V + SWA_KV_W].reshape(bsz, WINDOW, SWA_KV_HEADS, SWA_HEAD_DIM))
            small["pgc"].append(pm3[:, t - (CONV_W - 1):, EV_GQKV:EV_GQKV + GDN_CONV_CH])
            yp, xs, w_out = _outproj_first(xp, xs, [o_a_p, o_b_p], outs_s, w_out_even, e, tm=otm, tn=FIRST_TN)
            xp = _outproj_rest(xp, [o_a_p, o_b_p], w_out, yp, tm=otm, tn=OUT_EVEN_TN)
        else:
            o = l // 2
            ptm, otm = _tile(mp, PROJ_ODD_TM), _tile(mp, OUT_ODD_TM)
            pm, pt, sm, st, w_main, w_tail = _proj_first(xp, xs, gmix, w_odd_t, o, OD_MAIN, tm=ptm, tn=FIRST_TN)
            pm, pt = _proj_rest(xp, gmix, w_main, w_tail, pm, pt, tm=ptm, tn=PROJ_ODD_TN)
            pa = _lane_pad_row(ssm_A_log[o])
            pb = _lane_pad_row(ssm_dt_bias[o])
            d_e = jnp.repeat(ssm_D[o].astype(F32), SSM_HEAD_DIM).reshape(1, SSM_D_INNER)
            nw = ssm_norm_w[o].reshape(1, SSM_D_INNER).astype(F32)
            cb = ssm_conv_b[o].reshape(1, SSM_CONV_CH)

            xc, dtv, dec, cnew = _ssd_prep_sample(sm, jnp.swapaxes(state_ssm_conv[o], 0, 1), ssm_conv_w[o], cb,
                                                  st, pa, pb)
            y_s, s_ssm = _ssd_state_sample(dtv, dec, xc, sm, d_e, nw, ex, state_ssm, o, s_ssm)
            small["ssc"].append(jnp.swapaxes(cnew, 0, 1))

            y_p, p_ssm = _ssd_prompt(pm, pt, ssm_conv_w[o], cb, pa, pb, d_e, nw, ex, o, n_odd, p_ssm, bsz, t)
            pm3 = pm.reshape(bsz, t, OD_MAIN)
            small["psc"].append(pm3[:, t - (CONV_W - 1):, OD_XBC:OD_XBC + SSM_CONV_CH])
            yp, xs, w_out = _outproj_first(xp, xs, [y_p], [y_s], w_out_odd, o, tm=otm, tn=FIRST_TN)
            xp = _outproj_rest(xp, [y_p], w_out, yp, tm=otm, tn=OUT_ODD_TN)
        xp, xs = ffn_pair(xp, xs, norm_ffn2[l], l, 1, l == depth - 1)

    st_ = {k: jnp.stack(v) for k, v in small.items()}
    return (xp.reshape(bsz, t, d), xs.reshape(nb, 1, d),
            st_["pk"], st_["pv"], p_gdn, st_["pgc"], p_ssm, st_["psc"],
            st_["sk"], st_["sv"], s_gdn, st_["sgc"], s_ssm, st_["ssc"])
```

```python
import functools

import jax
import jax.numpy as jnp
from jax import lax
from jax.experimental import pallas as pl
from jax.experimental.pallas import tpu as pltpu

F32 = jnp.float32
BF16 = jnp.bfloat16

D_MODEL = 2048
SWA_HEADS = 16
SWA_KV_HEADS = 4
SWA_GROUP = SWA_HEADS // SWA_KV_HEADS
SWA_HEAD_DIM = 64
WINDOW = 128
GDN_HEADS = 8
GDN_DK = 128
GDN_DV = 128
CONV_W = 4
SSM_D_INNER = 2 * D_MODEL
SSM_HEAD_DIM = 64
SSM_HEADS = SSM_D_INNER // SSM_HEAD_DIM
SSM_GROUPS = 8
SSM_REP = SSM_HEADS // SSM_GROUPS
SSM_STATE = 128
EPS = 1e-6
NEG_INF = -1e30

SWA_Q_W = SWA_HEADS * SWA_HEAD_DIM
SWA_KV_W = SWA_KV_HEADS * SWA_HEAD_DIM
GDN_QK_W = GDN_HEADS * GDN_DK
GDN_V_W = GDN_HEADS * GDN_DV
GDN_CONV_CH = 2 * GDN_QK_W + GDN_V_W
GDN_HALF_W = GDN_QK_W // 2
SSM_BC_W = SSM_GROUPS * SSM_STATE
SSM_CONV_CH = SSM_D_INNER + 2 * SSM_BC_W
SSM_GROUP_W = SSM_REP * SSM_HEAD_DIM
SSD_SEC_W = SSM_CONV_CH // 3

EV_Q = 0
EV_K = EV_Q + SWA_Q_W
EV_V = EV_K + SWA_KV_W
EV_GQKV = EV_V + SWA_KV_W
EV_Z = EV_GQKV + GDN_CONV_CH
EV_MAIN = EV_Z + GDN_V_W
OD_XBC = SSM_D_INNER
OD_MAIN = SSM_D_INNER + SSM_CONV_CH

LANES = 128
SUBLANES = 8
CHUNK = 128
VMEM_LIMIT = 56 * 1024 * 1024

FFN_TM, FFN_TF = 512, 512
PROJ_EVEN_TM, PROJ_EVEN_TN = 512, EV_MAIN // 2
PROJ_ODD_TM, PROJ_ODD_TN = 1024, 1024
OUT_EVEN_TM, OUT_EVEN_TN = 512, D_MODEL
OUT_ODD_TM, OUT_ODD_TN = 1024, 1024
FFN_FIRST_TF, FIRST_TN = 256, 512


def _params(sem):
    return pltpu.CompilerParams(dimension_semantics=sem, vmem_limit_bytes=VMEM_LIMIT)


def _tile(m, pref):
    return pref if m % pref == 0 else m


def _sigmoid(x):
    return 1.0 / (1.0 + jnp.exp(-x))


def _silu(x):
    return x * _sigmoid(x)


def _softplus(x):
    return jnp.maximum(x, 0.0) + jnp.log1p(jnp.exp(-jnp.abs(x)))


def _dot(a, b):
    return jnp.dot(a.astype(BF16), b.astype(BF16), preferred_element_type=F32)


def _dot_nt(a, b):
    return lax.dot_general(a.astype(BF16), b.astype(BF16), (((1,), (1,)), ((), ())),
                           preferred_element_type=F32)


def _bdot(a, b):
    return jnp.einsum("bij,bjk->bik", a.astype(BF16), b.astype(BF16), preferred_element_type=F32)


def _bdot_nt(a, b):
    return jnp.einsum("bid,bjd->bij", a.astype(BF16), b.astype(BF16), preferred_element_type=F32)


def _split2(a):
    hi = a.astype(BF16)
    lo = (a - hi.astype(F32)).astype(BF16)
    return hi, lo


def _split3(a):
    hi = a.astype(BF16)
    r = a - hi.astype(F32)
    mid = r.astype(BF16)
    lo = (r - mid.astype(F32)).astype(BF16)
    return hi, mid, lo


def _sel_dot(m01, parts):
    out = jnp.dot(m01, parts[0], preferred_element_type=F32)
    for p in parts[1:]:
        out = out + jnp.dot(m01, p, preferred_element_type=F32)
    return out


def _dot_sel(parts, m01):
    out = jnp.dot(parts[0], m01, preferred_element_type=F32)
    for p in parts[1:]:
        out = out + jnp.dot(p, m01, preferred_element_type=F32)
    return out


def _rms_rows(x, gain):
    ms = jnp.mean(x * x, axis=-1, keepdims=True)
    return x * lax.rsqrt(ms + EPS) * gain


def _iota2(shape, dim):
    return lax.broadcasted_iota(jnp.int32, shape, dim)


def _drop_ref(kern, pos):
    def entry(*refs):
        return kern(*refs[:pos], *refs[pos + 1:])
    return entry


def _alias_prev(prev, n_in, out_idx):
    if prev is None:
        return [], [], {}
    return [pl.BlockSpec(memory_space=pl.ANY)], [prev], {n_in: out_idx}


def _ffn_kernel(x_ref, g_ref, gf_ref, wg_ref, wu_ref, wd_ref, *rest, final_norm, cast):
    if cast:
        o_ref, wgb_ref, wub_ref, wdb_ref, h_ref = rest
    else:
        o_ref, h_ref = rest
    j = pl.program_id(1)

    @pl.when(j == 0)
    def _():
        h_ref[...] = _rms_rows(x_ref[...], g_ref[...]).astype(BF16)
        o_ref[...] = jnp.zeros_like(o_ref)

    wg, wu, wd = wg_ref[...], wu_ref[...], wd_ref[...]
    if cast:
        wg, wu, wd = wg.astype(BF16), wu.astype(BF16), wd.astype(BF16)
        wgb_ref[...] = wg
        wub_ref[...] = wu
        wdb_ref[...] = wd
    h = h_ref[...]
    a = jnp.dot(h, wg, preferred_element_type=F32)
    b = jnp.dot(h, wu, preferred_element_type=F32)
    t = (_silu(a) * b).astype(BF16)
    o_ref[...] += jnp.dot(t, wd, preferred_element_type=F32)

    @pl.when(j == pl.num_programs(1) - 1)
    def _():
        y = x_ref[...] + 0.5 * o_ref[...]
        if final_norm:
            y = _rms_rows(y, gf_ref[...])
        o_ref[...] = y


def _ffn(x, gain, gain_final, wg, wu, wd, sel, *, final_norm, tm, tf):
    m, d = x.shape
    f = wg.shape[-1]
    cast = sel is not None
    if cast:
        assert m == tm
        layer, which = sel
        w_specs = [pl.BlockSpec((None, None, d, tf), lambda i, j: (layer, which, 0, j)),
                   pl.BlockSpec((None, None, d, tf), lambda i, j: (layer, which, 0, j)),
                   pl.BlockSpec((None, None, tf, d), lambda i, j: (layer, which, j, 0))]
    else:
        w_specs = [pl.BlockSpec((d, tf), lambda i, j: (0, j)),
                   pl.BlockSpec((d, tf), lambda i, j: (0, j)),
                   pl.BlockSpec((tf, d), lambda i, j: (j, 0))]
    out_specs = [pl.BlockSpec((tm, d), lambda i, j: (i, 0))]
    out_shape = [jax.ShapeDtypeStruct((m, d), F32)]
    if cast:
        out_specs += [pl.BlockSpec((d, tf), lambda i, j: (0, j)),
                      pl.BlockSpec((d, tf), lambda i, j: (0, j)),
                      pl.BlockSpec((tf, d), lambda i, j: (j, 0))]
        out_shape += [jax.ShapeDtypeStruct((d, f), BF16), jax.ShapeDtypeStruct((d, f), BF16),
                      jax.ShapeDtypeStruct((f, d), BF16)]
    return pl.pallas_call(
        functools.partial(_ffn_kernel, final_norm=final_norm, cast=cast),
        grid=(m // tm, f // tf),
        in_specs=[
            pl.BlockSpec((tm, d), lambda i, j: (i, 0)),
            pl.BlockSpec((1, d), lambda i, j: (0, 0)),
            pl.BlockSpec((1, d), lambda i, j: (0, 0)),
        ] + w_specs,
        out_specs=out_specs,
        out_shape=out_shape,
        scratch_shapes=[pltpu.VMEM((tm, d), BF16)],
        compiler_params=_params(("parallel", "arbitrary")),
        name="ffn_half",
    )(x, gain, gain_final, wg, wu, wd)


def _proj_kernel(x_ref, g_ref, w_ref, wt_ref, *rest, cast):
    if cast:
        o_ref, ot_ref, wb_ref, wtb_ref, h_ref = rest
    else:
        o_ref, ot_ref, h_ref = rest
    j = pl.program_id(1)

    @pl.when(j == 0)
    def _():
        h = _rms_rows(x_ref[...], g_ref[...]).astype(BF16)
        h_ref[...] = h
        wt = wt_ref[...]
        if cast:
            pad = jnp.zeros((LANES - wt.shape[0], wt.shape[1]), F32)
            wt = jnp.transpose(jnp.concatenate([wt, pad], axis=0)).astype(BF16)
            wtb_ref[...] = wt
        ot_ref[...] = jnp.dot(h, wt, preferred_element_type=F32)

    w = w_ref[...]
    if cast:
        w = jnp.transpose(w).astype(BF16)
        wb_ref[...] = w
    o_ref[...] = jnp.dot(h_ref[...], w, preferred_element_type=F32)


def _proj(x, gain, w_main, w_tail, layer, n, *, tm, tn):
    m, d = x.shape
    cast = layer is not None
    if cast:
        n_tail = w_main.shape[1] - n
        assert m == tm and n % n_tail == 0 and n_tail % SUBLANES == 0 and n_tail <= LANES
        w_specs = [pl.BlockSpec((None, tn, d), lambda i, j: (layer, j, 0)),
                   pl.BlockSpec((None, n_tail, d), lambda i, j: (layer, n // n_tail, 0))]
    else:
        w_specs = [pl.BlockSpec((d, tn), lambda i, j: (0, j)),
                   pl.BlockSpec((d, LANES), lambda i, j: (0, 0))]
    out_specs = [pl.BlockSpec((tm, tn), lambda i, j: (i, j)),
                 pl.BlockSpec((tm, LANES), lambda i, j: (i, 0))]
    out_shape = [jax.ShapeDtypeStruct((m, n), F32), jax.ShapeDtypeStruct((m, LANES), F32)]
    if cast:
        out_specs += [pl.BlockSpec((d, tn), lambda i, j: (0, j)),
                      pl.BlockSpec((d, LANES), lambda i, j: (0, 0))]
        out_shape += [jax.ShapeDtypeStruct((d, n), BF16), jax.ShapeDtypeStruct((d, LANES), BF16)]
    return pl.pallas_call(
        functools.partial(_proj_kernel, cast=cast),
        grid=(m // tm, n // tn),
        in_specs=[
            pl.BlockSpec((tm, d), lambda i, j: (i, 0)),
            pl.BlockSpec((1, d), lambda i, j: (0, 0)),
        ] + w_specs,
        out_specs=out_specs,
        out_shape=out_shape,
        scratch_shapes=[pltpu.VMEM((tm, d), BF16)],
        compiler_params=_params(("parallel", "arbitrary")),
        name="norm_proj",
    )(x, gain, w_main, w_tail)


def _outproj_kernel(*refs, n_in, cast):
    x_ref = refs[0]
    o_refs = refs[1:1 + n_in]
    w_refs = refs[1 + n_in:1 + 2 * n_in]
    out_ref = refs[1 + 2 * n_in]
    acc = x_ref[...]
    for kb, (o_ref, w_ref) in enumerate(zip(o_refs, w_refs)):
        w = w_ref[...]
        if cast:
            w = w.astype(BF16)
            kw = w.shape[0]
            refs[2 + 2 * n_in][kb * kw:(kb + 1) * kw, :] = w
        acc = acc + jnp.dot(o_ref[...], w, preferred_element_type=F32)
    out_ref[...] = acc


def _outproj(x, outs, w, layer, *, tm, tn):
    m, d = x.shape
    n_in = len(outs)
    kw = outs[0].shape[1]
    k_all = kw * n_in
    assert all(o.shape[1] == kw for o in outs) and k_all == w.shape[-2]
    cast = layer is not None
    in_specs = [pl.BlockSpec((tm, tn), lambda i, j: (i, j))]
    for o in outs:
        in_specs.append(pl.BlockSpec((tm, kw), lambda i, j: (i, 0)))
    for kb in range(n_in):
        if cast:
            in_specs.append(pl.BlockSpec((None, kw, tn), lambda i, j, kb=kb: (layer, kb, j)))
        else:
            in_specs.append(pl.BlockSpec((kw, tn), lambda i, j, kb=kb: (kb, j)))
    out_specs = [pl.BlockSpec((tm, tn), lambda i, j: (i, j))]
    out_shape = [jax.ShapeDtypeStruct((m, d), F32)]
    if cast:
        assert m == tm
        out_specs.append(pl.BlockSpec((k_all, tn), lambda i, j: (0, j)))
        out_shape.append(jax.ShapeDtypeStruct((k_all, d), BF16))
    return pl.pallas_call(
        functools.partial(_outproj_kernel, n_in=n_in, cast=cast),
        grid=(m // tm, d // tn),
        in_specs=in_specs,
        out_specs=out_specs,
        out_shape=out_shape,
        compiler_params=_params(("parallel", "arbitrary")),
        name="out_proj",
    )(x, *outs, *([w] * n_in))


def _ffn_rows(h, wg, wu, wd):
    a = jnp.dot(h, wg, preferred_element_type=F32)
    b = jnp.dot(h, wu, preferred_element_type=F32)
    return jnp.dot((_silu(a) * b).astype(BF16), wd, preferred_element_type=F32)


def _ffn_first_kernel(xp_ref, xs_ref, g_ref, gf_ref, wg_ref, wu_ref, wd_ref,
                      op_ref, os_ref, wgb_ref, wub_ref, wdb_ref, hp_ref, hs_ref, *, final_norm):
    j = pl.program_id(0)
    streams = ((xp_ref, hp_ref, op_ref), (xs_ref, hs_ref, os_ref))

    @pl.when(j == 0)
    def _():
        for x_ref, h_ref, o_ref in streams:
            h_ref[...] = _rms_rows(x_ref[...], g_ref[...]).astype(BF16)
            o_ref[...] = jnp.zeros_like(o_ref)

    wg, wu, wd = wg_ref[...].astype(BF16), wu_ref[...].astype(BF16), wd_ref[...].astype(BF16)
    wgb_ref[...] = wg
    wub_ref[...] = wu
    wdb_ref[...] = wd
    for x_ref, h_ref, o_ref in streams:
        o_ref[...] += _ffn_rows(h_ref[...], wg, wu, wd)

    @pl.when(j == pl.num_programs(0) - 1)
    def _():
        for x_ref, h_ref, o_ref in streams:
            y = x_ref[...] + 0.5 * o_ref[...]
            if final_norm:
                y = _rms_rows(y, gf_ref[...])
            o_ref[...] = y


def _ffn_first(xp, xs, gain, gain_final, wg, wu, wd, layer, which, *, final_norm, tm, tf):
    mp, d = xp.shape
    nb = xs.shape[0]
    f = wg.shape[-1]
    c2 = lambda j: (0, 0)
    return pl.pallas_call(
        functools.partial(_ffn_first_kernel, final_norm=final_norm),
        grid=(f // tf,),
        in_specs=[
            pl.BlockSpec((tm, d), c2),
            pl.BlockSpec((nb, d), c2),
            pl.BlockSpec((1, d), c2),
            pl.BlockSpec((1, d), c2),
            pl.BlockSpec((None, None, d, tf), lambda j: (layer, which, 0, j)),
            pl.BlockSpec((None, None, d, tf), lambda j: (layer, which, 0, j)),
            pl.BlockSpec((None, None, tf, d), lambda j: (layer, which, j, 0)),
        ],
        out_specs=[
            pl.BlockSpec((tm, d), c2),
            pl.BlockSpec((nb, d), c2),
            pl.BlockSpec((d, tf), lambda j: (0, j)),
            pl.BlockSpec((d, tf), lambda j: (0, j)),
            pl.BlockSpec((tf, d), lambda j: (j, 0)),
        ],
        out_shape=[
            jax.ShapeDtypeStruct((mp, d), F32), jax.ShapeDtypeStruct((nb, d), F32),
            jax.ShapeDtypeStruct((d, f), BF16), jax.ShapeDtypeStruct((d, f), BF16),
            jax.ShapeDtypeStruct((f, d), BF16),
        ],
        scratch_shapes=[pltpu.VMEM((tm, d), BF16), pltpu.VMEM((nb, d), BF16)],
        compiler_params=_params(("arbitrary",)),
        name="ffn_first",
    )(xp, xs, gain, gain_final, wg, wu, wd)


def _ffn_rest_kernel(x_ref, g_ref, gf_ref, wg_ref, wu_ref, wd_ref, prev_ref, o_ref, h_ref, *, final_norm):
    del prev_ref
    j = pl.program_id(1)

    @pl.when(j == 0)
    def _():
        h_ref[...] = _rms_rows(x_ref[...], g_ref[...]).astype(BF16)
        o_ref[...] = jnp.zeros_like(o_ref)

    o_ref[...] += _ffn_rows(h_ref[...], wg_ref[...], wu_ref[...], wd_ref[...])

    @pl.when(j == pl.num_programs(1) - 1)
    def _():
        y = x_ref[...] + 0.5 * o_ref[...]
        if final_norm:
            y = _rms_rows(y, gf_ref[...])
        o_ref[...] = y


def _ffn_rest(x, gain, gain_final, wg, wu, wd, prev, *, final_norm, tm, tf):
    m, d = x.shape
    f = wg.shape[-1]
    if m == tm:
        return prev
    return pl.pallas_call(
        functools.partial(_ffn_rest_kernel, final_norm=final_norm),
        grid=(m // tm - 1, f // tf),
        in_specs=[
            pl.BlockSpec((tm, d), lambda i, j: (i + 1, 0)),
            pl.BlockSpec((1, d), lambda i, j: (0, 0)),
            pl.BlockSpec((1, d), lambda i, j: (0, 0)),
            pl.BlockSpec((d, tf), lambda i, j: (0, j)),
            pl.BlockSpec((d, tf), lambda i, j: (0, j)),
            pl.BlockSpec((tf, d), lambda i, j: (j, 0)),
            pl.BlockSpec(memory_space=pl.ANY),
        ],
        out_specs=pl.BlockSpec((tm, d), lambda i, j: (i + 1, 0)),
        out_shape=jax.ShapeDtypeStruct((m, d), F32),
        scratch_shapes=[pltpu.VMEM((tm, d), BF16)],
        input_output_aliases={6: 0},
        compiler_params=_params(("parallel", "arbitrary")),
        name="ffn_rest",
    )(x, gain, gain_final, wg, wu, wd, prev)


def _proj_first_kernel(xp_ref, xs_ref, g_ref, w_ref, wt_ref,
                       op_ref, otp_ref, os_ref, ots_ref, wb_ref, wtb_ref, hp_ref, hs_ref):
    j = pl.program_id(0)
    streams = ((xp_ref, hp_ref, op_ref, otp_ref), (xs_ref, hs_ref, os_ref, ots_ref))

    @pl.when(j == 0)
    def _():
        wt = wt_ref[...]
        pad = jnp.zeros((LANES - wt.shape[0], wt.shape[1]), F32)
        wt = jnp.transpose(jnp.concatenate([wt, pad], axis=0)).astype(BF16)
        wtb_ref[...] = wt
        for x_ref, h_ref, _, ot_ref in streams:
            h = _rms_rows(x_ref[...], g_ref[...]).astype(BF16)
            h_ref[...] = h
            ot_ref[...] = jnp.dot(h, wt, preferred_element_type=F32)

    w = jnp.transpose(w_ref[...].astype(BF16))
    wb_ref[...] = w
    for _, h_ref, o_ref, _ in streams:
        o_ref[...] = jnp.dot(h_ref[...], w, preferred_element_type=F32)


def _proj_first(xp, xs, gain, w_t, layer, n, *, tm, tn):
    mp, d = xp.shape
    nb = xs.shape[0]
    n_tail = w_t.shape[1] - n
    assert n % n_tail == 0 and n_tail % SUBLANES == 0 and n_tail <= LANES
    c2 = lambda j: (0, 0)
    return pl.pallas_call(
        _proj_first_kernel,
        grid=(n // tn,),
        in_specs=[
            pl.BlockSpec((tm, d), c2),
            pl.BlockSpec((nb, d), c2),
            pl.BlockSpec((1, d), c2),
            pl.BlockSpec((None, tn, d), lambda j: (layer, j, 0)),
            pl.BlockSpec((None, n_tail, d), lambda j: (layer, n // n_tail, 0)),
        ],
        out_specs=[
            pl.BlockSpec((tm, tn), lambda j: (0, j)),
            pl.BlockSpec((tm, LANES), c2),
            pl.BlockSpec((nb, tn), lambda j: (0, j)),
            pl.BlockSpec((nb, LANES), c2),
            pl.BlockSpec((d, tn), lambda j: (0, j)),
            pl.BlockSpec((d, LANES), c2),
        ],
        out_shape=[
            jax.ShapeDtypeStruct((mp, n), F32), jax.ShapeDtypeStruct((mp, LANES), F32),
            jax.ShapeDtypeStruct((nb, n), F32), jax.ShapeDtypeStruct((nb, LANES), F32),
            jax.ShapeDtypeStruct((d, n), BF16), jax.ShapeDtypeStruct((d, LANES), BF16),
        ],
        scratch_shapes=[pltpu.VMEM((tm, d), BF16), pltpu.VMEM((nb, d), BF16)],
        compiler_params=_params(("arbitrary",)),
        name="proj_first",
    )(xp, xs, gain, w_t, w_t)


def _proj_rest_kernel(x_ref, g_ref, w_ref, wt_ref, prev_ref, prevt_ref, o_ref, ot_ref, h_ref):
    del prev_ref, prevt_ref
    j = pl.program_id(1)

    @pl.when(j == 0)
    def _():
        h = _rms_rows(x_ref[...], g_ref[...]).astype(BF16)
        h_ref[...] = h
        ot_ref[...] = jnp.dot(h, wt_ref[...], preferred_element_type=F32)

    o_ref[...] = jnp.dot(h_ref[...], w_ref[...], preferred_element_type=F32)


def _proj_rest(x, gain, w_main, w_tail, prev_main, prev_tail, *, tm, tn):
    m, d = x.shape
    n = w_main.shape[1]
    if m == tm:
        return prev_main, prev_tail
    return pl.pallas_call(
        _proj_rest_kernel,
        grid=(m // tm - 1, n // tn),
        in_specs=[
            pl.BlockSpec((tm, d), lambda i, j: (i + 1, 0)),
            pl.BlockSpec((1, d), lambda i, j: (0, 0)),
            pl.BlockSpec((d, tn), lambda i, j: (0, j)),
            pl.BlockSpec((d, LANES), lambda i, j: (0, 0)),
            pl.BlockSpec(memory_space=pl.ANY),
            pl.BlockSpec(memory_space=pl.ANY),
        ],
        out_specs=[
            pl.BlockSpec((tm, tn), lambda i, j: (i + 1, j)),
            pl.BlockSpec((tm, LANES), lambda i, j: (i + 1, 0)),
        ],
        out_shape=[jax.ShapeDtypeStruct((m, n), F32), jax.ShapeDtypeStruct((m, LANES), F32)],
        scratch_shapes=[pltpu.VMEM((tm, d), BF16)],
        input_output_aliases={4: 0, 5: 1},
        compiler_params=_params(("parallel", "arbitrary")),
        name="proj_rest",
    )(x, gain, w_main, w_tail, prev_main, prev_tail)


def _outproj_first_kernel(*refs, n_in):
    xp_ref, xs_ref = refs[0], refs[1]
    op_refs = refs[2:2 + n_in]
    os_refs = refs[2 + n_in:2 + 2 * n_in]
    w_refs = refs[2 + 2 * n_in:2 + 3 * n_in]
    yp_ref, ys_ref, wb_ref = refs[2 + 3 * n_in:]
    accp = xp_ref[...]
    accs = xs_ref[...]
    for kb in range(n_in):
        w = w_refs[kb][...].astype(BF16)
        kw = w.shape[0]
        wb_ref[kb * kw:(kb + 1) * kw, :] = w
        accp = accp + jnp.dot(op_refs[kb][...], w, preferred_element_type=F32)
        accs = accs + jnp.dot(os_refs[kb][...], w, preferred_element_type=F32)
    yp_ref[...] = accp
    ys_ref[...] = accs


def _outproj_first(xp, xs, outs_p, outs_s, w_all, layer, *, tm, tn):
    mp, d = xp.shape
    nb = xs.shape[0]
    n_in = len(outs_p)
    kw = outs_p[0].shape[1]
    k_all = kw * n_in
    assert all(o.shape[1] == kw for o in outs_p + outs_s) and k_all == w_all.shape[1]
    in_specs = [pl.BlockSpec((tm, tn), lambda j: (0, j)), pl.BlockSpec((nb, tn), lambda j: (0, j))]
    in_specs += [pl.BlockSpec((tm, kw), lambda j: (0, 0)) for _ in range(n_in)]
    in_specs += [pl.BlockSpec((nb, kw), lambda j: (0, 0)) for _ in range(n_in)]
    in_specs += [pl.BlockSpec((None, kw, tn), lambda j, kb=kb: (layer, kb, j)) for kb in range(n_in)]
    return pl.pallas_call(
        functools.partial(_outproj_first_kernel, n_in=n_in),
        grid=(d // tn,),
        in_specs=in_specs,
        out_specs=[
            pl.BlockSpec((tm, tn), lambda j: (0, j)),
            pl.BlockSpec((nb, tn), lambda j: (0, j)),
            pl.BlockSpec((k_all, tn), lambda j: (0, j)),
        ],
        out_shape=[
            jax.ShapeDtypeStruct((mp, d), F32), jax.ShapeDtypeStruct((nb, d), F32),
            jax.ShapeDtypeStruct((k_all, d), BF16),
        ],
        compiler_params=_params(("arbitrary",)),
        name="outproj_first",
    )(xp, xs, *outs_p, *outs_s, *([w_all] * n_in))


def _outproj_rest_kernel(*refs, n_in):
    x_ref = refs[0]
    o_refs = refs[1:1 + n_in]
    w_refs = refs[1 + n_in:1 + 2 * n_in]
    out_ref = refs[2 + 2 * n_in]
    acc = x_ref[...]
    for o_ref, w_ref in zip(o_refs, w_refs):
        acc = acc + jnp.dot(o_ref[...], w_ref[...], preferred_element_type=F32)
    out_ref[...] = acc


def _outproj_rest(x, outs, w, prev, *, tm, tn):
    m, d = x.shape
    n_in = len(outs)
    kw = outs[0].shape[1]
    if m == tm:
        return prev
    in_specs = [pl.BlockSpec((tm, tn), lambda i, j: (i + 1, j))]
    in_specs += [pl.BlockSpec((tm, kw), lambda i, j: (i + 1, 0)) for _ in range(n_in)]
    in_specs += [pl.BlockSpec((kw, tn), lambda i, j, kb=kb: (kb, j)) for kb in range(n_in)]
    in_specs.append(pl.BlockSpec(memory_space=pl.ANY))
    return pl.pallas_call(
        functools.partial(_outproj_rest_kernel, n_in=n_in),
        grid=(m // tm - 1, d // tn),
        in_specs=in_specs,
        out_specs=pl.BlockSpec((tm, tn), lambda i, j: (i + 1, j)),
        out_shape=jax.ShapeDtypeStruct((m, d), F32),
        input_output_aliases={1 + 2 * n_in: 0},
        compiler_params=_params(("parallel", "arbitrary")),
        name="outproj_rest",
    )(x, *outs, *([w] * n_in), prev)


def _swa_prompt_kernel(sink_ref, q_ref, kc_ref, kp_ref, vc_ref, vp_ref, o_ref):
    n = pl.program_id(1)
    w = WINDOW
    qi = _iota2((w, 2 * w), 0)
    kj = _iota2((w, 2 * w), 1)
    valid = (kj > qi) & (kj <= qi + w) & ((n > 0) | (kj >= w))
    scale = SWA_HEAD_DIM ** -0.5
    kk = jnp.concatenate([kp_ref[...], kc_ref[...]], axis=0).astype(BF16)
    vv = jnp.concatenate([vp_ref[...], vc_ref[...]], axis=0).astype(BF16)
    q = q_ref[...].astype(BF16)
    outs = []
    for h in range(SWA_HEADS):
        c = h // SWA_GROUP
        qh = q[:, h * SWA_HEAD_DIM:(h + 1) * SWA_HEAD_DIM]
        kh = kk[:, c * SWA_HEAD_DIM:(c + 1) * SWA_HEAD_DIM]
        vh = vv[:, c * SWA_HEAD_DIM:(c + 1) * SWA_HEAD_DIM]
        s = lax.dot_general(qh, kh, (((1,), (1,)), ((), ())), preferred_element_type=F32) * scale
        s = jnp.where(valid, s, NEG_INF)
        sink = sink_ref[h]
        mx = jnp.maximum(jnp.max(s, axis=-1, keepdims=True), sink)
        p = jnp.exp(s - mx)
        den = jnp.sum(p, axis=-1, keepdims=True) + jnp.exp(sink - mx)
        p = p / den
        outs.append(jnp.dot(p.astype(BF16), vh, preferred_element_type=F32))
    o_ref[...] = jnp.concatenate(outs, axis=-1).astype(o_ref.dtype)


def _swa_prompt(p_main, sinks, bsz, t):
    nb = t // WINDOW
    kblk = EV_K // SWA_KV_W
    vblk = EV_V // SWA_KV_W

    def cur(col):
        return lambda b, n: (b * nb + n, col)

    def prev(col):
        return lambda b, n: (b * nb + jnp.maximum(n - 1, 0), col)

    return pl.pallas_call(
        _swa_prompt_kernel,
        grid=(bsz, nb),
        in_specs=[
            pl.BlockSpec(memory_space=pltpu.SMEM),
            pl.BlockSpec((WINDOW, SWA_Q_W), cur(EV_Q // SWA_Q_W)),
            pl.BlockSpec((WINDOW, SWA_KV_W), cur(kblk)),
            pl.BlockSpec((WINDOW, SWA_KV_W), prev(kblk)),
            pl.BlockSpec((WINDOW, SWA_KV_W), cur(vblk)),
            pl.BlockSpec((WINDOW, SWA_KV_W), prev(vblk)),
        ],
        out_specs=pl.BlockSpec((WINDOW, SWA_Q_W), cur(0)),
        out_shape=jax.ShapeDtypeStruct((bsz * t, SWA_Q_W), BF16),
        compiler_params=_params(("parallel", "arbitrary")),
        name="swa_prompt",
    )(sinks, p_main, p_main, p_main, p_main, p_main)


def _unit_lower_inverse(a, ri, ci):
    a0 = jnp.where((ri >> 4) == (ci >> 4), a, 0.0)
    y = -a0
    p = _bdot(a0, a0)
    for step in range(3):
        y = y + p + _bdot(y, p)
        if step < 2:
            p = _bdot(p, p)
    for sh in (4, 5, 6):
        e = jnp.where(((ri >> (sh + 1)) == (ci >> (sh + 1))) & ((ri >> sh) != (ci >> sh)), a, 0.0)
        t = e + _bdot(y, e)
        y = y - t - _bdot(t, y)
    return y


def _conv_section(ext_ref, carry_ref, sec, u_ref, w, bias, rows):
    ext_ref[0:SUBLANES, :] = carry_ref[sec]
    ext_ref[SUBLANES:SUBLANES + rows, :] = u_ref[...]
    acc = ext_ref[SUBLANES:SUBLANES + rows, :] * w[CONV_W - 1:CONV_W, :]
    for j in range(CONV_W - 1):
        off = SUBLANES - (CONV_W - 1) + j
        acc = acc + ext_ref[off:off + rows, :] * w[j:j + 1, :]
    if bias is not None:
        acc = acc + bias
    carry_ref[sec] = u_ref[rows - SUBLANES:rows, :]
    return _silu(acc)


def _gdn_prompt_kernel(uq0, uq1, uk0, uk1, uv0, uv1, z0_ref, z1_ref, tail_ref, cw_ref, pa_ref, pb_ref, nw_ref,
                       tri_ref, ones_ref,
                       o_ref, sout_ref,
                       s_ref, carry_ref, ext_ref, cq_ref, ck_ref, cv_ref, *, rows):
    i = pl.program_id(1)
    nchunk = rows // CHUNK
    hh = GDN_HEADS
    hw = GDN_HALF_W

    @pl.when(i == 0)
    def _():
        s_ref[...] = jnp.zeros_like(s_ref)
        carry_ref[...] = jnp.zeros_like(carry_ref)

    for sec, (halves, c_ref) in enumerate((((uq0, uq1), cq_ref), ((uk0, uk1), ck_ref), ((uv0, uv1), cv_ref))):
        for half, u_ref in enumerate(halves):
            ws = slice(sec * GDN_QK_W + half * hw, sec * GDN_QK_W + (half + 1) * hw)
            c_ref[:, half * hw:(half + 1) * hw] = _conv_section(ext_ref, carry_ref, 2 * sec + half, u_ref,
                                                                 cw_ref[:, ws], None, rows)

    t = tail_ref[...]
    g_all = -jnp.exp(pa_ref[...]) * _softplus(t + pb_ref[...])
    beta_all = _sigmoid(t)
    g_parts = _split3(g_all)
    gcum_all = _sel_dot(tri_ref[...], g_parts)
    gtot_all = _sel_dot(ones_ref[...], g_parts)
    gcum_t = jnp.transpose(gcum_all)

    ri = _iota2((1, CHUNK, CHUNK), 1)
    ci = _iota2((1, CHUNK, CHUNK), 2)
    incl = ri >= ci
    strict = ri > ci

    def rs(c):
        return slice(c * CHUNK, (c + 1) * CHUNK)

    def hs(h):
        return slice(h * GDN_DK, (h + 1) * GDN_DK)

    def stack(fn):
        return jnp.stack([fn(c, h) for c in range(nchunk) for h in range(hh)], axis=0)

    q = stack(lambda c, h: cq_ref[rs(c), hs(h)])
    k = stack(lambda c, h: ck_ref[rs(c), hs(h)])
    v = stack(lambda c, h: cv_ref[rs(c), hs(h)])
    gc = stack(lambda c, h: gcum_all[rs(c), h:h + 1])
    gt = stack(lambda c, h: gtot_all[rs(c), h:h + 1])
    bc = stack(lambda c, h: beta_all[rs(c), hh + h:hh + h + 1])
    gr = stack(lambda c, h: gcum_t[h:h + 1, rs(c)])

    q = q * lax.rsqrt(jnp.sum(q * q, axis=-1, keepdims=True) + EPS) * (GDN_DK ** -0.5)
    k = k * lax.rsqrt(jnp.sum(k * k, axis=-1, keepdims=True) + EPS)
    decay = jnp.exp(jnp.where(incl, gc - gr, NEG_INF))
    kb = k * bc
    a = jnp.where(strict, _bdot_nt(kb, k) * decay, 0.0)
    y = _unit_lower_inverse(a, ri, ci)
    eg = jnp.exp(gc)
    rhs = jnp.concatenate([v * bc, kb * eg], axis=2)
    sol = rhs + _bdot(y, rhs)
    u = sol[:, :, :GDN_DV]
    w = sol[:, :, GDN_DV:]
    wq = jnp.concatenate([w, q * eg], axis=1)
    attn = _bdot_nt(q, k) * decay
    kdt = jnp.swapaxes(k * jnp.exp(gt - gc), 1, 2)
    egt = jnp.exp(gt)

    for c in range(nchunk):
        bs = slice(c * hh, (c + 1) * hh)
        s = s_ref[...]
        ws_qs = _bdot(wq[bs], s)
        v_new = u[bs] - ws_qs[:, :CHUNK]
        o = ws_qs[:, CHUNK:] + _bdot(attn[bs], v_new)
        s_ref[...] = s * egt[bs] + _bdot(kdt[bs], v_new)
        for h in range(hh):
            z_ref = z0_ref if h < hh // 2 else z1_ref
            zc = z_ref[rs(c), hs(h % (hh // 2))]
            og = _rms_rows(o[h], nw_ref[...]) * _silu(zc)
            o_ref[rs(c), hs(h)] = og.astype(o_ref.dtype)

    @pl.when(i == pl.num_programs(1) - 1)
    def _():
        sout_ref[0] = s_ref[...]


def _chunk_masks(rows):
    r = jnp.arange(rows)
    same = (r[:, None] // CHUNK) == (r[None, :] // CHUNK)
    tri = (same & (r[:, None] >= r[None, :])).astype(BF16)
    ones = same.astype(BF16)
    return tri, ones


def _gdn_prompt(p_main, p_tail, conv_w, pa, pb, norm_w, layer, n_layers, prev, bsz, t, *, rows):
    nblk = t // rows
    hw = GDN_HALF_W
    hh = GDN_HEADS
    tri, ones = _chunk_masks(rows)

    def col(base):
        return lambda b, i: (b * nblk + i, base // hw)

    const = lambda b, i: (0, 0)
    in_specs = [pl.BlockSpec((rows, hw), col(EV_GQKV + k * hw)) for k in range(6)]
    in_specs += [pl.BlockSpec((rows, hw), col(EV_Z)), pl.BlockSpec((rows, hw), col(EV_Z + hw))]
    in_specs += [
        pl.BlockSpec((rows, LANES), lambda b, i: (b * nblk + i, 0)),
        pl.BlockSpec((None, CONV_W, GDN_CONV_CH), lambda b, i: (layer, 0, 0)),
        pl.BlockSpec((1, LANES), const),
        pl.BlockSpec((1, LANES), const),
        pl.BlockSpec((1, GDN_DV), const),
        pl.BlockSpec((rows, rows), const),
        pl.BlockSpec((rows, rows), const),
    ]
    args = [p_main] * 8 + [p_tail, conv_w, pa, pb, norm_w, tri, ones]
    extra_specs, extra_args, aliases = _alias_prev(prev, len(args), 1)
    kern = functools.partial(_gdn_prompt_kernel, rows=rows)
    if prev is not None:
        kern = _drop_ref(kern, len(args))
    return pl.pallas_call(
        kern,
        grid=(bsz, nblk),
        in_specs=in_specs + extra_specs,
        out_specs=[
            pl.BlockSpec((rows, GDN_V_W), lambda b, i: (b * nblk + i, 0)),
            pl.BlockSpec((None, 1, hh, GDN_DK, GDN_DV), lambda b, i: (layer, b, 0, 0, 0)),
        ],
        out_shape=[
            jax.ShapeDtypeStruct((bsz * t, GDN_V_W), BF16),
            jax.ShapeDtypeStruct((n_layers, bsz, hh, GDN_DK, GDN_DV), F32),
        ],
        scratch_shapes=[
            pltpu.VMEM((hh, GDN_DK, GDN_DV), F32),
            pltpu.VMEM((6, SUBLANES, hw), F32),
            pltpu.VMEM((rows + SUBLANES, hw), F32),
            pltpu.VMEM((rows, GDN_QK_W), F32),
            pltpu.VMEM((rows, GDN_QK_W), F32),
            pltpu.VMEM((rows, GDN_V_W), F32),
        ],
        input_output_aliases=aliases,
        compiler_params=_params(("parallel", "arbitrary")),
        name="gdn_prompt",
    )(*args, *extra_args)


def _ssd_prompt_kernel(z_ref, xa_ref, xb_ref, xc3_ref, tail_ref, cw_ref, cb_ref, pa_ref, pb_ref, de_ref, nw_ref,
                       ex_ref, tri_ref,
                       y_ref, hout_ref,
                       ht_ref, carry_ref, ext_ref, xc_ref, cumt_ref, dtt_ref):
    i = pl.program_id(1)
    rows = CHUNK

    @pl.when(i == 0)
    def _():
        ht_ref[...] = jnp.zeros_like(ht_ref)
        carry_ref[...] = jnp.zeros_like(carry_ref)

    for sec, u_ref in enumerate((xa_ref, xb_ref, xc3_ref)):
        ss = slice(sec * SSD_SEC_W, (sec + 1) * SSD_SEC_W)
        xc_ref[:, ss] = _conv_section(ext_ref, carry_ref, sec, u_ref, cw_ref[:, ss], cb_ref[:, ss], rows)

    dt = _softplus(tail_ref[...] + pb_ref[...])
    da = dt * (-jnp.exp(pa_ref[...]))
    cum = _sel_dot(tri_ref[...], _split3(da))
    cumt_ref[...] = jnp.transpose(cum)
    dtt_ref[...] = jnp.transpose(dt)
    cum_parts = _split3(cum)
    dt_parts = _split3(dt)

    ri = _iota2((rows, rows), 0)
    ci = _iota2((rows, rows), 1)
    tri = ri >= ci
    lane = _iota2((rows, LANES), 1)
    lo_half = lane < SSM_HEAD_DIM

    for g in range(SSM_GROUPS):
        gs = slice(g * SSM_GROUP_W, (g + 1) * SSM_GROUP_W)
        bmat = xc_ref[:, SSM_D_INNER + g * SSM_STATE:SSM_D_INNER + (g + 1) * SSM_STATE]
        cmat = xc_ref[:, SSM_D_INNER + SSM_BC_W + g * SSM_STATE:SSM_D_INNER + SSM_BC_W + (g + 1) * SSM_STATE]
        xg = xc_ref[:, gs]
        cb = _dot_nt(cmat, bmat)
        exg = ex_ref[:, gs]
        cum_e = _dot_sel(cum_parts, exg)
        dt_e = _dot_sel(dt_parts, exg)
        htg = ht_ref[g]
        y = _dot(cmat, htg) * jnp.exp(cum_e)
        pieces = []
        for pr in range(SSM_REP // 2):
            scs = []
            for half in range(2):
                h = g * SSM_REP + 2 * pr + half
                ccol = cum[:, h:h + 1]
                crow = cumt_ref[h:h + 1, :]
                drow = dtt_ref[h:h + 1, :]
                lmat = jnp.exp(jnp.where(tri, ccol - crow, NEG_INF))
                scs.append((cb * lmat * drow).astype(BF16))
            xp = xg[:, pr * LANES:(pr + 1) * LANES]
            rhs = jnp.concatenate([jnp.where(lo_half, xp, 0.0), jnp.where(lo_half, 0.0, xp)], axis=0)
            pieces.append(jnp.dot(jnp.concatenate(scs, axis=1), rhs.astype(BF16),
                                  preferred_element_type=F32))
        y = y + jnp.concatenate(pieces, axis=1) + de_ref[:, gs] * xg
        last_e = cum_e[rows - 1:rows, :]
        wts = jnp.exp(last_e - cum_e) * dt_e
        ht_ref[g] = htg * jnp.exp(last_e) + _dot(jnp.transpose(bmat), xg * wts)
        zg = z_ref[:, gs]
        yz = y * _silu(zg)
        y_ref[:, gs] = _rms_rows(yz, nw_ref[:, gs]).astype(y_ref.dtype)

    @pl.when(i == pl.num_programs(1) - 1)
    def _():
        for g in range(SSM_GROUPS):
            hg = jnp.transpose(ht_ref[g])
            hout_ref[0, g * SSM_REP:(g + 1) * SSM_REP] = hg.reshape(SSM_REP, SSM_HEAD_DIM, SSM_STATE)


def _head_expand():
    h = jnp.arange(LANES)[:, None]
    c = jnp.arange(SSM_D_INNER)[None, :]
    return ((c // SSM_HEAD_DIM) == h).astype(BF16)


def _ssd_prompt(p_main, p_tail, conv_w, conv_b, pa, pb, d_e, norm_w, ex, layer, n_layers, prev, bsz, t):
    rows = CHUNK
    nblk = t // rows
    tri, _ = _chunk_masks(rows)
    const = lambda b, i: (0, 0)
    row = lambda b, i: (b * nblk + i, 0)
    args = [p_main, p_main, p_main, p_main, p_tail, conv_w, conv_b, pa, pb, d_e, norm_w, ex, tri]
    extra_specs, extra_args, aliases = _alias_prev(prev, len(args), 1)
    kern = _ssd_prompt_kernel if prev is None else _drop_ref(_ssd_prompt_kernel, len(args))
    return pl.pallas_call(
        kern,
        grid=(bsz, nblk),
        input_output_aliases=aliases,
        in_specs=[
            pl.BlockSpec((rows, SSM_D_INNER), row),
            pl.BlockSpec((rows, SSD_SEC_W), lambda b, i: (b * nblk + i, OD_XBC // SSD_SEC_W)),
            pl.BlockSpec((rows, SSD_SEC_W), lambda b, i: (b * nblk + i, OD_XBC // SSD_SEC_W + 1)),
            pl.BlockSpec((rows, SSD_SEC_W), lambda b, i: (b * nblk + i, OD_XBC // SSD_SEC_W + 2)),
            pl.BlockSpec((rows, LANES), row),
            pl.BlockSpec((CONV_W, SSM_CONV_CH), const),
            pl.BlockSpec((1, SSM_CONV_CH), const),
            pl.BlockSpec((1, LANES), const),
            pl.BlockSpec((1, LANES), const),
            pl.BlockSpec((1, SSM_D_INNER), const),
            pl.BlockSpec((1, SSM_D_INNER), const),
            pl.BlockSpec((LANES, SSM_D_INNER), const),
            pl.BlockSpec((rows, rows), const),
        ] + extra_specs,
        out_specs=[
            pl.BlockSpec((rows, SSM_D_INNER), row),
            pl.BlockSpec((None, 1, SSM_HEADS, SSM_HEAD_DIM, SSM_STATE), lambda b, i: (layer, b, 0, 0, 0)),
        ],
        out_shape=[
            jax.ShapeDtypeStruct((bsz * t, SSM_D_INNER), BF16),
            jax.ShapeDtypeStruct((n_layers, bsz, SSM_HEADS, SSM_HEAD_DIM, SSM_STATE), F32),
        ],
        scratch_shapes=[
            pltpu.VMEM((SSM_GROUPS, SSM_STATE, SSM_GROUP_W), F32),
            pltpu.VMEM((3, SUBLANES, SSD_SEC_W), F32),
            pltpu.VMEM((rows + SUBLANES, SSD_SEC_W), F32),
            pltpu.VMEM((rows, SSM_CONV_CH), F32),
            pltpu.VMEM((LANES, rows), F32),
            pltpu.VMEM((LANES, rows), F32),
        ],
        compiler_params=_params(("parallel", "arbitrary")),
        name="ssd_prompt",
    )(*args, *extra_args)


def _swa_sample_kernel(sink_ref, q_ref, kn_ref, vn_ref, ck_ref, cv_ref, o_ref):
    for b in range(q_ref.shape[0]):
        _swa_sample_row(sink_ref, q_ref, kn_ref, vn_ref, ck_ref, cv_ref, o_ref, b)


def _swa_sample_row(sink_ref, q_ref, kn_ref, vn_ref, ck_ref, cv_ref, o_ref, b):
    scale = SWA_HEAD_DIM ** -0.5
    q = q_ref[b].astype(BF16)
    kn = kn_ref[b].astype(BF16).astype(F32)
    vn = vn_ref[b].astype(BF16).astype(F32)
    ck = ck_ref[b].astype(BF16)
    cv = cv_ref[b].astype(BF16)
    sink = sink_ref[...]
    valid = _iota2((SWA_GROUP, WINDOW), 1) >= 1
    outs = []
    for c in range(SWA_KV_HEADS):
        cs = slice(c * SWA_HEAD_DIM, (c + 1) * SWA_HEAD_DIM)
        gs = slice(c * SWA_GROUP, (c + 1) * SWA_GROUP)
        qc = q[gs, :]
        s = lax.dot_general(qc, ck[:, cs], (((1,), (1,)), ((), ())), preferred_element_type=F32) * scale
        s = jnp.where(valid, s, NEG_INF)
        sn = jnp.sum(qc.astype(F32) * kn[:, cs], axis=-1, keepdims=True) * scale
        sk = sink[gs, :]
        mx = jnp.maximum(jnp.maximum(jnp.max(s, axis=-1, keepdims=True), sn), sk)
        p = jnp.exp(s - mx)
        pn = jnp.exp(sn - mx)
        den = jnp.sum(p, axis=-1, keepdims=True) + pn + jnp.exp(sk - mx)
        o = jnp.dot(p.astype(BF16), cv[:, cs], preferred_element_type=F32) + pn * vn[:, cs]
        outs.append(o / den)
    o_ref[b] = jnp.concatenate(outs, axis=0).astype(o_ref.dtype)


def _swa_sample(q, kn, vn, ck_all, cv_all, sinks, layer):
    nb = q.shape[0]
    kvw = SWA_KV_W
    bb = SUBLANES if nb % SUBLANES == 0 else 1
    b3 = lambda b: (b, 0, 0)
    cache = pl.BlockSpec((None, bb, WINDOW, kvw), lambda b: (layer, b, 0, 0))
    return pl.pallas_call(
        _swa_sample_kernel,
        grid=(nb // bb,),
        in_specs=[
            pl.BlockSpec((SWA_HEADS, 1), lambda b: (0, 0)),
            pl.BlockSpec((bb, SWA_HEADS, SWA_HEAD_DIM), b3),
            pl.BlockSpec((bb, 1, kvw), b3),
            pl.BlockSpec((bb, 1, kvw), b3),
            cache,
            cache,
        ],
        out_specs=pl.BlockSpec((bb, SWA_HEADS, SWA_HEAD_DIM), b3),
        out_shape=jax.ShapeDtypeStruct((nb, SWA_HEADS, SWA_HEAD_DIM), F32),
        compiler_params=_params(("parallel",)),
        name="swa_sample",
    )(sinks, q, kn, vn, ck_all, cv_all)


def _swa_sample_group_kernel(sink_ref, q_ref, kn_ref, vn_ref, ck_ref, cv_ref, o_ref, *, bb):
    scale = SWA_HEAD_DIM ** -0.5
    rows, keys = bb * SWA_GROUP, bb * WINDOW
    ck = ck_ref[...].reshape(keys, SWA_KV_W).astype(BF16)
    cv = cv_ref[...].reshape(keys, SWA_KV_W).astype(BF16)
    kn = kn_ref[...].astype(BF16).astype(F32)
    vn = vn_ref[...].astype(BF16).astype(F32)
    ri = _iota2((rows, keys), 0)
    cj = _iota2((rows, keys), 1)
    valid = ((cj // WINDOW) == (ri // SWA_GROUP)) & ((cj % WINDOW) >= 1)
    for c in range(SWA_KV_HEADS):
        cs = slice(c * SWA_HEAD_DIM, (c + 1) * SWA_HEAD_DIM)
        qc = q_ref[c].astype(BF16)
        s = lax.dot_general(qc, ck[:, cs], (((1,), (1,)), ((), ())), preferred_element_type=F32) * scale
        s = jnp.where(valid, s, NEG_INF)
        sn = jnp.sum(qc.astype(F32) * kn[:, cs], axis=-1, keepdims=True) * scale
        sk = sink_ref[c]
        mx = jnp.maximum(jnp.maximum(jnp.max(s, axis=-1, keepdims=True), sn), sk)
        p = jnp.exp(s - mx)
        pn = jnp.exp(sn - mx)
        den = jnp.sum(p, axis=-1, keepdims=True) + pn + jnp.exp(sk - mx)
        o = jnp.dot(p.astype(BF16), cv[:, cs], preferred_element_type=F32) + pn * vn[:, cs]
        o_ref[c] = (o / den).astype(o_ref.dtype)


def _swa_sample_grouped(q, kn, vn, ck_all, cv_all, sinks, layer):
    nb = q.shape[0]
    g, kvh, hd, kvw = SWA_GROUP, SWA_KV_HEADS, SWA_HEAD_DIM, SWA_KV_W
    bb = SUBLANES if nb % SUBLANES == 0 else nb
    qg = jnp.transpose(q.reshape(nb, kvh, g, hd), (1, 0, 2, 3)).reshape(kvh, nb * g, hd)
    kn4 = jnp.repeat(kn, g, axis=0)
    vn4 = jnp.repeat(vn, g, axis=0)
    sk = jnp.tile(sinks.astype(F32).reshape(kvh, 1, g), (1, bb, 1)).reshape(kvh, bb * g, 1)
    cache = pl.BlockSpec((None, bb, WINDOW, kvw), lambda b: (layer, b, 0, 0))
    out = pl.pallas_call(
        functools.partial(_swa_sample_group_kernel, bb=bb),
        grid=(nb // bb,),
        in_specs=[
            pl.BlockSpec((kvh, bb * g, 1), lambda b: (0, 0, 0)),
            pl.BlockSpec((kvh, bb * g, hd), lambda b: (0, b, 0)),
            pl.BlockSpec((bb * g, kvw), lambda b: (b, 0)),
            pl.BlockSpec((bb * g, kvw), lambda b: (b, 0)),
            cache,
            cache,
        ],
        out_specs=pl.BlockSpec((kvh, bb * g, hd), lambda b: (0, b, 0)),
        out_shape=jax.ShapeDtypeStruct((kvh, nb * g, hd), F32),
        compiler_params=_params(("parallel",)),
        name="swa_sample",
    )(sk, qg, kn4, vn4, ck_all, cv_all)
    return jnp.transpose(out.reshape(kvh, nb, g, hd), (1, 0, 2, 3)).reshape(nb, kvh * g * hd)


def _gdn_sample_kernel(u_ref, prev_ref, w_ref, s_ref, z_ref, ab_ref, pa_ref, pb_ref, nw_ref,
                       o_ref, sout_ref, cnew_ref):
    hh = GDN_HEADS
    u = u_ref[0]
    pv = prev_ref[0]
    w = w_ref[...]
    x = u * w[CONV_W - 1]
    for j in range(CONV_W - 1):
        x = x + pv[j] * w[j]
        cnew_ref[0, j] = pv[j + 1] if j + 1 < CONV_W - 1 else u
    x = _silu(x)
    q = x[0:hh]
    k = x[hh:2 * hh]
    v = x[2 * hh:3 * hh]
    q = q * lax.rsqrt(jnp.sum(q * q, axis=-1, keepdims=True) + EPS) * (GDN_DK ** -0.5)
    k = k * lax.rsqrt(jnp.sum(k * k, axis=-1, keepdims=True) + EPS)
    ab = ab_ref[0]
    dec = jnp.exp(-jnp.exp(pa_ref[...]) * _softplus(ab[0:hh] + pb_ref[...]))
    beta = _sigmoid(ab[hh:2 * hh])
    qk = jnp.concatenate([q, k, jnp.zeros((LANES - 2 * hh, GDN_DK), F32)], axis=0)
    qkt = jnp.transpose(qk)
    outs = []
    for h in range(hh):
        s = s_ref[0, h] * dec[h:h + 1, :]
        kcol = qkt[:, hh + h:hh + h + 1]
        ks = jnp.sum(s * kcol, axis=0, keepdims=True)
        delta = beta[h:h + 1, :] * (v[h:h + 1, :] - ks)
        s = s + kcol * delta
        sout_ref[0, h] = s
        outs.append(jnp.sum(s * qkt[:, h:h + 1], axis=0, keepdims=True))
    o = jnp.concatenate(outs, axis=0)
    o_ref[0] = _rms_rows(o, nw_ref[...]) * _silu(z_ref[0])


def _gdn_sample(u, prev, w, s_all, z, ab, pa, pb, nw, layer, prev_out):
    nb = u.shape[0]
    hh = GDN_HEADS
    n_layers = s_all.shape[0]
    b3 = lambda b: (b, 0, 0)
    b4 = lambda b: (b, 0, 0, 0)
    c2 = lambda b: (0, 0)
    state_spec = pl.BlockSpec((None, 1, hh, GDN_DK, GDN_DV), lambda b: (layer, b, 0, 0, 0))
    args = [u, prev, w, s_all, z, ab, pa, pb, nw]
    extra_specs, extra_args, aliases = _alias_prev(prev_out, len(args), 1)
    kern = _gdn_sample_kernel if prev_out is None else _drop_ref(_gdn_sample_kernel, len(args))
    return pl.pallas_call(
        kern,
        grid=(nb,),
        input_output_aliases=aliases,
        in_specs=[
            pl.BlockSpec((1, 3 * hh, GDN_DK), b3),
            pl.BlockSpec((1, CONV_W - 1, 3 * hh, GDN_DK), b4),
            pl.BlockSpec((CONV_W, 3 * hh, GDN_DK), lambda b: (0, 0, 0)),
            state_spec,
            pl.BlockSpec((1, hh, GDN_DV), b3),
            pl.BlockSpec((1, 2 * hh, LANES), b3),
            pl.BlockSpec((hh, LANES), c2),
            pl.BlockSpec((hh, LANES), c2),
            pl.BlockSpec((1, GDN_DV), c2),
        ] + extra_specs,
        out_specs=[
            pl.BlockSpec((1, hh, GDN_DV), b3),
            state_spec,
            pl.BlockSpec((1, CONV_W - 1, 3 * hh, GDN_DK), b4),
        ],
        out_shape=[
            jax.ShapeDtypeStruct((nb, hh, GDN_DV), F32),
            jax.ShapeDtypeStruct((n_layers, nb, hh, GDN_DK, GDN_DV), F32),
            jax.ShapeDtypeStruct((nb, CONV_W - 1, 3 * hh, GDN_DK), F32),
        ],
        compiler_params=_params(("parallel",)),
        name="gdn_sample",
    )(*args, *extra_args)


def _ssd_prep_kernel(x_ref, prev_ref, cw_ref, cb_ref, tail_ref, pa_ref, pb_ref,
                     xc_ref, dt_ref, dec_ref, cnew_ref):
    u = x_ref[...]
    w = cw_ref[...]
    acc = u * w[CONV_W - 1:CONV_W, :] + cb_ref[...]
    for j in range(CONV_W - 1):
        acc = acc + prev_ref[j] * w[j:j + 1, :]
        cnew_ref[j] = prev_ref[j + 1] if j + 1 < CONV_W - 1 else u
    xc_ref[...] = _silu(acc)
    dt = _softplus(tail_ref[...] + pb_ref[...])
    dt_ref[...] = dt
    dec_ref[...] = jnp.exp(dt * (-jnp.exp(pa_ref[...])))


def _ssd_prep_sample(p_main, prev_t, cw, cb, tail, pa, pb):
    nb = p_main.shape[0]
    sec = SSD_SEC_W
    nsec = SSM_CONV_CH // sec
    c2 = lambda s: (0, 0)
    return pl.pallas_call(
        _ssd_prep_kernel,
        grid=(nsec,),
        in_specs=[
            pl.BlockSpec((nb, sec), lambda s: (0, OD_XBC // sec + s)),
            pl.BlockSpec((CONV_W - 1, nb, sec), lambda s: (0, 0, s)),
            pl.BlockSpec((CONV_W, sec), lambda s: (0, s)),
            pl.BlockSpec((1, sec), lambda s: (0, s)),
            pl.BlockSpec((nb, LANES), c2),
            pl.BlockSpec((1, LANES), c2),
            pl.BlockSpec((1, LANES), c2),
        ],
        out_specs=[
            pl.BlockSpec((nb, sec), lambda s: (0, s)),
            pl.BlockSpec((nb, LANES), c2),
            pl.BlockSpec((nb, LANES), c2),
            pl.BlockSpec((CONV_W - 1, nb, sec), lambda s: (0, 0, s)),
        ],
        out_shape=[
            jax.ShapeDtypeStruct((nb, SSM_CONV_CH), F32),
            jax.ShapeDtypeStruct((nb, LANES), F32),
            jax.ShapeDtypeStruct((nb, LANES), F32),
            jax.ShapeDtypeStruct((CONV_W - 1, nb, SSM_CONV_CH), F32),
        ],
        compiler_params=_params(("arbitrary",)),
        name="ssd_prep_sample",
    )(p_main, prev_t, cw, cb, tail, pa, pb)


def _ssd_state_kernel(dec_ref, x_ref, dt_ref, b_ref, c_ref, z_ref, de_ref, nw_ref, ex_ref, h_ref,
                      y_ref, hout_ref, *, nb):
    g = pl.program_id(0)
    x = x_ref[...]
    dtx = x * _dot_sel(_split3(dt_ref[...]), ex_ref[...])
    zrows = LANES - nb
    xt = jnp.transpose(jnp.concatenate([dtx, jnp.zeros((zrows, SSM_GROUP_W), F32)], axis=0))
    xh, xl = _split2(xt)
    bpad = jnp.concatenate([b_ref[...], jnp.zeros((zrows, SSM_STATE), F32)], axis=0)
    cpad_t = jnp.transpose(jnp.concatenate([c_ref[...], jnp.zeros((zrows, SSM_STATE), F32)], axis=0))
    rowi = _iota2((LANES, SSM_STATE), 0)
    coli = _iota2((SSM_STATE, LANES), 1)
    yt = jnp.zeros((SSM_GROUP_W, LANES), F32)
    for b in range(nb):
        bh, bl = _split2(jnp.where(rowi == b, bpad, 0.0))
        upd = (jnp.dot(xh, bh, preferred_element_type=F32) + jnp.dot(xh, bl, preferred_element_type=F32)
               + jnp.dot(xl, bh, preferred_element_type=F32))
        parts = []
        for r in range(SSM_REP):
            hn = h_ref[b, r] * dec_ref[b, g * SSM_REP + r] + upd[r * SSM_HEAD_DIM:(r + 1) * SSM_HEAD_DIM]
            hout_ref[b, r] = hn
            parts.append(hn)
        hh_, hl_ = _split2(jnp.concatenate(parts, axis=0))
        ch, cl = _split2(jnp.where(coli == b, cpad_t, 0.0))
        yt = yt + (jnp.dot(hh_, ch, preferred_element_type=F32) + jnp.dot(hh_, cl, preferred_element_type=F32)
                   + jnp.dot(hl_, ch, preferred_element_type=F32))
    y = jnp.transpose(yt)[0:nb, :] + de_ref[...] * x
    y_ref[...] = _rms_rows(y * _silu(z_ref[...]), nw_ref[...]).astype(y_ref.dtype)


def _ssd_state_sample(dtv, dec, xc, p_main, d_e, nw, ex, h_all, layer, prev_out):
    nb = xc.shape[0]
    assert nb <= LANES
    gw = SSM_GROUP_W
    n_layers = h_all.shape[0]
    state_spec = pl.BlockSpec((None, nb, SSM_REP, SSM_HEAD_DIM, SSM_STATE), lambda g: (layer, 0, g, 0, 0))
    args = [dec, xc, dtv, xc, xc, p_main, d_e, nw, ex, h_all]
    extra_specs, extra_args, aliases = _alias_prev(prev_out, len(args), 1)
    kern = functools.partial(_ssd_state_kernel, nb=nb)
    if prev_out is not None:
        kern = _drop_ref(kern, len(args))
    return pl.pallas_call(
        kern,
        grid=(SSM_GROUPS,),
        input_output_aliases=aliases,
        in_specs=[
            pl.BlockSpec(memory_space=pltpu.SMEM),
            pl.BlockSpec((nb, gw), lambda g: (0, g)),
            pl.BlockSpec((nb, LANES), lambda g: (0, 0)),
            pl.BlockSpec((nb, SSM_STATE), lambda g: (0, SSM_D_INNER // SSM_STATE + g)),
            pl.BlockSpec((nb, SSM_STATE), lambda g: (0, (SSM_D_INNER + SSM_BC_W) // SSM_STATE + g)),
            pl.BlockSpec((nb, gw), lambda g: (0, g)),
            pl.BlockSpec((1, gw), lambda g: (0, g)),
            pl.BlockSpec((1, gw), lambda g: (0, g)),
            pl.BlockSpec((LANES, gw), lambda g: (0, g)),
            state_spec,
        ] + extra_specs,
        out_specs=[
            pl.BlockSpec((nb, gw), lambda g: (0, g)),
            state_spec,
        ],
        out_shape=[
            jax.ShapeDtypeStruct((nb, SSM_D_INNER), BF16),
            jax.ShapeDtypeStruct((n_layers, nb, SSM_HEADS, SSM_HEAD_DIM, SSM_STATE), F32),
        ],
        compiler_params=_params(("arbitrary",)),
        name="ssd_state_sample",
    )(*args, *extra_args)


def _lane_pad_row(v):
    return jnp.zeros((1, LANES), F32).at[0, :v.shape[0]].set(v.astype(F32))


def kernel(x_prompt, x_sample, cache_swa_k, cache_swa_v, state_gdn, state_gdn_conv, state_ssm, state_ssm_conv,
           norm_ffn1, norm_mix, norm_ffn2, norm_final, w_ffn_gate, w_ffn_up, w_ffn_down,
           w_in_even, w_out_even, attn_sinks, gdn_conv_w, gdn_A_log, gdn_dt_bias, gdn_norm_w,
           w_in_odd, w_out_odd, ssm_conv_w, ssm_conv_b, ssm_A_log, ssm_dt_bias, ssm_D, ssm_norm_w):
    bsz, t, d = x_prompt.shape
    nb = x_sample.shape[0]
    depth = norm_ffn1.shape[0]
    n_even, n_odd = (depth + 1) // 2, depth // 2
    mp = bsz * t
    assert x_sample.shape[1] == 1 and d == D_MODEL and t % (2 * CHUNK) == 0
    hh = GDN_HEADS

    w_even_t = jnp.swapaxes(w_in_even, 1, 2)
    w_odd_t = jnp.swapaxes(w_in_odd, 1, 2)
    ex = _head_expand()
    cache_k = cache_swa_k.reshape(n_even, nb, WINDOW, SWA_KV_W)
    cache_v = cache_swa_v.reshape(n_even, nb, WINDOW, SWA_KV_W)

    xp = x_prompt.reshape(mp, d)
    xs = x_sample.reshape(nb, d)
    gain_f = norm_final.reshape(1, d)
    ffn_tm = _tile(mp, FFN_TM)

    small = {k: [] for k in ("pk", "pv", "pgc", "psc", "sk", "sv", "sgc", "ssc")}
    p_gdn = p_ssm = s_gdn = s_ssm = None

    def ffn_pair(xp, xs, gain, l, which, final):
        gain = gain.reshape(1, d)
        yp, ys, wg, wu, wd = _ffn_first(xp, xs, gain, gain_f, w_ffn_gate, w_ffn_up, w_ffn_down, l, which,
                                        final_norm=final, tm=ffn_tm, tf=FFN_FIRST_TF)
        yp = _ffn_rest(xp, gain, gain_f, wg, wu, wd, yp, final_norm=final, tm=ffn_tm, tf=FFN_TF)
        return yp, ys

    for l in range(depth):
        xp, xs = ffn_pair(xp, xs, norm_ffn1[l], l, 0, False)
        gmix = norm_mix[l].reshape(1, d)
        if l % 2 == 0:
            e = l // 2
            ptm, otm = _tile(mp, PROJ_EVEN_TM), _tile(mp, OUT_EVEN_TM)
            pm, pt, sm, st, w_main, w_tail = _proj_first(xp, xs, gmix, w_even_t, e, EV_MAIN, tm=ptm, tn=FIRST_TN)
            pm, pt = _proj_rest(xp, gmix, w_main, w_tail, pm, pt, tm=ptm, tn=PROJ_EVEN_TN)
            pa = _lane_pad_row(gdn_A_log[e])
            pb = _lane_pad_row(gdn_dt_bias[e])
            nw = gdn_norm_w[e].reshape(1, GDN_DV).astype(F32)

            kn = sm[:, EV_K:EV_K + SWA_KV_W]
            vn = sm[:, EV_V:EV_V + SWA_KV_W]
            o_a_s = _swa_sample_grouped(sm[:, EV_Q:EV_Q + SWA_Q_W], kn, vn, cache_k, cache_v, attn_sinks[e], e)
            ab = jnp.broadcast_to(st[:, :2 * hh, None], (nb, 2 * hh, LANES))
            o_b_s, s_gdn, gc_s = _gdn_sample(
                sm[:, EV_GQKV:EV_GQKV + GDN_CONV_CH].reshape(nb, 3 * hh, GDN_DK),
                state_gdn_conv[e].reshape(nb, CONV_W - 1, 3 * hh, GDN_DK),
                gdn_conv_w[e].reshape(CONV_W, 3 * hh, GDN_DK),
                state_gdn, sm[:, EV_Z:EV_Z + GDN_V_W].reshape(nb, hh, GDN_DV), ab,
                jnp.broadcast_to(gdn_A_log[e].astype(F32)[:, None], (hh, LANES)),
                jnp.broadcast_to(gdn_dt_bias[e].astype(F32)[:, None], (hh, LANES)), nw, e, s_gdn)
            small["sk"].append(kn.reshape(nb, 1, SWA_KV_HEADS, SWA_HEAD_DIM))
            small["sv"].append(vn.reshape(nb, 1, SWA_KV_HEADS, SWA_HEAD_DIM))
            small["sgc"].append(gc_s.reshape(nb, CONV_W - 1, GDN_CONV_CH))
            outs_s = [o_a_s.reshape(nb, SWA_Q_W).astype(BF16), o_b_s.reshape(nb, GDN_V_W).astype(BF16)]

            o_a_p = _swa_prompt(pm, attn_sinks[e].astype(F32), bsz, t)
            o_b_p, p_gdn = _gdn_prompt(pm, pt, gdn_conv_w, pa, pb, nw, e, n_even, p_gdn, bsz, t, rows=2 * CHUNK)
            pm3 = pm.reshape(bsz, t, EV_MAIN)
            small["pk"].append(pm3[:, t - WINDOW:, EV_K:EV_K + SWA_KV_W].reshape(bsz, WINDOW, SWA_KV_HEADS, SWA_HEAD_DIM))
            small["pv"].append(pm3[:, t - WINDOW:, EV_V:EV_V + SWA_KV_W].reshape(bsz, WINDOW, SWA_KV_HEADS, SWA_HEAD_DIM))
            small["pgc"].append(pm3[:, t - (CONV_W - 1):, EV_GQKV:EV_GQKV + GDN_CONV_CH])
            yp, xs, w_out = _outproj_first(xp, xs, [o_a_p, o_b_p], outs_s, w_out_even, e, tm=otm, tn=FIRST_TN)
            xp = _outproj_rest(xp, [o_a_p, o_b_p], w_out, yp, tm=otm, tn=OUT_EVEN_TN)
        else:
            o = l // 2
            ptm, otm = _tile(mp, PROJ_ODD_TM), _tile(mp, OUT_ODD_TM)
            pm, pt, sm, st, w_main, w_tail = _proj_first(xp, xs, gmix, w_odd_t, o, OD_MAIN, tm=ptm, tn=FIRST_TN)
            pm, pt = _proj_rest(xp, gmix, w_main, w_tail, pm, pt, tm=ptm, tn=PROJ_ODD_TN)
            pa = _lane_pad_row(ssm_A_log[o])
            pb = _lane_pad_row(ssm_dt_bias[o])
            d_e = jnp.repeat(ssm_D[o].astype(F32), SSM_HEAD_DIM).reshape(1, SSM_D_INNER)
            nw = ssm_norm_w[o].reshape(1, SSM_D_INNER).astype(F32)
            cb = ssm_conv_b[o].reshape(1, SSM_CONV_CH)

            xc, dtv, dec, cnew = _ssd_prep_sample(sm, jnp.swapaxes(state_ssm_conv[o], 0, 1), ssm_conv_w[o], cb,
                                                  st, pa, pb)
            y_s, s_ssm = _ssd_state_sample(dtv, dec, xc, sm, d_e, nw, ex, state_ssm, o, s_ssm)
            small["ssc"].append(jnp.swapaxes(cnew, 0, 1))

            y_p, p_ssm = _ssd_prompt(pm, pt, ssm_conv_w[o], cb, pa, pb, d_e, nw, ex, o, n_odd, p_ssm, bsz, t)
            pm3 = pm.reshape(bsz, t, OD_MAIN)
            small["psc"].append(pm3[:, t - (CONV_W - 1):, OD_XBC:OD_XBC + SSM_CONV_CH])
            yp, xs, w_out = _outproj_first(xp, xs, [y_p], [y_s], w_out_odd, o, tm=otm, tn=FIRST_TN)
            xp = _outproj_rest(xp, [y_p], w_out, yp, tm=otm, tn=OUT_ODD_TN)
        xp, xs = ffn_pair(xp, xs, norm_ffn2[l], l, 1, l == depth - 1)

    st_ = {k: jnp.stack(v) for k, v in small.items()}
    return (xp.reshape(bsz, t, d), xs.reshape(nb, 1, d),
            st_["pk"], st_["pv"], p_gdn, st_["pgc"], p_ssm, st_["psc"],
            st_["sk"], st_["sv"], s_gdn, st_["sgc"], s_ssm, st_["ssc"])
```

```python
import functools

import jax
import jax.numpy as jnp
from jax import lax
from jax.experimental import pallas as pl
from jax.experimental.pallas import tpu as pltpu

F32 = jnp.float32
BF16 = jnp.bfloat16

D_MODEL = 2048
SWA_HEADS = 16
SWA_KV_HEADS = 4
SWA_GROUP = SWA_HEADS // SWA_KV_HEADS
SWA_HEAD_DIM = 64
WINDOW = 128
GDN_HEADS = 8
GDN_DK = 128
GDN_DV = 128
CONV_W = 4
SSM_D_INNER = 2 * D_MODEL
SSM_HEAD_DIM = 64
SSM_HEADS = SSM_D_INNER // SSM_HEAD_DIM
SSM_GROUPS = 8
SSM_REP = SSM_HEADS // SSM_GROUPS
SSM_STATE = 128
EPS = 1e-6
NEG_INF = -1e30

SWA_Q_W = SWA_HEADS * SWA_HEAD_DIM
SWA_KV_W = SWA_KV_HEADS * SWA_HEAD_DIM
GDN_QK_W = GDN_HEADS * GDN_DK
GDN_V_W = GDN_HEADS * GDN_DV
GDN_CONV_CH = 2 * GDN_QK_W + GDN_V_W
GDN_HALF_W = GDN_QK_W // 2
SSM_BC_W = SSM_GROUPS * SSM_STATE
SSM_CONV_CH = SSM_D_INNER + 2 * SSM_BC_W
SSM_GROUP_W = SSM_REP * SSM_HEAD_DIM
SSD_SEC_W = SSM_CONV_CH // 3

EV_Q = 0
EV_K = EV_Q + SWA_Q_W
EV_V = EV_K + SWA_KV_W
EV_GQKV = EV_V + SWA_KV_W
EV_Z = EV_GQKV + GDN_CONV_CH
EV_MAIN = EV_Z + GDN_V_W
OD_XBC = SSM_D_INNER
OD_MAIN = SSM_D_INNER + SSM_CONV_CH

LANES = 128
SUBLANES = 8
CHUNK = 128
VMEM_LIMIT = 56 * 1024 * 1024

FFN_TM, FFN_TF = 512, 512
PROJ_EVEN_TM, PROJ_EVEN_TN = 1024, EV_MAIN // 4
PROJ_ODD_TM, PROJ_ODD_TN = 1024, 1024
OUT_EVEN_TM, OUT_EVEN_TN = 512, D_MODEL
OUT_ODD_TM, OUT_ODD_TN = 1024, 1024
FFN_FIRST_TF, FIRST_TN = 256, 512


def _params(sem):
    return pltpu.CompilerParams(dimension_semantics=sem, vmem_limit_bytes=VMEM_LIMIT)


def _tile(m, pref):
    return pref if m % pref == 0 else m


def _sigmoid(x):
    return 1.0 / (1.0 + jnp.exp(-x))


def _silu(x):
    return x * _sigmoid(x)


def _softplus(x):
    return jnp.maximum(x, 0.0) + jnp.log1p(jnp.exp(-jnp.abs(x)))


def _dot(a, b):
    return jnp.dot(a.astype(BF16), b.astype(BF16), preferred_element_type=F32)


def _dot_nt(a, b):
    return lax.dot_general(a.astype(BF16), b.astype(BF16), (((1,), (1,)), ((), ())),
                           preferred_element_type=F32)


def _bdot(a, b):
    return jnp.einsum("bij,bjk->bik", a.astype(BF16), b.astype(BF16), preferred_element_type=F32)


def _bdot_nt(a, b):
    return jnp.einsum("bid,bjd->bij", a.astype(BF16), b.astype(BF16), preferred_element_type=F32)


def _split2(a):
    hi = a.astype(BF16)
    lo = (a - hi.astype(F32)).astype(BF16)
    return hi, lo


def _split3(a):
    hi = a.astype(BF16)
    r = a - hi.astype(F32)
    mid = r.astype(BF16)
    lo = (r - mid.astype(F32)).astype(BF16)
    return hi, mid, lo


def _sel_dot(m01, parts):
    out = jnp.dot(m01, parts[0], preferred_element_type=F32)
    for p in parts[1:]:
        out = out + jnp.dot(m01, p, preferred_element_type=F32)
    return out


def _dot_sel(parts, m01):
    out = jnp.dot(parts[0], m01, preferred_element_type=F32)
    for p in parts[1:]:
        out = out + jnp.dot(p, m01, preferred_element_type=F32)
    return out


def _rms_rows(x, gain):
    ms = jnp.mean(x * x, axis=-1, keepdims=True)
    return x * lax.rsqrt(ms + EPS) * gain


def _iota2(shape, dim):
    return lax.broadcasted_iota(jnp.int32, shape, dim)


def _drop_ref(kern, pos):
    def entry(*refs):
        return kern(*refs[:pos], *refs[pos + 1:])
    return entry


def _alias_prev(prev, n_in, out_idx):
    if prev is None:
        return [], [], {}
    return [pl.BlockSpec(memory_space=pl.ANY)], [prev], {n_in: out_idx}


def _ffn_rows(h, wg, wu, wd):
    a = jnp.dot(h, wg, preferred_element_type=F32)
    b = jnp.dot(h, wu, preferred_element_type=F32)
    return jnp.dot((_silu(a) * b).astype(BF16), wd, preferred_element_type=F32)


def _ffn_step(j, nsteps, streams, g_ref, gf_ref, wg, wu, wd, final_norm):
    def finish(x_ref, acc):
        y = x_ref[...] + 0.5 * acc
        return _rms_rows(y, gf_ref[...]) if final_norm else y

    @pl.when(j == 0)
    def _():
        for x_ref, h_ref, o_ref in streams:
            h = _rms_rows(x_ref[...], g_ref[...]).astype(BF16)
            h_ref[...] = h
            acc = _ffn_rows(h, wg, wu, wd)
            o_ref[...] = finish(x_ref, acc) if nsteps == 1 else acc

    if nsteps > 2:
        @pl.when((j > 0) & (j < nsteps - 1))
        def _():
            for _, h_ref, o_ref in streams:
                o_ref[...] += _ffn_rows(h_ref[...], wg, wu, wd)

    if nsteps > 1:
        @pl.when(j == nsteps - 1)
        def _():
            for x_ref, h_ref, o_ref in streams:
                o_ref[...] = finish(x_ref, o_ref[...] + _ffn_rows(h_ref[...], wg, wu, wd))


def _ffn_first_kernel(xp_ref, xs_ref, g_ref, gf_ref, wg_ref, wu_ref, wd_ref,
                      op_ref, os_ref, wgb_ref, wub_ref, wdb_ref, hp_ref, hs_ref, *, final_norm):
    wg, wu, wd = wg_ref[...].astype(BF16), wu_ref[...].astype(BF16), wd_ref[...].astype(BF16)
    wgb_ref[...] = wg
    wub_ref[...] = wu
    wdb_ref[...] = wd
    streams = ((xp_ref, hp_ref, op_ref), (xs_ref, hs_ref, os_ref))
    _ffn_step(pl.program_id(0), pl.num_programs(0), streams, g_ref, gf_ref, wg, wu, wd, final_norm)


def _ffn_first(xp, xs, gain, gain_final, wg, wu, wd, layer, which, *, final_norm, tm, tf):
    mp, d = xp.shape
    nb = xs.shape[0]
    f = wg.shape[-1]
    c2 = lambda j: (0, 0)
    return pl.pallas_call(
        functools.partial(_ffn_first_kernel, final_norm=final_norm),
        grid=(f // tf,),
        in_specs=[
            pl.BlockSpec((tm, d), c2),
            pl.BlockSpec((nb, d), c2),
            pl.BlockSpec((1, d), c2),
            pl.BlockSpec((1, d), c2),
            pl.BlockSpec((None, None, d, tf), lambda j: (layer, which, 0, j)),
            pl.BlockSpec((None, None, d, tf), lambda j: (layer, which, 0, j)),
            pl.BlockSpec((None, None, tf, d), lambda j: (layer, which, j, 0)),
        ],
        out_specs=[
            pl.BlockSpec((tm, d), c2),
            pl.BlockSpec((nb, d), c2),
            pl.BlockSpec((d, tf), lambda j: (0, j)),
            pl.BlockSpec((d, tf), lambda j: (0, j)),
            pl.BlockSpec((tf, d), lambda j: (j, 0)),
        ],
        out_shape=[
            jax.ShapeDtypeStruct((mp, d), F32), jax.ShapeDtypeStruct((nb, d), F32),
            jax.ShapeDtypeStruct((d, f), BF16), jax.ShapeDtypeStruct((d, f), BF16),
            jax.ShapeDtypeStruct((f, d), BF16),
        ],
        scratch_shapes=[pltpu.VMEM((tm, d), BF16), pltpu.VMEM((nb, d), BF16)],
        compiler_params=_params(("arbitrary",)),
        name="ffn_first",
    )(xp, xs, gain, gain_final, wg, wu, wd)


def _ffn_rest_kernel(x_ref, g_ref, gf_ref, wg_ref, wu_ref, wd_ref, prev_ref, o_ref, h_ref, *, final_norm):
    del prev_ref
    _ffn_step(pl.program_id(1), pl.num_programs(1), ((x_ref, h_ref, o_ref),), g_ref, gf_ref,
              wg_ref[...], wu_ref[...], wd_ref[...], final_norm)


def _ffn_rest(x, gain, gain_final, wg, wu, wd, prev, *, final_norm, tm, tf):
    m, d = x.shape
    f = wg.shape[-1]
    if m == tm:
        return prev
    return pl.pallas_call(
        functools.partial(_ffn_rest_kernel, final_norm=final_norm),
        grid=(m // tm - 1, f // tf),
        in_specs=[
            pl.BlockSpec((tm, d), lambda i, j: (i + 1, 0)),
            pl.BlockSpec((1, d), lambda i, j: (0, 0)),
            pl.BlockSpec((1, d), lambda i, j: (0, 0)),
            pl.BlockSpec((d, tf), lambda i, j: (0, j)),
            pl.BlockSpec((d, tf), lambda i, j: (0, j)),
            pl.BlockSpec((tf, d), lambda i, j: (j, 0)),
            pl.BlockSpec(memory_space=pl.ANY),
        ],
        out_specs=pl.BlockSpec((tm, d), lambda i, j: (i + 1, 0)),
        out_shape=jax.ShapeDtypeStruct((m, d), F32),
        scratch_shapes=[pltpu.VMEM((tm, d), BF16)],
        input_output_aliases={6: 0},
        compiler_params=_params(("parallel", "arbitrary")),
        name="ffn_rest",
    )(x, gain, gain_final, wg, wu, wd, prev)


def _proj_first_kernel(xp_ref, xs_ref, g_ref, w_ref, wt_ref,
                       op_ref, otp_ref, os_ref, ots_ref, wb_ref, wtb_ref, hp_ref, hs_ref):
    j = pl.program_id(0)
    streams = ((xp_ref, hp_ref, op_ref, otp_ref), (xs_ref, hs_ref, os_ref, ots_ref))

    @pl.when(j == 0)
    def _():
        wt = wt_ref[...]
        pad = jnp.zeros((LANES - wt.shape[0], wt.shape[1]), F32)
        wt = jnp.transpose(jnp.concatenate([wt, pad], axis=0)).astype(BF16)
        wtb_ref[...] = wt
        for x_ref, h_ref, _, ot_ref in streams:
            h = _rms_rows(x_ref[...], g_ref[...]).astype(BF16)
            h_ref[...] = h
            ot_ref[...] = jnp.dot(h, wt, preferred_element_type=F32)

    w = jnp.transpose(w_ref[...].astype(BF16))
    wb_ref[...] = w
    for _, h_ref, o_ref, _ in streams:
        o_ref[...] = jnp.dot(h_ref[...], w, preferred_element_type=F32)


def _proj_first(xp, xs, gain, w_t, layer, n, *, tm, tn):
    mp, d = xp.shape
    nb = xs.shape[0]
    n_tail = w_t.shape[1] - n
    assert n % n_tail == 0 and n_tail % SUBLANES == 0 and n_tail <= LANES
    c2 = lambda j: (0, 0)
    return pl.pallas_call(
        _proj_first_kernel,
        grid=(n // tn,),
        in_specs=[
            pl.BlockSpec((tm, d), c2),
            pl.BlockSpec((nb, d), c2),
            pl.BlockSpec((1, d), c2),
            pl.BlockSpec((None, tn, d), lambda j: (layer, j, 0)),
            pl.BlockSpec((None, n_tail, d), lambda j: (layer, n // n_tail, 0)),
        ],
        out_specs=[
            pl.BlockSpec((tm, tn), lambda j: (0, j)),
            pl.BlockSpec((tm, LANES), c2),
            pl.BlockSpec((nb, tn), lambda j: (0, j)),
            pl.BlockSpec((nb, LANES), c2),
            pl.BlockSpec((d, tn), lambda j: (0, j)),
            pl.BlockSpec((d, LANES), c2),
        ],
        out_shape=[
            jax.ShapeDtypeStruct((mp, n), F32), jax.ShapeDtypeStruct((mp, LANES), F32),
            jax.ShapeDtypeStruct((nb, n), F32), jax.ShapeDtypeStruct((nb, LANES), F32),
            jax.ShapeDtypeStruct((d, n), BF16), jax.ShapeDtypeStruct((d, LANES), BF16),
        ],
        scratch_shapes=[pltpu.VMEM((tm, d), BF16), pltpu.VMEM((nb, d), BF16)],
        compiler_params=_params(("arbitrary",)),
        name="proj_first",
    )(xp, xs, gain, w_t, w_t)


def _proj_rest_kernel(x_ref, g_ref, w_ref, wt_ref, prev_ref, prevt_ref, o_ref, ot_ref, h_ref):
    del prev_ref, prevt_ref
    j = pl.program_id(1)

    @pl.when(j == 0)
    def _():
        h = _rms_rows(x_ref[...], g_ref[...]).astype(BF16)
        h_ref[...] = h
        ot_ref[...] = jnp.dot(h, wt_ref[...], preferred_element_type=F32)

    o_ref[...] = jnp.dot(h_ref[...], w_ref[...], preferred_element_type=F32)


def _proj_rest(x, gain, w_main, w_tail, prev_main, prev_tail, *, tm, tn):
    m, d = x.shape
    n = w_main.shape[1]
    if m == tm:
        return prev_main, prev_tail
    return pl.pallas_call(
        _proj_rest_kernel,
        grid=(m // tm - 1, n // tn),
        in_specs=[
            pl.BlockSpec((tm, d), lambda i, j: (i + 1, 0)),
            pl.BlockSpec((1, d), lambda i, j: (0, 0)),
            pl.BlockSpec((d, tn), lambda i, j: (0, j)),
            pl.BlockSpec((d, LANES), lambda i, j: (0, 0)),
            pl.BlockSpec(memory_space=pl.ANY),
            pl.BlockSpec(memory_space=pl.ANY),
        ],
        out_specs=[
            pl.BlockSpec((tm, tn), lambda i, j: (i + 1, j)),
            pl.BlockSpec((tm, LANES), lambda i, j: (i + 1, 0)),
        ],
        out_shape=[jax.ShapeDtypeStruct((m, n), F32), jax.ShapeDtypeStruct((m, LANES), F32)],
        scratch_shapes=[pltpu.VMEM((tm, d), BF16)],
        input_output_aliases={4: 0, 5: 1},
        compiler_params=_params(("parallel", "arbitrary")),
        name="proj_rest",
    )(x, gain, w_main, w_tail, prev_main, prev_tail)


def _outproj_first_kernel(*refs, n_in):
    xp_ref, xs_ref = refs[0], refs[1]
    op_refs = refs[2:2 + n_in]
    os_refs = refs[2 + n_in:2 + 2 * n_in]
    w_refs = refs[2 + 2 * n_in:2 + 3 * n_in]
    yp_ref, ys_ref, wb_ref = refs[2 + 3 * n_in:]
    accp = xp_ref[...]
    accs = xs_ref[...]
    for kb in range(n_in):
        w = w_refs[kb][...].astype(BF16)
        kw = w.shape[0]
        wb_ref[kb * kw:(kb + 1) * kw, :] = w
        accp = accp + jnp.dot(op_refs[kb][...], w, preferred_element_type=F32)
        accs = accs + jnp.dot(os_refs[kb][...], w, preferred_element_type=F32)
    yp_ref[...] = accp
    ys_ref[...] = accs


def _outproj_first(xp, xs, outs_p, outs_s, w_all, layer, *, tm, tn):
    mp, d = xp.shape
    nb = xs.shape[0]
    n_in = len(outs_p)
    kw = outs_p[0].shape[1]
    k_all = kw * n_in
    assert all(o.shape[1] == kw for o in outs_p + outs_s) and k_all == w_all.shape[1]
    in_specs = [pl.BlockSpec((tm, tn), lambda j: (0, j)), pl.BlockSpec((nb, tn), lambda j: (0, j))]
    in_specs += [pl.BlockSpec((tm, kw), lambda j: (0, 0)) for _ in range(n_in)]
    in_specs += [pl.BlockSpec((nb, kw), lambda j: (0, 0)) for _ in range(n_in)]
    in_specs += [pl.BlockSpec((None, kw, tn), lambda j, kb=kb: (layer, kb, j)) for kb in range(n_in)]
    return pl.pallas_call(
        functools.partial(_outproj_first_kernel, n_in=n_in),
        grid=(d // tn,),
        in_specs=in_specs,
        out_specs=[
            pl.BlockSpec((tm, tn), lambda j: (0, j)),
            pl.BlockSpec((nb, tn), lambda j: (0, j)),
            pl.BlockSpec((k_all, tn), lambda j: (0, j)),
        ],
        out_shape=[
            jax.ShapeDtypeStruct((mp, d), F32), jax.ShapeDtypeStruct((nb, d), F32),
            jax.ShapeDtypeStruct((k_all, d), BF16),
        ],
        compiler_params=_params(("arbitrary",)),
        name="outproj_first",
    )(xp, xs, *outs_p, *outs_s, *([w_all] * n_in))


def _outproj_rest_kernel(*refs, n_in):
    x_ref = refs[0]
    o_refs = refs[1:1 + n_in]
    w_refs = refs[1 + n_in:1 + 2 * n_in]
    out_ref = refs[2 + 2 * n_in]
    acc = x_ref[...]
    for o_ref, w_ref in zip(o_refs, w_refs):
        acc = acc + jnp.dot(o_ref[...], w_ref[...], preferred_element_type=F32)
    out_ref[...] = acc


def _outproj_rest(x, outs, w, prev, *, tm, tn):
    m, d = x.shape
    n_in = len(outs)
    kw = outs[0].shape[1]
    if m == tm:
        return prev
    in_specs = [pl.BlockSpec((tm, tn), lambda i, j: (i + 1, j))]
    in_specs += [pl.BlockSpec((tm, kw), lambda i, j: (i + 1, 0)) for _ in range(n_in)]
    in_specs += [pl.BlockSpec((kw, tn), lambda i, j, kb=kb: (kb, j)) for kb in range(n_in)]
    in_specs.append(pl.BlockSpec(memory_space=pl.ANY))
    return pl.pallas_call(
        functools.partial(_outproj_rest_kernel, n_in=n_in),
        grid=(m // tm - 1, d // tn),
        in_specs=in_specs,
        out_specs=pl.BlockSpec((tm, tn), lambda i, j: (i + 1, j)),
        out_shape=jax.ShapeDtypeStruct((m, d), F32),
        input_output_aliases={1 + 2 * n_in: 0},
        compiler_params=_params(("parallel", "arbitrary")),
        name="outproj_rest",
    )(x, *outs, *([w] * n_in), prev)


def _swa_prompt_kernel(sink_ref, q_ref, kc_ref, kp_ref, vc_ref, vp_ref, o_ref):
    n = pl.program_id(1)
    w = WINDOW
    qi = _iota2((w, 2 * w), 0)
    kj = _iota2((w, 2 * w), 1)
    valid = (kj > qi) & (kj <= qi + w) & ((n > 0) | (kj >= w))
    scale = SWA_HEAD_DIM ** -0.5
    kk = jnp.concatenate([kp_ref[...], kc_ref[...]], axis=0).astype(BF16)
    vv = jnp.concatenate([vp_ref[...], vc_ref[...]], axis=0).astype(BF16)
    q = q_ref[...].astype(BF16)
    outs = []
    for h in range(SWA_HEADS):
        c = h // SWA_GROUP
        qh = q[:, h * SWA_HEAD_DIM:(h + 1) * SWA_HEAD_DIM]
        kh = kk[:, c * SWA_HEAD_DIM:(c + 1) * SWA_HEAD_DIM]
        vh = vv[:, c * SWA_HEAD_DIM:(c + 1) * SWA_HEAD_DIM]
        s = lax.dot_general(qh, kh, (((1,), (1,)), ((), ())), preferred_element_type=F32) * scale
        s = jnp.where(valid, s, NEG_INF)
        sink = sink_ref[h]
        mx = jnp.maximum(jnp.max(s, axis=-1, keepdims=True), sink)
        p = jnp.exp(s - mx)
        den = jnp.sum(p, axis=-1, keepdims=True) + jnp.exp(sink - mx)
        p = p / den
        outs.append(jnp.dot(p.astype(BF16), vh, preferred_element_type=F32))
    o_ref[...] = jnp.concatenate(outs, axis=-1).astype(o_ref.dtype)


def _swa_prompt(p_main, sinks, bsz, t):
    nb = t // WINDOW
    kblk = EV_K // SWA_KV_W
    vblk = EV_V // SWA_KV_W

    def cur(col):
        return lambda b, n: (b * nb + n, col)

    def prev(col):
        return lambda b, n: (b * nb + jnp.maximum(n - 1, 0), col)

    return pl.pallas_call(
        _swa_prompt_kernel,
        grid=(bsz, nb),
        in_specs=[
            pl.BlockSpec(memory_space=pltpu.SMEM),
            pl.BlockSpec((WINDOW, SWA_Q_W), cur(EV_Q // SWA_Q_W)),
            pl.BlockSpec((WINDOW, SWA_KV_W), cur(kblk)),
            pl.BlockSpec((WINDOW, SWA_KV_W), prev(kblk)),
            pl.BlockSpec((WINDOW, SWA_KV_W), cur(vblk)),
            pl.BlockSpec((WINDOW, SWA_KV_W), prev(vblk)),
        ],
        out_specs=pl.BlockSpec((WINDOW, SWA_Q_W), cur(0)),
        out_shape=jax.ShapeDtypeStruct((bsz * t, SWA_Q_W), BF16),
        compiler_params=_params(("parallel", "arbitrary")),
        name="swa_prompt",
    )(sinks, p_main, p_main, p_main, p_main, p_main)


def _unit_lower_inverse(a, ri, ci):
    a0 = jnp.where((ri >> 4) == (ci >> 4), a, 0.0)
    y = -a0
    p = _bdot(a0, a0)
    for step in range(3):
        y = y + p + _bdot(y, p)
        if step < 2:
            p = _bdot(p, p)
    for sh in (4, 5, 6):
        e = jnp.where(((ri >> (sh + 1)) == (ci >> (sh + 1))) & ((ri >> sh) != (ci >> sh)), a, 0.0)
        t = e + _bdot(y, e)
        y = y - t - _bdot(t, y)
    return y


def _conv_section(ext_ref, carry_ref, sec, u_ref, w, bias, rows):
    ext_ref[0:SUBLANES, :] = carry_ref[sec]
    ext_ref[SUBLANES:SUBLANES + rows, :] = u_ref[...]
    acc = ext_ref[SUBLANES:SUBLANES + rows, :] * w[CONV_W - 1:CONV_W, :]
    for j in range(CONV_W - 1):
        off = SUBLANES - (CONV_W - 1) + j
        acc = acc + ext_ref[off:off + rows, :] * w[j:j + 1, :]
    if bias is not None:
        acc = acc + bias
    carry_ref[sec] = u_ref[rows - SUBLANES:rows, :]
    return _silu(acc)


def _gdn_prompt_kernel(uq0, uq1, uk0, uk1, uv0, uv1, z0_ref, z1_ref, tail_ref, cw_ref, pa_ref, pb_ref, nw_ref,
                       tri_ref, ones_ref,
                       o_ref, sout_ref,
                       s_ref, carry_ref, ext_ref, cq_ref, ck_ref, cv_ref, *, rows):
    i = pl.program_id(1)
    nchunk = rows // CHUNK
    hh = GDN_HEADS
    hw = GDN_HALF_W

    @pl.when(i == 0)
    def _():
        s_ref[...] = jnp.zeros_like(s_ref)
        carry_ref[...] = jnp.zeros_like(carry_ref)

    for sec, (halves, c_ref) in enumerate((((uq0, uq1), cq_ref), ((uk0, uk1), ck_ref), ((uv0, uv1), cv_ref))):
        for half, u_ref in enumerate(halves):
            ws = slice(sec * GDN_QK_W + half * hw, sec * GDN_QK_W + (half + 1) * hw)
            c_ref[:, half * hw:(half + 1) * hw] = _conv_section(ext_ref, carry_ref, 2 * sec + half, u_ref,
                                                                 cw_ref[:, ws], None, rows)

    t = tail_ref[...]
    g_all = -jnp.exp(pa_ref[...]) * _softplus(t + pb_ref[...])
    beta_all = _sigmoid(t)
    g_parts = _split3(g_all)
    gcum_all = _sel_dot(tri_ref[...], g_parts)
    gtot_all = _sel_dot(ones_ref[...], g_parts)
    gcum_t = jnp.transpose(gcum_all)

    ri = _iota2((1, CHUNK, CHUNK), 1)
    ci = _iota2((1, CHUNK, CHUNK), 2)
    incl = ri >= ci
    strict = ri > ci

    def rs(c):
        return slice(c * CHUNK, (c + 1) * CHUNK)

    def hs(h):
        return slice(h * GDN_DK, (h + 1) * GDN_DK)

    def stack(fn):
        return jnp.stack([fn(c, h) for c in range(nchunk) for h in range(hh)], axis=0)

    q = stack(lambda c, h: cq_ref[rs(c), hs(h)])
    k = stack(lambda c, h: ck_ref[rs(c), hs(h)])
    v = stack(lambda c, h: cv_ref[rs(c), hs(h)])
    gc = stack(lambda c, h: gcum_all[rs(c), h:h + 1])
    gt = stack(lambda c, h: gtot_all[rs(c), h:h + 1])
    bc = stack(lambda c, h: beta_all[rs(c), hh + h:hh + h + 1])
    gr = stack(lambda c, h: gcum_t[h:h + 1, rs(c)])

    q = q * lax.rsqrt(jnp.sum(q * q, axis=-1, keepdims=True) + EPS) * (GDN_DK ** -0.5)
    k = k * lax.rsqrt(jnp.sum(k * k, axis=-1, keepdims=True) + EPS)
    decay = jnp.exp(jnp.where(incl, gc - gr, NEG_INF))
    kb = k * bc
    a = jnp.where(strict, _bdot_nt(kb, k) * decay, 0.0)
    y = _unit_lower_inverse(a, ri, ci)
    eg = jnp.exp(gc)
    rhs = jnp.concatenate([v * bc, kb * eg], axis=2)
    sol = rhs + _bdot(y, rhs)
    u = sol[:, :, :GDN_DV]
    w = sol[:, :, GDN_DV:]
    wq = jnp.concatenate([w, q * eg], axis=1)
    attn = _bdot_nt(q, k) * decay
    kdt = jnp.swapaxes(k * jnp.exp(gt - gc), 1, 2)
    egt = jnp.exp(gt)

    for c in range(nchunk):
        bs = slice(c * hh, (c + 1) * hh)
        s = s_ref[...]
        ws_qs = _bdot(wq[bs], s)
        v_new = u[bs] - ws_qs[:, :CHUNK]
        o = ws_qs[:, CHUNK:] + _bdot(attn[bs], v_new)
        s_ref[...] = s * egt[bs] + _bdot(kdt[bs], v_new)
        for h in range(hh):
            z_ref = z0_ref if h < hh // 2 else z1_ref
            zc = z_ref[rs(c), hs(h % (hh // 2))]
            og = _rms_rows(o[h], nw_ref[...]) * _silu(zc)
            o_ref[rs(c), hs(h)] = og.astype(o_ref.dtype)

    @pl.when(i == pl.num_programs(1) - 1)
    def _():
        sout_ref[0] = s_ref[...]


def _chunk_masks(rows):
    r = jnp.arange(rows)
    same = (r[:, None] // CHUNK) == (r[None, :] // CHUNK)
    tri = (same & (r[:, None] >= r[None, :])).astype(BF16)
    ones = same.astype(BF16)
    return tri, ones


def _gdn_prompt(p_main, p_tail, conv_w, pa, pb, norm_w, layer, n_layers, prev, bsz, t, *, rows):
    nblk = t // rows
    hw = GDN_HALF_W
    hh = GDN_HEADS
    tri, ones = _chunk_masks(rows)

    def col(base):
        return lambda b, i: (b * nblk + i, base // hw)

    const = lambda b, i: (0, 0)
    in_specs = [pl.BlockSpec((rows, hw), col(EV_GQKV + k * hw)) for k in range(6)]
    in_specs += [pl.BlockSpec((rows, hw), col(EV_Z)), pl.BlockSpec((rows, hw), col(EV_Z + hw))]
    in_specs += [
        pl.BlockSpec((rows, LANES), lambda b, i: (b * nblk + i, 0)),
        pl.BlockSpec((None, CONV_W, GDN_CONV_CH), lambda b, i: (layer, 0, 0)),
        pl.BlockSpec((1, LANES), const),
        pl.BlockSpec((1, LANES), const),
        pl.BlockSpec((1, GDN_DV), const),
        pl.BlockSpec((rows, rows), const),
        pl.BlockSpec((rows, rows), const),
    ]
    args = [p_main] * 8 + [p_tail, conv_w, pa, pb, norm_w, tri, ones]
    extra_specs, extra_args, aliases = _alias_prev(prev, len(args), 1)
    kern = functools.partial(_gdn_prompt_kernel, rows=rows)
    if prev is not None:
        kern = _drop_ref(kern, len(args))
    return pl.pallas_call(
        kern,
        grid=(bsz, nblk),
        in_specs=in_specs + extra_specs,
        out_specs=[
            pl.BlockSpec((rows, GDN_V_W), lambda b, i: (b * nblk + i, 0)),
            pl.BlockSpec((None, 1, hh, GDN_DK, GDN_DV), lambda b, i: (layer, b, 0, 0, 0)),
        ],
        out_shape=[
            jax.ShapeDtypeStruct((bsz * t, GDN_V_W), BF16),
            jax.ShapeDtypeStruct((n_layers, bsz, hh, GDN_DK, GDN_DV), F32),
        ],
        scratch_shapes=[
            pltpu.VMEM((hh, GDN_DK, GDN_DV), F32),
            pltpu.VMEM((6, SUBLANES, hw), F32),
            pltpu.VMEM((rows + SUBLANES, hw), F32),
            pltpu.VMEM((rows, GDN_QK_W), F32),
            pltpu.VMEM((rows, GDN_QK_W), F32),
            pltpu.VMEM((rows, GDN_V_W), F32),
        ],
        input_output_aliases=aliases,
        compiler_params=_params(("parallel", "arbitrary")),
        name="gdn_prompt",
    )(*args, *extra_args)


def _ssd_prompt_kernel(z_ref, xa_ref, xb_ref, xc3_ref, tail_ref, cw_ref, cb_ref, pa_ref, pb_ref, de_ref, nw_ref,
                       ex_ref, tri_ref,
                       y_ref, hout_ref,
                       ht_ref, carry_ref, ext_ref, xc_ref, cumt_ref, dtt_ref):
    i = pl.program_id(1)
    rows = CHUNK

    @pl.when(i == 0)
    def _():
        ht_ref[...] = jnp.zeros_like(ht_ref)
        carry_ref[...] = jnp.zeros_like(carry_ref)

    for sec, u_ref in enumerate((xa_ref, xb_ref, xc3_ref)):
        ss = slice(sec * SSD_SEC_W, (sec + 1) * SSD_SEC_W)
        xc_ref[:, ss] = _conv_section(ext_ref, carry_ref, sec, u_ref, cw_ref[:, ss], cb_ref[:, ss], rows)

    dt = _softplus(tail_ref[...] + pb_ref[...])
    da = dt * (-jnp.exp(pa_ref[...]))
    cum = _sel_dot(tri_ref[...], _split3(da))
    cumt_ref[...] = jnp.transpose(cum)
    dtt_ref[...] = jnp.transpose(dt)
    last = cum[rows - 1:rows, :]
    ecum_parts = _split3(jnp.exp(cum))
    wts_parts = _split3(jnp.exp(last - cum) * dt)

    ri = _iota2((rows, rows), 0)
    ci = _iota2((rows, rows), 1)
    tri = ri >= ci
    lane = _iota2((rows, LANES), 1)
    lo_half = lane < SSM_HEAD_DIM

    for g in range(SSM_GROUPS):
        gs = slice(g * SSM_GROUP_W, (g + 1) * SSM_GROUP_W)
        bmat = xc_ref[:, SSM_D_INNER + g * SSM_STATE:SSM_D_INNER + (g + 1) * SSM_STATE]
        cmat = xc_ref[:, SSM_D_INNER + SSM_BC_W + g * SSM_STATE:SSM_D_INNER + SSM_BC_W + (g + 1) * SSM_STATE]
        xg = xc_ref[:, gs]
        cb = _dot_nt(cmat, bmat)
        exg = ex_ref[:, gs]
        ecum_e = _dot_sel(ecum_parts, exg)
        wts = _dot_sel(wts_parts, exg)
        htg = ht_ref[g]
        y = _dot(cmat, htg) * ecum_e
        pieces = []
        for pr in range(SSM_REP // 2):
            scs = []
            for half in range(2):
                h = g * SSM_REP + 2 * pr + half
                ccol = cum[:, h:h + 1]
                crow = cumt_ref[h:h + 1, :]
                drow = dtt_ref[h:h + 1, :]
                lmat = jnp.exp(jnp.where(tri, ccol - crow, NEG_INF))
                scs.append((cb * lmat * drow).astype(BF16))
            xp = xg[:, pr * LANES:(pr + 1) * LANES]
            rhs = jnp.concatenate([jnp.where(lo_half, xp, 0.0), jnp.where(lo_half, 0.0, xp)], axis=0)
            pieces.append(jnp.dot(jnp.concatenate(scs, axis=1), rhs.astype(BF16),
                                  preferred_element_type=F32))
        y = y + jnp.concatenate(pieces, axis=1) + de_ref[:, gs] * xg
        ht_ref[g] = htg * ecum_e[rows - 1:rows, :] + _dot(jnp.transpose(bmat), xg * wts)
        zg = z_ref[:, gs]
        yz = y * _silu(zg)
        y_ref[:, gs] = _rms_rows(yz, nw_ref[:, gs]).astype(y_ref.dtype)

    @pl.when(i == pl.num_programs(1) - 1)
    def _():
        for g in range(SSM_GROUPS):
            hg = jnp.transpose(ht_ref[g])
            hout_ref[0, g * SSM_REP:(g + 1) * SSM_REP] = hg.reshape(SSM_REP, SSM_HEAD_DIM, SSM_STATE)


def _head_expand():
    h = jnp.arange(LANES)[:, None]
    c = jnp.arange(SSM_D_INNER)[None, :]
    return ((c // SSM_HEAD_DIM) == h).astype(BF16)


def _ssd_prompt(p_main, p_tail, conv_w, conv_b, pa, pb, d_e, norm_w, ex, layer, n_layers, prev, bsz, t):
    rows = CHUNK
    nblk = t // rows
    tri, _ = _chunk_masks(rows)
    const = lambda b, i: (0, 0)
    row = lambda b, i: (b * nblk + i, 0)
    args = [p_main, p_main, p_main, p_main, p_tail, conv_w, conv_b, pa, pb, d_e, norm_w, ex, tri]
    extra_specs, extra_args, aliases = _alias_prev(prev, len(args), 1)
    kern = _ssd_prompt_kernel if prev is None else _drop_ref(_ssd_prompt_kernel, len(args))
    return pl.pallas_call(
        kern,
        grid=(bsz, nblk),
        input_output_aliases=aliases,
        in_specs=[
            pl.BlockSpec((rows, SSM_D_INNER), row),
            pl.BlockSpec((rows, SSD_SEC_W), lambda b, i: (b * nblk + i, OD_XBC // SSD_SEC_W)),
            pl.BlockSpec((rows, SSD_SEC_W), lambda b, i: (b * nblk + i, OD_XBC // SSD_SEC_W + 1)),
            pl.BlockSpec((rows, SSD_SEC_W), lambda b, i: (b * nblk + i, OD_XBC // SSD_SEC_W + 2)),
            pl.BlockSpec((rows, LANES), row),
            pl.BlockSpec((CONV_W, SSM_CONV_CH), const),
            pl.BlockSpec((1, SSM_CONV_CH), const),
            pl.BlockSpec((1, LANES), const),
            pl.BlockSpec((1, LANES), const),
            pl.BlockSpec((1, SSM_D_INNER), const),
            pl.BlockSpec((1, SSM_D_INNER), const),
            pl.BlockSpec((LANES, SSM_D_INNER), const),
            pl.BlockSpec((rows, rows), const),
        ] + extra_specs,
        out_specs=[
            pl.BlockSpec((rows, SSM_D_INNER), row),
            pl.BlockSpec((None, 1, SSM_HEADS, SSM_HEAD_DIM, SSM_STATE), lambda b, i: (layer, b, 0, 0, 0)),
        ],
        out_shape=[
            jax.ShapeDtypeStruct((bsz * t, SSM_D_INNER), BF16),
            jax.ShapeDtypeStruct((n_layers, bsz, SSM_HEADS, SSM_HEAD_DIM, SSM_STATE), F32),
        ],
        scratch_shapes=[
            pltpu.VMEM((SSM_GROUPS, SSM_STATE, SSM_GROUP_W), F32),
            pltpu.VMEM((3, SUBLANES, SSD_SEC_W), F32),
            pltpu.VMEM((rows + SUBLANES, SSD_SEC_W), F32),
            pltpu.VMEM((rows, SSM_CONV_CH), F32),
            pltpu.VMEM((LANES, rows), F32),
            pltpu.VMEM((LANES, rows), F32),
        ],
        compiler_params=_params(("parallel", "arbitrary")),
        name="ssd_prompt",
    )(*args, *extra_args)


def _swa_sample_group_kernel(sink_ref, q_ref, kn_ref, vn_ref, ck_ref, cv_ref, o_ref, *, bb):
    scale = SWA_HEAD_DIM ** -0.5
    rows, keys = bb * SWA_GROUP, bb * WINDOW
    ck = ck_ref[...].reshape(keys, SWA_KV_W).astype(BF16)
    cv = cv_ref[...].reshape(keys, SWA_KV_W).astype(BF16)
    kn = kn_ref[...].astype(BF16).astype(F32)
    vn = vn_ref[...].astype(BF16).astype(F32)
    ri = _iota2((rows, keys), 0)
    cj = _iota2((rows, keys), 1)
    valid = ((cj // WINDOW) == (ri // SWA_GROUP)) & ((cj % WINDOW) >= 1)
    for c in range(SWA_KV_HEADS):
        cs = slice(c * SWA_HEAD_DIM, (c + 1) * SWA_HEAD_DIM)
        qc = q_ref[c].astype(BF16)
        s = lax.dot_general(qc, ck[:, cs], (((1,), (1,)), ((), ())), preferred_element_type=F32) * scale
        s = jnp.where(valid, s, NEG_INF)
        sn = jnp.sum(qc.astype(F32) * kn[:, cs], axis=-1, keepdims=True) * scale
        sk = sink_ref[c]
        mx = jnp.maximum(jnp.maximum(jnp.max(s, axis=-1, keepdims=True), sn), sk)
        p = jnp.exp(s - mx)
        pn = jnp.exp(sn - mx)
        den = jnp.sum(p, axis=-1, keepdims=True) + pn + jnp.exp(sk - mx)
        o = jnp.dot(p.astype(BF16), cv[:, cs], preferred_element_type=F32) + pn * vn[:, cs]
        o_ref[c] = (o / den).astype(o_ref.dtype)


def _swa_sample_grouped(q, kn, vn, ck_all, cv_all, sinks, layer):
    nb = q.shape[0]
    g, kvh, hd, kvw = SWA_GROUP, SWA_KV_HEADS, SWA_HEAD_DIM, SWA_KV_W
    bb = SUBLANES if nb % SUBLANES == 0 else nb
    qg = jnp.transpose(q.reshape(nb, kvh, g, hd), (1, 0, 2, 3)).reshape(kvh, nb * g, hd)
    kn4 = jnp.repeat(kn, g, axis=0)
    vn4 = jnp.repeat(vn, g, axis=0)
    sk = jnp.tile(sinks.astype(F32).reshape(kvh, 1, g), (1, bb, 1)).reshape(kvh, bb * g, 1)
    cache = pl.BlockSpec((None, bb, WINDOW, kvw), lambda b: (layer, b, 0, 0))
    out = pl.pallas_call(
        functools.partial(_swa_sample_group_kernel, bb=bb),
        grid=(nb // bb,),
        in_specs=[
            pl.BlockSpec((kvh, bb * g, 1), lambda b: (0, 0, 0)),
            pl.BlockSpec((kvh, bb * g, hd), lambda b: (0, b, 0)),
            pl.BlockSpec((bb * g, kvw), lambda b: (b, 0)),
            pl.BlockSpec((bb * g, kvw), lambda b: (b, 0)),
            cache,
            cache,
        ],
        out_specs=pl.BlockSpec((kvh, bb * g, hd), lambda b: (0, b, 0)),
        out_shape=jax.ShapeDtypeStruct((kvh, nb * g, hd), F32),
        compiler_params=_params(("parallel",)),
        name="swa_sample",
    )(sk, qg, kn4, vn4, ck_all, cv_all)
    return jnp.transpose(out.reshape(kvh, nb, g, hd), (1, 0, 2, 3)).reshape(nb, kvh * g * hd)


def _gdn_sample_kernel(u_ref, prev_ref, w_ref, s_ref, z_ref, ab_ref, pa_ref, pb_ref, nw_ref,
                       o_ref, sout_ref, cnew_ref):
    for b in range(u_ref.shape[0]):
        _gdn_sample_row(u_ref, prev_ref, w_ref, s_ref, z_ref, ab_ref, pa_ref, pb_ref, nw_ref,
                        o_ref, sout_ref, cnew_ref, b)


def _gdn_sample_row(u_ref, prev_ref, w_ref, s_ref, z_ref, ab_ref, pa_ref, pb_ref, nw_ref,
                    o_ref, sout_ref, cnew_ref, b):
    hh = GDN_HEADS
    u = u_ref[b]
    pv = prev_ref[b]
    w = w_ref[...]
    x = u * w[CONV_W - 1]
    for j in range(CONV_W - 1):
        x = x + pv[j] * w[j]
        cnew_ref[b, j] = pv[j + 1] if j + 1 < CONV_W - 1 else u
    x = _silu(x)
    q = x[0:hh]
    k = x[hh:2 * hh]
    v = x[2 * hh:3 * hh]
    q = q * lax.rsqrt(jnp.sum(q * q, axis=-1, keepdims=True) + EPS) * (GDN_DK ** -0.5)
    k = k * lax.rsqrt(jnp.sum(k * k, axis=-1, keepdims=True) + EPS)
    ab = ab_ref[b]
    dec = jnp.exp(-jnp.exp(pa_ref[...]) * _softplus(ab[0:hh] + pb_ref[...]))
    beta = _sigmoid(ab[hh:2 * hh])
    qk = jnp.concatenate([q, k, jnp.zeros((LANES - 2 * hh, GDN_DK), F32)], axis=0)
    qkt = jnp.transpose(qk)
    outs = []
    for h in range(hh):
        s = s_ref[b, h] * dec[h:h + 1, :]
        kcol = qkt[:, hh + h:hh + h + 1]
        ks = jnp.sum(s * kcol, axis=0, keepdims=True)
        delta = beta[h:h + 1, :] * (v[h:h + 1, :] - ks)
        s = s + kcol * delta
        sout_ref[b, h] = s
        outs.append(jnp.sum(s * qkt[:, h:h + 1], axis=0, keepdims=True))
    o = jnp.concatenate(outs, axis=0)
    o_ref[b] = _rms_rows(o, nw_ref[...]) * _silu(z_ref[b])


def _gdn_sample(u, prev, w, s_all, z, ab, pa, pb, nw, layer, prev_out):
    nb = u.shape[0]
    hh = GDN_HEADS
    n_layers = s_all.shape[0]
    bb = 4 if nb % 4 == 0 else 1
    b3 = lambda b: (b, 0, 0)
    b4 = lambda b: (b, 0, 0, 0)
    c2 = lambda b: (0, 0)
    state_spec = pl.BlockSpec((None, bb, hh, GDN_DK, GDN_DV), lambda b: (layer, b, 0, 0, 0))
    args = [u, prev, w, s_all, z, ab, pa, pb, nw]
    extra_specs, extra_args, aliases = _alias_prev(prev_out, len(args), 1)
    kern = _gdn_sample_kernel if prev_out is None else _drop_ref(_gdn_sample_kernel, len(args))
    return pl.pallas_call(
        kern,
        grid=(nb // bb,),
        input_output_aliases=aliases,
        in_specs=[
            pl.BlockSpec((bb, 3 * hh, GDN_DK), b3),
            pl.BlockSpec((bb, CONV_W - 1, 3 * hh, GDN_DK), b4),
            pl.BlockSpec((CONV_W, 3 * hh, GDN_DK), lambda b: (0, 0, 0)),
            state_spec,
            pl.BlockSpec((bb, hh, GDN_DV), b3),
            pl.BlockSpec((bb, 2 * hh, LANES), b3),
            pl.BlockSpec((hh, LANES), c2),
            pl.BlockSpec((hh, LANES), c2),
            pl.BlockSpec((1, GDN_DV), c2),
        ] + extra_specs,
        out_specs=[
            pl.BlockSpec((bb, hh, GDN_DV), b3),
            state_spec,
            pl.BlockSpec((bb, CONV_W - 1, 3 * hh, GDN_DK), b4),
        ],
        out_shape=[
            jax.ShapeDtypeStruct((nb, hh, GDN_DV), F32),
            jax.ShapeDtypeStruct((n_layers, nb, hh, GDN_DK, GDN_DV), F32),
            jax.ShapeDtypeStruct((nb, CONV_W - 1, 3 * hh, GDN_DK), F32),
        ],
        compiler_params=_params(("parallel",)),
        name="gdn_sample",
    )(*args, *extra_args)


def _ssd_prep_kernel(x_ref, prev_ref, cw_ref, cb_ref, tail_ref, pa_ref, pb_ref,
                     xc_ref, dt_ref, dec_ref, cnew_ref):
    u = x_ref[...]
    w = cw_ref[...]
    acc = u * w[CONV_W - 1:CONV_W, :] + cb_ref[...]
    for j in range(CONV_W - 1):
        acc = acc + prev_ref[j] * w[j:j + 1, :]
        cnew_ref[j] = prev_ref[j + 1] if j + 1 < CONV_W - 1 else u
    xc_ref[...] = _silu(acc)
    dt = _softplus(tail_ref[...] + pb_ref[...])
    dt_ref[...] = dt
    dec_ref[...] = jnp.exp(dt * (-jnp.exp(pa_ref[...])))


def _ssd_prep_sample(p_main, prev_t, cw, cb, tail, pa, pb):
    nb = p_main.shape[0]
    sec = SSD_SEC_W
    nsec = SSM_CONV_CH // sec
    c2 = lambda s: (0, 0)
    return pl.pallas_call(
        _ssd_prep_kernel,
        grid=(nsec,),
        in_specs=[
            pl.BlockSpec((nb, sec), lambda s: (0, OD_XBC // sec + s)),
            pl.BlockSpec((CONV_W - 1, nb, sec), lambda s: (0, 0, s)),
            pl.BlockSpec((CONV_W, sec), lambda s: (0, s)),
            pl.BlockSpec((1, sec), lambda s: (0, s)),
            pl.BlockSpec((nb, LANES), c2),
            pl.BlockSpec((1, LANES), c2),
            pl.BlockSpec((1, LANES), c2),
        ],
        out_specs=[
            pl.BlockSpec((nb, sec), lambda s: (0, s)),
            pl.BlockSpec((nb, LANES), c2),
            pl.BlockSpec((nb, LANES), c2),
            pl.BlockSpec((CONV_W - 1, nb, sec), lambda s: (0, 0, s)),
        ],
        out_shape=[
            jax.ShapeDtypeStruct((nb, SSM_CONV_CH), F32),
            jax.ShapeDtypeStruct((nb, LANES), F32),
            jax.ShapeDtypeStruct((nb, LANES), F32),
            jax.ShapeDtypeStruct((CONV_W - 1, nb, SSM_CONV_CH), F32),
        ],
        compiler_params=_params(("arbitrary",)),
        name="ssd_prep_sample",
    )(p_main, prev_t, cw, cb, tail, pa, pb)


def _ssd_state_kernel(dec_ref, x_ref, dt_ref, b_ref, c_ref, z_ref, de_ref, nw_ref, ex_ref, h_ref,
                      y_ref, hout_ref, *, nb):
    g = pl.program_id(0)
    x = x_ref[...]
    dtx = x * _dot_sel(_split3(dt_ref[...]), ex_ref[...])
    zrows = LANES - nb
    xt = jnp.transpose(jnp.concatenate([dtx, jnp.zeros((zrows, SSM_GROUP_W), F32)], axis=0))
    xh, xl = _split2(xt)
    bpad = jnp.concatenate([b_ref[...], jnp.zeros((zrows, SSM_STATE), F32)], axis=0)
    cpad_t = jnp.transpose(jnp.concatenate([c_ref[...], jnp.zeros((zrows, SSM_STATE), F32)], axis=0))
    rowi = _iota2((LANES, SSM_STATE), 0)
    coli = _iota2((SSM_STATE, LANES), 1)
    yt = jnp.zeros((SSM_GROUP_W, LANES), F32)
    for b in range(nb):
        bh, bl = _split2(jnp.where(rowi == b, bpad, 0.0))
        upd = (jnp.dot(xh, bh, preferred_element_type=F32) + jnp.dot(xh, bl, preferred_element_type=F32)
               + jnp.dot(xl, bh, preferred_element_type=F32))
        parts = []
        for r in range(SSM_REP):
            hn = h_ref[b, r] * dec_ref[b, g * SSM_REP + r] + upd[r * SSM_HEAD_DIM:(r + 1) * SSM_HEAD_DIM]
            hout_ref[b, r] = hn
            parts.append(hn)
        hh_, hl_ = _split2(jnp.concatenate(parts, axis=0))
        ch, cl = _split2(jnp.where(coli == b, cpad_t, 0.0))
        yt = yt + (jnp.dot(hh_, ch, preferred_element_type=F32) + jnp.dot(hh_, cl, preferred_element_type=F32)
                   + jnp.dot(hl_, ch, preferred_element_type=F32))
    y = jnp.transpose(yt)[0:nb, :] + de_ref[...] * x
    y_ref[...] = _rms_rows(y * _silu(z_ref[...]), nw_ref[...]).astype(y_ref.dtype)


def _ssd_state_sample(dtv, dec, xc, p_main, d_e, nw, ex, h_all, layer, prev_out):
    nb = xc.shape[0]
    assert nb <= LANES
    gw = SSM_GROUP_W
    n_layers = h_all.shape[0]
    state_spec = pl.BlockSpec((None, nb, SSM_REP, SSM_HEAD_DIM, SSM_STATE), lambda g: (layer, 0, g, 0, 0))
    args = [dec, xc, dtv, xc, xc, p_main, d_e, nw, ex, h_all]
    extra_specs, extra_args, aliases = _alias_prev(prev_out, len(args), 1)
    kern = functools.partial(_ssd_state_kernel, nb=nb)
    if prev_out is not None:
        kern = _drop_ref(kern, len(args))
    return pl.pallas_call(
        kern,
        grid=(SSM_GROUPS,),
        input_output_aliases=aliases,
        in_specs=[
            pl.BlockSpec(memory_space=pltpu.SMEM),
            pl.BlockSpec((nb, gw), lambda g: (0, g)),
            pl.BlockSpec((nb, LANES), lambda g: (0, 0)),
            pl.BlockSpec((nb, SSM_STATE), lambda g: (0, SSM_D_INNER // SSM_STATE + g)),
            pl.BlockSpec((nb, SSM_STATE), lambda g: (0, (SSM_D_INNER + SSM_BC_W) // SSM_STATE + g)),
            pl.BlockSpec((nb, gw), lambda g: (0, g)),
            pl.BlockSpec((1, gw), lambda g: (0, g)),
            pl.BlockSpec((1, gw), lambda g: (0, g)),
            pl.BlockSpec((LANES, gw), lambda g: (0, g)),
            state_spec,
        ] + extra_specs,
        out_specs=[
            pl.BlockSpec((nb, gw), lambda g: (0, g)),
            state_spec,
        ],
        out_shape=[
            jax.ShapeDtypeStruct((nb, SSM_D_INNER), BF16),
            jax.ShapeDtypeStruct((n_layers, nb, SSM_HEADS, SSM_HEAD_DIM, SSM_STATE), F32),
        ],
        compiler_params=_params(("arbitrary",)),
        name="ssd_state_sample",
    )(*args, *extra_args)


def _lane_pad_row(v):
    return jnp.zeros((1, LANES), F32).at[0, :v.shape[0]].set(v.astype(F32))


def kernel(x_prompt, x_sample, cache_swa_k, cache_swa_v, state_gdn, state_gdn_conv, state_ssm, state_ssm_conv,
           norm_ffn1, norm_mix, norm_ffn2, norm_final, w_ffn_gate, w_ffn_up, w_ffn_down,
           w_in_even, w_out_even, attn_sinks, gdn_conv_w, gdn_A_log, gdn_dt_bias, gdn_norm_w,
           w_in_odd, w_out_odd, ssm_conv_w, ssm_conv_b, ssm_A_log, ssm_dt_bias, ssm_D, ssm_norm_w):
    bsz, t, d = x_prompt.shape
    nb = x_sample.shape[0]
    depth = norm_ffn1.shape[0]
    n_even, n_odd = (depth + 1) // 2, depth // 2
    mp = bsz * t
    assert x_sample.shape[1] == 1 and d == D_MODEL and t % (2 * CHUNK) == 0
    hh = GDN_HEADS

    w_even_t = jnp.swapaxes(w_in_even, 1, 2)
    w_odd_t = jnp.swapaxes(w_in_odd, 1, 2)
    ex = _head_expand()
    cache_k = cache_swa_k.reshape(n_even, nb, WINDOW, SWA_KV_W)
    cache_v = cache_swa_v.reshape(n_even, nb, WINDOW, SWA_KV_W)

    xp = x_prompt.reshape(mp, d)
    xs = x_sample.reshape(nb, d)
    gain_f = norm_final.reshape(1, d)
    ffn_tm = _tile(mp, FFN_TM)

    small = {k: [] for k in ("pk", "pv", "pgc", "psc", "sk", "sv", "sgc", "ssc")}
    p_gdn = p_ssm = s_gdn = s_ssm = None

    def ffn_pair(xp, xs, gain, l, which, final):
        gain = gain.reshape(1, d)
        yp, ys, wg, wu, wd = _ffn_first(xp, xs, gain, gain_f, w_ffn_gate, w_ffn_up, w_ffn_down, l, which,
                                        final_norm=final, tm=ffn_tm, tf=FFN_FIRST_TF)
        yp = _ffn_rest(xp, gain, gain_f, wg, wu, wd, yp, final_norm=final, tm=ffn_tm, tf=FFN_TF)
        return yp, ys

    for l in range(depth):
        xp, xs = ffn_pair(xp, xs, norm_ffn1[l], l, 0, False)
        gmix = norm_mix[l].reshape(1, d)
        if l % 2 == 0:
            e = l // 2
            ptm, otm = _tile(mp, PROJ_EVEN_TM), _tile(mp, OUT_EVEN_TM)
            pm, pt, sm, st, w_main, w_tail = _proj_first(xp, xs, gmix, w_even_t, e, EV_MAIN, tm=ptm, tn=FIRST_TN)
            pm, pt = _proj_rest(xp, gmix, w_main, w_tail, pm, pt, tm=ptm, tn=PROJ_EVEN_TN)
            pa = _lane_pad_row(gdn_A_log[e])
            pb = _lane_pad_row(gdn_dt_bias[e])
            nw = gdn_norm_w[e].reshape(1, GDN_DV).astype(F32)

            kn = sm[:, EV_K:EV_K + SWA_KV_W]
            vn = sm[:, EV_V:EV_V + SWA_KV_W]
            o_a_s = _swa_sample_grouped(sm[:, EV_Q:EV_Q + SWA_Q_W], kn, vn, cache_k, cache_v, attn_sinks[e], e)
            ab = jnp.broadcast_to(st[:, :2 * hh, None], (nb, 2 * hh, LANES))
            o_b_s, s_gdn, gc_s = _gdn_sample(
                sm[:, EV_GQKV:EV_GQKV + GDN_CONV_CH].reshape(nb, 3 * hh, GDN_DK),
                state_gdn_conv[e].reshape(nb, CONV_W - 1, 3 * hh, GDN_DK),
                gdn_conv_w[e].reshape(CONV_W, 3 * hh, GDN_DK),
                state_gdn, sm[:, EV_Z:EV_Z + GDN_V_W].reshape(nb, hh, GDN_DV), ab,
                jnp.broadcast_to(gdn_A_log[e].astype(F32)[:, None], (hh, LANES)),
                jnp.broadcast_to(gdn_dt_bias[e].astype(F32)[:, None], (hh, LANES)), nw, e, s_gdn)
            small["sk"].append(kn.reshape(nb, 1, SWA_KV_HEADS, SWA_HEAD_DIM))
            small["sv"].append(vn.reshape(nb, 1, SWA_KV_HEADS, SWA_HEAD_DIM))
            small["sgc"].append(gc_s.reshape(nb, CONV_W - 1, GDN_CONV_CH))
            outs_s = [o_a_s.reshape(nb, SWA_Q_W).astype(BF16), o_b_s.reshape(nb, GDN_V_W).astype(BF16)]

            o_a_p = _swa_prompt(pm, attn_sinks[e].astype(F32), bsz, t)
            o_b_p, p_gdn = _gdn_prompt(pm, pt, gdn_conv_w, pa, pb, nw, e, n_even, p_gdn, bsz, t, rows=2 * CHUNK)
            pm3 = pm.reshape(bsz, t, EV_MAIN)
            small["pk"].append(pm3[:, t - WINDOW:, EV_K:EV_K + SWA_KV_W].reshape(bsz, WINDOW, SWA_KV_HEADS, SWA_HEAD_DIM))
            small["pv"].append(pm3[:, t - WINDOW:, EV_V:EV_V + SWA_KV_W].reshape(bsz, WINDOW, SWA_KV_HEADS, SWA_HEAD_DIM))
            small["pgc"].append(pm3[:, t - (CONV_W - 1):, EV_GQKV:EV_GQKV + GDN_CONV_CH])
            yp, xs, w_out = _outproj_first(xp, xs, [o_a_p, o_b_p], outs_s, w_out_even, e, tm=otm, tn=FIRST_TN)
            xp = _outproj_rest(xp, [o_a_p, o_b_p], w_out, yp, tm=otm, tn=OUT_EVEN_TN)
        else:
            o = l // 2
            ptm, otm = _tile(mp, PROJ_ODD_TM), _tile(mp, OUT_ODD_TM)
            pm, pt, sm, st, w_main, w_tail = _proj_first(xp, xs, gmix, w_odd_t, o, OD_MAIN, tm=ptm, tn=FIRST_TN)
            pm, pt = _proj_rest(xp, gmix, w_main, w_tail, pm, pt, tm=ptm, tn=PROJ_ODD_TN)
            pa = _lane_pad_row(ssm_A_log[o])
            pb = _lane_pad_row(ssm_dt_bias[o])
            d_e = jnp.repeat(ssm_D[o].astype(F32), SSM_HEAD_DIM).reshape(1, SSM_D_INNER)
            nw = ssm_norm_w[o].reshape(1, SSM_D_INNER).astype(F32)
            cb = ssm_conv_b[o].reshape(1, SSM_CONV_CH)

            xc, dtv, dec, cnew = _ssd_prep_sample(sm, jnp.swapaxes(state_ssm_conv[o], 0, 1), ssm_conv_w[o], cb,
                                                  st, pa, pb)
            y_s, s_ssm = _ssd_state_sample(dtv, dec, xc, sm, d_e, nw, ex, state_ssm, o, s_ssm)
            small["ssc"].append(jnp.swapaxes(cnew, 0, 1))

            y_p, p_ssm = _ssd_prompt(pm, pt, ssm_conv_w[o], cb, pa, pb, d_e, nw, ex, o, n_odd, p_ssm, bsz, t)
            pm3 = pm.reshape(bsz, t, OD_MAIN)
            small["psc"].append(pm3[:, t - (CONV_W - 1):, OD_XBC:OD_XBC + SSM_CONV_CH])
            yp, xs, w_out = _outproj_first(xp, xs, [y_p], [y_s], w_out_odd, o, tm=otm, tn=FIRST_TN)
            xp = _outproj_rest(xp, [y_p], w_out, yp, tm=otm, tn=OUT_ODD_TN)
        xp, xs = ffn_pair(xp, xs, norm_ffn2[l], l, 1, l == depth - 1)

    st_ = {k: jnp.stack(v) for k, v in small.items()}
    return (xp.reshape(bsz, t, d), xs.reshape(nb, 1, d),
            st_["pk"], st_["pv"], p_gdn, st_["pgc"], p_ssm, st_["psc"],
            st_["sk"], st_["sv"], s_gdn, st_["sgc"], s_ssm, st_["ssc"])
```

```python
import functools

import jax
import jax.numpy as jnp
from jax import lax
from jax.experimental import pallas as pl
from jax.experimental.pallas import tpu as pltpu

F32 = jnp.float32
BF16 = jnp.bfloat16

D_MODEL = 2048
SWA_HEADS = 16
SWA_KV_HEADS = 4
SWA_GROUP = SWA_HEADS // SWA_KV_HEADS
SWA_HEAD_DIM = 64
WINDOW = 128
GDN_HEADS = 8
GDN_DK = 128
GDN_DV = 128
CONV_W = 4
SSM_D_INNER = 2 * D_MODEL
SSM_HEAD_DIM = 64
SSM_HEADS = SSM_D_INNER // SSM_HEAD_DIM
SSM_GROUPS = 8
SSM_REP = SSM_HEADS // SSM_GROUPS
SSM_STATE = 128
EPS = 1e-6
NEG_INF = -1e30

SWA_Q_W = SWA_HEADS * SWA_HEAD_DIM
SWA_KV_W = SWA_KV_HEADS * SWA_HEAD_DIM
GDN_QK_W = GDN_HEADS * GDN_DK
GDN_V_W = GDN_HEADS * GDN_DV
GDN_CONV_CH = 2 * GDN_QK_W + GDN_V_W
GDN_HALF_W = GDN_QK_W // 2
SSM_BC_W = SSM_GROUPS * SSM_STATE
SSM_CONV_CH = SSM_D_INNER + 2 * SSM_BC_W
SSM_GROUP_W = SSM_REP * SSM_HEAD_DIM
SSD_SEC_W = SSM_CONV_CH // 3

EV_Q = 0
EV_K = EV_Q + SWA_Q_W
EV_V = EV_K + SWA_KV_W
EV_GQKV = EV_V + SWA_KV_W
EV_Z = EV_GQKV + GDN_CONV_CH
EV_MAIN = EV_Z + GDN_V_W
OD_XBC = SSM_D_INNER
OD_MAIN = SSM_D_INNER + SSM_CONV_CH

LANES = 128
SUBLANES = 8
CHUNK = 128
VMEM_LIMIT = 56 * 1024 * 1024

FFN_TM, FFN_TF = 512, 512
PROJ_EVEN_TM, PROJ_EVEN_TN = 1024, EV_MAIN // 4
PROJ_ODD_TM, PROJ_ODD_TN = 1024, 1024
OUT_EVEN_TM, OUT_EVEN_TN = 512, D_MODEL
OUT_ODD_TM, OUT_ODD_TN = 1024, 1024
FFN_FIRST_TF, FIRST_TN = 256, 512


def _params(sem):
    return pltpu.CompilerParams(dimension_semantics=sem, vmem_limit_bytes=VMEM_LIMIT)


def _tile(m, pref):
    return pref if m % pref == 0 else m


def _sigmoid(x):
    return 1.0 / (1.0 + jnp.exp(-x))


def _silu(x):
    return x * _sigmoid(x)


def _softplus(x):
    return jnp.maximum(x, 0.0) + jnp.log1p(jnp.exp(-jnp.abs(x)))


def _dot(a, b):
    return jnp.dot(a.astype(BF16), b.astype(BF16), preferred_element_type=F32)


def _dot_nt(a, b):
    return lax.dot_general(a.astype(BF16), b.astype(BF16), (((1,), (1,)), ((), ())),
                           preferred_element_type=F32)


def _bdot(a, b):
    return jnp.einsum("bij,bjk->bik", a.astype(BF16), b.astype(BF16), preferred_element_type=F32)


def _bdot_nt(a, b):
    return jnp.einsum("bid,bjd->bij", a.astype(BF16), b.astype(BF16), preferred_element_type=F32)


def _split2(a):
    hi = a.astype(BF16)
    lo = (a - hi.astype(F32)).astype(BF16)
    return hi, lo


def _split3(a):
    hi = a.astype(BF16)
    r = a - hi.astype(F32)
    mid = r.astype(BF16)
    lo = (r - mid.astype(F32)).astype(BF16)
    return hi, mid, lo


def _sel_dot(m01, parts):
    out = jnp.dot(m01, parts[0], preferred_element_type=F32)
    for p in parts[1:]:
        out = out + jnp.dot(m01, p, preferred_element_type=F32)
    return out


def _dot_sel(parts, m01):
    out = jnp.dot(parts[0], m01, preferred_element_type=F32)
    for p in parts[1:]:
        out = out + jnp.dot(p, m01, preferred_element_type=F32)
    return out


def _rms_rows(x, gain):
    ms = jnp.mean(x * x, axis=-1, keepdims=True)
    return x * lax.rsqrt(ms + EPS) * gain


def _iota2(shape, dim):
    return lax.broadcasted_iota(jnp.int32, shape, dim)


def _drop_ref(kern, pos):
    def entry(*refs):
        return kern(*refs[:pos], *refs[pos + 1:])
    return entry


def _alias_prev(prev, n_in, out_idx):
    if prev is None:
        return [], [], {}
    return [pl.BlockSpec(memory_space=pl.ANY)], [prev], {n_in: out_idx}


def _ffn_rows(h, wg, wu, wd):
    a = jnp.dot(h, wg, preferred_element_type=F32)
    b = jnp.dot(h, wu, preferred_element_type=F32)
    return jnp.dot((_silu(a) * b).astype(BF16), wd, preferred_element_type=F32)


def _ffn_step(j, nsteps, streams, g_ref, gf_ref, wg, wu, wd, final_norm):
    @pl.when(j == 0)
    def _():
        for x_ref, h_ref, o_ref in streams:
            h_ref[...] = _rms_rows(x_ref[...], g_ref[...]).astype(BF16)
            o_ref[...] = jnp.zeros_like(o_ref)

    for _, h_ref, o_ref in streams:
        o_ref[...] += _ffn_rows(h_ref[...], wg, wu, wd)

    @pl.when(j == nsteps - 1)
    def _():
        for x_ref, _, o_ref in streams:
            y = x_ref[...] + 0.5 * o_ref[...]
            if final_norm:
                y = _rms_rows(y, gf_ref[...])
            o_ref[...] = y


def _ffn_first_kernel(xp_ref, xs_ref, g_ref, gf_ref, wg_ref, wu_ref, wd_ref,
                      op_ref, os_ref, wgb_ref, wub_ref, wdb_ref, hp_ref, hs_ref, *, final_norm):
    wg, wu, wd = wg_ref[...].astype(BF16), wu_ref[...].astype(BF16), wd_ref[...].astype(BF16)
    wgb_ref[...] = wg
    wub_ref[...] = wu
    wdb_ref[...] = wd
    streams = ((xp_ref, hp_ref, op_ref), (xs_ref, hs_ref, os_ref))
    _ffn_step(pl.program_id(0), pl.num_programs(0), streams, g_ref, gf_ref, wg, wu, wd, final_norm)


def _ffn_first(xp, xs, gain, gain_final, wg, wu, wd, layer, which, *, final_norm, tm, tf):
    mp, d = xp.shape
    nb = xs.shape[0]
    f = wg.shape[-1]
    c2 = lambda j: (0, 0)
    return pl.pallas_call(
        functools.partial(_ffn_first_kernel, final_norm=final_norm),
        grid=(f // tf,),
        in_specs=[
            pl.BlockSpec((tm, d), c2),
            pl.BlockSpec((nb, d), c2),
            pl.BlockSpec((1, d), c2),
            pl.BlockSpec((1, d), c2),
            pl.BlockSpec((None, None, d, tf), lambda j: (layer, which, 0, j)),
            pl.BlockSpec((None, None, d, tf), lambda j: (layer, which, 0, j)),
            pl.BlockSpec((None, None, tf, d), lambda j: (layer, which, j, 0)),
        ],
        out_specs=[
            pl.BlockSpec((tm, d), c2),
            pl.BlockSpec((nb, d), c2),
            pl.BlockSpec((d, tf), lambda j: (0, j)),
            pl.BlockSpec((d, tf), lambda j: (0, j)),
            pl.BlockSpec((tf, d), lambda j: (j, 0)),
        ],
        out_shape=[
            jax.ShapeDtypeStruct((mp, d), F32), jax.ShapeDtypeStruct((nb, d), F32),
            jax.ShapeDtypeStruct((d, f), BF16), jax.ShapeDtypeStruct((d, f), BF16),
            jax.ShapeDtypeStruct((f, d), BF16),
        ],
        scratch_shapes=[pltpu.VMEM((tm, d), BF16), pltpu.VMEM((nb, d), BF16)],
        compiler_params=_params(("arbitrary",)),
        name="ffn_first",
    )(xp, xs, gain, gain_final, wg, wu, wd)


def _ffn_rest_kernel(x_ref, g_ref, gf_ref, wg_ref, wu_ref, wd_ref, prev_ref, o_ref, h_ref, *, final_norm):
    del prev_ref
    _ffn_step(pl.program_id(1), pl.num_programs(1), ((x_ref, h_ref, o_ref),), g_ref, gf_ref,
              wg_ref[...], wu_ref[...], wd_ref[...], final_norm)


def _ffn_rest(x, gain, gain_final, wg, wu, wd, prev, *, final_norm, tm, tf):
    m, d = x.shape
    f = wg.shape[-1]
    if m == tm:
        return prev
    return pl.pallas_call(
        functools.partial(_ffn_rest_kernel, final_norm=final_norm),
        grid=(m // tm - 1, f // tf),
        in_specs=[
            pl.BlockSpec((tm, d), lambda i, j: (i + 1, 0)),
            pl.BlockSpec((1, d), lambda i, j: (0, 0)),
            pl.BlockSpec((1, d), lambda i, j: (0, 0)),
            pl.BlockSpec((d, tf), lambda i, j: (0, j)),
            pl.BlockSpec((d, tf), lambda i, j: (0, j)),
            pl.BlockSpec((tf, d), lambda i, j: (j, 0)),
            pl.BlockSpec(memory_space=pl.ANY),
        ],
        out_specs=pl.BlockSpec((tm, d), lambda i, j: (i + 1, 0)),
        out_shape=jax.ShapeDtypeStruct((m, d), F32),
        scratch_shapes=[pltpu.VMEM((tm, d), BF16)],
        input_output_aliases={6: 0},
        compiler_params=_params(("parallel", "arbitrary")),
        name="ffn_rest",
    )(x, gain, gain_final, wg, wu, wd, prev)


def _proj_first_kernel(xp_ref, xs_ref, g_ref, w_ref, wt_ref,
                       op_ref, otp_ref, os_ref, ots_ref, wb_ref, wtb_ref, hp_ref, hs_ref):
    j = pl.program_id(0)
    streams = ((xp_ref, hp_ref, op_ref, otp_ref), (xs_ref, hs_ref, os_ref, ots_ref))

    @pl.when(j == 0)
    def _():
        wt = wt_ref[...]
        pad = jnp.zeros((LANES - wt.shape[0], wt.shape[1]), F32)
        wt = jnp.transpose(jnp.concatenate([wt, pad], axis=0)).astype(BF16)
        wtb_ref[...] = wt
        for x_ref, h_ref, _, ot_ref in streams:
            h = _rms_rows(x_ref[...], g_ref[...]).astype(BF16)
            h_ref[...] = h
            ot_ref[...] = jnp.dot(h, wt, preferred_element_type=F32)

    w = jnp.transpose(w_ref[...].astype(BF16))
    wb_ref[...] = w
    for _, h_ref, o_ref, _ in streams:
        o_ref[...] = jnp.dot(h_ref[...], w, preferred_element_type=F32)


def _proj_first(xp, xs, gain, w_t, layer, n, *, tm, tn):
    mp, d = xp.shape
    nb = xs.shape[0]
    n_tail = w_t.shape[1] - n
    assert n % n_tail == 0 and n_tail % SUBLANES == 0 and n_tail <= LANES
    c2 = lambda j: (0, 0)
    return pl.pallas_call(
        _proj_first_kernel,
        grid=(n // tn,),
        in_specs=[
            pl.BlockSpec((tm, d), c2),
            pl.BlockSpec((nb, d), c2),
            pl.BlockSpec((1, d), c2),
            pl.BlockSpec((None, tn, d), lambda j: (layer, j, 0)),
            pl.BlockSpec((None, n_tail, d), lambda j: (layer, n // n_tail, 0)),
        ],
        out_specs=[
            pl.BlockSpec((tm, tn), lambda j: (0, j)),
            pl.BlockSpec((tm, LANES), c2),
            pl.BlockSpec((nb, tn), lambda j: (0, j)),
            pl.BlockSpec((nb, LANES), c2),
            pl.BlockSpec((d, tn), lambda j: (0, j)),
            pl.BlockSpec((d, LANES), c2),
        ],
        out_shape=[
            jax.ShapeDtypeStruct((mp, n), F32), jax.ShapeDtypeStruct((mp, LANES), F32),
            jax.ShapeDtypeStruct((nb, n), F32), jax.ShapeDtypeStruct((nb, LANES), F32),
            jax.ShapeDtypeStruct((d, n), BF16), jax.ShapeDtypeStruct((d, LANES), BF16),
        ],
        scratch_shapes=[pltpu.VMEM((tm, d), BF16), pltpu.VMEM((nb, d), BF16)],
        compiler_params=_params(("arbitrary",)),
        name="proj_first",
    )(xp, xs, gain, w_t, w_t)


def _proj_rest_kernel(x_ref, g_ref, w_ref, wt_ref, prev_ref, prevt_ref, o_ref, ot_ref, h_ref):
    del prev_ref, prevt_ref
    j = pl.program_id(1)

    @pl.when(j == 0)
    def _():
        h = _rms_rows(x_ref[...], g_ref[...]).astype(BF16)
        h_ref[...] = h
        ot_ref[...] = jnp.dot(h, wt_ref[...], preferred_element_type=F32)

    o_ref[...] = jnp.dot(h_ref[...], w_ref[...], preferred_element_type=F32)


def _proj_rest(x, gain, w_main, w_tail, prev_main, prev_tail, *, tm, tn):
    m, d = x.shape
    n = w_main.shape[1]
    if m == tm:
        return prev_main, prev_tail
    return pl.pallas_call(
        _proj_rest_kernel,
        grid=(m // tm - 1, n // tn),
        in_specs=[
            pl.BlockSpec((tm, d), lambda i, j: (i + 1, 0)),
            pl.BlockSpec((1, d), lambda i, j: (0, 0)),
            pl.BlockSpec((d, tn), lambda i, j: (0, j)),
            pl.BlockSpec((d, LANES), lambda i, j: (0, 0)),
            pl.BlockSpec(memory_space=pl.ANY),
            pl.BlockSpec(memory_space=pl.ANY),
        ],
        out_specs=[
            pl.BlockSpec((tm, tn), lambda i, j: (i + 1, j)),
            pl.BlockSpec((tm, LANES), lambda i, j: (i + 1, 0)),
        ],
        out_shape=[jax.ShapeDtypeStruct((m, n), F32), jax.ShapeDtypeStruct((m, LANES), F32)],
        scratch_shapes=[pltpu.VMEM((tm, d), BF16)],
        input_output_aliases={4: 0, 5: 1},
        compiler_params=_params(("parallel", "arbitrary")),
        name="proj_rest",
    )(x, gain, w_main, w_tail, prev_main, prev_tail)


def _outproj_first_kernel(*refs, n_in):
    xp_ref, xs_ref = refs[0], refs[1]
    op_refs = refs[2:2 + n_in]
    os_refs = refs[2 + n_in:2 + 2 * n_in]
    w_refs = refs[2 + 2 * n_in:2 + 3 * n_in]
    yp_ref, ys_ref, wb_ref = refs[2 + 3 * n_in:]
    accp = xp_ref[...]
    accs = xs_ref[...]
    for kb in range(n_in):
        w = w_refs[kb][...].astype(BF16)
        kw = w.shape[0]
        wb_ref[kb * kw:(kb + 1) * kw, :] = w
        accp = accp + jnp.dot(op_refs[kb][...], w, preferred_element_type=F32)
        accs = accs + jnp.dot(os_refs[kb][...], w, preferred_element_type=F32)
    yp_ref[...] = accp
    ys_ref[...] = accs


def _outproj_first(xp, xs, outs_p, outs_s, w_all, layer, *, tm, tn):
    mp, d = xp.shape
    nb = xs.shape[0]
    n_in = len(outs_p)
    kw = outs_p[0].shape[1]
    k_all = kw * n_in
    assert all(o.shape[1] == kw for o in outs_p + outs_s) and k_all == w_all.shape[1]
    in_specs = [pl.BlockSpec((tm, tn), lambda j: (0, j)), pl.BlockSpec((nb, tn), lambda j: (0, j))]
    in_specs += [pl.BlockSpec((tm, kw), lambda j: (0, 0)) for _ in range(n_in)]
    in_specs += [pl.BlockSpec((nb, kw), lambda j: (0, 0)) for _ in range(n_in)]
    in_specs += [pl.BlockSpec((None, kw, tn), lambda j, kb=kb: (layer, kb, j)) for kb in range(n_in)]
    return pl.pallas_call(
        functools.partial(_outproj_first_kernel, n_in=n_in),
        grid=(d // tn,),
        in_specs=in_specs,
        out_specs=[
            pl.BlockSpec((tm, tn), lambda j: (0, j)),
            pl.BlockSpec((nb, tn), lambda j: (0, j)),
            pl.BlockSpec((k_all, tn), lambda j: (0, j)),
        ],
        out_shape=[
            jax.ShapeDtypeStruct((mp, d), F32), jax.ShapeDtypeStruct((nb, d), F32),
            jax.ShapeDtypeStruct((k_all, d), BF16),
        ],
        compiler_params=_params(("arbitrary",)),
        name="outproj_first",
    )(xp, xs, *outs_p, *outs_s, *([w_all] * n_in))


def _outproj_rest_kernel(*refs, n_in):
    x_ref = refs[0]
    o_refs = refs[1:1 + n_in]
    w_refs = refs[1 + n_in:1 + 2 * n_in]
    out_ref = refs[2 + 2 * n_in]
    acc = x_ref[...]
    for o_ref, w_ref in zip(o_refs, w_refs):
        acc = acc + jnp.dot(o_ref[...], w_ref[...], preferred_element_type=F32)
    out_ref[...] = acc


def _outproj_rest(x, outs, w, prev, *, tm, tn):
    m, d = x.shape
    n_in = len(outs)
    kw = outs[0].shape[1]
    if m == tm:
        return prev
    in_specs = [pl.BlockSpec((tm, tn), lambda i, j: (i + 1, j))]
    in_specs += [pl.BlockSpec((tm, kw), lambda i, j: (i + 1, 0)) for _ in range(n_in)]
    in_specs += [pl.BlockSpec((kw, tn), lambda i, j, kb=kb: (kb, j)) for kb in range(n_in)]
    in_specs.append(pl.BlockSpec(memory_space=pl.ANY))
    return pl.pallas_call(
        functools.partial(_outproj_rest_kernel, n_in=n_in),
        grid=(m // tm - 1, d // tn),
        in_specs=in_specs,
        out_specs=pl.BlockSpec((tm, tn), lambda i, j: (i + 1, j)),
        out_shape=jax.ShapeDtypeStruct((m, d), F32),
        input_output_aliases={1 + 2 * n_in: 0},
        compiler_params=_params(("parallel", "arbitrary")),
        name="outproj_rest",
    )(x, *outs, *([w] * n_in), prev)


def _swa_prompt_kernel(sink_ref, q_ref, kc_ref, kp_ref, vc_ref, vp_ref, o_ref):
    n = pl.program_id(1)
    w = WINDOW
    qi = _iota2((w, 2 * w), 0)
    kj = _iota2((w, 2 * w), 1)
    valid = (kj > qi) & (kj <= qi + w) & ((n > 0) | (kj >= w))
    scale = SWA_HEAD_DIM ** -0.5
    kk = jnp.concatenate([kp_ref[...], kc_ref[...]], axis=0).astype(BF16)
    vv = jnp.concatenate([vp_ref[...], vc_ref[...]], axis=0).astype(BF16)
    q = q_ref[...].astype(BF16)
    outs = []
    for h in range(SWA_HEADS):
        c = h // SWA_GROUP
        qh = q[:, h * SWA_HEAD_DIM:(h + 1) * SWA_HEAD_DIM]
        kh = kk[:, c * SWA_HEAD_DIM:(c + 1) * SWA_HEAD_DIM]
        vh = vv[:, c * SWA_HEAD_DIM:(c + 1) * SWA_HEAD_DIM]
        s = lax.dot_general(qh, kh, (((1,), (1,)), ((), ())), preferred_element_type=F32) * scale
        s = jnp.where(valid, s, NEG_INF)
        sink = sink_ref[h]
        mx = jnp.maximum(jnp.max(s, axis=-1, keepdims=True), sink)
        p = jnp.exp(s - mx)
        den = jnp.sum(p, axis=-1, keepdims=True) + jnp.exp(sink - mx)
        p = p / den
        outs.append(jnp.dot(p.astype(BF16), vh, preferred_element_type=F32))
    o_ref[...] = jnp.concatenate(outs, axis=-1).astype(o_ref.dtype)


def _swa_prompt(p_main, sinks, bsz, t):
    nb = t // WINDOW
    kblk = EV_K // SWA_KV_W
    vblk = EV_V // SWA_KV_W

    def cur(col):
        return lambda b, n: (b * nb + n, col)

    def prev(col):
        return lambda b, n: (b * nb + jnp.maximum(n - 1, 0), col)

    return pl.pallas_call(
        _swa_prompt_kernel,
        grid=(bsz, nb),
        in_specs=[
            pl.BlockSpec(memory_space=pltpu.SMEM),
            pl.BlockSpec((WINDOW, SWA_Q_W), cur(EV_Q // SWA_Q_W)),
            pl.BlockSpec((WINDOW, SWA_KV_W), cur(kblk)),
            pl.BlockSpec((WINDOW, SWA_KV_W), prev(kblk)),
            pl.BlockSpec((WINDOW, SWA_KV_W), cur(vblk)),
            pl.BlockSpec((WINDOW, SWA_KV_W), prev(vblk)),
        ],
        out_specs=pl.BlockSpec((WINDOW, SWA_Q_W), cur(0)),
        out_shape=jax.ShapeDtypeStruct((bsz * t, SWA_Q_W), BF16),
        compiler_params=_params(("parallel", "arbitrary")),
        name="swa_prompt",
    )(sinks, p_main, p_main, p_main, p_main, p_main)


def _unit_lower_inverse(a, ri, ci):
    a0 = jnp.where((ri >> 4) == (ci >> 4), a, 0.0)
    y = -a0
    p = _bdot(a0, a0)
    for step in range(3):
        y = y + p + _bdot(y, p)
        if step < 2:
            p = _bdot(p, p)
    for sh in (4, 5, 6):
        e = jnp.where(((ri >> (sh + 1)) == (ci >> (sh + 1))) & ((ri >> sh) != (ci >> sh)), a, 0.0)
        t = e + _bdot(y, e)
        y = y - t - _bdot(t, y)
    return y


def _conv_section(ext_ref, carry_ref, sec, u_ref, w, bias, rows):
    ext_ref[0:SUBLANES, :] = carry_ref[sec]
    ext_ref[SUBLANES:SUBLANES + rows, :] = u_ref[...]
    acc = ext_ref[SUBLANES:SUBLANES + rows, :] * w[CONV_W - 1:CONV_W, :]
    for j in range(CONV_W - 1):
        off = SUBLANES - (CONV_W - 1) + j
        acc = acc + ext_ref[off:off + rows, :] * w[j:j + 1, :]
    if bias is not None:
        acc = acc + bias
    carry_ref[sec] = u_ref[rows - SUBLANES:rows, :]
    return _silu(acc)


def _gdn_prompt_kernel(uq0, uq1, uk0, uk1, uv0, uv1, z0_ref, z1_ref, tail_ref, cw_ref, pa_ref, pb_ref, nw_ref,
                       tri_ref, ones_ref,
                       o_ref, sout_ref,
                       s_ref, carry_ref, ext_ref, cq_ref, ck_ref, cv_ref, *, rows):
    i = pl.program_id(1)
    nchunk = rows // CHUNK
    hh = GDN_HEADS
    hw = GDN_HALF_W

    @pl.when(i == 0)
    def _():
        s_ref[...] = jnp.zeros_like(s_ref)
        carry_ref[...] = jnp.zeros_like(carry_ref)

    for sec, (halves, c_ref) in enumerate((((uq0, uq1), cq_ref), ((uk0, uk1), ck_ref), ((uv0, uv1), cv_ref))):
        for half, u_ref in enumerate(halves):
            ws = slice(sec * GDN_QK_W + half * hw, sec * GDN_QK_W + (half + 1) * hw)
            c_ref[:, half * hw:(half + 1) * hw] = _conv_section(ext_ref, carry_ref, 2 * sec + half, u_ref,
                                                                 cw_ref[:, ws], None, rows)

    t = tail_ref[...]
    g_all = -jnp.exp(pa_ref[...]) * _softplus(t + pb_ref[...])
    beta_all = _sigmoid(t)
    g_parts = _split3(g_all)
    gcum_all = _sel_dot(tri_ref[...], g_parts)
    gtot_all = _sel_dot(ones_ref[...], g_parts)
    gcum_t = jnp.transpose(gcum_all)

    ri = _iota2((1, CHUNK, CHUNK), 1)
    ci = _iota2((1, CHUNK, CHUNK), 2)
    incl = ri >= ci
    strict = ri > ci

    def rs(c):
        return slice(c * CHUNK, (c + 1) * CHUNK)

    def hs(h):
        return slice(h * GDN_DK, (h + 1) * GDN_DK)

    def stack(fn):
        return jnp.stack([fn(c, h) for c in range(nchunk) for h in range(hh)], axis=0)

    q = stack(lambda c, h: cq_ref[rs(c), hs(h)])
    k = stack(lambda c, h: ck_ref[rs(c), hs(h)])
    v = stack(lambda c, h: cv_ref[rs(c), hs(h)])
    gc = stack(lambda c, h: gcum_all[rs(c), h:h + 1])
    gt = stack(lambda c, h: gtot_all[rs(c), h:h + 1])
    bc = stack(lambda c, h: beta_all[rs(c), hh + h:hh + h + 1])
    gr = stack(lambda c, h: gcum_t[h:h + 1, rs(c)])

    q = q * lax.rsqrt(jnp.sum(q * q, axis=-1, keepdims=True) + EPS) * (GDN_DK ** -0.5)
    k = k * lax.rsqrt(jnp.sum(k * k, axis=-1, keepdims=True) + EPS)
    decay = jnp.exp(jnp.where(incl, gc - gr, NEG_INF))
    kb = k * bc
    a = jnp.where(strict, _bdot_nt(kb, k) * decay, 0.0)
    y = _unit_lower_inverse(a, ri, ci)
    eg = jnp.exp(gc)
    rhs = jnp.concatenate([v * bc, kb * eg], axis=2)
    sol = rhs + _bdot(y, rhs)
    u = sol[:, :, :GDN_DV]
    w = sol[:, :, GDN_DV:]
    wq = jnp.concatenate([w, q * eg], axis=1)
    attn = _bdot_nt(q, k) * decay
    kdt = jnp.swapaxes(k * jnp.exp(gt - gc), 1, 2)
    egt = jnp.exp(gt)

    for c in range(nchunk):
        bs = slice(c * hh, (c + 1) * hh)
        s = s_ref[...]
        ws_qs = _bdot(wq[bs], s)
        v_new = u[bs] - ws_qs[:, :CHUNK]
        o = ws_qs[:, CHUNK:] + _bdot(attn[bs], v_new)
        s_ref[...] = s * egt[bs] + _bdot(kdt[bs], v_new)
        for h in range(hh):
            z_ref = z0_ref if h < hh // 2 else z1_ref
            zc = z_ref[rs(c), hs(h % (hh // 2))]
            og = _rms_rows(o[h], nw_ref[...]) * _silu(zc)
            o_ref[rs(c), hs(h)] = og.astype(o_ref.dtype)

    @pl.when(i == pl.num_programs(1) - 1)
    def _():
        sout_ref[0] = s_ref[...]


def _chunk_masks(rows):
    r = jnp.arange(rows)
    same = (r[:, None] // CHUNK) == (r[None, :] // CHUNK)
    tri = (same & (r[:, None] >= r[None, :])).astype(BF16)
    ones = same.astype(BF16)
    return tri, ones


def _gdn_prompt(p_main, p_tail, conv_w, pa, pb, norm_w, layer, n_layers, prev, bsz, t, *, rows):
    nblk = t // rows
    hw = GDN_HALF_W
    hh = GDN_HEADS
    tri, ones = _chunk_masks(rows)

    def col(base):
        return lambda b, i: (b * nblk + i, base // hw)

    const = lambda b, i: (0, 0)
    in_specs = [pl.BlockSpec((rows, hw), col(EV_GQKV + k * hw)) for k in range(6)]
    in_specs += [pl.BlockSpec((rows, hw), col(EV_Z)), pl.BlockSpec((rows, hw), col(EV_Z + hw))]
    in_specs += [
        pl.BlockSpec((rows, LANES), lambda b, i: (b * nblk + i, 0)),
        pl.BlockSpec((None, CONV_W, GDN_CONV_CH), lambda b, i: (layer, 0, 0)),
        pl.BlockSpec((1, LANES), const),
        pl.BlockSpec((1, LANES), const),
        pl.BlockSpec((1, GDN_DV), const),
        pl.BlockSpec((rows, rows), const),
        pl.BlockSpec((rows, rows), const),
    ]
    args = [p_main] * 8 + [p_tail, conv_w, pa, pb, norm_w, tri, ones]
    extra_specs, extra_args, aliases = _alias_prev(prev, len(args), 1)
    kern = functools.partial(_gdn_prompt_kernel, rows=rows)
    if prev is not None:
        kern = _drop_ref(kern, len(args))
    return pl.pallas_call(
        kern,
        grid=(bsz, nblk),
        in_specs=in_specs + extra_specs,
        out_specs=[
            pl.BlockSpec((rows, GDN_V_W), lambda b, i: (b * nblk + i, 0)),
            pl.BlockSpec((None, 1, hh, GDN_DK, GDN_DV), lambda b, i: (layer, b, 0, 0, 0)),
        ],
        out_shape=[
            jax.ShapeDtypeStruct((bsz * t, GDN_V_W), BF16),
            jax.ShapeDtypeStruct((n_layers, bsz, hh, GDN_DK, GDN_DV), F32),
        ],
        scratch_shapes=[
            pltpu.VMEM((hh, GDN_DK, GDN_DV), F32),
            pltpu.VMEM((6, SUBLANES, hw), F32),
            pltpu.VMEM((rows + SUBLANES, hw), F32),
            pltpu.VMEM((rows, GDN_QK_W), F32),
            pltpu.VMEM((rows, GDN_QK_W), F32),
            pltpu.VMEM((rows, GDN_V_W), F32),
        ],
        input_output_aliases=aliases,
        compiler_params=_params(("parallel", "arbitrary")),
        name="gdn_prompt",
    )(*args, *extra_args)


def _ssd_prompt_kernel(z_ref, xa_ref, xb_ref, xc3_ref, tail_ref, cw_ref, cb_ref, pa_ref, pb_ref, de_ref, nw_ref,
                       ex_ref, tri_ref,
                       y_ref, hout_ref,
                       ht_ref, carry_ref, ext_ref, xc_ref, cumt_ref, dtt_ref):
    i = pl.program_id(1)
    rows = CHUNK

    @pl.when(i == 0)
    def _():
        ht_ref[...] = jnp.zeros_like(ht_ref)
        carry_ref[...] = jnp.zeros_like(carry_ref)

    for sec, u_ref in enumerate((xa_ref, xb_ref, xc3_ref)):
        ss = slice(sec * SSD_SEC_W, (sec + 1) * SSD_SEC_W)
        xc_ref[:, ss] = _conv_section(ext_ref, carry_ref, sec, u_ref, cw_ref[:, ss], cb_ref[:, ss], rows)

    dt = _softplus(tail_ref[...] + pb_ref[...])
    da = dt * (-jnp.exp(pa_ref[...]))
    cum = _sel_dot(tri_ref[...], _split3(da))
    cumt_ref[...] = jnp.transpose(cum)
    dtt_ref[...] = jnp.transpose(dt)
    last = cum[rows - 1:rows, :]
    ecum_parts = _split3(jnp.exp(cum))
    wts_parts = _split3(jnp.exp(last - cum) * dt)

    ri = _iota2((rows, rows), 0)
    ci = _iota2((rows, rows), 1)
    tri = ri >= ci
    lane = _iota2((rows, LANES), 1)
    lo_half = lane < SSM_HEAD_DIM

    for g in range(SSM_GROUPS):
        gs = slice(g * SSM_GROUP_W, (g + 1) * SSM_GROUP_W)
        bmat = xc_ref[:, SSM_D_INNER + g * SSM_STATE:SSM_D_INNER + (g + 1) * SSM_STATE]
        cmat = xc_ref[:, SSM_D_INNER + SSM_BC_W + g * SSM_STATE:SSM_D_INNER + SSM_BC_W + (g + 1) * SSM_STATE]
        xg = xc_ref[:, gs]
        cb = _dot_nt(cmat, bmat)
        exg = ex_ref[:, gs]
        ecum_e = _dot_sel(ecum_parts, exg)
        wts = _dot_sel(wts_parts, exg)
        htg = ht_ref[g]
        y = _dot(cmat, htg) * ecum_e
        pieces = []
        for pr in range(SSM_REP // 2):
            scs = []
            for half in range(2):
                h = g * SSM_REP + 2 * pr + half
                ccol = cum[:, h:h + 1]
                crow = cumt_ref[h:h + 1, :]
                drow = dtt_ref[h:h + 1, :]
                lmat = jnp.exp(jnp.where(tri, ccol - crow, NEG_INF))
                scs.append((cb * lmat * drow).astype(BF16))
            xp = xg[:, pr * LANES:(pr + 1) * LANES]
            rhs = jnp.concatenate([jnp.where(lo_half, xp, 0.0), jnp.where(lo_half, 0.0, xp)], axis=0)
            pieces.append(jnp.dot(jnp.concatenate(scs, axis=1), rhs.astype(BF16),
                                  preferred_element_type=F32))
        y = y + jnp.concatenate(pieces, axis=1) + de_ref[:, gs] * xg
        ht_ref[g] = htg * ecum_e[rows - 1:rows, :] + _dot(jnp.transpose(bmat), xg * wts)
        zg = z_ref[:, gs]
        yz = y * _silu(zg)
        y_ref[:, gs] = _rms_rows(yz, nw_ref[:, gs]).astype(y_ref.dtype)

    @pl.when(i == pl.num_programs(1) - 1)
    def _():
        for g in range(SSM_GROUPS):
            hg = jnp.transpose(ht_ref[g])
            hout_ref[0, g * SSM_REP:(g + 1) * SSM_REP] = hg.reshape(SSM_REP, SSM_HEAD_DIM, SSM_STATE)


def _head_expand():
    h = jnp.arange(LANES)[:, None]
    c = jnp.arange(SSM_D_INNER)[None, :]
    return ((c // SSM_HEAD_DIM) == h).astype(BF16)


def _ssd_prompt(p_main, p_tail, conv_w, conv_b, pa, pb, d_e, norm_w, ex, layer, n_layers, prev, bsz, t):
    rows = CHUNK
    nblk = t // rows
    tri, _ = _chunk_masks(rows)
    const = lambda b, i: (0, 0)
    row = lambda b, i: (b * nblk + i, 0)
    args = [p_main, p_main, p_main, p_main, p_tail, conv_w, conv_b, pa, pb, d_e, norm_w, ex, tri]
    extra_specs, extra_args, aliases = _alias_prev(prev, len(args), 1)
    kern = _ssd_prompt_kernel if prev is None else _drop_ref(_ssd_prompt_kernel, len(args))
    return pl.pallas_call(
        kern,
        grid=(bsz, nblk),
        input_output_aliases=aliases,
        in_specs=[
            pl.BlockSpec((rows, SSM_D_INNER), row),
            pl.BlockSpec((rows, SSD_SEC_W), lambda b, i: (b * nblk + i, OD_XBC // SSD_SEC_W)),
            pl.BlockSpec((rows, SSD_SEC_W), lambda b, i: (b * nblk + i, OD_XBC // SSD_SEC_W + 1)),
            pl.BlockSpec((rows, SSD_SEC_W), lambda b, i: (b * nblk + i, OD_XBC // SSD_SEC_W + 2)),
            pl.BlockSpec((rows, LANES), row),
            pl.BlockSpec((CONV_W, SSM_CONV_CH), const),
            pl.BlockSpec((1, SSM_CONV_CH), const),
            pl.BlockSpec((1, LANES), const),
            pl.BlockSpec((1, LANES), const),
            pl.BlockSpec((1, SSM_D_INNER), const),
            pl.BlockSpec((1, SSM_D_INNER), const),
            pl.BlockSpec((LANES, SSM_D_INNER), const),
            pl.BlockSpec((rows, rows), const),
        ] + extra_specs,
        out_specs=[
            pl.BlockSpec((rows, SSM_D_INNER), row),
            pl.BlockSpec((None, 1, SSM_HEADS, SSM_HEAD_DIM, SSM_STATE), lambda b, i: (layer, b, 0, 0, 0)),
        ],
        out_shape=[
            jax.ShapeDtypeStruct((bsz * t, SSM_D_INNER), BF16),
            jax.ShapeDtypeStruct((n_layers, bsz, SSM_HEADS, SSM_HEAD_DIM, SSM_STATE), F32),
        ],
        scratch_shapes=[
            pltpu.VMEM((SSM_GROUPS, SSM_STATE, SSM_GROUP_W), F32),
            pltpu.VMEM((3, SUBLANES, SSD_SEC_W), F32),
            pltpu.VMEM((rows + SUBLANES, SSD_SEC_W), F32),
            pltpu.VMEM((rows, SSM_CONV_CH), F32),
            pltpu.VMEM((LANES, rows), F32),
            pltpu.VMEM((LANES, rows), F32),
        ],
        compiler_params=_params(("parallel", "arbitrary")),
        name="ssd_prompt",
    )(*args, *extra_args)


def _swa_sample_group_kernel(sink_ref, q_ref, kn_ref, vn_ref, ck_ref, cv_ref, o_ref, *, bb):
    scale = SWA_HEAD_DIM ** -0.5
    rows, keys = bb * SWA_GROUP, bb * WINDOW
    ck = ck_ref[...].reshape(keys, SWA_KV_W).astype(BF16)
    cv = cv_ref[...].reshape(keys, SWA_KV_W).astype(BF16)
    kn = kn_ref[...].astype(BF16).astype(F32)
    vn = vn_ref[...].astype(BF16).astype(F32)
    ri = _iota2((rows, keys), 0)
    cj = _iota2((rows, keys), 1)
    valid = ((cj // WINDOW) == (ri // SWA_GROUP)) & ((cj % WINDOW) >= 1)
    for c in range(SWA_KV_HEADS):
        cs = slice(c * SWA_HEAD_DIM, (c + 1) * SWA_HEAD_DIM)
        qc = q_ref[c].astype(BF16)
        s = lax.dot_general(qc, ck[:, cs], (((1,), (1,)), ((), ())), preferred_element_type=F32) * scale
        s = jnp.where(valid, s, NEG_INF)
        sn = jnp.sum(qc.astype(F32) * kn[:, cs], axis=-1, keepdims=True) * scale
        sk = sink_ref[c]
        mx = jnp.maximum(jnp.maximum(jnp.max(s, axis=-1, keepdims=True), sn), sk)
        p = jnp.exp(s - mx)
        pn = jnp.exp(sn - mx)
        den = jnp.sum(p, axis=-1, keepdims=True) + pn + jnp.exp(sk - mx)
        o = jnp.dot(p.astype(BF16), cv[:, cs], preferred_element_type=F32) + pn * vn[:, cs]
        o_ref[c] = (o / den).astype(o_ref.dtype)


def _swa_sample_grouped(q, kn, vn, ck_all, cv_all, sinks, layer):
    nb = q.shape[0]
    g, kvh, hd, kvw = SWA_GROUP, SWA_KV_HEADS, SWA_HEAD_DIM, SWA_KV_W
    bb = SUBLANES if nb % SUBLANES == 0 else nb
    qg = jnp.transpose(q.reshape(nb, kvh, g, hd), (1, 0, 2, 3)).reshape(kvh, nb * g, hd)
    kn4 = jnp.repeat(kn, g, axis=0)
    vn4 = jnp.repeat(vn, g, axis=0)
    sk = jnp.tile(sinks.astype(F32).reshape(kvh, 1, g), (1, bb, 1)).reshape(kvh, bb * g, 1)
    cache = pl.BlockSpec((None, bb, WINDOW, kvw), lambda b: (layer, b, 0, 0))
    out = pl.pallas_call(
        functools.partial(_swa_sample_group_kernel, bb=bb),
        grid=(nb // bb,),
        in_specs=[
            pl.BlockSpec((kvh, bb * g, 1), lambda b: (0, 0, 0)),
            pl.BlockSpec((kvh, bb * g, hd), lambda b: (0, b, 0)),
            pl.BlockSpec((bb * g, kvw), lambda b: (b, 0)),
            pl.BlockSpec((bb * g, kvw), lambda b: (b, 0)),
            cache,
            cache,
        ],
        out_specs=pl.BlockSpec((kvh, bb * g, hd), lambda b: (0, b, 0)),
        out_shape=jax.ShapeDtypeStruct((kvh, nb * g, hd), F32),
        compiler_params=_params(("parallel",)),
        name="swa_sample",
    )(sk, qg, kn4, vn4, ck_all, cv_all)
    return jnp.transpose(out.reshape(kvh, nb, g, hd), (1, 0, 2, 3)).reshape(nb, kvh * g * hd)


def _gdn_sample_kernel(u_ref, prev_ref, w_ref, s_ref, z_ref, ab_ref, pa_ref, pb_ref, nw_ref,
                       o_ref, sout_ref, cnew_ref):
    for b in range(u_ref.shape[0]):
        _gdn_sample_row(u_ref, prev_ref, w_ref, s_ref, z_ref, ab_ref, pa_ref, pb_ref, nw_ref,
                        o_ref, sout_ref, cnew_ref, b)


def _gdn_sample_row(u_ref, prev_ref, w_ref, s_ref, z_ref, ab_ref, pa_ref, pb_ref, nw_ref,
                    o_ref, sout_ref, cnew_ref, b):
    hh = GDN_HEADS
    u = u_ref[b]
    pv = prev_ref[b]
    w = w_ref[...]
    x = u * w[CONV_W - 1]
    for j in range(CONV_W - 1):
        x = x + pv[j] * w[j]
        cnew_ref[b, j] = pv[j + 1] if j + 1 < CONV_W - 1 else u
    x = _silu(x)
    q = x[0:hh]
    k = x[hh:2 * hh]
    v = x[2 * hh:3 * hh]
    q = q * lax.rsqrt(jnp.sum(q * q, axis=-1, keepdims=True) + EPS) * (GDN_DK ** -0.5)
    k = k * lax.rsqrt(jnp.sum(k * k, axis=-1, keepdims=True) + EPS)
    ab = ab_ref[b]
    dec = jnp.exp(-jnp.exp(pa_ref[...]) * _softplus(ab[0:hh] + pb_ref[...]))
    beta = _sigmoid(ab[hh:2 * hh])
    qk = jnp.concatenate([q, k, jnp.zeros((LANES - 2 * hh, GDN_DK), F32)], axis=0)
    qkt = jnp.transpose(qk)
    outs = []
    for h in range(hh):
        s = s_ref[b, h] * dec[h:h + 1, :]
        kcol = qkt[:, hh + h:hh + h + 1]
        ks = jnp.sum(s * kcol, axis=0, keepdims=True)
        delta = beta[h:h + 1, :] * (v[h:h + 1, :] - ks)
        s = s + kcol * delta
        sout_ref[b, h] = s
        outs.append(jnp.sum(s * qkt[:, h:h + 1], axis=0, keepdims=True))
    o = jnp.concatenate(outs, axis=0)
    o_ref[b] = _rms_rows(o, nw_ref[...]) * _silu(z_ref[b])


def _gdn_sample(u, prev, w, s_all, z, ab, pa, pb, nw, layer, prev_out):
    nb = u.shape[0]
    hh = GDN_HEADS
    n_layers = s_all.shape[0]
    bb = 4 if nb % 4 == 0 else 1
    b3 = lambda b: (b, 0, 0)
    b4 = lambda b: (b, 0, 0, 0)
    c2 = lambda b: (0, 0)
    state_spec = pl.BlockSpec((None, bb, hh, GDN_DK, GDN_DV), lambda b: (layer, b, 0, 0, 0))
    args = [u, prev, w, s_all, z, ab, pa, pb, nw]
    extra_specs, extra_args, aliases = _alias_prev(prev_out, len(args), 1)
    kern = _gdn_sample_kernel if prev_out is None else _drop_ref(_gdn_sample_kernel, len(args))
    return pl.pallas_call(
        kern,
        grid=(nb // bb,),
        input_output_aliases=aliases,
        in_specs=[
            pl.BlockSpec((bb, 3 * hh, GDN_DK), b3),
            pl.BlockSpec((bb, CONV_W - 1, 3 * hh, GDN_DK), b4),
            pl.BlockSpec((CONV_W, 3 * hh, GDN_DK), lambda b: (0, 0, 0)),
            state_spec,
            pl.BlockSpec((bb, hh, GDN_DV), b3),
            pl.BlockSpec((bb, 2 * hh, LANES), b3),
            pl.BlockSpec((hh, LANES), c2),
            pl.BlockSpec((hh, LANES), c2),
            pl.BlockSpec((1, GDN_DV), c2),
        ] + extra_specs,
        out_specs=[
            pl.BlockSpec((bb, hh, GDN_DV), b3),
            state_spec,
            pl.BlockSpec((bb, CONV_W - 1, 3 * hh, GDN_DK), b4),
        ],
        out_shape=[
            jax.ShapeDtypeStruct((nb, hh, GDN_DV), F32),
            jax.ShapeDtypeStruct((n_layers, nb, hh, GDN_DK, GDN_DV), F32),
            jax.ShapeDtypeStruct((nb, CONV_W - 1, 3 * hh, GDN_DK), F32),
        ],
        compiler_params=_params(("parallel",)),
        name="gdn_sample",
    )(*args, *extra_args)


def _ssd_prep_kernel(x_ref, prev_ref, cw_ref, cb_ref, tail_ref, pa_ref, pb_ref,
                     xc_ref, dt_ref, dec_ref, cnew_ref):
    u = x_ref[...]
    w = cw_ref[...]
    acc = u * w[CONV_W - 1:CONV_W, :] + cb_ref[...]
    for j in range(CONV_W - 1):
        acc = acc + prev_ref[j] * w[j:j + 1, :]
        cnew_ref[j] = prev_ref[j + 1] if j + 1 < CONV_W - 1 else u
    xc_ref[...] = _silu(acc)
    dt = _softplus(tail_ref[...] + pb_ref[...])
    dt_ref[...] = dt
    dec_ref[...] = jnp.exp(dt * (-jnp.exp(pa_ref[...])))


def _ssd_prep_sample(p_main, prev_t, cw, cb, tail, pa, pb):
    nb = p_main.shape[0]
    sec = SSD_SEC_W
    nsec = SSM_CONV_CH // sec
    c2 = lambda s: (0, 0)
    return pl.pallas_call(
        _ssd_prep_kernel,
        grid=(nsec,),
        in_specs=[
            pl.BlockSpec((nb, sec), lambda s: (0, OD_XBC // sec + s)),
            pl.BlockSpec((CONV_W - 1, nb, sec), lambda s: (0, 0, s)),
            pl.BlockSpec((CONV_W, sec), lambda s: (0, s)),
            pl.BlockSpec((1, sec), lambda s: (0, s)),
            pl.BlockSpec((nb, LANES), c2),
            pl.BlockSpec((1, LANES), c2),
            pl.BlockSpec((1, LANES), c2),
        ],
        out_specs=[
            pl.BlockSpec((nb, sec), lambda s: (0, s)),
            pl.BlockSpec((nb, LANES), c2),
            pl.BlockSpec((nb, LANES), c2),
            pl.BlockSpec((CONV_W - 1, nb, sec), lambda s: (0, 0, s)),
        ],
        out_shape=[
            jax.ShapeDtypeStruct((nb, SSM_CONV_CH), F32),
            jax.ShapeDtypeStruct((nb, LANES), F32),
            jax.ShapeDtypeStruct((nb, LANES), F32),
            jax.ShapeDtypeStruct((CONV_W - 1, nb, SSM_CONV_CH), F32),
        ],
        compiler_params=_params(("arbitrary",)),
        name="ssd_prep_sample",
    )(p_main, prev_t, cw, cb, tail, pa, pb)


def _ssd_state_kernel(dec_ref, x_ref, dt_ref, b_ref, c_ref, z_ref, de_ref, nw_ref, ex_ref, h_ref,
                      y_ref, hout_ref, *, nb):
    g = pl.program_id(0)
    x = x_ref[...]
    dtx = x * _dot_sel(_split3(dt_ref[...]), ex_ref[...])
    zrows = LANES - nb
    xt = jnp.transpose(jnp.concatenate([dtx, jnp.zeros((zrows, SSM_GROUP_W), F32)], axis=0))
    xh, xl = _split2(xt)
    bpad = jnp.concatenate([b_ref[...], jnp.zeros((zrows, SSM_STATE), F32)], axis=0)
    cpad_t = jnp.transpose(jnp.concatenate([c_ref[...], jnp.zeros((zrows, SSM_STATE), F32)], axis=0))
    rowi = _iota2((LANES, SSM_STATE), 0)
    coli = _iota2((SSM_STATE, LANES), 1)
    yt = jnp.zeros((SSM_GROUP_W, LANES), F32)
    for b in range(nb):
        bh, bl = _split2(jnp.where(rowi == b, bpad, 0.0))
        upd = (jnp.dot(xh, bh, preferred_element_type=F32) + jnp.dot(xh, bl, preferred_element_type=F32)
               + jnp.dot(xl, bh, preferred_element_type=F32))
        parts = []
        for r in range(SSM_REP):
            hn = h_ref[b, r] * dec_ref[b, g * SSM_REP + r] + upd[r * SSM_HEAD_DIM:(r + 1) * SSM_HEAD_DIM]
            hout_ref[b, r] = hn
            parts.append(hn)
        hh_, hl_ = _split2(jnp.concatenate(parts, axis=0))
        ch, cl = _split2(jnp.where(coli == b, cpad_t, 0.0))
        yt = yt + (jnp.dot(hh_, ch, preferred_element_type=F32) + jnp.dot(hh_, cl, preferred_element_type=F32)
                   + jnp.dot(hl_, ch, preferred_element_type=F32))
    y = jnp.transpose(yt)[0:nb, :] + de_ref[...] * x
    y_ref[...] = _rms_rows(y * _silu(z_ref[...]), nw_ref[...]).astype(y_ref.dtype)


def _ssd_state_sample(dtv, dec, xc, p_main, d_e, nw, ex, h_all, layer, prev_out):
    nb = xc.shape[0]
    assert nb <= LANES
    gw = SSM_GROUP_W
    n_layers = h_all.shape[0]
    state_spec = pl.BlockSpec((None, nb, SSM_REP, SSM_HEAD_DIM, SSM_STATE), lambda g: (layer, 0, g, 0, 0))
    args = [dec, xc, dtv, xc, xc, p_main, d_e, nw, ex, h_all]
    extra_specs, extra_args, aliases = _alias_prev(prev_out, len(args), 1)
    kern = functools.partial(_ssd_state_kernel, nb=nb)
    if prev_out is not None:
        kern = _drop_ref(kern, len(args))
    return pl.pallas_call(
        kern,
        grid=(SSM_GROUPS,),
        input_output_aliases=aliases,
        in_specs=[
            pl.BlockSpec(memory_space=pltpu.SMEM),
            pl.BlockSpec((nb, gw), lambda g: (0, g)),
            pl.BlockSpec((nb, LANES), lambda g: (0, 0)),
            pl.BlockSpec((nb, SSM_STATE), lambda g: (0, SSM_D_INNER // SSM_STATE + g)),
            pl.BlockSpec((nb, SSM_STATE), lambda g: (0, (SSM_D_INNER + SSM_BC_W) // SSM_STATE + g)),
            pl.BlockSpec((nb, gw), lambda g: (0, g)),
            pl.BlockSpec((1, gw), lambda g: (0, g)),
            pl.BlockSpec((1, gw), lambda g: (0, g)),
            pl.BlockSpec((LANES, gw), lambda g: (0, g)),
            state_spec,
        ] + extra_specs,
        out_specs=[
            pl.BlockSpec((nb, gw), lambda g: (0, g)),
            state_spec,
        ],
        out_shape=[
            jax.ShapeDtypeStruct((nb, SSM_D_INNER), BF16),
            jax.ShapeDtypeStruct((n_layers, nb, SSM_HEADS, SSM_HEAD_DIM, SSM_STATE), F32),
        ],
        compiler_params=_params(("arbitrary",)),
        name="ssd_state_sample",
    )(*args, *extra_args)


def _lane_pad_row(v):
    return jnp.zeros((1, LANES), F32).at[0, :v.shape[0]].set(v.astype(F32))


def kernel(x_prompt, x_sample, cache_swa_k, cache_swa_v, state_gdn, state_gdn_conv, state_ssm, state_ssm_conv,
           norm_ffn1, norm_mix, norm_ffn2, norm_final, w_ffn_gate, w_ffn_up, w_ffn_down,
           w_in_even, w_out_even, attn_sinks, gdn_conv_w, gdn_A_log, gdn_dt_bias, gdn_norm_w,
           w_in_odd, w_out_odd, ssm_conv_w, ssm_conv_b, ssm_A_log, ssm_dt_bias, ssm_D, ssm_norm_w):
    bsz, t, d = x_prompt.shape
    nb = x_sample.shape[0]
    depth = norm_ffn1.shape[0]
    n_even, n_odd = (depth + 1) // 2, depth // 2
    mp = bsz * t
    assert x_sample.shape[1] == 1 and d == D_MODEL and t % (2 * CHUNK) == 0
    hh = GDN_HEADS

    w_even_t = jnp.swapaxes(w_in_even, 1, 2)
    w_odd_t = jnp.swapaxes(w_in_odd, 1, 2)
    ex = _head_expand()
    cache_k = cache_swa_k.reshape(n_even, nb, WINDOW, SWA_KV_W)
    cache_v = cache_swa_v.reshape(n_even, nb, WINDOW, SWA_KV_W)

    xp = x_prompt.reshape(mp, d)
    xs = x_sample.reshape(nb, d)
    gain_f = norm_final.reshape(1, d)
    ffn_tm = _tile(mp, FFN_TM)

    small = {k: [] for k in ("pk", "pv", "pgc", "psc", "sk", "sv", "sgc", "ssc")}
    p_gdn = p_ssm = s_gdn = s_ssm = None

    def ffn_pair(xp, xs, gain, l, which, final):
        gain = gain.reshape(1, d)
        yp, ys, wg, wu, wd = _ffn_first(xp, xs, gain, gain_f, w_ffn_gate, w_ffn_up, w_ffn_down, l, which,
                                        final_norm=final, tm=ffn_tm, tf=FFN_FIRST_TF)
        yp = _ffn_rest(xp, gain, gain_f, wg, wu, wd, yp, final_norm=final, tm=ffn_tm, tf=FFN_TF)
        return yp, ys

    for l in range(depth):
        xp, xs = ffn_pair(xp, xs, norm_ffn1[l], l, 0, False)
        gmix = norm_mix[l].reshape(1, d)
        if l % 2 == 0:
            e = l // 2
            ptm, otm = _tile(mp, PROJ_EVEN_TM), _tile(mp, OUT_EVEN_TM)
            pm, pt, sm, st, w_main, w_tail = _proj_first(xp, xs, gmix, w_even_t, e, EV_MAIN, tm=ptm, tn=FIRST_TN)
            pm, pt = _proj_rest(xp, gmix, w_main, w_tail, pm, pt, tm=ptm, tn=PROJ_EVEN_TN)
            pa = _lane_pad_row(gdn_A_log[e])
            pb = _lane_pad_row(gdn_dt_bias[e])
            nw = gdn_norm_w[e].reshape(1, GDN_DV).astype(F32)

            kn = sm[:, EV_K:EV_K + SWA_KV_W]
            vn = sm[:, EV_V:EV_V + SWA_KV_W]
            o_a_s = _swa_sample_grouped(sm[:, EV_Q:EV_Q + SWA_Q_W], kn, vn, cache_k, cache_v, attn_sinks[e], e)
            ab = jnp.broadcast_to(st[:, :2 * hh, None], (nb, 2 * hh, LANES))
            o_b_s, s_gdn, gc_s = _gdn_sample(
                sm[:, EV_GQKV:EV_GQKV + GDN_CONV_CH].reshape(nb, 3 * hh, GDN_DK),
                state_gdn_conv[e].reshape(nb, CONV_W - 1, 3 * hh, GDN_DK),
                gdn_conv_w[e].reshape(CONV_W, 3 * hh, GDN_DK),
                state_gdn, sm[:, EV_Z:EV_Z + GDN_V_W].reshape(nb, hh, GDN_DV), ab,
                jnp.broadcast_to(gdn_A_log[e].astype(F32)[:, None], (hh, LANES)),
                jnp.broadcast_to(gdn_dt_bias[e].astype(F32)[:, None], (hh, LANES)), nw, e, s_gdn)
            small["sk"].append(kn.reshape(nb, 1, SWA_KV_HEADS, SWA_HEAD_DIM))
            small["sv"].append(vn.reshape(nb, 1, SWA_KV_HEADS, SWA_HEAD_DIM))
            small["sgc"].append(gc_s.reshape(nb, CONV_W - 1, GDN_CONV_CH))
            outs_s = [o_a_s.reshape(nb, SWA_Q_W).astype(BF16), o_b_s.reshape(nb, GDN_V_W).astype(BF16)]

            o_a_p = _swa_prompt(pm, attn_sinks[e].astype(F32), bsz, t)
            o_b_p, p_gdn = _gdn_prompt(pm, pt, gdn_conv_w, pa, pb, nw, e, n_even, p_gdn, bsz, t, rows=2 * CHUNK)
            pm3 = pm.reshape(bsz, t, EV_MAIN)
            small["pk"].append(pm3[:, t - WINDOW:, EV_K:EV_K + SWA_KV_W].reshape(bsz, WINDOW, SWA_KV_HEADS, SWA_HEAD_DIM))
            small["pv"].append(pm3[:, t - WINDOW:, EV_V:EV_V + SWA_KV_W].reshape(bsz, WINDOW, SWA_KV_HEADS, SWA_HEAD_DIM))
            small["pgc"].append(pm3[:, t - (CONV_W - 1):, EV_GQKV:EV_GQKV + GDN_CONV_CH])
            yp, xs, w_out = _outproj_first(xp, xs, [o_a_p, o_b_p], outs_s, w_out_even, e, tm=otm, tn=FIRST_TN)
            xp = _outproj_rest(xp, [o_a_p, o_b_p], w_out, yp, tm=otm, tn=OUT_EVEN_TN)
        else:
            o = l // 2
            ptm, otm = _tile(mp, PROJ_ODD_TM), _tile(mp, OUT_ODD_TM)
            pm, pt, sm, st, w_main, w_tail = _proj_first(xp, xs, gmix, w_odd_t, o, OD_MAIN, tm=ptm, tn=FIRST_TN)
            pm, pt = _proj_rest(xp, gmix, w_main, w_tail, pm, pt, tm=ptm, tn=PROJ_ODD_TN)
            pa = _lane_pad_row(ssm_A_log[o])
            pb = _lane_pad_row(ssm_dt_bias[o])
            d_e = jnp.repeat(ssm_D[o].astype(F32), SSM_HEAD_DIM).reshape(1, SSM_D_INNER)
            nw = ssm_norm_w[o].reshape(1, SSM_D_INNER).astype(F32)
            cb = ssm_conv_b[o].reshape(1, SSM_CONV_CH)

            xc, dtv, dec, cnew = _ssd_prep_sample(sm, jnp.swapaxes(state_ssm_conv[o], 0, 1), ssm_conv_w[o], cb,
                                                  st, pa, pb)
            y_s, s_ssm = _ssd_state_sample(dtv, dec, xc, sm, d_e, nw, ex, state_ssm, o, s_ssm)
            small["ssc"].append(jnp.swapaxes(cnew, 0, 1))

            y_p, p_ssm = _ssd_prompt(pm, pt, ssm_conv_w[o], cb, pa, pb, d_e, nw, ex, o, n_odd, p_ssm, bsz, t)
            pm3 = pm.reshape(bsz, t, OD_MAIN)
            small["psc"].append(pm3[:, t - (CONV_W - 1):, OD_XBC:OD_XBC + SSM_CONV_CH])
            yp, xs, w_out = _outproj_first(xp, xs, [y_p], [y_s], w_out_odd, o, tm=otm, tn=FIRST_TN)
            xp = _outproj_rest(xp, [y_p], w_out, yp, tm=otm, tn=OUT_ODD_TN)
        xp, xs = ffn_pair(xp, xs, norm_ffn2[l], l, 1, l == depth - 1)

    st_ = {k: jnp.stack(v) for k, v in small.items()}
    return (xp.reshape(bsz, t, d), xs.reshape(nb, 1, d),
            st_["pk"], st_["pv"], p_gdn, st_["pgc"], p_ssm, st_["psc"],
            st_["sk"], st_["sv"], s_gdn, st_["sgc"], s_ssm, st_["ssc"])
```

```python
import functools

import jax
import jax.numpy as jnp
from jax import lax
from jax.experimental import pallas as pl
from jax.experimental.pallas import tpu as pltpu

F32 = jnp.float32
BF16 = jnp.bfloat16

D_MODEL = 2048
SWA_HEADS = 16
SWA_KV_HEADS = 4
SWA_GROUP = SWA_HEADS // SWA_KV_HEADS
SWA_HEAD_DIM = 64
WINDOW = 128
GDN_HEADS = 8
GDN_DK = 128
GDN_DV = 128
CONV_W = 4
SSM_D_INNER = 2 * D_MODEL
SSM_HEAD_DIM = 64
SSM_HEADS = SSM_D_INNER // SSM_HEAD_DIM
SSM_GROUPS = 8
SSM_REP = SSM_HEADS // SSM_GROUPS
SSM_STATE = 128
EPS = 1e-6
NEG_INF = -1e30

SWA_Q_W = SWA_HEADS * SWA_HEAD_DIM
SWA_KV_W = SWA_KV_HEADS * SWA_HEAD_DIM
GDN_QK_W = GDN_HEADS * GDN_DK
GDN_V_W = GDN_HEADS * GDN_DV
GDN_CONV_CH = 2 * GDN_QK_W + GDN_V_W
GDN_HALF_W = GDN_QK_W // 2
SSM_BC_W = SSM_GROUPS * SSM_STATE
SSM_CONV_CH = SSM_D_INNER + 2 * SSM_BC_W
SSM_GROUP_W = SSM_REP * SSM_HEAD_DIM
SSD_SEC_W = SSM_CONV_CH // 3

EV_Q = 0
EV_K = EV_Q + SWA_Q_W
EV_V = EV_K + SWA_KV_W
EV_GQKV = EV_V + SWA_KV_W
EV_Z = EV_GQKV + GDN_CONV_CH
EV_MAIN = EV_Z + GDN_V_W
OD_XBC = SSM_D_INNER
OD_MAIN = SSM_D_INNER + SSM_CONV_CH

LANES = 128
SUBLANES = 8
CHUNK = 128
VMEM_LIMIT = 56 * 1024 * 1024

FFN_TM, FFN_TF = 512, 512
PROJ_EVEN_TM, PROJ_EVEN_TN = 1024, EV_MAIN // 4
PROJ_ODD_TM, PROJ_ODD_TN = 1024, 1024
OUT_EVEN_TM, OUT_EVEN_TN = 512, D_MODEL
OUT_ODD_TM, OUT_ODD_TN = 1024, 1024
FFN_FIRST_TF, FIRST_TN = 256, 512


def _params(sem):
    return pltpu.CompilerParams(dimension_semantics=sem, vmem_limit_bytes=VMEM_LIMIT)


def _tile(m, pref):
    return pref if m % pref == 0 else m


def _sigmoid(x):
    return 1.0 / (1.0 + jnp.exp(-x))


def _silu(x):
    return x * _sigmoid(x)


def _softplus(x):
    return jnp.maximum(x, 0.0) + jnp.log1p(jnp.exp(-jnp.abs(x)))


def _dot(a, b):
    return jnp.dot(a.astype(BF16), b.astype(BF16), preferred_element_type=F32)


def _dot_nt(a, b):
    return lax.dot_general(a.astype(BF16), b.astype(BF16), (((1,), (1,)), ((), ())),
                           preferred_element_type=F32)


def _bdot(a, b):
    return jnp.einsum("bij,bjk->bik", a.astype(BF16), b.astype(BF16), preferred_element_type=F32)


def _bdot_nt(a, b):
    return jnp.einsum("bid,bjd->bij", a.astype(BF16), b.astype(BF16), preferred_element_type=F32)


def _split2(a):
    hi = a.astype(BF16)
    lo = (a - hi.astype(F32)).astype(BF16)
    return hi, lo


def _split3(a):
    hi = a.astype(BF16)
    r = a - hi.astype(F32)
    mid = r.astype(BF16)
    lo = (r - mid.astype(F32)).astype(BF16)
    return hi, mid, lo


def _sel_dot(m01, parts):
    out = jnp.dot(m01, parts[0], preferred_element_type=F32)
    for p in parts[1:]:
        out = out + jnp.dot(m01, p, preferred_element_type=F32)
    return out


def _dot_sel(parts, m01):
    out = jnp.dot(parts[0], m01, preferred_element_type=F32)
    for p in parts[1:]:
        out = out + jnp.dot(p, m01, preferred_element_type=F32)
    return out


def _rms_rows(x, gain):
    ms = jnp.mean(x * x, axis=-1, keepdims=True)
    return x * lax.rsqrt(ms + EPS) * gain


def _iota2(shape, dim):
    return lax.broadcasted_iota(jnp.int32, shape, dim)


def _drop_ref(kern, pos):
    def entry(*refs):
        return kern(*refs[:pos], *refs[pos + 1:])
    return entry


def _alias_prev(prev, n_in, out_idx):
    if prev is None:
        return [], [], {}
    return [pl.BlockSpec(memory_space=pl.ANY)], [prev], {n_in: out_idx}


def _ffn_rows(h, wg, wu, wd):
    a = jnp.dot(h, wg, preferred_element_type=F32)
    b = jnp.dot(h, wu, preferred_element_type=F32)
    return jnp.dot((_silu(a) * b).astype(BF16), wd, preferred_element_type=F32)


def _ffn_step(j, nsteps, streams, g_ref, gf_ref, weights, final_norm):
    def finish(x_ref, acc):
        y = x_ref[...] + 0.5 * acc
        return _rms_rows(y, gf_ref[...]) if final_norm else y

    @pl.when(j == 0)
    def _():
        w = weights()
        for x_ref, h_ref, o_ref in streams:
            h = _rms_rows(x_ref[...], g_ref[...]).astype(BF16)
            h_ref[...] = h
            o_ref[...] = _ffn_rows(h, *w)

    @pl.when((j > 0) & (j < nsteps - 1))
    def _():
        w = weights()
        for _, h_ref, o_ref in streams:
            o_ref[...] += _ffn_rows(h_ref[...], *w)

    @pl.when(j == nsteps - 1)
    def _():
        w = weights()
        for x_ref, h_ref, o_ref in streams:
            o_ref[...] = finish(x_ref, o_ref[...] + _ffn_rows(h_ref[...], *w))


def _ffn_first_kernel(xp_ref, xs_ref, g_ref, gf_ref, wg_ref, wu_ref, wd_ref,
                      op_ref, os_ref, wgb_ref, wub_ref, wdb_ref, hp_ref, hs_ref, *, final_norm):
    def weights():
        wg, wu, wd = wg_ref[...].astype(BF16), wu_ref[...].astype(BF16), wd_ref[...].astype(BF16)
        wgb_ref[...] = wg
        wub_ref[...] = wu
        wdb_ref[...] = wd
        return wg, wu, wd

    streams = ((xp_ref, hp_ref, op_ref), (xs_ref, hs_ref, os_ref))
    _ffn_step(pl.program_id(0), pl.num_programs(0), streams, g_ref, gf_ref, weights, final_norm)


def _ffn_first(xp, xs, gain, gain_final, wg, wu, wd, layer, which, *, final_norm, tm, tf):
    mp, d = xp.shape
    nb = xs.shape[0]
    f = wg.shape[-1]
    c2 = lambda j: (0, 0)
    return pl.pallas_call(
        functools.partial(_ffn_first_kernel, final_norm=final_norm),
        grid=(f // tf,),
        in_specs=[
            pl.BlockSpec((tm, d), c2),
            pl.BlockSpec((nb, d), c2),
            pl.BlockSpec((1, d), c2),
            pl.BlockSpec((1, d), c2),
            pl.BlockSpec((None, None, d, tf), lambda j: (layer, which, 0, j)),
            pl.BlockSpec((None, None, d, tf), lambda j: (layer, which, 0, j)),
            pl.BlockSpec((None, None, tf, d), lambda j: (layer, which, j, 0)),
        ],
        out_specs=[
            pl.BlockSpec((tm, d), c2),
            pl.BlockSpec((nb, d), c2),
            pl.BlockSpec((d, tf), lambda j: (0, j)),
            pl.BlockSpec((d, tf), lambda j: (0, j)),
            pl.BlockSpec((tf, d), lambda j: (j, 0)),
        ],
        out_shape=[
            jax.ShapeDtypeStruct((mp, d), F32), jax.ShapeDtypeStruct((nb, d), F32),
            jax.ShapeDtypeStruct((d, f), BF16), jax.ShapeDtypeStruct((d, f), BF16),
            jax.ShapeDtypeStruct((f, d), BF16),
        ],
        scratch_shapes=[pltpu.VMEM((tm, d), BF16), pltpu.VMEM((nb, d), BF16)],
        compiler_params=_params(("arbitrary",)),
        name="ffn_first",
    )(xp, xs, gain, gain_final, wg, wu, wd)


def _ffn_rest_kernel(x_ref, g_ref, gf_ref, wg_ref, wu_ref, wd_ref, prev_ref, o_ref, h_ref, *, final_norm):
    del prev_ref
    _ffn_step(pl.program_id(1), pl.num_programs(1), ((x_ref, h_ref, o_ref),), g_ref, gf_ref,
              lambda: (wg_ref[...], wu_ref[...], wd_ref[...]), final_norm)


def _ffn_rest(x, gain, gain_final, wg, wu, wd, prev, *, final_norm, tm, tf):
    m, d = x.shape
    f = wg.shape[-1]
    if m == tm:
        return prev
    return pl.pallas_call(
        functools.partial(_ffn_rest_kernel, final_norm=final_norm),
        grid=(m // tm - 1, f // tf),
        in_specs=[
            pl.BlockSpec((tm, d), lambda i, j: (i + 1, 0)),
            pl.BlockSpec((1, d), lambda i, j: (0, 0)),
            pl.BlockSpec((1, d), lambda i, j: (0, 0)),
            pl.BlockSpec((d, tf), lambda i, j: (0, j)),
            pl.BlockSpec((d, tf), lambda i, j: (0, j)),
            pl.BlockSpec((tf, d), lambda i, j: (j, 0)),
            pl.BlockSpec(memory_space=pl.ANY),
        ],
        out_specs=pl.BlockSpec((tm, d), lambda i, j: (i + 1, 0)),
        out_shape=jax.ShapeDtypeStruct((m, d), F32),
        scratch_shapes=[pltpu.VMEM((tm, d), BF16)],
        input_output_aliases={6: 0},
        compiler_params=_params(("parallel", "arbitrary")),
        name="ffn_rest",
    )(x, gain, gain_final, wg, wu, wd, prev)


def _proj_first_kernel(xp_ref, xs_ref, g_ref, w_ref, wt_ref,
                       op_ref, otp_ref, os_ref, ots_ref, wb_ref, wtb_ref, hp_ref, hs_ref):
    j = pl.program_id(0)
    streams = ((xp_ref, hp_ref, op_ref, otp_ref), (xs_ref, hs_ref, os_ref, ots_ref))

    @pl.when(j == 0)
    def _():
        wt = wt_ref[...]
        pad = jnp.zeros((LANES - wt.shape[0], wt.shape[1]), F32)
        wt = jnp.transpose(jnp.concatenate([wt, pad], axis=0)).astype(BF16)
        wtb_ref[...] = wt
        for x_ref, h_ref, _, ot_ref in streams:
            h = _rms_rows(x_ref[...], g_ref[...]).astype(BF16)
            h_ref[...] = h
            ot_ref[...] = jnp.dot(h, wt, preferred_element_type=F32)

    w = jnp.transpose(w_ref[...].astype(BF16))
    wb_ref[...] = w
    for _, h_ref, o_ref, _ in streams:
        o_ref[...] = jnp.dot(h_ref[...], w, preferred_element_type=F32)


def _proj_first(xp, xs, gain, w_t, layer, n, *, tm, tn):
    mp, d = xp.shape
    nb = xs.shape[0]
    n_tail = w_t.shape[1] - n
    assert n % n_tail == 0 and n_tail % SUBLANES == 0 and n_tail <= LANES
    c2 = lambda j: (0, 0)
    return pl.pallas_call(
        _proj_first_kernel,
        grid=(n // tn,),
        in_specs=[
            pl.BlockSpec((tm, d), c2),
            pl.BlockSpec((nb, d), c2),
            pl.BlockSpec((1, d), c2),
            pl.BlockSpec((None, tn, d), lambda j: (layer, j, 0)),
            pl.BlockSpec((None, n_tail, d), lambda j: (layer, n // n_tail, 0)),
        ],
        out_specs=[
            pl.BlockSpec((tm, tn), lambda j: (0, j)),
            pl.BlockSpec((tm, LANES), c2),
            pl.BlockSpec((nb, tn), lambda j: (0, j)),
            pl.BlockSpec((nb, LANES), c2),
            pl.BlockSpec((d, tn), lambda j: (0, j)),
            pl.BlockSpec((d, LANES), c2),
        ],
        out_shape=[
            jax.ShapeDtypeStruct((mp, n), F32), jax.ShapeDtypeStruct((mp, LANES), F32),
            jax.ShapeDtypeStruct((nb, n), F32), jax.ShapeDtypeStruct((nb, LANES), F32),
            jax.ShapeDtypeStruct((d, n), BF16), jax.ShapeDtypeStruct((d, LANES), BF16),
        ],
        scratch_shapes=[pltpu.VMEM((tm, d), BF16), pltpu.VMEM((nb, d), BF16)],
        compiler_params=_params(("arbitrary",)),
        name="proj_first",
    )(xp, xs, gain, w_t, w_t)


def _proj_rest_kernel(x_ref, g_ref, w_ref, wt_ref, prev_ref, prevt_ref, o_ref, ot_ref, h_ref):
    del prev_ref, prevt_ref
    j = pl.program_id(1)

    @pl.when(j == 0)
    def _():
        h = _rms_rows(x_ref[...], g_ref[...]).astype(BF16)
        h_ref[...] = h
        ot_ref[...] = jnp.dot(h, wt_ref[...], preferred_element_type=F32)

    o_ref[...] = jnp.dot(h_ref[...], w_ref[...], preferred_element_type=F32)


def _proj_rest(x, gain, w_main, w_tail, prev_main, prev_tail, *, tm, tn):
    m, d = x.shape
    n = w_main.shape[1]
    if m == tm:
        return prev_main, prev_tail
    return pl.pallas_call(
        _proj_rest_kernel,
        grid=(m // tm - 1, n // tn),
        in_specs=[
            pl.BlockSpec((tm, d), lambda i, j: (i + 1, 0)),
            pl.BlockSpec((1, d), lambda i, j: (0, 0)),
            pl.BlockSpec((d, tn), lambda i, j: (0, j)),
            pl.BlockSpec((d, LANES), lambda i, j: (0, 0)),
            pl.BlockSpec(memory_space=pl.ANY),
            pl.BlockSpec(memory_space=pl.ANY),
        ],
        out_specs=[
            pl.BlockSpec((tm, tn), lambda i, j: (i + 1, j)),
            pl.BlockSpec((tm, LANES), lambda i, j: (i + 1, 0)),
        ],
        out_shape=[jax.ShapeDtypeStruct((m, n), F32), jax.ShapeDtypeStruct((m, LANES), F32)],
        scratch_shapes=[pltpu.VMEM((tm, d), BF16)],
        input_output_aliases={4: 0, 5: 1},
        compiler_params=_params(("parallel", "arbitrary")),
        name="proj_rest",
    )(x, gain, w_main, w_tail, prev_main, prev_tail)


def _outproj_first_kernel(*refs, n_in):
    xp_ref, xs_ref = refs[0], refs[1]
    op_refs = refs[2:2 + n_in]
    os_refs = refs[2 + n_in:2 + 2 * n_in]
    w_refs = refs[2 + 2 * n_in:2 + 3 * n_in]
    yp_ref, ys_ref, wb_ref = refs[2 + 3 * n_in:]
    accp = xp_ref[...]
    accs = xs_ref[...]
    for kb in range(n_in):
        w = w_refs[kb][...].astype(BF16)
        kw = w.shape[0]
        wb_ref[kb * kw:(kb + 1) * kw, :] = w
        accp = accp + jnp.dot(op_refs[kb][...], w, preferred_element_type=F32)
        accs = accs + jnp.dot(os_refs[kb][...], w, preferred_element_type=F32)
    yp_ref[...] = accp
    ys_ref[...] = accs


def _outproj_first(xp, xs, outs_p, outs_s, w_all, layer, *, tm, tn):
    mp, d = xp.shape
    nb = xs.shape[0]
    n_in = len(outs_p)
    kw = outs_p[0].shape[1]
    k_all = kw * n_in
    assert all(o.shape[1] == kw for o in outs_p + outs_s) and k_all == w_all.shape[1]
    in_specs = [pl.BlockSpec((tm, tn), lambda j: (0, j)), pl.BlockSpec((nb, tn), lambda j: (0, j))]
    in_specs += [pl.BlockSpec((tm, kw), lambda j: (0, 0)) for _ in range(n_in)]
    in_specs += [pl.BlockSpec((nb, kw), lambda j: (0, 0)) for _ in range(n_in)]
    in_specs += [pl.BlockSpec((None, kw, tn), lambda j, kb=kb: (layer, kb, j)) for kb in range(n_in)]
    return pl.pallas_call(
        functools.partial(_outproj_first_kernel, n_in=n_in),
        grid=(d // tn,),
        in_specs=in_specs,
        out_specs=[
            pl.BlockSpec((tm, tn), lambda j: (0, j)),
            pl.BlockSpec((nb, tn), lambda j: (0, j)),
            pl.BlockSpec((k_all, tn), lambda j: (0, j)),
        ],
        out_shape=[
            jax.ShapeDtypeStruct((mp, d), F32), jax.ShapeDtypeStruct((nb, d), F32),
            jax.ShapeDtypeStruct((k_all, d), BF16),
        ],
        compiler_params=_params(("arbitrary",)),
        name="outproj_first",
    )(xp, xs, *outs_p, *outs_s, *([w_all] * n_in))


def _outproj_rest_kernel(*refs, n_in):
    x_ref = refs[0]
    o_refs = refs[1:1 + n_in]
    w_refs = refs[1 + n_in:1 + 2 * n_in]
    out_ref = refs[2 + 2 * n_in]
    acc = x_ref[...]
    for o_ref, w_ref in zip(o_refs, w_refs):
        acc = acc + jnp.dot(o_ref[...], w_ref[...], preferred_element_type=F32)
    out_ref[...] = acc


def _outproj_rest(x, outs, w, prev, *, tm, tn):
    m, d = x.shape
    n_in = len(outs)
    kw = outs[0].shape[1]
    if m == tm:
        return prev
    in_specs = [pl.BlockSpec((tm, tn), lambda i, j: (i + 1, j))]
    in_specs += [pl.BlockSpec((tm, kw), lambda i, j: (i + 1, 0)) for _ in range(n_in)]
    in_specs += [pl.BlockSpec((kw, tn), lambda i, j, kb=kb: (kb, j)) for kb in range(n_in)]
    in_specs.append(pl.BlockSpec(memory_space=pl.ANY))
    return pl.pallas_call(
        functools.partial(_outproj_rest_kernel, n_in=n_in),
        grid=(m // tm - 1, d // tn),
        in_specs=in_specs,
        out_specs=pl.BlockSpec((tm, tn), lambda i, j: (i + 1, j)),
        out_shape=jax.ShapeDtypeStruct((m, d), F32),
        input_output_aliases={1 + 2 * n_in: 0},
        compiler_params=_params(("parallel", "arbitrary")),
        name="outproj_rest",
    )(x, *outs, *([w] * n_in), prev)


def _swa_prompt_kernel(sink_ref, q_ref, kc_ref, kp_ref, vc_ref, vp_ref, o_ref):
    n = pl.program_id(1)
    w = WINDOW
    qi = _iota2((w, 2 * w), 0)
    kj = _iota2((w, 2 * w), 1)
    valid = (kj > qi) & (kj <= qi + w) & ((n > 0) | (kj >= w))
    scale = SWA_HEAD_DIM ** -0.5
    kk = jnp.concatenate([kp_ref[...], kc_ref[...]], axis=0).astype(BF16)
    vv = jnp.concatenate([vp_ref[...], vc_ref[...]], axis=0).astype(BF16)
    q = q_ref[...].astype(BF16)
    outs = []
    for h in range(SWA_HEADS):
        c = h // SWA_GROUP
        qh = q[:, h * SWA_HEAD_DIM:(h + 1) * SWA_HEAD_DIM]
        kh = kk[:, c * SWA_HEAD_DIM:(c + 1) * SWA_HEAD_DIM]
        vh = vv[:, c * SWA_HEAD_DIM:(c + 1) * SWA_HEAD_DIM]
        s = lax.dot_general(qh, kh, (((1,), (1,)), ((), ())), preferred_element_type=F32) * scale
        s = jnp.where(valid, s, NEG_INF)
        sink = sink_ref[h]
        mx = jnp.maximum(jnp.max(s, axis=-1, keepdims=True), sink)
        p = jnp.exp(s - mx)
        den = jnp.sum(p, axis=-1, keepdims=True) + jnp.exp(sink - mx)
        p = p / den
        outs.append(jnp.dot(p.astype(BF16), vh, preferred_element_type=F32))
    o_ref[...] = jnp.concatenate(outs, axis=-1).astype(o_ref.dtype)


def _swa_prompt(p_main, sinks, bsz, t):
    nb = t // WINDOW
    kblk = EV_K // SWA_KV_W
    vblk = EV_V // SWA_KV_W

    def cur(col):
        return lambda b, n: (b * nb + n, col)

    def prev(col):
        return lambda b, n: (b * nb + jnp.maximum(n - 1, 0), col)

    return pl.pallas_call(
        _swa_prompt_kernel,
        grid=(bsz, nb),
        in_specs=[
            pl.BlockSpec(memory_space=pltpu.SMEM),
            pl.BlockSpec((WINDOW, SWA_Q_W), cur(EV_Q // SWA_Q_W)),
            pl.BlockSpec((WINDOW, SWA_KV_W), cur(kblk)),
            pl.BlockSpec((WINDOW, SWA_KV_W), prev(kblk)),
            pl.BlockSpec((WINDOW, SWA_KV_W), cur(vblk)),
            pl.BlockSpec((WINDOW, SWA_KV_W), prev(vblk)),
        ],
        out_specs=pl.BlockSpec((WINDOW, SWA_Q_W), cur(0)),
        out_shape=jax.ShapeDtypeStruct((bsz * t, SWA_Q_W), BF16),
        compiler_params=_params(("parallel", "arbitrary")),
        name="swa_prompt",
    )(sinks, p_main, p_main, p_main, p_main, p_main)


def _unit_lower_inverse(a, ri, ci):
    a0 = jnp.where((ri >> 4) == (ci >> 4), a, 0.0)
    y = -a0
    p = _bdot(a0, a0)
    for step in range(3):
        y = y + p + _bdot(y, p)
        if step < 2:
            p = _bdot(p, p)
    for sh in (4, 5, 6):
        e = jnp.where(((ri >> (sh + 1)) == (ci >> (sh + 1))) & ((ri >> sh) != (ci >> sh)), a, 0.0)
        t = e + _bdot(y, e)
        y = y - t - _bdot(t, y)
    return y


def _conv_section(ext_ref, carry_ref, sec, u_ref, w, bias, rows):
    ext_ref[0:SUBLANES, :] = carry_ref[sec]
    ext_ref[SUBLANES:SUBLANES + rows, :] = u_ref[...]
    acc = ext_ref[SUBLANES:SUBLANES + rows, :] * w[CONV_W - 1:CONV_W, :]
    for j in range(CONV_W - 1):
        off = SUBLANES - (CONV_W - 1) + j
        acc = acc + ext_ref[off:off + rows, :] * w[j:j + 1, :]
    if bias is not None:
        acc = acc + bias
    carry_ref[sec] = u_ref[rows - SUBLANES:rows, :]
    return _silu(acc)


def _gdn_prompt_kernel(uq0, uq1, uk0, uk1, uv0, uv1, z0_ref, z1_ref, tail_ref, cw_ref, pa_ref, pb_ref, nw_ref,
                       tri_ref, ones_ref,
                       o_ref, sout_ref,
                       s_ref, carry_ref, ext_ref, cq_ref, ck_ref, cv_ref, *, rows):
    i = pl.program_id(1)
    nchunk = rows // CHUNK
    hh = GDN_HEADS
    hw = GDN_HALF_W

    @pl.when(i == 0)
    def _():
        s_ref[...] = jnp.zeros_like(s_ref)
        carry_ref[...] = jnp.zeros_like(carry_ref)

    for sec, (halves, c_ref) in enumerate((((uq0, uq1), cq_ref), ((uk0, uk1), ck_ref), ((uv0, uv1), cv_ref))):
        for half, u_ref in enumerate(halves):
            ws = slice(sec * GDN_QK_W + half * hw, sec * GDN_QK_W + (half + 1) * hw)
            c_ref[:, half * hw:(half + 1) * hw] = _conv_section(ext_ref, carry_ref, 2 * sec + half, u_ref,
                                                                 cw_ref[:, ws], None, rows)

    t = tail_ref[...]
    g_all = -jnp.exp(pa_ref[...]) * _softplus(t + pb_ref[...])
    beta_all = _sigmoid(t)
    g_parts = _split3(g_all)
    gcum_all = _sel_dot(tri_ref[...], g_parts)
    gtot_all = _sel_dot(ones_ref[...], g_parts)
    gcum_t = jnp.transpose(gcum_all)

    ri = _iota2((1, CHUNK, CHUNK), 1)
    ci = _iota2((1, CHUNK, CHUNK), 2)
    incl = ri >= ci
    strict = ri > ci

    def rs(c):
        return slice(c * CHUNK, (c + 1) * CHUNK)

    def hs(h):
        return slice(h * GDN_DK, (h + 1) * GDN_DK)

    def stack(fn):
        return jnp.stack([fn(c, h) for c in range(nchunk) for h in range(hh)], axis=0)

    q = stack(lambda c, h: cq_ref[rs(c), hs(h)])
    k = stack(lambda c, h: ck_ref[rs(c), hs(h)])
    v = stack(lambda c, h: cv_ref[rs(c), hs(h)])
    gc = stack(lambda c, h: gcum_all[rs(c), h:h + 1])
    gt = stack(lambda c, h: gtot_all[rs(c), h:h + 1])
    bc = stack(lambda c, h: beta_all[rs(c), hh + h:hh + h + 1])
    gr = stack(lambda c, h: gcum_t[h:h + 1, rs(c)])

    q = q * lax.rsqrt(jnp.sum(q * q, axis=-1, keepdims=True) + EPS) * (GDN_DK ** -0.5)
    k = k * lax.rsqrt(jnp.sum(k * k, axis=-1, keepdims=True) + EPS)
    decay = jnp.exp(jnp.where(incl, gc - gr, NEG_INF))
    kb = k * bc
    a = jnp.where(strict, _bdot_nt(kb, k) * decay, 0.0)
    y = _unit_lower_inverse(a, ri, ci)
    eg = jnp.exp(gc)
    rhs = jnp.concatenate([v * bc, kb * eg], axis=2)
    sol = rhs + _bdot(y, rhs)
    u = sol[:, :, :GDN_DV]
    w = sol[:, :, GDN_DV:]
    wq = jnp.concatenate([w, q * eg], axis=1)
    attn = _bdot_nt(q, k) * decay
    kdt = jnp.swapaxes(k * jnp.exp(gt - gc), 1, 2)
    egt = jnp.exp(gt)

    for c in range(nchunk):
        bs = slice(c * hh, (c + 1) * hh)
        s = s_ref[...]
        ws_qs = _bdot(wq[bs], s)
        v_new = u[bs] - ws_qs[:, :CHUNK]
        o = ws_qs[:, CHUNK:] + _bdot(attn[bs], v_new)
        s_ref[...] = s * egt[bs] + _bdot(kdt[bs], v_new)
        for h in range(hh):
            z_ref = z0_ref if h < hh // 2 else z1_ref
            zc = z_ref[rs(c), hs(h % (hh // 2))]
            og = _rms_rows(o[h], nw_ref[...]) * _silu(zc)
            o_ref[rs(c), hs(h)] = og.astype(o_ref.dtype)

    @pl.when(i == pl.num_programs(1) - 1)
    def _():
        sout_ref[0] = s_ref[...]


def _chunk_masks(rows):
    r = jnp.arange(rows)
    same = (r[:, None] // CHUNK) == (r[None, :] // CHUNK)
    tri = (same & (r[:, None] >= r[None, :])).astype(BF16)
    ones = same.astype(BF16)
    return tri, ones


def _gdn_prompt(p_main, p_tail, conv_w, pa, pb, norm_w, layer, n_layers, prev, bsz, t, *, rows):
    nblk = t // rows
    hw = GDN_HALF_W
    hh = GDN_HEADS
    tri, ones = _chunk_masks(rows)

    def col(base):
        return lambda b, i: (b * nblk + i, base // hw)

    const = lambda b, i: (0, 0)
    in_specs = [pl.BlockSpec((rows, hw), col(EV_GQKV + k * hw)) for k in range(6)]
    in_specs += [pl.BlockSpec((rows, hw), col(EV_Z)), pl.BlockSpec((rows, hw), col(EV_Z + hw))]
    in_specs += [
        pl.BlockSpec((rows, LANES), lambda b, i: (b * nblk + i, 0)),
        pl.BlockSpec((None, CONV_W, GDN_CONV_CH), lambda b, i: (layer, 0, 0)),
        pl.BlockSpec((1, LANES), const),
        pl.BlockSpec((1, LANES), const),
        pl.BlockSpec((1, GDN_DV), const),
        pl.BlockSpec((rows, rows), const),
        pl.BlockSpec((rows, rows), const),
    ]
    args = [p_main] * 8 + [p_tail, conv_w, pa, pb, norm_w, tri, ones]
    extra_specs, extra_args, aliases = _alias_prev(prev, len(args), 1)
    kern = functools.partial(_gdn_prompt_kernel, rows=rows)
    if prev is not None:
        kern = _drop_ref(kern, len(args))
    return pl.pallas_call(
        kern,
        grid=(bsz, nblk),
        in_specs=in_specs + extra_specs,
        out_specs=[
            pl.BlockSpec((rows, GDN_V_W), lambda b, i: (b * nblk + i, 0)),
            pl.BlockSpec((None, 1, hh, GDN_DK, GDN_DV), lambda b, i: (layer, b, 0, 0, 0)),
        ],
        out_shape=[
            jax.ShapeDtypeStruct((bsz * t, GDN_V_W), BF16),
            jax.ShapeDtypeStruct((n_layers, bsz, hh, GDN_DK, GDN_DV), F32),
        ],
        scratch_shapes=[
            pltpu.VMEM((hh, GDN_DK, GDN_DV), F32),
            pltpu.VMEM((6, SUBLANES, hw), F32),
            pltpu.VMEM((rows + SUBLANES, hw), F32),
            pltpu.VMEM((rows, GDN_QK_W), F32),
            pltpu.VMEM((rows, GDN_QK_W), F32),
            pltpu.VMEM((rows, GDN_V_W), F32),
        ],
        input_output_aliases=aliases,
        compiler_params=_params(("parallel", "arbitrary")),
        name="gdn_prompt",
    )(*args, *extra_args)


def _ssd_prompt_kernel(z_ref, xa_ref, xb_ref, xc3_ref, tail_ref, cw_ref, cb_ref, pa_ref, pb_ref, de_ref, nw_ref,
                       ex_ref, tri_ref,
                       y_ref, hout_ref,
                       ht_ref, carry_ref, ext_ref, xc_ref, cumt_ref, dtt_ref):
    i = pl.program_id(1)
    rows = CHUNK

    @pl.when(i == 0)
    def _():
        ht_ref[...] = jnp.zeros_like(ht_ref)
        carry_ref[...] = jnp.zeros_like(carry_ref)

    for sec, u_ref in enumerate((xa_ref, xb_ref, xc3_ref)):
        ss = slice(sec * SSD_SEC_W, (sec + 1) * SSD_SEC_W)
        xc_ref[:, ss] = _conv_section(ext_ref, carry_ref, sec, u_ref, cw_ref[:, ss], cb_ref[:, ss], rows)

    dt = _softplus(tail_ref[...] + pb_ref[...])
    da = dt * (-jnp.exp(pa_ref[...]))
    cum = _sel_dot(tri_ref[...], _split3(da))
    cumt_ref[...] = jnp.transpose(cum)
    dtt_ref[...] = jnp.transpose(dt)
    last = cum[rows - 1:rows, :]
    ecum_parts = _split3(jnp.exp(cum))
    wts_parts = _split3(jnp.exp(last - cum) * dt)

    ri = _iota2((rows, rows), 0)
    ci = _iota2((rows, rows), 1)
    tri = ri >= ci
    lane = _iota2((rows, LANES), 1)
    lo_half = lane < SSM_HEAD_DIM

    for g in range(SSM_GROUPS):
        gs = slice(g * SSM_GROUP_W, (g + 1) * SSM_GROUP_W)
        bmat = xc_ref[:, SSM_D_INNER + g * SSM_STATE:SSM_D_INNER + (g + 1) * SSM_STATE]
        cmat = xc_ref[:, SSM_D_INNER + SSM_BC_W + g * SSM_STATE:SSM_D_INNER + SSM_BC_W + (g + 1) * SSM_STATE]
        xg = xc_ref[:, gs]
        cb = _dot_nt(cmat, bmat)
        exg = ex_ref[:, gs]
        ecum_e = _dot_sel(ecum_parts, exg)
        wts = _dot_sel(wts_parts, exg)
        htg = ht_ref[g]
        y = _dot(cmat, htg) * ecum_e
        pieces = []
        for pr in range(SSM_REP // 2):
            scs = []
            for half in range(2):
                h = g * SSM_REP + 2 * pr + half
                ccol = cum[:, h:h + 1]
                crow = cumt_ref[h:h + 1, :]
                drow = dtt_ref[h:h + 1, :]
                lmat = jnp.exp(jnp.where(tri, ccol - crow, NEG_INF))
                scs.append((cb * lmat * drow).astype(BF16))
            xp = xg[:, pr * LANES:(pr + 1) * LANES]
            rhs = jnp.concatenate([jnp.where(lo_half, xp, 0.0), jnp.where(lo_half, 0.0, xp)], axis=0)
            pieces.append(jnp.dot(jnp.concatenate(scs, axis=1), rhs.astype(BF16),
                                  preferred_element_type=F32))
        y = y + jnp.concatenate(pieces, axis=1) + de_ref[:, gs] * xg
        ht_ref[g] = htg * ecum_e[rows - 1:rows, :] + _dot(jnp.transpose(bmat), xg * wts)
        zg = z_ref[:, gs]
        yz = y * _silu(zg)
        y_ref[:, gs] = _rms_rows(yz, nw_ref[:, gs]).astype(y_ref.dtype)

    @pl.when(i == pl.num_programs(1) - 1)
    def _():
        for g in range(SSM_GROUPS):
            hg = jnp.transpose(ht_ref[g])
            hout_ref[0, g * SSM_REP:(g + 1) * SSM_REP] = hg.reshape(SSM_REP, SSM_HEAD_DIM, SSM_STATE)


def _head_expand():
    h = jnp.arange(LANES)[:, None]
    c = jnp.arange(SSM_D_INNER)[None, :]
    return ((c // SSM_HEAD_DIM) == h).astype(BF16)


def _ssd_prompt(p_main, p_tail, conv_w, conv_b, pa, pb, d_e, norm_w, ex, layer, n_layers, prev, bsz, t):
    rows = CHUNK
    nblk = t // rows
    tri, _ = _chunk_masks(rows)
    const = lambda b, i: (0, 0)
    row = lambda b, i: (b * nblk + i, 0)
    args = [p_main, p_main, p_main, p_main, p_tail, conv_w, conv_b, pa, pb, d_e, norm_w, ex, tri]
    extra_specs, extra_args, aliases = _alias_prev(prev, len(args), 1)
    kern = _ssd_prompt_kernel if prev is None else _drop_ref(_ssd_prompt_kernel, len(args))
    return pl.pallas_call(
        kern,
        grid=(bsz, nblk),
        input_output_aliases=aliases,
        in_specs=[
            pl.BlockSpec((rows, SSM_D_INNER), row),
            pl.BlockSpec((rows, SSD_SEC_W), lambda b, i: (b * nblk + i, OD_XBC // SSD_SEC_W)),
            pl.BlockSpec((rows, SSD_SEC_W), lambda b, i: (b * nblk + i, OD_XBC // SSD_SEC_W + 1)),
            pl.BlockSpec((rows, SSD_SEC_W), lambda b, i: (b * nblk + i, OD_XBC // SSD_SEC_W + 2)),
            pl.BlockSpec((rows, LANES), row),
            pl.BlockSpec((CONV_W, SSM_CONV_CH), const),
            pl.BlockSpec((1, SSM_CONV_CH), const),
            pl.BlockSpec((1, LANES), const),
            pl.BlockSpec((1, LANES), const),
            pl.BlockSpec((1, SSM_D_INNER), const),
            pl.BlockSpec((1, SSM_D_INNER), const),
            pl.BlockSpec((LANES, SSM_D_INNER), const),
            pl.BlockSpec((rows, rows), const),
        ] + extra_specs,
        out_specs=[
            pl.BlockSpec((rows, SSM_D_INNER), row),
            pl.BlockSpec((None, 1, SSM_HEADS, SSM_HEAD_DIM, SSM_STATE), lambda b, i: (layer, b, 0, 0, 0)),
        ],
        out_shape=[
            jax.ShapeDtypeStruct((bsz * t, SSM_D_INNER), BF16),
            jax.ShapeDtypeStruct((n_layers, bsz, SSM_HEADS, SSM_HEAD_DIM, SSM_STATE), F32),
        ],
        scratch_shapes=[
            pltpu.VMEM((SSM_GROUPS, SSM_STATE, SSM_GROUP_W), F32),
            pltpu.VMEM((3, SUBLANES, SSD_SEC_W), F32),
            pltpu.VMEM((rows + SUBLANES, SSD_SEC_W), F32),
            pltpu.VMEM((rows, SSM_CONV_CH), F32),
            pltpu.VMEM((LANES, rows), F32),
            pltpu.VMEM((LANES, rows), F32),
        ],
        compiler_params=_params(("parallel", "arbitrary")),
        name="ssd_prompt",
    )(*args, *extra_args)


def _swa_sample_group_kernel(sink_ref, q_ref, kn_ref, vn_ref, ck_ref, cv_ref, o_ref, *, bb):
    scale = SWA_HEAD_DIM ** -0.5
    rows, keys = bb * SWA_GROUP, bb * WINDOW
    ck = ck_ref[...].reshape(keys, SWA_KV_W).astype(BF16)
    cv = cv_ref[...].reshape(keys, SWA_KV_W).astype(BF16)
    kn = kn_ref[...].astype(BF16).astype(F32)
    vn = vn_ref[...].astype(BF16).astype(F32)
    ri = _iota2((rows, keys), 0)
    cj = _iota2((rows, keys), 1)
    valid = ((cj // WINDOW) == (ri // SWA_GROUP)) & ((cj % WINDOW) >= 1)
    for c in range(SWA_KV_HEADS):
        cs = slice(c * SWA_HEAD_DIM, (c + 1) * SWA_HEAD_DIM)
        qc = q_ref[c].astype(BF16)
        s = lax.dot_general(qc, ck[:, cs], (((1,), (1,)), ((), ())), preferred_element_type=F32) * scale
        s = jnp.where(valid, s, NEG_INF)
        sn = jnp.sum(qc.astype(F32) * kn[:, cs], axis=-1, keepdims=True) * scale
        sk = sink_ref[c]
        mx = jnp.maximum(jnp.maximum(jnp.max(s, axis=-1, keepdims=True), sn), sk)
        p = jnp.exp(s - mx)
        pn = jnp.exp(sn - mx)
        den = jnp.sum(p, axis=-1, keepdims=True) + pn + jnp.exp(sk - mx)
        o = jnp.dot(p.astype(BF16), cv[:, cs], preferred_element_type=F32) + pn * vn[:, cs]
        o_ref[c] = (o / den).astype(o_ref.dtype)


def _swa_sample_grouped(q, kn, vn, ck_all, cv_all, sinks, layer):
    nb = q.shape[0]
    g, kvh, hd, kvw = SWA_GROUP, SWA_KV_HEADS, SWA_HEAD_DIM, SWA_KV_W
    bb = SUBLANES if nb % SUBLANES == 0 else nb
    qg = jnp.transpose(q.reshape(nb, kvh, g, hd), (1, 0, 2, 3)).reshape(kvh, nb * g, hd)
    kn4 = jnp.repeat(kn, g, axis=0)
    vn4 = jnp.repeat(vn, g, axis=0)
    sk = jnp.tile(sinks.astype(F32).reshape(kvh, 1, g), (1, bb, 1)).reshape(kvh, bb * g, 1)
    cache = pl.BlockSpec((None, bb, WINDOW, kvw), lambda b: (layer, b, 0, 0))
    out = pl.pallas_call(
        functools.partial(_swa_sample_group_kernel, bb=bb),
        grid=(nb // bb,),
        in_specs=[
            pl.BlockSpec((kvh, bb * g, 1), lambda b: (0, 0, 0)),
            pl.BlockSpec((kvh, bb * g, hd), lambda b: (0, b, 0)),
            pl.BlockSpec((bb * g, kvw), lambda b: (b, 0)),
            pl.BlockSpec((bb * g, kvw), lambda b: (b, 0)),
            cache,
            cache,
        ],
        out_specs=pl.BlockSpec((kvh, bb * g, hd), lambda b: (0, b, 0)),
        out_shape=jax.ShapeDtypeStruct((kvh, nb * g, hd), F32),
        compiler_params=_params(("parallel",)),
        name="swa_sample",
    )(sk, qg, kn4, vn4, ck_all, cv_all)
    return jnp.transpose(out.reshape(kvh, nb, g, hd), (1, 0, 2, 3)).reshape(nb, kvh * g * hd)


def _gdn_sample_kernel(u_ref, prev_ref, w_ref, s_ref, z_ref, ab_ref, pa_ref, pb_ref, nw_ref,
                       o_ref, sout_ref, cnew_ref):
    for b in range(u_ref.shape[0]):
        _gdn_sample_row(u_ref, prev_ref, w_ref, s_ref, z_ref, ab_ref, pa_ref, pb_ref, nw_ref,
                        o_ref, sout_ref, cnew_ref, b)


def _gdn_sample_row(u_ref, prev_ref, w_ref, s_ref, z_ref, ab_ref, pa_ref, pb_ref, nw_ref,
                    o_ref, sout_ref, cnew_ref, b):
    hh = GDN_HEADS
    u = u_ref[b]
    pv = prev_ref[b]
    w = w_ref[...]
    x = u * w[CONV_W - 1]
    for j in range(CONV_W - 1):
        x = x + pv[j] * w[j]
        cnew_ref[b, j] = pv[j + 1] if j + 1 < CONV_W - 1 else u
    x = _silu(x)
    q = x[0:hh]
    k = x[hh:2 * hh]
    v = x[2 * hh:3 * hh]
    q = q * lax.rsqrt(jnp.sum(q * q, axis=-1, keepdims=True) + EPS) * (GDN_DK ** -0.5)
    k = k * lax.rsqrt(jnp.sum(k * k, axis=-1, keepdims=True) + EPS)
    ab = ab_ref[b]
    dec = jnp.exp(-jnp.exp(pa_ref[...]) * _softplus(ab[0:hh] + pb_ref[...]))
    beta = _sigmoid(ab[hh:2 * hh])
    qk = jnp.concatenate([q, k, jnp.zeros((LANES - 2 * hh, GDN_DK), F32)], axis=0)
    qkt = jnp.transpose(qk)
    outs = []
    for h in range(hh):
        s = s_ref[b, h] * dec[h:h + 1, :]
        kcol = qkt[:, hh + h:hh + h + 1]
        ks = jnp.sum(s * kcol, axis=0, keepdims=True)
        delta = beta[h:h + 1, :] * (v[h:h + 1, :] - ks)
        s = s + kcol * delta
        sout_ref[b, h] = s
        outs.append(jnp.sum(s * qkt[:, h:h + 1], axis=0, keepdims=True))
    o = jnp.concatenate(outs, axis=0)
    o_ref[b] = _rms_rows(o, nw_ref[...]) * _silu(z_ref[b])


def _gdn_sample(u, prev, w, s_all, z, ab, pa, pb, nw, layer, prev_out):
    nb = u.shape[0]
    hh = GDN_HEADS
    n_layers = s_all.shape[0]
    bb = 4 if nb % 4 == 0 else 1
    b3 = lambda b: (b, 0, 0)
    b4 = lambda b: (b, 0, 0, 0)
    c2 = lambda b: (0, 0)
    state_spec = pl.BlockSpec((None, bb, hh, GDN_DK, GDN_DV), lambda b: (layer, b, 0, 0, 0))
    args = [u, prev, w, s_all, z, ab, pa, pb, nw]
    extra_specs, extra_args, aliases = _alias_prev(prev_out, len(args), 1)
    kern = _gdn_sample_kernel if prev_out is None else _drop_ref(_gdn_sample_kernel, len(args))
    return pl.pallas_call(
        kern,
        grid=(nb // bb,),
        input_output_aliases=aliases,
        in_specs=[
            pl.BlockSpec((bb, 3 * hh, GDN_DK), b3),
            pl.BlockSpec((bb, CONV_W - 1, 3 * hh, GDN_DK), b4),
            pl.BlockSpec((CONV_W, 3 * hh, GDN_DK), lambda b: (0, 0, 0)),
            state_spec,
            pl.BlockSpec((bb, hh, GDN_DV), b3),
            pl.BlockSpec((bb, 2 * hh, LANES), b3),
            pl.BlockSpec((hh, LANES), c2),
            pl.BlockSpec((hh, LANES), c2),
            pl.BlockSpec((1, GDN_DV), c2),
        ] + extra_specs,
        out_specs=[
            pl.BlockSpec((bb, hh, GDN_DV), b3),
            state_spec,
            pl.BlockSpec((bb, CONV_W - 1, 3 * hh, GDN_DK), b4),
        ],
        out_shape=[
            jax.ShapeDtypeStruct((nb, hh, GDN_DV), F32),
            jax.ShapeDtypeStruct((n_layers, nb, hh, GDN_DK, GDN_DV), F32),
            jax.ShapeDtypeStruct((nb, CONV_W - 1, 3 * hh, GDN_DK), F32),
        ],
        compiler_params=_params(("parallel",)),
        name="gdn_sample",
    )(*args, *extra_args)


def _ssd_prep_kernel(x_ref, prev_ref, cw_ref, cb_ref, tail_ref, pa_ref, pb_ref,
                     xc_ref, dt_ref, dec_ref, cnew_ref):
    u = x_ref[...]
    w = cw_ref[...]
    acc = u * w[CONV_W - 1:CONV_W, :] + cb_ref[...]
    for j in range(CONV_W - 1):
        acc = acc + prev_ref[j] * w[j:j + 1, :]
        cnew_ref[j] = prev_ref[j + 1] if j + 1 < CONV_W - 1 else u
    xc_ref[...] = _silu(acc)
    dt = _softplus(tail_ref[...] + pb_ref[...])
    dt_ref[...] = dt
    dec_ref[...] = jnp.exp(dt * (-jnp.exp(pa_ref[...])))


def _ssd_prep_sample(p_main, prev_t, cw, cb, tail, pa, pb):
    nb = p_main.shape[0]
    sec = SSD_SEC_W
    nsec = SSM_CONV_CH // sec
    c2 = lambda s: (0, 0)
    return pl.pallas_call(
        _ssd_prep_kernel,
        grid=(nsec,),
        in_specs=[
            pl.BlockSpec((nb, sec), lambda s: (0, OD_XBC // sec + s)),
            pl.BlockSpec((CONV_W - 1, nb, sec), lambda s: (0, 0, s)),
            pl.BlockSpec((CONV_W, sec), lambda s: (0, s)),
            pl.BlockSpec((1, sec), lambda s: (0, s)),
            pl.BlockSpec((nb, LANES), c2),
            pl.BlockSpec((1, LANES), c2),
            pl.BlockSpec((1, LANES), c2),
        ],
        out_specs=[
            pl.BlockSpec((nb, sec), lambda s: (0, s)),
            pl.BlockSpec((nb, LANES), c2),
            pl.BlockSpec((nb, LANES), c2),
            pl.BlockSpec((CONV_W - 1, nb, sec), lambda s: (0, 0, s)),
        ],
        out_shape=[
            jax.ShapeDtypeStruct((nb, SSM_CONV_CH), F32),
            jax.ShapeDtypeStruct((nb, LANES), F32),
            jax.ShapeDtypeStruct((nb, LANES), F32),
            jax.ShapeDtypeStruct((CONV_W - 1, nb, SSM_CONV_CH), F32),
        ],
        compiler_params=_params(("arbitrary",)),
        name="ssd_prep_sample",
    )(p_main, prev_t, cw, cb, tail, pa, pb)


def _ssd_state_kernel(dec_ref, x_ref, dt_ref, b_ref, c_ref, z_ref, de_ref, nw_ref, ex_ref, h_ref,
                      y_ref, hout_ref, *, nb):
    g = pl.program_id(0)
    x = x_ref[...]
    dtx = x * _dot_sel(_split3(dt_ref[...]), ex_ref[...])
    zrows = LANES - nb
    xt = jnp.transpose(jnp.concatenate([dtx, jnp.zeros((zrows, SSM_GROUP_W), F32)], axis=0))
    xh, xl = _split2(xt)
    bpad = jnp.concatenate([b_ref[...], jnp.zeros((zrows, SSM_STATE), F32)], axis=0)
    cpad_t = jnp.transpose(jnp.concatenate([c_ref[...], jnp.zeros((zrows, SSM_STATE), F32)], axis=0))
    rowi = _iota2((LANES, SSM_STATE), 0)
    coli = _iota2((SSM_STATE, LANES), 1)
    yt = jnp.zeros((SSM_GROUP_W, LANES), F32)
    for b in range(nb):
        bh, bl = _split2(jnp.where(rowi == b, bpad, 0.0))
        upd = (jnp.dot(xh, bh, preferred_element_type=F32) + jnp.dot(xh, bl, preferred_element_type=F32)
               + jnp.dot(xl, bh, preferred_element_type=F32))
        parts = []
        for r in range(SSM_REP):
            hn = h_ref[b, r] * dec_ref[b, g * SSM_REP + r] + upd[r * SSM_HEAD_DIM:(r + 1) * SSM_HEAD_DIM]
            hout_ref[b, r] = hn
            parts.append(hn)
        hh_, hl_ = _split2(jnp.concatenate(parts, axis=0))
        ch, cl = _split2(jnp.where(coli == b, cpad_t, 0.0))
        yt = yt + (jnp.dot(hh_, ch, preferred_element_type=F32) + jnp.dot(hh_, cl, preferred_element_type=F32)
                   + jnp.dot(hl_, ch, preferred_element_type=F32))
    y = jnp.transpose(yt)[0:nb, :] + de_ref[...] * x
    y_ref[...] = _rms_rows(y * _silu(z_ref[...]), nw_ref[...]).astype(y_ref.dtype)


def _ssd_state_sample(dtv, dec, xc, p_main, d_e, nw, ex, h_all, layer, prev_out):
    nb = xc.shape[0]
    assert nb <= LANES
    gw = SSM_GROUP_W
    n_layers = h_all.shape[0]
    state_spec = pl.BlockSpec((None, nb, SSM_REP, SSM_HEAD_DIM, SSM_STATE), lambda g: (layer, 0, g, 0, 0))
    args = [dec, xc, dtv, xc, xc, p_main, d_e, nw, ex, h_all]
    extra_specs, extra_args, aliases = _alias_prev(prev_out, len(args), 1)
    kern = functools.partial(_ssd_state_kernel, nb=nb)
    if prev_out is not None:
        kern = _drop_ref(kern, len(args))
    return pl.pallas_call(
        kern,
        grid=(SSM_GROUPS,),
        input_output_aliases=aliases,
        in_specs=[
            pl.BlockSpec(memory_space=pltpu.SMEM),
            pl.BlockSpec((nb, gw), lambda g: (0, g)),
            pl.BlockSpec((nb, LANES), lambda g: (0, 0)),
            pl.BlockSpec((nb, SSM_STATE), lambda g: (0, SSM_D_INNER // SSM_STATE + g)),
            pl.BlockSpec((nb, SSM_STATE), lambda g: (0, (SSM_D_INNER + SSM_BC_W) // SSM_STATE + g)),
            pl.BlockSpec((nb, gw), lambda g: (0, g)),
            pl.BlockSpec((1, gw), lambda g: (0, g)),
            pl.BlockSpec((1, gw), lambda g: (0, g)),
            pl.BlockSpec((LANES, gw), lambda g: (0, g)),
            state_spec,
        ] + extra_specs,
        out_specs=[
            pl.BlockSpec((nb, gw), lambda g: (0, g)),
            state_spec,
        ],
        out_shape=[
            jax.ShapeDtypeStruct((nb, SSM_D_INNER), BF16),
            jax.ShapeDtypeStruct((n_layers, nb, SSM_HEADS, SSM_HEAD_DIM, SSM_STATE), F32),
        ],
        compiler_params=_params(("arbitrary",)),
        name="ssd_state_sample",
    )(*args, *extra_args)


def _lane_pad_row(v):
    return jnp.zeros((1, LANES), F32).at[0, :v.shape[0]].set(v.astype(F32))


def kernel(x_prompt, x_sample, cache_swa_k, cache_swa_v, state_gdn, state_gdn_conv, state_ssm, state_ssm_conv,
           norm_ffn1, norm_mix, norm_ffn2, norm_final, w_ffn_gate, w_ffn_up, w_ffn_down,
           w_in_even, w_out_even, attn_sinks, gdn_conv_w, gdn_A_log, gdn_dt_bias, gdn_norm_w,
           w_in_odd, w_out_odd, ssm_conv_w, ssm_conv_b, ssm_A_log, ssm_dt_bias, ssm_D, ssm_norm_w):
    bsz, t, d = x_prompt.shape
    nb = x_sample.shape[0]
    depth = norm_ffn1.shape[0]
    n_even, n_odd = (depth + 1) // 2, depth // 2
    mp = bsz * t
    assert x_sample.shape[1] == 1 and d == D_MODEL and t % (2 * CHUNK) == 0
    hh = GDN_HEADS

    w_even_t = jnp.swapaxes(w_in_even, 1, 2)
    w_odd_t = jnp.swapaxes(w_in_odd, 1, 2)
    ex = _head_expand()
    cache_k = cache_swa_k.reshape(n_even, nb, WINDOW, SWA_KV_W)
    cache_v = cache_swa_v.reshape(n_even, nb, WINDOW, SWA_KV_W)

    xp = x_prompt.reshape(mp, d)
    xs = x_sample.reshape(nb, d)
    gain_f = norm_final.reshape(1, d)
    ffn_tm = _tile(mp, FFN_TM)

    small = {k: [] for k in ("pk", "pv", "pgc", "psc", "sk", "sv", "sgc", "ssc")}
    p_gdn = p_ssm = s_gdn = s_ssm = None

    def ffn_pair(xp, xs, gain, l, which, final):
        gain = gain.reshape(1, d)
        yp, ys, wg, wu, wd = _ffn_first(xp, xs, gain, gain_f, w_ffn_gate, w_ffn_up, w_ffn_down, l, which,
                                        final_norm=final, tm=ffn_tm, tf=FFN_FIRST_TF)
        yp = _ffn_rest(xp, gain, gain_f, wg, wu, wd, yp, final_norm=final, tm=ffn_tm, tf=FFN_TF)
        return yp, ys

    for l in range(depth):
        xp, xs = ffn_pair(xp, xs, norm_ffn1[l], l, 0, False)
        gmix = norm_mix[l].reshape(1, d)
        if l % 2 == 0:
            e = l // 2
            ptm, otm = _tile(mp, PROJ_EVEN_TM), _tile(mp, OUT_EVEN_TM)
            pm, pt, sm, st, w_main, w_tail = _proj_first(xp, xs, gmix, w_even_t, e, EV_MAIN, tm=ptm, tn=FIRST_TN)
            pm, pt = _proj_rest(xp, gmix, w_main, w_tail, pm, pt, tm=ptm, tn=PROJ_EVEN_TN)
            pa = _lane_pad_row(gdn_A_log[e])
            pb = _lane_pad_row(gdn_dt_bias[e])
            nw = gdn_norm_w[e].reshape(1, GDN_DV).astype(F32)

            kn = sm[:, EV_K:EV_K + SWA_KV_W]
            vn = sm[:, EV_V:EV_V + SWA_KV_W]
            o_a_s = _swa_sample_grouped(sm[:, EV_Q:EV_Q + SWA_Q_W], kn, vn, cache_k, cache_v, attn_sinks[e], e)
            ab = jnp.broadcast_to(st[:, :2 * hh, None], (nb, 2 * hh, LANES))
            o_b_s, s_gdn, gc_s = _gdn_sample(
                sm[:, EV_GQKV:EV_GQKV + GDN_CONV_CH].reshape(nb, 3 * hh, GDN_DK),
                state_gdn_conv[e].reshape(nb, CONV_W - 1, 3 * hh, GDN_DK),
                gdn_conv_w[e].reshape(CONV_W, 3 * hh, GDN_DK),
                state_gdn, sm[:, EV_Z:EV_Z + GDN_V_W].reshape(nb, hh, GDN_DV), ab,
                jnp.broadcast_to(gdn_A_log[e].astype(F32)[:, None], (hh, LANES)),
                jnp.broadcast_to(gdn_dt_bias[e].astype(F32)[:, None], (hh, LANES)), nw, e, s_gdn)
            small["sk"].append(kn.reshape(nb, 1, SWA_KV_HEADS, SWA_HEAD_DIM))
            small["sv"].append(vn.reshape(nb, 1, SWA_KV_HEADS, SWA_HEAD_DIM))
            small["sgc"].append(gc_s.reshape(nb, CONV_W - 1, GDN_CONV_CH))
            outs_s = [o_a_s.reshape(nb, SWA_Q_W).astype(BF16), o_b_s.reshape(nb, GDN_V_W).astype(BF16)]

            o_a_p = _swa_prompt(pm, attn_sinks[e].astype(F32), bsz, t)
            o_b_p, p_gdn = _gdn_prompt(pm, pt, gdn_conv_w, pa, pb, nw, e, n_even, p_gdn, bsz, t, rows=2 * CHUNK)
            pm3 = pm.reshape(bsz, t, EV_MAIN)
            small["pk"].append(pm3[:, t - WINDOW:, EV_K:EV_K + SWA_KV_W].reshape(bsz, WINDOW, SWA_KV_HEADS, SWA_HEAD_DIM))
            small["pv"].append(pm3[:, t - WINDOW:, EV_V:EV_V + SWA_KV_W].reshape(bsz, WINDOW, SWA_KV_HEADS, SWA_HEAD_DIM))
            small["pgc"].append(pm3[:, t - (CONV_W - 1):, EV_GQKV:EV_GQKV + GDN_CONV_CH])
            yp, xs, w_out = _outproj_first(xp, xs, [o_a_p, o_b_p], outs_s, w_out_even, e, tm=otm, tn=FIRST_TN)
            xp = _outproj_rest(xp, [o_a_p, o_b_p], w_out, yp, tm=otm, tn=OUT_EVEN_TN)
        else:
            o = l // 2
            ptm, otm = _tile(mp, PROJ_ODD_TM), _tile(mp, OUT_ODD_TM)
            pm, pt, sm, st, w_main, w_tail = _proj_first(xp, xs, gmix, w_odd_t, o, OD_MAIN, tm=ptm, tn=FIRST_TN)
            pm, pt = _proj_rest(xp, gmix, w_main, w_tail, pm, pt, tm=ptm, tn=PROJ_ODD_TN)
            pa = _lane_pad_row(ssm_A_log[o])
            pb = _lane_pad_row(ssm_dt_bias[o])
            d_e = jnp.repeat(ssm_D[o].astype(F32), SSM_HEAD_DIM).reshape(1, SSM_D_INNER)
            nw = ssm_norm_w[o].reshape(1, SSM_D_INNER).astype(F32)
            cb = ssm_conv_b[o].reshape(1, SSM_CONV_CH)

            xc, dtv, dec, cnew = _ssd_prep_sample(sm, jnp.swapaxes(state_ssm_conv[o], 0, 1), ssm_conv_w[o], cb,
                                                  st, pa, pb)
            y_s, s_ssm = _ssd_state_sample(dtv, dec, xc, sm, d_e, nw, ex, state_ssm, o, s_ssm)
            small["ssc"].append(jnp.swapaxes(cnew, 0, 1))

            y_p, p_ssm = _ssd_prompt(pm, pt, ssm_conv_w[o], cb, pa, pb, d_e, nw, ex, o, n_odd, p_ssm, bsz, t)
            pm3 = pm.reshape(bsz, t, OD_MAIN)
            small["psc"].append(pm3[:, t - (CONV_W - 1):, OD_XBC:OD_XBC + SSM_CONV_CH])
            yp, xs, w_out = _outproj_first(xp, xs, [y_p], [y_s], w_out_odd, o, tm=otm, tn=FIRST_TN)
            xp = _outproj_rest(xp, [y_p], w_out, yp, tm=otm, tn=OUT_ODD_TN)
        xp, xs = ffn_pair(xp, xs, norm_ffn2[l], l, 1, l == depth - 1)

    st_ = {k: jnp.stack(v) for k, v in small.items()}
    return (xp.reshape(bsz, t, d), xs.reshape(nb, 1, d),
            st_["pk"], st_["pv"], p_gdn, st_["pgc"], p_ssm, st_["psc"],
            st_["sk"], st_["sv"], s_gdn, st_["sgc"], s_ssm, st_["ssc"])
```

```python
import functools

import jax
import jax.numpy as jnp
from jax import lax
from jax.experimental import pallas as pl
from jax.experimental.pallas import tpu as pltpu

F32 = jnp.float32
BF16 = jnp.bfloat16

D_MODEL = 2048
SWA_HEADS = 16
SWA_KV_HEADS = 4
SWA_GROUP = SWA_HEADS // SWA_KV_HEADS
SWA_HEAD_DIM = 64
WINDOW = 128
GDN_HEADS = 8
GDN_DK = 128
GDN_DV = 128
CONV_W = 4
SSM_D_INNER = 2 * D_MODEL
SSM_HEAD_DIM = 64
SSM_HEADS = SSM_D_INNER // SSM_HEAD_DIM
SSM_GROUPS = 8
SSM_REP = SSM_HEADS // SSM_GROUPS
SSM_STATE = 128
EPS = 1e-6
NEG_INF = -1e30

SWA_Q_W = SWA_HEADS * SWA_HEAD_DIM
SWA_KV_W = SWA_KV_HEADS * SWA_HEAD_DIM
GDN_QK_W = GDN_HEADS * GDN_DK
GDN_V_W = GDN_HEADS * GDN_DV
GDN_CONV_CH = 2 * GDN_QK_W + GDN_V_W
GDN_HALF_W = GDN_QK_W // 2
SSM_BC_W = SSM_GROUPS * SSM_STATE
SSM_CONV_CH = SSM_D_INNER + 2 * SSM_BC_W
SSM_GROUP_W = SSM_REP * SSM_HEAD_DIM
SSD_SEC_W = SSM_CONV_CH // 3

EV_Q = 0
EV_K = EV_Q + SWA_Q_W
EV_V = EV_K + SWA_KV_W
EV_GQKV = EV_V + SWA_KV_W
EV_Z = EV_GQKV + GDN_CONV_CH
EV_MAIN = EV_Z + GDN_V_W
OD_XBC = SSM_D_INNER
OD_MAIN = SSM_D_INNER + SSM_CONV_CH

LANES = 128
SUBLANES = 8
CHUNK = 128
VMEM_LIMIT = 56 * 1024 * 1024

FFN_TM, FFN_TF = 512, 512
PROJ_EVEN_TM, PROJ_EVEN_TN = 1024, EV_MAIN // 4
PROJ_ODD_TM, PROJ_ODD_TN = 1024, 1024
OUT_EVEN_TM, OUT_EVEN_TN = 512, D_MODEL
OUT_ODD_TM, OUT_ODD_TN = 1024, 1024
FFN_FIRST_TF, FIRST_TN = 512, 512
FFN_FIRST_VMEM_LIMIT = 62 * 1024 * 1024


def _params(sem):
    return pltpu.CompilerParams(dimension_semantics=sem, vmem_limit_bytes=VMEM_LIMIT)


def _tile(m, pref):
    return pref if m % pref == 0 else m


def _sigmoid(x):
    return 1.0 / (1.0 + jnp.exp(-x))


def _silu(x):
    return x * _sigmoid(x)


def _softplus(x):
    return jnp.maximum(x, 0.0) + jnp.log(1.0 + jnp.exp(-jnp.abs(x)))


def _dot(a, b):
    return jnp.dot(a.astype(BF16), b.astype(BF16), preferred_element_type=F32)


def _dot_nt(a, b):
    return lax.dot_general(a.astype(BF16), b.astype(BF16), (((1,), (1,)), ((), ())),
                           preferred_element_type=F32)


def _bdot(a, b):
    return jnp.einsum("bij,bjk->bik", a.astype(BF16), b.astype(BF16), preferred_element_type=F32)


def _bdot_nt(a, b):
    return jnp.einsum("bid,bjd->bij", a.astype(BF16), b.astype(BF16), preferred_element_type=F32)


def _split2(a):
    hi = a.astype(BF16)
    lo = (a - hi.astype(F32)).astype(BF16)
    return hi, lo


def _split3(a):
    hi = a.astype(BF16)
    r = a - hi.astype(F32)
    mid = r.astype(BF16)
    lo = (r - mid.astype(F32)).astype(BF16)
    return hi, mid, lo


def _sel_dot(m01, parts):
    out = jnp.dot(m01, parts[0], preferred_element_type=F32)
    for p in parts[1:]:
        out = out + jnp.dot(m01, p, preferred_element_type=F32)
    return out


def _dot_sel(parts, m01):
    out = jnp.dot(parts[0], m01, preferred_element_type=F32)
    for p in parts[1:]:
        out = out + jnp.dot(p, m01, preferred_element_type=F32)
    return out


def _rms_rows(x, gain):
    ms = jnp.mean(x * x, axis=-1, keepdims=True)
    return x * lax.rsqrt(ms + EPS) * gain


def _iota2(shape, dim):
    return lax.broadcasted_iota(jnp.int32, shape, dim)


def _drop_ref(kern, pos):
    def entry(*refs):
        return kern(*refs[:pos], *refs[pos + 1:])
    return entry


def _alias_prev(prev, n_in, out_idx):
    if prev is None:
        return [], [], {}
    return [pl.BlockSpec(memory_space=pl.ANY)], [prev], {n_in: out_idx}


def _ffn_rows(h, wg, wu, wd):
    a = jnp.dot(h, wg, preferred_element_type=F32)
    b = jnp.dot(h, wu, preferred_element_type=F32)
    return jnp.dot((_silu(a) * b).astype(BF16), wd, preferred_element_type=F32)


def _ffn_step(j, nsteps, streams, g_ref, gf_ref, weights, final_norm):
    def finish(x_ref, acc):
        y = x_ref[...] + 0.5 * acc
        return _rms_rows(y, gf_ref[...]) if final_norm else y

    @pl.when(j == 0)
    def _():
        w = weights()
        for x_ref, h_ref, o_ref in streams:
            h = _rms_rows(x_ref[...], g_ref[...]).astype(BF16)
            h_ref[...] = h
            o_ref[...] = _ffn_rows(h, *w)

    @pl.when((j > 0) & (j < nsteps - 1))
    def _():
        w = weights()
        for _, h_ref, o_ref in streams:
            o_ref[...] += _ffn_rows(h_ref[...], *w)

    @pl.when(j == nsteps - 1)
    def _():
        w = weights()
        for x_ref, h_ref, o_ref in streams:
            o_ref[...] = finish(x_ref, o_ref[...] + _ffn_rows(h_ref[...], *w))


def _ffn_first_kernel(xp_ref, xs_ref, g_ref, gf_ref, wg_ref, wu_ref, wd_ref,
                      op_ref, os_ref, wgb_ref, wub_ref, wdb_ref, hp_ref, hs_ref, *, final_norm):
    def weights():
        wg, wu, wd = wg_ref[...].astype(BF16), wu_ref[...].astype(BF16), wd_ref[...].astype(BF16)
        wgb_ref[...] = wg
        wub_ref[...] = wu
        wdb_ref[...] = wd
        return wg, wu, wd

    streams = ((xp_ref, hp_ref, op_ref), (xs_ref, hs_ref, os_ref))
    _ffn_step(pl.program_id(0), pl.num_programs(0), streams, g_ref, gf_ref, weights, final_norm)


def _ffn_first(xp, xs, gain, gain_final, wg, wu, wd, layer, which, *, final_norm, tm, tf):
    mp, d = xp.shape
    nb = xs.shape[0]
    f = wg.shape[-1]
    c2 = lambda j: (0, 0)
    return pl.pallas_call(
        functools.partial(_ffn_first_kernel, final_norm=final_norm),
        grid=(f // tf,),
        in_specs=[
            pl.BlockSpec((tm, d), c2, pipeline_mode=pl.Buffered(1)),
            pl.BlockSpec((nb, d), c2),
            pl.BlockSpec((1, d), c2),
            pl.BlockSpec((1, d), c2),
            pl.BlockSpec((None, None, d, tf), lambda j: (layer, which, 0, j)),
            pl.BlockSpec((None, None, d, tf), lambda j: (layer, which, 0, j)),
            pl.BlockSpec((None, None, tf, d), lambda j: (layer, which, j, 0)),
        ],
        out_specs=[
            pl.BlockSpec((tm, d), c2),
            pl.BlockSpec((nb, d), c2),
            pl.BlockSpec((d, tf), lambda j: (0, j)),
            pl.BlockSpec((d, tf), lambda j: (0, j)),
            pl.BlockSpec((tf, d), lambda j: (j, 0)),
        ],
        out_shape=[
            jax.ShapeDtypeStruct((mp, d), F32), jax.ShapeDtypeStruct((nb, d), F32),
            jax.ShapeDtypeStruct((d, f), BF16), jax.ShapeDtypeStruct((d, f), BF16),
            jax.ShapeDtypeStruct((f, d), BF16),
        ],
        scratch_shapes=[pltpu.VMEM((tm, d), BF16), pltpu.VMEM((nb, d), BF16)],
        compiler_params=pltpu.CompilerParams(dimension_semantics=("arbitrary",),
                                             vmem_limit_bytes=FFN_FIRST_VMEM_LIMIT),
        name="ffn_first",
    )(xp, xs, gain, gain_final, wg, wu, wd)


def _ffn_rest_kernel(x_ref, g_ref, gf_ref, wg_ref, wu_ref, wd_ref, prev_ref, o_ref, h_ref, *, final_norm):
    del prev_ref
    _ffn_step(pl.program_id(1), pl.num_programs(1), ((x_ref, h_ref, o_ref),), g_ref, gf_ref,
              lambda: (wg_ref[...], wu_ref[...], wd_ref[...]), final_norm)


def _ffn_rest(x, gain, gain_final, wg, wu, wd, prev, *, final_norm, tm, tf):
    m, d = x.shape
    f = wg.shape[-1]
    if m == tm:
        return prev
    return pl.pallas_call(
        functools.partial(_ffn_rest_kernel, final_norm=final_norm),
        grid=(m // tm - 1, f // tf),
        in_specs=[
            pl.BlockSpec((tm, d), lambda i, j: (i + 1, 0)),
            pl.BlockSpec((1, d), lambda i, j: (0, 0)),
            pl.BlockSpec((1, d), lambda i, j: (0, 0)),
            pl.BlockSpec((d, tf), lambda i, j: (0, j)),
            pl.BlockSpec((d, tf), lambda i, j: (0, j)),
            pl.BlockSpec((tf, d), lambda i, j: (j, 0)),
            pl.BlockSpec(memory_space=pl.ANY),
        ],
        out_specs=pl.BlockSpec((tm, d), lambda i, j: (i + 1, 0)),
        out_shape=jax.ShapeDtypeStruct((m, d), F32),
        scratch_shapes=[pltpu.VMEM((tm, d), BF16)],
        input_output_aliases={6: 0},
        compiler_params=_params(("parallel", "arbitrary")),
        name="ffn_rest",
    )(x, gain, gain_final, wg, wu, wd, prev)


def _proj_first_kernel(xp_ref, xs_ref, g_ref, w_ref, wt_ref,
                       op_ref, otp_ref, os_ref, ots_ref, wb_ref, wtb_ref, hp_ref, hs_ref):
    j = pl.program_id(0)
    streams = ((xp_ref, hp_ref, op_ref, otp_ref), (xs_ref, hs_ref, os_ref, ots_ref))

    @pl.when(j == 0)
    def _():
        wt = wt_ref[...]
        pad = jnp.zeros((LANES - wt.shape[0], wt.shape[1]), F32)
        wt = jnp.transpose(jnp.concatenate([wt, pad], axis=0)).astype(BF16)
        wtb_ref[...] = wt
        for x_ref, h_ref, _, ot_ref in streams:
            h = _rms_rows(x_ref[...], g_ref[...]).astype(BF16)
            h_ref[...] = h
            ot_ref[...] = jnp.dot(h, wt, preferred_element_type=F32)

    w = jnp.transpose(w_ref[...].astype(BF16))
    wb_ref[...] = w
    for _, h_ref, o_ref, _ in streams:
        o_ref[...] = jnp.dot(h_ref[...], w, preferred_element_type=F32)


def _proj_first(xp, xs, gain, w_t, layer, n, *, tm, tn):
    mp, d = xp.shape
    nb = xs.shape[0]
    n_tail = w_t.shape[1] - n
    assert n % n_tail == 0 and n_tail % SUBLANES == 0 and n_tail <= LANES
    c2 = lambda j: (0, 0)
    return pl.pallas_call(
        _proj_first_kernel,
        grid=(n // tn,),
        in_specs=[
            pl.BlockSpec((tm, d), c2),
            pl.BlockSpec((nb, d), c2),
            pl.BlockSpec((1, d), c2),
            pl.BlockSpec((None, tn, d), lambda j: (layer, j, 0)),
            pl.BlockSpec((None, n_tail, d), lambda j: (layer, n // n_tail, 0)),
        ],
        out_specs=[
            pl.BlockSpec((tm, tn), lambda j: (0, j)),
            pl.BlockSpec((tm, LANES), c2),
            pl.BlockSpec((nb, tn), lambda j: (0, j)),
            pl.BlockSpec((nb, LANES), c2),
            pl.BlockSpec((d, tn), lambda j: (0, j)),
            pl.BlockSpec((d, LANES), c2),
        ],
        out_shape=[
            jax.ShapeDtypeStruct((mp, n), F32), jax.ShapeDtypeStruct((mp, LANES), F32),
            jax.ShapeDtypeStruct((nb, n), F32), jax.ShapeDtypeStruct((nb, LANES), F32),
            jax.ShapeDtypeStruct((d, n), BF16), jax.ShapeDtypeStruct((d, LANES), BF16),
        ],
        scratch_shapes=[pltpu.VMEM((tm, d), BF16), pltpu.VMEM((nb, d), BF16)],
        compiler_params=_params(("arbitrary",)),
        name="proj_first",
    )(xp, xs, gain, w_t, w_t)


def _proj_rest_kernel(x_ref, g_ref, w_ref, wt_ref, prev_ref, prevt_ref, o_ref, ot_ref, h_ref):
    del prev_ref, prevt_ref
    j = pl.program_id(1)

    @pl.when(j == 0)
    def _():
        h = _rms_rows(x_ref[...], g_ref[...]).astype(BF16)
        h_ref[...] = h
        ot_ref[...] = jnp.dot(h, wt_ref[...], preferred_element_type=F32)

    o_ref[...] = jnp.dot(h_ref[...], w_ref[...], preferred_element_type=F32)


def _proj_rest(x, gain, w_main, w_tail, prev_main, prev_tail, *, tm, tn):
    m, d = x.shape
    n = w_main.shape[1]
    if m == tm:
        return prev_main, prev_tail
    return pl.pallas_call(
        _proj_rest_kernel,
        grid=(m // tm - 1, n // tn),
        in_specs=[
            pl.BlockSpec((tm, d), lambda i, j: (i + 1, 0)),
            pl.BlockSpec((1, d), lambda i, j: (0, 0)),
            pl.BlockSpec((d, tn), lambda i, j: (0, j)),
            pl.BlockSpec((d, LANES), lambda i, j: (0, 0)),
            pl.BlockSpec(memory_space=pl.ANY),
            pl.BlockSpec(memory_space=pl.ANY),
        ],
        out_specs=[
            pl.BlockSpec((tm, tn), lambda i, j: (i + 1, j)),
            pl.BlockSpec((tm, LANES), lambda i, j: (i + 1, 0)),
        ],
        out_shape=[jax.ShapeDtypeStruct((m, n), F32), jax.ShapeDtypeStruct((m, LANES), F32)],
        scratch_shapes=[pltpu.VMEM((tm, d), BF16)],
        input_output_aliases={4: 0, 5: 1},
        compiler_params=_params(("parallel", "arbitrary")),
        name="proj_rest",
    )(x, gain, w_main, w_tail, prev_main, prev_tail)


def _outproj_first_kernel(*refs, n_in):
    xp_ref, xs_ref = refs[0], refs[1]
    op_refs = refs[2:2 + n_in]
    os_refs = refs[2 + n_in:2 + 2 * n_in]
    w_refs = refs[2 + 2 * n_in:2 + 3 * n_in]
    yp_ref, ys_ref, wb_ref = refs[2 + 3 * n_in:]
    accp = xp_ref[...]
    accs = xs_ref[...]
    for kb in range(n_in):
        w = w_refs[kb][...].astype(BF16)
        kw = w.shape[0]
        wb_ref[kb * kw:(kb + 1) * kw, :] = w
        accp = accp + jnp.dot(op_refs[kb][...], w, preferred_element_type=F32)
        accs = accs + jnp.dot(os_refs[kb][...], w, preferred_element_type=F32)
    yp_ref[...] = accp
    ys_ref[...] = accs


def _outproj_first(xp, xs, outs_p, outs_s, w_all, layer, *, tm, tn):
    mp, d = xp.shape
    nb = xs.shape[0]
    n_in = len(outs_p)
    kw = outs_p[0].shape[1]
    k_all = kw * n_in
    assert all(o.shape[1] == kw for o in outs_p + outs_s) and k_all == w_all.shape[1]
    in_specs = [pl.BlockSpec((tm, tn), lambda j: (0, j)), pl.BlockSpec((nb, tn), lambda j: (0, j))]
    in_specs += [pl.BlockSpec((tm, kw), lambda j: (0, 0)) for _ in range(n_in)]
    in_specs += [pl.BlockSpec((nb, kw), lambda j: (0, 0)) for _ in range(n_in)]
    in_specs += [pl.BlockSpec((None, kw, tn), lambda j, kb=kb: (layer, kb, j)) for kb in range(n_in)]
    return pl.pallas_call(
        functools.partial(_outproj_first_kernel, n_in=n_in),
        grid=(d // tn,),
        in_specs=in_specs,
        out_specs=[
            pl.BlockSpec((tm, tn), lambda j: (0, j)),
            pl.BlockSpec((nb, tn), lambda j: (0, j)),
            pl.BlockSpec((k_all, tn), lambda j: (0, j)),
        ],
        out_shape=[
            jax.ShapeDtypeStruct((mp, d), F32), jax.ShapeDtypeStruct((nb, d), F32),
            jax.ShapeDtypeStruct((k_all, d), BF16),
        ],
        compiler_params=_params(("arbitrary",)),
        name="outproj_first",
    )(xp, xs, *outs_p, *outs_s, *([w_all] * n_in))


def _outproj_rest_kernel(*refs, n_in):
    x_ref = refs[0]
    o_refs = refs[1:1 + n_in]
    w_refs = refs[1 + n_in:1 + 2 * n_in]
    out_ref = refs[2 + 2 * n_in]
    acc = x_ref[...]
    for o_ref, w_ref in zip(o_refs, w_refs):
        acc = acc + jnp.dot(o_ref[...], w_ref[...], preferred_element_type=F32)
    out_ref[...] = acc


def _outproj_rest(x, outs, w, prev, *, tm, tn):
    m, d = x.shape
    n_in = len(outs)
    kw = outs[0].shape[1]
    if m == tm:
        return prev
    in_specs = [pl.BlockSpec((tm, tn), lambda i, j: (i + 1, j))]
    in_specs += [pl.BlockSpec((tm, kw), lambda i, j: (i + 1, 0)) for _ in range(n_in)]
    in_specs += [pl.BlockSpec((kw, tn), lambda i, j, kb=kb: (kb, j)) for kb in range(n_in)]
    in_specs.append(pl.BlockSpec(memory_space=pl.ANY))
    return pl.pallas_call(
        functools.partial(_outproj_rest_kernel, n_in=n_in),
        grid=(m // tm - 1, d // tn),
        in_specs=in_specs,
        out_specs=pl.BlockSpec((tm, tn), lambda i, j: (i + 1, j)),
        out_shape=jax.ShapeDtypeStruct((m, d), F32),
        input_output_aliases={1 + 2 * n_in: 0},
        compiler_params=_params(("parallel", "arbitrary")),
        name="outproj_rest",
    )(x, *outs, *([w] * n_in), prev)


def _swa_prompt_kernel(sink_ref, q_ref, kc_ref, kp_ref, vc_ref, vp_ref, o_ref):
    n = pl.program_id(1)
    w = WINDOW
    qi = _iota2((w, 2 * w), 0)
    kj = _iota2((w, 2 * w), 1)
    valid = (kj > qi) & (kj <= qi + w) & ((n > 0) | (kj >= w))
    scale = SWA_HEAD_DIM ** -0.5
    kk = jnp.concatenate([kp_ref[...], kc_ref[...]], axis=0).astype(BF16)
    vv = jnp.concatenate([vp_ref[...], vc_ref[...]], axis=0).astype(BF16)
    q = q_ref[...].astype(BF16)
    outs = []
    for h in range(SWA_HEADS):
        c = h // SWA_GROUP
        qh = q[:, h * SWA_HEAD_DIM:(h + 1) * SWA_HEAD_DIM]
        kh = kk[:, c * SWA_HEAD_DIM:(c + 1) * SWA_HEAD_DIM]
        vh = vv[:, c * SWA_HEAD_DIM:(c + 1) * SWA_HEAD_DIM]
        s = lax.dot_general(qh, kh, (((1,), (1,)), ((), ())), preferred_element_type=F32) * scale
        s = jnp.where(valid, s, NEG_INF)
        sink = sink_ref[h]
        mx = jnp.maximum(jnp.max(s, axis=-1, keepdims=True), sink)
        p = jnp.exp(s - mx)
        den = jnp.sum(p, axis=-1, keepdims=True) + jnp.exp(sink - mx)
        p = p / den
        outs.append(jnp.dot(p.astype(BF16), vh, preferred_element_type=F32))
    o_ref[...] = jnp.concatenate(outs, axis=-1).astype(o_ref.dtype)


def _swa_prompt(p_main, sinks, bsz, t):
    nb = t // WINDOW
    kblk = EV_K // SWA_KV_W
    vblk = EV_V // SWA_KV_W

    def cur(col):
        return lambda b, n: (b * nb + n, col)

    def prev(col):
        return lambda b, n: (b * nb + jnp.maximum(n - 1, 0), col)

    return pl.pallas_call(
        _swa_prompt_kernel,
        grid=(bsz, nb),
        in_specs=[
            pl.BlockSpec(memory_space=pltpu.SMEM),
            pl.BlockSpec((WINDOW, SWA_Q_W), cur(EV_Q // SWA_Q_W)),
            pl.BlockSpec((WINDOW, SWA_KV_W), cur(kblk)),
            pl.BlockSpec((WINDOW, SWA_KV_W), prev(kblk)),
            pl.BlockSpec((WINDOW, SWA_KV_W), cur(vblk)),
            pl.BlockSpec((WINDOW, SWA_KV_W), prev(vblk)),
        ],
        out_specs=pl.BlockSpec((WINDOW, SWA_Q_W), cur(0)),
        out_shape=jax.ShapeDtypeStruct((bsz * t, SWA_Q_W), BF16),
        compiler_params=_params(("parallel", "arbitrary")),
        name="swa_prompt",
    )(sinks, p_main, p_main, p_main, p_main, p_main)


def _unit_lower_inverse(a, ri, ci):
    a0 = jnp.where((ri >> 4) == (ci >> 4), a, 0.0)
    y = -a0
    p = _bdot(a0, a0)
    for step in range(3):
        y = y + p + _bdot(y, p)
        if step < 2:
            p = _bdot(p, p)
    for sh in (4, 5, 6):
        e = jnp.where(((ri >> (sh + 1)) == (ci >> (sh + 1))) & ((ri >> sh) != (ci >> sh)), a, 0.0)
        t = e + _bdot(y, e)
        y = y - t - _bdot(t, y)
    return y


def _conv_section(ext_ref, carry_ref, sec, u_ref, w, bias, rows):
    ext_ref[0:SUBLANES, :] = carry_ref[sec]
    ext_ref[SUBLANES:SUBLANES + rows, :] = u_ref[...]
    acc = ext_ref[SUBLANES:SUBLANES + rows, :] * w[CONV_W - 1:CONV_W, :]
    for j in range(CONV_W - 1):
        off = SUBLANES - (CONV_W - 1) + j
        acc = acc + ext_ref[off:off + rows, :] * w[j:j + 1, :]
    if bias is not None:
        acc = acc + bias
    carry_ref[sec] = u_ref[rows - SUBLANES:rows, :]
    return _silu(acc)


def _gdn_prompt_kernel(uq0, uq1, uk0, uk1, uv0, uv1, z0_ref, z1_ref, tail_ref, cw_ref, pa_ref, pb_ref, nw_ref,
                       tri_ref, ones_ref,
                       o_ref, sout_ref,
                       s_ref, carry_ref, ext_ref, cq_ref, ck_ref, cv_ref, *, rows):
    i = pl.program_id(1)
    nchunk = rows // CHUNK
    hh = GDN_HEADS
    hw = GDN_HALF_W

    @pl.when(i == 0)
    def _():
        s_ref[...] = jnp.zeros_like(s_ref)
        carry_ref[...] = jnp.zeros_like(carry_ref)

    for sec, (halves, c_ref) in enumerate((((uq0, uq1), cq_ref), ((uk0, uk1), ck_ref), ((uv0, uv1), cv_ref))):
        for half, u_ref in enumerate(halves):
            ws = slice(sec * GDN_QK_W + half * hw, sec * GDN_QK_W + (half + 1) * hw)
            c_ref[:, half * hw:(half + 1) * hw] = _conv_section(ext_ref, carry_ref, 2 * sec + half, u_ref,
                                                                 cw_ref[:, ws], None, rows)

    t = tail_ref[...]
    g_all = -jnp.exp(pa_ref[...]) * _softplus(t + pb_ref[...])
    beta_all = _sigmoid(t)
    g_parts = _split3(g_all)
    gcum_all = _sel_dot(tri_ref[...], g_parts)
    gtot_all = _sel_dot(ones_ref[...], g_parts)
    gcum_t = jnp.transpose(gcum_all)

    ri = _iota2((1, CHUNK, CHUNK), 1)
    ci = _iota2((1, CHUNK, CHUNK), 2)
    incl = ri >= ci
    strict = ri > ci

    def rs(c):
        return slice(c * CHUNK, (c + 1) * CHUNK)

    def hs(h):
        return slice(h * GDN_DK, (h + 1) * GDN_DK)

    def stack(fn):
        return jnp.stack([fn(c, h) for c in range(nchunk) for h in range(hh)], axis=0)

    q = stack(lambda c, h: cq_ref[rs(c), hs(h)])
    k = stack(lambda c, h: ck_ref[rs(c), hs(h)])
    v = stack(lambda c, h: cv_ref[rs(c), hs(h)])
    gc = stack(lambda c, h: gcum_all[rs(c), h:h + 1])
    gt = stack(lambda c, h: gtot_all[rs(c), h:h + 1])
    bc = stack(lambda c, h: beta_all[rs(c), hh + h:hh + h + 1])
    gr = stack(lambda c, h: gcum_t[h:h + 1, rs(c)])

    ones_dk = jnp.ones((GDN_DK, GDN_DK), BF16)

    def sumsq(x):
        x2 = (x * x).reshape(x.shape[0] * CHUNK, GDN_DK)
        return _dot(x2, ones_dk).reshape(x.shape)

    q = q * lax.rsqrt(sumsq(q) + EPS) * (GDN_DK ** -0.5)
    k = k * lax.rsqrt(sumsq(k) + EPS)
    decay = jnp.exp(jnp.where(incl, gc - gr, NEG_INF))
    kb = k * bc
    a = jnp.where(strict, _bdot_nt(kb, k) * decay, 0.0)
    y = _unit_lower_inverse(a, ri, ci)
    eg = jnp.exp(gc)
    rhs = jnp.concatenate([v * bc, kb * eg], axis=2)
    sol = rhs + _bdot(y, rhs)
    u = sol[:, :, :GDN_DV]
    w = sol[:, :, GDN_DV:]
    wq = jnp.concatenate([w, q * eg], axis=1)
    attn = _bdot_nt(q, k) * decay
    kdt = jnp.swapaxes(k * jnp.exp(gt - gc), 1, 2)
    egt = jnp.exp(gt)

    for c in range(nchunk):
        bs = slice(c * hh, (c + 1) * hh)
        s = s_ref[...]
        ws_qs = _bdot(wq[bs], s)
        v_new = u[bs] - ws_qs[:, :CHUNK]
        o = ws_qs[:, CHUNK:] + _bdot(attn[bs], v_new)
        s_ref[...] = s * egt[bs] + _bdot(kdt[bs], v_new)
        for h in range(hh):
            z_ref = z0_ref if h < hh // 2 else z1_ref
            zc = z_ref[rs(c), hs(h % (hh // 2))]
            og = _rms_rows(o[h], nw_ref[...]) * _silu(zc)
            o_ref[rs(c), hs(h)] = og.astype(o_ref.dtype)

    @pl.when(i == pl.num_programs(1) - 1)
    def _():
        sout_ref[0] = s_ref[...]


def _chunk_masks(rows):
    r = jnp.arange(rows)
    same = (r[:, None] // CHUNK) == (r[None, :] // CHUNK)
    tri = (same & (r[:, None] >= r[None, :])).astype(BF16)
    ones = same.astype(BF16)
    return tri, ones


def _gdn_prompt(p_main, p_tail, conv_w, pa, pb, norm_w, layer, n_layers, prev, bsz, t, *, rows):
    nblk = t // rows
    hw = GDN_HALF_W
    hh = GDN_HEADS
    tri, ones = _chunk_masks(rows)

    def col(base):
        return lambda b, i: (b * nblk + i, base // hw)

    const = lambda b, i: (0, 0)
    in_specs = [pl.BlockSpec((rows, hw), col(EV_GQKV + k * hw)) for k in range(6)]
    in_specs += [pl.BlockSpec((rows, hw), col(EV_Z)), pl.BlockSpec((rows, hw), col(EV_Z + hw))]
    in_specs += [
        pl.BlockSpec((rows, LANES), lambda b, i: (b * nblk + i, 0)),
        pl.BlockSpec((None, CONV_W, GDN_CONV_CH), lambda b, i: (layer, 0, 0)),
        pl.BlockSpec((1, LANES), const),
        pl.BlockSpec((1, LANES), const),
        pl.BlockSpec((1, GDN_DV), const),
        pl.BlockSpec((rows, rows), const),
        pl.BlockSpec((rows, rows), const),
    ]
    args = [p_main] * 8 + [p_tail, conv_w, pa, pb, norm_w, tri, ones]
    extra_specs, extra_args, aliases = _alias_prev(prev, len(args), 1)
    kern = functools.partial(_gdn_prompt_kernel, rows=rows)
    if prev is not None:
        kern = _drop_ref(kern, len(args))
    return pl.pallas_call(
        kern,
        grid=(bsz, nblk),
        in_specs=in_specs + extra_specs,
        out_specs=[
            pl.BlockSpec((rows, GDN_V_W), lambda b, i: (b * nblk + i, 0)),
            pl.BlockSpec((None, 1, hh, GDN_DK, GDN_DV), lambda b, i: (layer, b, 0, 0, 0)),
        ],
        out_shape=[
            jax.ShapeDtypeStruct((bsz * t, GDN_V_W), BF16),
            jax.ShapeDtypeStruct((n_layers, bsz, hh, GDN_DK, GDN_DV), F32),
        ],
        scratch_shapes=[
            pltpu.VMEM((hh, GDN_DK, GDN_DV), F32),
            pltpu.VMEM((6, SUBLANES, hw), F32),
            pltpu.VMEM((rows + SUBLANES, hw), F32),
            pltpu.VMEM((rows, GDN_QK_W), F32),
            pltpu.VMEM((rows, GDN_QK_W), F32),
            pltpu.VMEM((rows, GDN_V_W), F32),
        ],
        input_output_aliases=aliases,
        compiler_params=_params(("parallel", "arbitrary")),
        name="gdn_prompt",
    )(*args, *extra_args)


def _ssd_prompt_kernel(z_ref, xa_ref, xb_ref, xc3_ref, tail_ref, cw_ref, cb_ref, pa_ref, pb_ref, de_ref, nw_ref,
                       ex_ref, tri_ref,
                       y_ref, hout_ref,
                       ht_ref, carry_ref, ext_ref, xc_ref, cumt_ref, dtt_ref):
    i = pl.program_id(1)
    rows = CHUNK

    @pl.when(i == 0)
    def _():
        ht_ref[...] = jnp.zeros_like(ht_ref)
        carry_ref[...] = jnp.zeros_like(carry_ref)

    for sec, u_ref in enumerate((xa_ref, xb_ref, xc3_ref)):
        ss = slice(sec * SSD_SEC_W, (sec + 1) * SSD_SEC_W)
        xc_ref[:, ss] = _conv_section(ext_ref, carry_ref, sec, u_ref, cw_ref[:, ss], cb_ref[:, ss], rows)

    dt = _softplus(tail_ref[...] + pb_ref[...])
    da = dt * (-jnp.exp(pa_ref[...]))
    cum = _sel_dot(tri_ref[...], _split3(da))
    cumt_ref[...] = jnp.transpose(cum)
    dtt_ref[...] = jnp.transpose(dt)
    last = cum[rows - 1:rows, :]
    ecum_parts = _split3(jnp.exp(cum))
    wts_parts = _split3(jnp.exp(last - cum) * dt)

    ri = _iota2((rows, rows), 0)
    ci = _iota2((rows, rows), 1)
    tri = ri >= ci
    lane = _iota2((rows, LANES), 1)
    lo_half = lane < SSM_HEAD_DIM

    for g in range(SSM_GROUPS):
        gs = slice(g * SSM_GROUP_W, (g + 1) * SSM_GROUP_W)
        bmat = xc_ref[:, SSM_D_INNER + g * SSM_STATE:SSM_D_INNER + (g + 1) * SSM_STATE]
        cmat = xc_ref[:, SSM_D_INNER + SSM_BC_W + g * SSM_STATE:SSM_D_INNER + SSM_BC_W + (g + 1) * SSM_STATE]
        xg = xc_ref[:, gs]
        cb = _dot_nt(cmat, bmat)
        exg = ex_ref[:, gs]
        ecum_e = _dot_sel(ecum_parts, exg)
        wts = _dot_sel(wts_parts, exg)
        htg = ht_ref[g]
        y = _dot(cmat, htg) * ecum_e
        pieces = []
        for pr in range(SSM_REP // 2):
            scs = []
            for half in range(2):
                h = g * SSM_REP + 2 * pr + half
                ccol = cum[:, h:h + 1]
                crow = cumt_ref[h:h + 1, :]
                drow = dtt_ref[h:h + 1, :]
                lmat = jnp.exp(jnp.where(tri, ccol - crow, NEG_INF))
                scs.append((cb * lmat * drow).astype(BF16))
            xp = xg[:, pr * LANES:(pr + 1) * LANES]
            rhs = jnp.concatenate([jnp.where(lo_half, xp, 0.0), jnp.where(lo_half, 0.0, xp)], axis=0)
            pieces.append(jnp.dot(jnp.concatenate(scs, axis=1), rhs.astype(BF16),
                                  preferred_element_type=F32))
        y = y + jnp.concatenate(pieces, axis=1) + de_ref[:, gs] * xg
        ht_ref[g] = htg * ecum_e[rows - 1:rows, :] + _dot(jnp.transpose(bmat), xg * wts)
        zg = z_ref[:, gs]
        yz = y * _silu(zg)
        y_ref[:, gs] = _rms_rows(yz, nw_ref[:, gs]).astype(y_ref.dtype)

    @pl.when(i == pl.num_programs(1) - 1)
    def _():
        for g in range(SSM_GROUPS):
            hg = jnp.transpose(ht_ref[g])
            hout_ref[0, g * SSM_REP:(g + 1) * SSM_REP] = hg.reshape(SSM_REP, SSM_HEAD_DIM, SSM_STATE)


def _head_expand():
    h = jnp.arange(LANES)[:, None]
    c = jnp.arange(SSM_D_INNER)[None, :]
    return ((c // SSM_HEAD_DIM) == h).astype(BF16)


def _ssd_prompt(p_main, p_tail, conv_w, conv_b, pa, pb, d_e, norm_w, ex, layer, n_layers, prev, bsz, t):
    rows = CHUNK
    nblk = t // rows
    tri, _ = _chunk_masks(rows)
    const = lambda b, i: (0, 0)
    row = lambda b, i: (b * nblk + i, 0)
    args = [p_main, p_main, p_main, p_main, p_tail, conv_w, conv_b, pa, pb, d_e, norm_w, ex, tri]
    extra_specs, extra_args, aliases = _alias_prev(prev, len(args), 1)
    kern = _ssd_prompt_kernel if prev is None else _drop_ref(_ssd_prompt_kernel, len(args))
    return pl.pallas_call(
        kern,
        grid=(bsz, nblk),
        input_output_aliases=aliases,
        in_specs=[
            pl.BlockSpec((rows, SSM_D_INNER), row),
            pl.BlockSpec((rows, SSD_SEC_W), lambda b, i: (b * nblk + i, OD_XBC // SSD_SEC_W)),
            pl.BlockSpec((rows, SSD_SEC_W), lambda b, i: (b * nblk + i, OD_XBC // SSD_SEC_W + 1)),
            pl.BlockSpec((rows, SSD_SEC_W), lambda b, i: (b * nblk + i, OD_XBC // SSD_SEC_W + 2)),
            pl.BlockSpec((rows, LANES), row),
            pl.BlockSpec((CONV_W, SSM_CONV_CH), const),
            pl.BlockSpec((1, SSM_CONV_CH), const),
            pl.BlockSpec((1, LANES), const),
            pl.BlockSpec((1, LANES), const),
            pl.BlockSpec((1, SSM_D_INNER), const),
            pl.BlockSpec((1, SSM_D_INNER), const),
            pl.BlockSpec((LANES, SSM_D_INNER), const),
            pl.BlockSpec((rows, rows), const),
        ] + extra_specs,
        out_specs=[
            pl.BlockSpec((rows, SSM_D_INNER), row),
            pl.BlockSpec((None, 1, SSM_HEADS, SSM_HEAD_DIM, SSM_STATE), lambda b, i: (layer, b, 0, 0, 0)),
        ],
        out_shape=[
            jax.ShapeDtypeStruct((bsz * t, SSM_D_INNER), BF16),
            jax.ShapeDtypeStruct((n_layers, bsz, SSM_HEADS, SSM_HEAD_DIM, SSM_STATE), F32),
        ],
        scratch_shapes=[
            pltpu.VMEM((SSM_GROUPS, SSM_STATE, SSM_GROUP_W), F32),
            pltpu.VMEM((3, SUBLANES, SSD_SEC_W), F32),
            pltpu.VMEM((rows + SUBLANES, SSD_SEC_W), F32),
            pltpu.VMEM((rows, SSM_CONV_CH), F32),
            pltpu.VMEM((LANES, rows), F32),
            pltpu.VMEM((LANES, rows), F32),
        ],
        compiler_params=_params(("parallel", "arbitrary")),
        name="ssd_prompt",
    )(*args, *extra_args)


def _swa_sample_group_kernel(sink_ref, q_ref, kn_ref, vn_ref, ck_ref, cv_ref, o_ref, *, bb):
    scale = SWA_HEAD_DIM ** -0.5
    rows, keys = bb * SWA_GROUP, bb * WINDOW
    ck = ck_ref[...].reshape(keys, SWA_KV_W).astype(BF16)
    cv = cv_ref[...].reshape(keys, SWA_KV_W).astype(BF16)
    kn = kn_ref[...].astype(BF16).astype(F32)
    vn = vn_ref[...].astype(BF16).astype(F32)
    ri = _iota2((rows, keys), 0)
    cj = _iota2((rows, keys), 1)
    valid = ((cj // WINDOW) == (ri // SWA_GROUP)) & ((cj % WINDOW) >= 1)
    for c in range(SWA_KV_HEADS):
        cs = slice(c * SWA_HEAD_DIM, (c + 1) * SWA_HEAD_DIM)
        qc = q_ref[c].astype(BF16)
        s = lax.dot_general(qc, ck[:, cs], (((1,), (1,)), ((), ())), preferred_element_type=F32) * scale
        s = jnp.where(valid, s, NEG_INF)
        sn = jnp.sum(qc.astype(F32) * kn[:, cs], axis=-1, keepdims=True) * scale
        sk = sink_ref[c]
        mx = jnp.maximum(jnp.maximum(jnp.max(s, axis=-1, keepdims=True), sn), sk)
        p = jnp.exp(s - mx)
        pn = jnp.exp(sn - mx)
        den = jnp.sum(p, axis=-1, keepdims=True) + pn + jnp.exp(sk - mx)
        o = jnp.dot(p.astype(BF16), cv[:, cs], preferred_element_type=F32) + pn * vn[:, cs]
        o_ref[c] = (o / den).astype(o_ref.dtype)


def _swa_sample_grouped(q, kn, vn, ck_all, cv_all, sinks, layer):
    nb = q.shape[0]
    g, kvh, hd, kvw = SWA_GROUP, SWA_KV_HEADS, SWA_HEAD_DIM, SWA_KV_W
    bb = SUBLANES if nb % SUBLANES == 0 else nb
    qg = jnp.transpose(q.reshape(nb, kvh, g, hd), (1, 0, 2, 3)).reshape(kvh, nb * g, hd)
    kn4 = jnp.repeat(kn, g, axis=0)
    vn4 = jnp.repeat(vn, g, axis=0)
    sk = jnp.tile(sinks.astype(F32).reshape(kvh, 1, g), (1, bb, 1)).reshape(kvh, bb * g, 1)
    cache = pl.BlockSpec((None, bb, WINDOW, kvw), lambda b: (layer, b, 0, 0))
    out = pl.pallas_call(
        functools.partial(_swa_sample_group_kernel, bb=bb),
        grid=(nb // bb,),
        in_specs=[
            pl.BlockSpec((kvh, bb * g, 1), lambda b: (0, 0, 0)),
            pl.BlockSpec((kvh, bb * g, hd), lambda b: (0, b, 0)),
            pl.BlockSpec((bb * g, kvw), lambda b: (b, 0)),
            pl.BlockSpec((bb * g, kvw), lambda b: (b, 0)),
            cache,
            cache,
        ],
        out_specs=pl.BlockSpec((kvh, bb * g, hd), lambda b: (0, b, 0)),
        out_shape=jax.ShapeDtypeStruct((kvh, nb * g, hd), F32),
        compiler_params=_params(("parallel",)),
        name="swa_sample",
    )(sk, qg, kn4, vn4, ck_all, cv_all)
    return jnp.transpose(out.reshape(kvh, nb, g, hd), (1, 0, 2, 3)).reshape(nb, kvh * g * hd)


def _gdn_sample_kernel(u_ref, prev_ref, w_ref, s_ref, z_ref, ab_ref, pa_ref, pb_ref, nw_ref,
                       o_ref, sout_ref, cnew_ref):
    for b in range(u_ref.shape[0]):
        _gdn_sample_row(u_ref, prev_ref, w_ref, s_ref, z_ref, ab_ref, pa_ref, pb_ref, nw_ref,
                        o_ref, sout_ref, cnew_ref, b)


def _gdn_sample_row(u_ref, prev_ref, w_ref, s_ref, z_ref, ab_ref, pa_ref, pb_ref, nw_ref,
                    o_ref, sout_ref, cnew_ref, b):
    hh = GDN_HEADS
    u = u_ref[b]
    pv = prev_ref[b]
    w = w_ref[...]
    x = u * w[CONV_W - 1]
    for j in range(CONV_W - 1):
        x = x + pv[j] * w[j]
        cnew_ref[b, j] = pv[j + 1] if j + 1 < CONV_W - 1 else u
    x = _silu(x)
    q = x[0:hh]
    k = x[hh:2 * hh]
    v = x[2 * hh:3 * hh]
    q = q * lax.rsqrt(jnp.sum(q * q, axis=-1, keepdims=True) + EPS) * (GDN_DK ** -0.5)
    k = k * lax.rsqrt(jnp.sum(k * k, axis=-1, keepdims=True) + EPS)
    ab = ab_ref[b]
    dec = jnp.exp(-jnp.exp(pa_ref[...]) * _softplus(ab[0:hh] + pb_ref[...]))
    beta = _sigmoid(ab[hh:2 * hh])
    qk = jnp.concatenate([q, k, jnp.zeros((LANES - 2 * hh, GDN_DK), F32)], axis=0)
    qkt = jnp.transpose(qk)
    outs = []
    for h in range(hh):
        s = s_ref[b, h] * dec[h:h + 1, :]
        kcol = qkt[:, hh + h:hh + h + 1]
        ks = jnp.sum(s * kcol, axis=0, keepdims=True)
        delta = beta[h:h + 1, :] * (v[h:h + 1, :] - ks)
        s = s + kcol * delta
        sout_ref[b, h] = s
        outs.append(jnp.sum(s * qkt[:, h:h + 1], axis=0, keepdims=True))
    o = jnp.concatenate(outs, axis=0)
    o_ref[b] = _rms_rows(o, nw_ref[...]) * _silu(z_ref[b])


def _gdn_sample(u, prev, w, s_all, z, ab, pa, pb, nw, layer, prev_out):
    nb = u.shape[0]
    hh = GDN_HEADS
    n_layers = s_all.shape[0]
    bb = 4 if nb % 4 == 0 else 1
    b3 = lambda b: (b, 0, 0)
    b4 = lambda b: (b, 0, 0, 0)
    c2 = lambda b: (0, 0)
    state_spec = pl.BlockSpec((None, bb, hh, GDN_DK, GDN_DV), lambda b: (layer, b, 0, 0, 0))
    args = [u, prev, w, s_all, z, ab, pa, pb, nw]
    extra_specs, extra_args, aliases = _alias_prev(prev_out, len(args), 1)
    kern = _gdn_sample_kernel if prev_out is None else _drop_ref(_gdn_sample_kernel, len(args))
    return pl.pallas_call(
        kern,
        grid=(nb // bb,),
        input_output_aliases=aliases,
        in_specs=[
            pl.BlockSpec((bb, 3 * hh, GDN_DK), b3),
            pl.BlockSpec((bb, CONV_W - 1, 3 * hh, GDN_DK), b4),
            pl.BlockSpec((CONV_W, 3 * hh, GDN_DK), lambda b: (0, 0, 0)),
            state_spec,
            pl.BlockSpec((bb, hh, GDN_DV), b3),
            pl.BlockSpec((bb, 2 * hh, LANES), b3),
            pl.BlockSpec((hh, LANES), c2),
            pl.BlockSpec((hh, LANES), c2),
            pl.BlockSpec((1, GDN_DV), c2),
        ] + extra_specs,
        out_specs=[
            pl.BlockSpec((bb, hh, GDN_DV), b3),
            state_spec,
            pl.BlockSpec((bb, CONV_W - 1, 3 * hh, GDN_DK), b4),
        ],
        out_shape=[
            jax.ShapeDtypeStruct((nb, hh, GDN_DV), F32),
            jax.ShapeDtypeStruct((n_layers, nb, hh, GDN_DK, GDN_DV), F32),
            jax.ShapeDtypeStruct((nb, CONV_W - 1, 3 * hh, GDN_DK), F32),
        ],
        compiler_params=_params(("parallel",)),
        name="gdn_sample",
    )(*args, *extra_args)


def _ssd_prep_kernel(x_ref, prev_ref, cw_ref, cb_ref, tail_ref, pa_ref, pb_ref,
                     xc_ref, dt_ref, dec_ref, cnew_ref):
    u = x_ref[...]
    w = cw_ref[...]
    acc = u * w[CONV_W - 1:CONV_W, :] + cb_ref[...]
    for j in range(CONV_W - 1):
        acc = acc + prev_ref[j] * w[j:j + 1, :]
        cnew_ref[j] = prev_ref[j + 1] if j + 1 < CONV_W - 1 else u
    xc_ref[...] = _silu(acc)
    dt = _softplus(tail_ref[...] + pb_ref[...])
    dt_ref[...] = dt
    dec_ref[...] = jnp.exp(dt * (-jnp.exp(pa_ref[...])))


def _ssd_prep_sample(p_main, prev_t, cw, cb, tail, pa, pb):
    nb = p_main.shape[0]
    sec = SSD_SEC_W
    nsec = SSM_CONV_CH // sec
    c2 = lambda s: (0, 0)
    return pl.pallas_call(
        _ssd_prep_kernel,
        grid=(nsec,),
        in_specs=[
            pl.BlockSpec((nb, sec), lambda s: (0, OD_XBC // sec + s)),
            pl.BlockSpec((CONV_W - 1, nb, sec), lambda s: (0, 0, s)),
            pl.BlockSpec((CONV_W, sec), lambda s: (0, s)),
            pl.BlockSpec((1, sec), lambda s: (0, s)),
            pl.BlockSpec((nb, LANES), c2),
            pl.BlockSpec((1, LANES), c2),
            pl.BlockSpec((1, LANES), c2),
        ],
        out_specs=[
            pl.BlockSpec((nb, sec), lambda s: (0, s)),
            pl.BlockSpec((nb, LANES), c2),
            pl.BlockSpec((nb, LANES), c2),
            pl.BlockSpec((CONV_W - 1, nb, sec), lambda s: (0, 0, s)),
        ],
        out_shape=[
            jax.ShapeDtypeStruct((nb, SSM_CONV_CH), F32),
            jax.ShapeDtypeStruct((nb, LANES), F32),
            jax.ShapeDtypeStruct((nb, LANES), F32),
            jax.ShapeDtypeStruct((CONV_W - 1, nb, SSM_CONV_CH), F32),
        ],
        compiler_params=_params(("arbitrary",)),
        name="ssd_prep_sample",
    )(p_main, prev_t, cw, cb, tail, pa, pb)


def _ssd_state_kernel(dec_ref, x_ref, dt_ref, b_ref, c_ref, z_ref, de_ref, nw_ref, ex_ref, h_ref,
                      y_ref, hout_ref, *, nb):
    g = pl.program_id(0)
    x = x_ref[...]
    dtx = x * _dot_sel(_split3(dt_ref[...]), ex_ref[...])
    zrows = LANES - nb
    xt = jnp.transpose(jnp.concatenate([dtx, jnp.zeros((zrows, SSM_GROUP_W), F32)], axis=0))
    xh, xl = _split2(xt)
    bpad = jnp.concatenate([b_ref[...], jnp.zeros((zrows, SSM_STATE), F32)], axis=0)
    cpad_t = jnp.transpose(jnp.concatenate([c_ref[...], jnp.zeros((zrows, SSM_STATE), F32)], axis=0))
    rowi = _iota2((LANES, SSM_STATE), 0)
    coli = _iota2((SSM_STATE, LANES), 1)
    yt = jnp.zeros((SSM_GROUP_W, LANES), F32)
    for b in range(nb):
        bh, bl = _split2(jnp.where(rowi == b, bpad, 0.0))
        upd = (jnp.dot(xh, bh, preferred_element_type=F32) + jnp.dot(xh, bl, preferred_element_type=F32)
               + jnp.dot(xl, bh, preferred_element_type=F32))
        parts = []
        for r in range(SSM_REP):
            hn = h_ref[b, r] * dec_ref[b, g * SSM_REP + r] + upd[r * SSM_HEAD_DIM:(r + 1) * SSM_HEAD_DIM]
            hout_ref[b, r] = hn
            parts.append(hn)
        hh_, hl_ = _split2(jnp.concatenate(parts, axis=0))
        ch, cl = _split2(jnp.where(coli == b, cpad_t, 0.0))
        yt = yt + (jnp.dot(hh_, ch, preferred_element_type=F32) + jnp.dot(hh_, cl, preferred_element_type=F32)
                   + jnp.dot(hl_, ch, preferred_element_type=F32))
    y = jnp.transpose(yt)[0:nb, :] + de_ref[...] * x
    y_ref[...] = _rms_rows(y * _silu(z_ref[...]), nw_ref[...]).astype(y_ref.dtype)


def _ssd_state_sample(dtv, dec, xc, p_main, d_e, nw, ex, h_all, layer, prev_out):
    nb = xc.shape[0]
    assert nb <= LANES
    gw = SSM_GROUP_W
    n_layers = h_all.shape[0]
    state_spec = pl.BlockSpec((None, nb, SSM_REP, SSM_HEAD_DIM, SSM_STATE), lambda g: (layer, 0, g, 0, 0))
    args = [dec, xc, dtv, xc, xc, p_main, d_e, nw, ex, h_all]
    extra_specs, extra_args, aliases = _alias_prev(prev_out, len(args), 1)
    kern = functools.partial(_ssd_state_kernel, nb=nb)
    if prev_out is not None:
        kern = _drop_ref(kern, len(args))
    return pl.pallas_call(
        kern,
        grid=(SSM_GROUPS,),
        input_output_aliases=aliases,
        in_specs=[
            pl.BlockSpec(memory_space=pltpu.SMEM),
            pl.BlockSpec((nb, gw), lambda g: (0, g)),
            pl.BlockSpec((nb, LANES), lambda g: (0, 0)),
            pl.BlockSpec((nb, SSM_STATE), lambda g: (0, SSM_D_INNER // SSM_STATE + g)),
            pl.BlockSpec((nb, SSM_STATE), lambda g: (0, (SSM_D_INNER + SSM_BC_W) // SSM_STATE + g)),
            pl.BlockSpec((nb, gw), lambda g: (0, g)),
            pl.BlockSpec((1, gw), lambda g: (0, g)),
            pl.BlockSpec((1, gw), lambda g: (0, g)),
            pl.BlockSpec((LANES, gw), lambda g: (0, g)),
            state_spec,
        ] + extra_specs,
        out_specs=[
            pl.BlockSpec((nb, gw), lambda g: (0, g)),
            state_spec,
        ],
        out_shape=[
            jax.ShapeDtypeStruct((nb, SSM_D_INNER), BF16),
            jax.ShapeDtypeStruct((n_layers, nb, SSM_HEADS, SSM_HEAD_DIM, SSM_STATE), F32),
        ],
        compiler_params=_params(("arbitrary",)),
        name="ssd_state_sample",
    )(*args, *extra_args)


def _lane_pad_row(v):
    return jnp.zeros((1, LANES), F32).at[0, :v.shape[0]].set(v.astype(F32))


def kernel(x_prompt, x_sample, cache_swa_k, cache_swa_v, state_gdn, state_gdn_conv, state_ssm, state_ssm_conv,
           norm_ffn1, norm_mix, norm_ffn2, norm_final, w_ffn_gate, w_ffn_up, w_ffn_down,
           w_in_even, w_out_even, attn_sinks, gdn_conv_w, gdn_A_log, gdn_dt_bias, gdn_norm_w,
           w_in_odd, w_out_odd, ssm_conv_w, ssm_conv_b, ssm_A_log, ssm_dt_bias, ssm_D, ssm_norm_w):
    bsz, t, d = x_prompt.shape
    nb = x_sample.shape[0]
    depth = norm_ffn1.shape[0]
    n_even, n_odd = (depth + 1) // 2, depth // 2
    mp = bsz * t
    assert x_sample.shape[1] == 1 and d == D_MODEL and t % (2 * CHUNK) == 0
    hh = GDN_HEADS

    w_even_t = jnp.swapaxes(w_in_even, 1, 2)
    w_odd_t = jnp.swapaxes(w_in_odd, 1, 2)
    ex = _head_expand()
    cache_k = cache_swa_k.reshape(n_even, nb, WINDOW, SWA_KV_W)
    cache_v = cache_swa_v.reshape(n_even, nb, WINDOW, SWA_KV_W)

    xp = x_prompt.reshape(mp, d)
    xs = x_sample.reshape(nb, d)
    gain_f = norm_final.reshape(1, d)
    ffn_tm = _tile(mp, FFN_TM)

    small = {k: [] for k in ("pk", "pv", "pgc", "psc", "sk", "sv", "sgc", "ssc")}
    p_gdn = p_ssm = s_gdn = s_ssm = None

    def ffn_pair(xp, xs, gain, l, which, final):
        gain = gain.reshape(1, d)
        yp, ys, wg, wu, wd = _ffn_first(xp, xs, gain, gain_f, w_ffn_gate, w_ffn_up, w_ffn_down, l, which,
                                        final_norm=final, tm=ffn_tm, tf=FFN_FIRST_TF)
        yp = _ffn_rest(xp, gain, gain_f, wg, wu, wd, yp, final_norm=final, tm=ffn_tm, tf=FFN_TF)
        return yp, ys

    for l in range(depth):
        xp, xs = ffn_pair(xp, xs, norm_ffn1[l], l, 0, False)
        gmix = norm_mix[l].reshape(1, d)
        if l % 2 == 0:
            e = l // 2
            ptm, otm = _tile(mp, PROJ_EVEN_TM), _tile(mp, OUT_EVEN_TM)
            pm, pt, sm, st, w_main, w_tail = _proj_first(xp, xs, gmix, w_even_t, e, EV_MAIN, tm=ptm, tn=FIRST_TN)
            pm, pt = _proj_rest(xp, gmix, w_main, w_tail, pm, pt, tm=ptm, tn=PROJ_EVEN_TN)
            pa = _lane_pad_row(gdn_A_log[e])
            pb = _lane_pad_row(gdn_dt_bias[e])
            nw = gdn_norm_w[e].reshape(1, GDN_DV).astype(F32)

            kn = sm[:, EV_K:EV_K + SWA_KV_W]
            vn = sm[:, EV_V:EV_V + SWA_KV_W]
            o_a_s = _swa_sample_grouped(sm[:, EV_Q:EV_Q + SWA_Q_W], kn, vn, cache_k, cache_v, attn_sinks[e], e)
            ab = jnp.broadcast_to(st[:, :2 * hh, None], (nb, 2 * hh, LANES))
            o_b_s, s_gdn, gc_s = _gdn_sample(
                sm[:, EV_GQKV:EV_GQKV + GDN_CONV_CH].reshape(nb, 3 * hh, GDN_DK),
                state_gdn_conv[e].reshape(nb, CONV_W - 1, 3 * hh, GDN_DK),
                gdn_conv_w[e].reshape(CONV_W, 3 * hh, GDN_DK),
                state_gdn, sm[:, EV_Z:EV_Z + GDN_V_W].reshape(nb, hh, GDN_DV), ab,
                jnp.broadcast_to(gdn_A_log[e].astype(F32)[:, None], (hh, LANES)),
                jnp.broadcast_to(gdn_dt_bias[e].astype(F32)[:, None], (hh, LANES)), nw, e, s_gdn)
            small["sk"].append(kn.reshape(nb, 1, SWA_KV_HEADS, SWA_HEAD_DIM))
            small["sv"].append(vn.reshape(nb, 1, SWA_KV_HEADS, SWA_HEAD_DIM))
            small["sgc"].append(gc_s.reshape(nb, CONV_W - 1, GDN_CONV_CH))
            outs_s = [o_a_s.reshape(nb, SWA_Q_W).astype(BF16), o_b_s.reshape(nb, GDN_V_W).astype(BF16)]

            o_a_p = _swa_prompt(pm, attn_sinks[e].astype(F32), bsz, t)
            o_b_p, p_gdn = _gdn_prompt(pm, pt, gdn_conv_w, pa, pb, nw, e, n_even, p_gdn, bsz, t, rows=2 * CHUNK)
            pm3 = pm.reshape(bsz, t, EV_MAIN)
            small["pk"].append(pm3[:, t - WINDOW:, EV_K:EV_K + SWA_KV_W].reshape(bsz, WINDOW, SWA_KV_HEADS, SWA_HEAD_DIM))
            small["pv"].append(pm3[:, t - WINDOW:, EV_V:EV_V + SWA_KV_W].reshape(bsz, WINDOW, SWA_KV_HEADS, SWA_HEAD_DIM))
            small["pgc"].append(pm3[:, t - (CONV_W - 1):, EV_GQKV:EV_GQKV + GDN_CONV_CH])
            yp, xs, w_out = _outproj_first(xp, xs, [o_a_p, o_b_p], outs_s, w_out_even, e, tm=otm, tn=FIRST_TN)
            xp = _outproj_rest(xp, [o_a_p, o_b_p], w_out, yp, tm=otm, tn=OUT_EVEN_TN)
        else:
            o = l // 2
            ptm, otm = _tile(mp, PROJ_ODD_TM), _tile(mp, OUT_ODD_TM)
            pm, pt, sm, st, w_main, w_tail = _proj_first(xp, xs, gmix, w_odd_t, o, OD_MAIN, tm=ptm, tn=FIRST_TN)
            pm, pt = _proj_rest(xp, gmix, w_main, w_tail, pm, pt, tm=ptm, tn=PROJ_ODD_TN)
            pa = _lane_pad_row(ssm_A_log[o])
            pb = _lane_pad_row(ssm_dt_bias[o])
            d_e = jnp.repeat(ssm_D[o].astype(F32), SSM_HEAD_DIM).reshape(1, SSM_D_INNER)
            nw = ssm_norm_w[o].reshape(1, SSM_D_INNER).astype(F32)
            cb = ssm_conv_b[o].reshape(1, SSM_CONV_CH)

            xc, dtv, dec, cnew = _ssd_prep_sample(sm, jnp.swapaxes(state_ssm_conv[o], 0, 1), ssm_conv_w[o], cb,
                                                  st, pa, pb)
            y_s, s_ssm = _ssd_state_sample(dtv, dec, xc, sm, d_e, nw, ex, state_ssm, o, s_ssm)
            small["ssc"].append(jnp.swapaxes(cnew, 0, 1))

            y_p, p_ssm = _ssd_prompt(pm, pt, ssm_conv_w[o], cb, pa, pb, d_e, nw, ex, o, n_odd, p_ssm, bsz, t)
            pm3 = pm.reshape(bsz, t, OD_MAIN)
            small["psc"].append(pm3[:, t - (CONV_W - 1):, OD_XBC:OD_XBC + SSM_CONV_CH])
            yp, xs, w_out = _outproj_first(xp, xs, [y_p], [y_s], w_out_odd, o, tm=otm, tn=FIRST_TN)
            xp = _outproj_rest(xp, [y_p], w_out, yp, tm=otm, tn=OUT_ODD_TN)
        xp, xs = ffn_pair(xp, xs, norm_ffn2[l], l, 1, l == depth - 1)

    st_ = {k: jnp.stack(v) for k, v in small.items()}
    return (xp.reshape(bsz, t, d), xs.reshape(nb, 1, d),
            st_["pk"], st_["pv"], p_gdn, st_["pgc"], p_ssm, st_["psc"],
            st_["sk"], st_["sv"], s_gdn, st_["sgc"], s_ssm, st_["ssc"])
```

```python
import functools

import jax
import jax.numpy as jnp
from jax import lax
from jax.experimental import pallas as pl
from jax.experimental.pallas import tpu as pltpu

F32 = jnp.float32
BF16 = jnp.bfloat16

D_MODEL = 2048
SWA_HEADS = 16
SWA_KV_HEADS = 4
SWA_GROUP = SWA_HEADS // SWA_KV_HEADS
SWA_HEAD_DIM = 64
WINDOW = 128
GDN_HEADS = 8
GDN_DK = 128
GDN_DV = 128
CONV_W = 4
SSM_D_INNER = 2 * D_MODEL
SSM_HEAD_DIM = 64
SSM_HEADS = SSM_D_INNER // SSM_HEAD_DIM
SSM_GROUPS = 8
SSM_REP = SSM_HEADS // SSM_GROUPS
SSM_STATE = 128
EPS = 1e-6
NEG_INF = -1e30

SWA_Q_W = SWA_HEADS * SWA_HEAD_DIM
SWA_KV_W = SWA_KV_HEADS * SWA_HEAD_DIM
GDN_QK_W = GDN_HEADS * GDN_DK
GDN_V_W = GDN_HEADS * GDN_DV
GDN_CONV_CH = 2 * GDN_QK_W + GDN_V_W
GDN_HALF_W = GDN_QK_W // 2
SSM_BC_W = SSM_GROUPS * SSM_STATE
SSM_CONV_CH = SSM_D_INNER + 2 * SSM_BC_W
SSM_GROUP_W = SSM_REP * SSM_HEAD_DIM
SSD_SEC_W = SSM_CONV_CH // 3

EV_Q = 0
EV_K = EV_Q + SWA_Q_W
EV_V = EV_K + SWA_KV_W
EV_GQKV = EV_V + SWA_KV_W
EV_Z = EV_GQKV + GDN_CONV_CH
EV_MAIN = EV_Z + GDN_V_W
OD_XBC = SSM_D_INNER
OD_MAIN = SSM_D_INNER + SSM_CONV_CH

LANES = 128
SUBLANES = 8
CHUNK = 128
VMEM_LIMIT = 56 * 1024 * 1024

FFN_TM, FFN_TF = 512, 512
PROJ_EVEN_TM, PROJ_EVEN_TN = 1024, EV_MAIN // 4
PROJ_ODD_TM, PROJ_ODD_TN = 1024, 1024
OUT_EVEN_TM, OUT_EVEN_TN = 512, D_MODEL
OUT_ODD_TM, OUT_ODD_TN = 1024, 1024
FFN_FIRST_TF, FIRST_TN = 256, 512
FFN_FIRST_TM = 1024


def _params(sem):
    return pltpu.CompilerParams(dimension_semantics=sem, vmem_limit_bytes=VMEM_LIMIT)


def _tile(m, pref):
    return pref if m % pref == 0 else m


def _sigmoid(x):
    return 1.0 / (1.0 + jnp.exp(-x))


def _silu(x):
    return x * _sigmoid(x)


def _softplus(x):
    return jnp.maximum(x, 0.0) + jnp.log(1.0 + jnp.exp(-jnp.abs(x)))


def _dot(a, b):
    return jnp.dot(a.astype(BF16), b.astype(BF16), preferred_element_type=F32)


def _dot_nt(a, b):
    return lax.dot_general(a.astype(BF16), b.astype(BF16), (((1,), (1,)), ((), ())),
                           preferred_element_type=F32)


def _bdot(a, b):
    return jnp.einsum("bij,bjk->bik", a.astype(BF16), b.astype(BF16), preferred_element_type=F32)


def _bdot_nt(a, b):
    return jnp.einsum("bid,bjd->bij", a.astype(BF16), b.astype(BF16), preferred_element_type=F32)


def _split2(a):
    hi = a.astype(BF16)
    lo = (a - hi.astype(F32)).astype(BF16)
    return hi, lo


def _split3(a):
    hi = a.astype(BF16)
    r = a - hi.astype(F32)
    mid = r.astype(BF16)
    lo = (r - mid.astype(F32)).astype(BF16)
    return hi, mid, lo


def _sel_dot(m01, parts):
    out = jnp.dot(m01, parts[0], preferred_element_type=F32)
    for p in parts[1:]:
        out = out + jnp.dot(m01, p, preferred_element_type=F32)
    return out


def _dot_sel(parts, m01):
    out = jnp.dot(parts[0], m01, preferred_element_type=F32)
    for p in parts[1:]:
        out = out + jnp.dot(p, m01, preferred_element_type=F32)
    return out


def _rms_rows(x, gain):
    ms = jnp.mean(x * x, axis=-1, keepdims=True)
    return x * lax.rsqrt(ms + EPS) * gain


def _iota2(shape, dim):
    return lax.broadcasted_iota(jnp.int32, shape, dim)


def _drop_ref(kern, pos):
    def entry(*refs):
        return kern(*refs[:pos], *refs[pos + 1:])
    return entry


def _alias_prev(prev, n_in, out_idx):
    if prev is None:
        return [], [], {}
    return [pl.BlockSpec(memory_space=pl.ANY)], [prev], {n_in: out_idx}


def _ffn_rows(h, wg, wu, wd):
    a = jnp.dot(h, wg, preferred_element_type=F32)
    b = jnp.dot(h, wu, preferred_element_type=F32)
    return jnp.dot((_silu(a) * b).astype(BF16), wd, preferred_element_type=F32)


def _ffn_step(j, nsteps, streams, g_ref, gf_ref, weights, final_norm):
    def finish(x_ref, acc):
        y = x_ref[...] + 0.5 * acc
        return _rms_rows(y, gf_ref[...]) if final_norm else y

    @pl.when(j == 0)
    def _():
        w = weights()
        for x_ref, h_ref, o_ref in streams:
            h = _rms_rows(x_ref[...], g_ref[...]).astype(BF16)
            h_ref[...] = h
            o_ref[...] = _ffn_rows(h, *w)

    @pl.when((j > 0) & (j < nsteps - 1))
    def _():
        w = weights()
        for _, h_ref, o_ref in streams:
            o_ref[...] += _ffn_rows(h_ref[...], *w)

    @pl.when(j == nsteps - 1)
    def _():
        w = weights()
        for x_ref, h_ref, o_ref in streams:
            o_ref[...] = finish(x_ref, o_ref[...] + _ffn_rows(h_ref[...], *w))


def _ffn_first_kernel(xp_ref, xs_ref, g_ref, gf_ref, wg_ref, wu_ref, wd_ref,
                      op_ref, os_ref, wgb_ref, wub_ref, wdb_ref, hp_ref, hs_ref, *, final_norm):
    def weights():
        wg, wu, wd = wg_ref[...].astype(BF16), wu_ref[...].astype(BF16), wd_ref[...].astype(BF16)
        wgb_ref[...] = wg
        wub_ref[...] = wu
        wdb_ref[...] = wd
        return wg, wu, wd

    streams = ((xp_ref, hp_ref, op_ref), (xs_ref, hs_ref, os_ref))
    _ffn_step(pl.program_id(0), pl.num_programs(0), streams, g_ref, gf_ref, weights, final_norm)


def _ffn_first(xp, xs, gain, gain_final, wg, wu, wd, layer, which, *, final_norm, tm, tf):
    mp, d = xp.shape
    nb = xs.shape[0]
    f = wg.shape[-1]
    c2 = lambda j: (0, 0)
    return pl.pallas_call(
        functools.partial(_ffn_first_kernel, final_norm=final_norm),
        grid=(f // tf,),
        in_specs=[
            pl.BlockSpec((tm, d), c2, pipeline_mode=pl.Buffered(1)),
            pl.BlockSpec((nb, d), c2),
            pl.BlockSpec((1, d), c2),
            pl.BlockSpec((1, d), c2),
            pl.BlockSpec((None, None, d, tf), lambda j: (layer, which, 0, j)),
            pl.BlockSpec((None, None, d, tf), lambda j: (layer, which, 0, j)),
            pl.BlockSpec((None, None, tf, d), lambda j: (layer, which, j, 0)),
        ],
        out_specs=[
            pl.BlockSpec((tm, d), c2, pipeline_mode=pl.Buffered(1)),
            pl.BlockSpec((nb, d), c2),
            pl.BlockSpec((d, tf), lambda j: (0, j)),
            pl.BlockSpec((d, tf), lambda j: (0, j)),
            pl.BlockSpec((tf, d), lambda j: (j, 0)),
        ],
        out_shape=[
            jax.ShapeDtypeStruct((mp, d), F32), jax.ShapeDtypeStruct((nb, d), F32),
            jax.ShapeDtypeStruct((d, f), BF16), jax.ShapeDtypeStruct((d, f), BF16),
            jax.ShapeDtypeStruct((f, d), BF16),
        ],
        scratch_shapes=[pltpu.VMEM((tm, d), BF16), pltpu.VMEM((nb, d), BF16)],
        compiler_params=_params(("arbitrary",)),
        name="ffn_first",
    )(xp, xs, gain, gain_final, wg, wu, wd)


def _ffn_rest_kernel(x_ref, g_ref, gf_ref, wg_ref, wu_ref, wd_ref, prev_ref, o_ref, h_ref, *, final_norm):
    del prev_ref
    _ffn_step(pl.program_id(1), pl.num_programs(1), ((x_ref, h_ref, o_ref),), g_ref, gf_ref,
              lambda: (wg_ref[...], wu_ref[...], wd_ref[...]), final_norm)


def _ffn_rest(x, gain, gain_final, wg, wu, wd, prev, *, final_norm, tm, tf, rows_done):
    m, d = x.shape
    f = wg.shape[-1]
    if m == rows_done:
        return prev
    assert rows_done % tm == 0
    k0 = rows_done // tm
    return pl.pallas_call(
        functools.partial(_ffn_rest_kernel, final_norm=final_norm),
        grid=(m // tm - k0, f // tf),
        in_specs=[
            pl.BlockSpec((tm, d), lambda i, j: (i + k0, 0)),
            pl.BlockSpec((1, d), lambda i, j: (0, 0)),
            pl.BlockSpec((1, d), lambda i, j: (0, 0)),
            pl.BlockSpec((d, tf), lambda i, j: (0, j)),
            pl.BlockSpec((d, tf), lambda i, j: (0, j)),
            pl.BlockSpec((tf, d), lambda i, j: (j, 0)),
            pl.BlockSpec(memory_space=pl.ANY),
        ],
        out_specs=pl.BlockSpec((tm, d), lambda i, j: (i + k0, 0)),
        out_shape=jax.ShapeDtypeStruct((m, d), F32),
        scratch_shapes=[pltpu.VMEM((tm, d), BF16)],
        input_output_aliases={6: 0},
        compiler_params=_params(("parallel", "arbitrary")),
        name="ffn_rest",
    )(x, gain, gain_final, wg, wu, wd, prev)


def _proj_first_kernel(xp_ref, xs_ref, g_ref, w_ref, wt_ref,
                       op_ref, otp_ref, os_ref, ots_ref, wb_ref, wtb_ref, hp_ref, hs_ref):
    j = pl.program_id(0)
    streams = ((xp_ref, hp_ref, op_ref, otp_ref), (xs_ref, hs_ref, os_ref, ots_ref))

    @pl.when(j == 0)
    def _():
        wt = wt_ref[...]
        pad = jnp.zeros((LANES - wt.shape[0], wt.shape[1]), F32)
        wt = jnp.transpose(jnp.concatenate([wt, pad], axis=0)).astype(BF16)
        wtb_ref[...] = wt
        for x_ref, h_ref, _, ot_ref in streams:
            h = _rms_rows(x_ref[...], g_ref[...]).astype(BF16)
            h_ref[...] = h
            ot_ref[...] = jnp.dot(h, wt, preferred_element_type=F32)

    w = jnp.transpose(w_ref[...].astype(BF16))
    wb_ref[...] = w
    for _, h_ref, o_ref, _ in streams:
        o_ref[...] = jnp.dot(h_ref[...], w, preferred_element_type=F32)


def _proj_first(xp, xs, gain, w_t, layer, n, *, tm, tn):
    mp, d = xp.shape
    nb = xs.shape[0]
    n_tail = w_t.shape[1] - n
    assert n % n_tail == 0 and n_tail % SUBLANES == 0 and n_tail <= LANES
    c2 = lambda j: (0, 0)
    return pl.pallas_call(
        _proj_first_kernel,
        grid=(n // tn,),
        in_specs=[
            pl.BlockSpec((tm, d), c2),
            pl.BlockSpec((nb, d), c2),
            pl.BlockSpec((1, d), c2),
            pl.BlockSpec((None, tn, d), lambda j: (layer, j, 0)),
            pl.BlockSpec((None, n_tail, d), lambda j: (layer, n // n_tail, 0)),
        ],
        out_specs=[
            pl.BlockSpec((tm, tn), lambda j: (0, j)),
            pl.BlockSpec((tm, LANES), c2),
            pl.BlockSpec((nb, tn), lambda j: (0, j)),
            pl.BlockSpec((nb, LANES), c2),
            pl.BlockSpec((d, tn), lambda j: (0, j)),
            pl.BlockSpec((d, LANES), c2),
        ],
        out_shape=[
            jax.ShapeDtypeStruct((mp, n), F32), jax.ShapeDtypeStruct((mp, LANES), F32),
            jax.ShapeDtypeStruct((nb, n), F32), jax.ShapeDtypeStruct((nb, LANES), F32),
            jax.ShapeDtypeStruct((d, n), BF16), jax.ShapeDtypeStruct((d, LANES), BF16),
        ],
        scratch_shapes=[pltpu.VMEM((tm, d), BF16), pltpu.VMEM((nb, d), BF16)],
        compiler_params=_params(("arbitrary",)),
        name="proj_first",
    )(xp, xs, gain, w_t, w_t)


def _proj_rest_kernel(x_ref, g_ref, w_ref, wt_ref, prev_ref, prevt_ref, o_ref, ot_ref, h_ref):
    del prev_ref, prevt_ref
    j = pl.program_id(1)

    @pl.when(j == 0)
    def _():
        h = _rms_rows(x_ref[...], g_ref[...]).astype(BF16)
        h_ref[...] = h
        ot_ref[...] = jnp.dot(h, wt_ref[...], preferred_element_type=F32)

    o_ref[...] = jnp.dot(h_ref[...], w_ref[...], preferred_element_type=F32)


def _proj_rest(x, gain, w_main, w_tail, prev_main, prev_tail, *, tm, tn):
    m, d = x.shape
    n = w_main.shape[1]
    if m == tm:
        return prev_main, prev_tail
    return pl.pallas_call(
        _proj_rest_kernel,
        grid=(m // tm - 1, n // tn),
        in_specs=[
            pl.BlockSpec((tm, d), lambda i, j: (i + 1, 0)),
            pl.BlockSpec((1, d), lambda i, j: (0, 0)),
            pl.BlockSpec((d, tn), lambda i, j: (0, j)),
            pl.BlockSpec((d, LANES), lambda i, j: (0, 0)),
            pl.BlockSpec(memory_space=pl.ANY),
            pl.BlockSpec(memory_space=pl.ANY),
        ],
        out_specs=[
            pl.BlockSpec((tm, tn), lambda i, j: (i + 1, j)),
            pl.BlockSpec((tm, LANES), lambda i, j: (i + 1, 0)),
        ],
        out_shape=[jax.ShapeDtypeStruct((m, n), F32), jax.ShapeDtypeStruct((m, LANES), F32)],
        scratch_shapes=[pltpu.VMEM((tm, d), BF16)],
        input_output_aliases={4: 0, 5: 1},
        compiler_params=_params(("parallel", "arbitrary")),
        name="proj_rest",
    )(x, gain, w_main, w_tail, prev_main, prev_tail)


def _outproj_first_kernel(*refs, n_in):
    xp_ref, xs_ref = refs[0], refs[1]
    op_refs = refs[2:2 + n_in]
    os_refs = refs[2 + n_in:2 + 2 * n_in]
    w_refs = refs[2 + 2 * n_in:2 + 3 * n_in]
    yp_ref, ys_ref, wb_ref = refs[2 + 3 * n_in:]
    accp = xp_ref[...]
    accs = xs_ref[...]
    for kb in range(n_in):
        w = w_refs[kb][...].astype(BF16)
        kw = w.shape[0]
        wb_ref[kb * kw:(kb + 1) * kw, :] = w
        accp = accp + jnp.dot(op_refs[kb][...], w, preferred_element_type=F32)
        accs = accs + jnp.dot(os_refs[kb][...], w, preferred_element_type=F32)
    yp_ref[...] = accp
    ys_ref[...] = accs


def _outproj_first(xp, xs, outs_p, outs_s, w_all, layer, *, tm, tn):
    mp, d = xp.shape
    nb = xs.shape[0]
    n_in = len(outs_p)
    kw = outs_p[0].shape[1]
    k_all = kw * n_in
    assert all(o.shape[1] == kw for o in outs_p + outs_s) and k_all == w_all.shape[1]
    in_specs = [pl.BlockSpec((tm, tn), lambda j: (0, j)), pl.BlockSpec((nb, tn), lambda j: (0, j))]
    in_specs += [pl.BlockSpec((tm, kw), lambda j: (0, 0)) for _ in range(n_in)]
    in_specs += [pl.BlockSpec((nb, kw), lambda j: (0, 0)) for _ in range(n_in)]
    in_specs += [pl.BlockSpec((None, kw, tn), lambda j, kb=kb: (layer, kb, j)) for kb in range(n_in)]
    return pl.pallas_call(
        functools.partial(_outproj_first_kernel, n_in=n_in),
        grid=(d // tn,),
        in_specs=in_specs,
        out_specs=[
            pl.BlockSpec((tm, tn), lambda j: (0, j)),
            pl.BlockSpec((nb, tn), lambda j: (0, j)),
            pl.BlockSpec((k_all, tn), lambda j: (0, j)),
        ],
        out_shape=[
            jax.ShapeDtypeStruct((mp, d), F32), jax.ShapeDtypeStruct((nb, d), F32),
            jax.ShapeDtypeStruct((k_all, d), BF16),
        ],
        compiler_params=_params(("arbitrary",)),
        name="outproj_first",
    )(xp, xs, *outs_p, *outs_s, *([w_all] * n_in))


def _outproj_rest_kernel(*refs, n_in):
    x_ref = refs[0]
    o_refs = refs[1:1 + n_in]
    w_refs = refs[1 + n_in:1 + 2 * n_in]
    out_ref = refs[2 + 2 * n_in]
    acc = x_ref[...]
    for o_ref, w_ref in zip(o_refs, w_refs):
        acc = acc + jnp.dot(o_ref[...], w_ref[...], preferred_element_type=F32)
    out_ref[...] = acc


def _outproj_rest(x, outs, w, prev, *, tm, tn):
    m, d = x.shape
    n_in = len(outs)
    kw = outs[0].shape[1]
    if m == tm:
        return prev
    in_specs = [pl.BlockSpec((tm, tn), lambda i, j: (i + 1, j))]
    in_specs += [pl.BlockSpec((tm, kw), lambda i, j: (i + 1, 0)) for _ in range(n_in)]
    in_specs += [pl.BlockSpec((kw, tn), lambda i, j, kb=kb: (kb, j)) for kb in range(n_in)]
    in_specs.append(pl.BlockSpec(memory_space=pl.ANY))
    return pl.pallas_call(
        functools.partial(_outproj_rest_kernel, n_in=n_in),
        grid=(m // tm - 1, d // tn),
        in_specs=in_specs,
        out_specs=pl.BlockSpec((tm, tn), lambda i, j: (i + 1, j)),
        out_shape=jax.ShapeDtypeStruct((m, d), F32),
        input_output_aliases={1 + 2 * n_in: 0},
        compiler_params=_params(("parallel", "arbitrary")),
        name="outproj_rest",
    )(x, *outs, *([w] * n_in), prev)


def _swa_prompt_kernel(sink_ref, q_ref, kc_ref, kp_ref, vc_ref, vp_ref, o_ref):
    n = pl.program_id(1)
    w = WINDOW
    qi = _iota2((w, 2 * w), 0)
    kj = _iota2((w, 2 * w), 1)
    valid = (kj > qi) & (kj <= qi + w) & ((n > 0) | (kj >= w))
    scale = SWA_HEAD_DIM ** -0.5
    kk = jnp.concatenate([kp_ref[...], kc_ref[...]], axis=0).astype(BF16)
    vv = jnp.concatenate([vp_ref[...], vc_ref[...]], axis=0).astype(BF16)
    q = q_ref[...].astype(BF16)
    outs = []
    for h in range(SWA_HEADS):
        c = h // SWA_GROUP
        qh = q[:, h * SWA_HEAD_DIM:(h + 1) * SWA_HEAD_DIM]
        kh = kk[:, c * SWA_HEAD_DIM:(c + 1) * SWA_HEAD_DIM]
        vh = vv[:, c * SWA_HEAD_DIM:(c + 1) * SWA_HEAD_DIM]
        s = lax.dot_general(qh, kh, (((1,), (1,)), ((), ())), preferred_element_type=F32) * scale
        s = jnp.where(valid, s, NEG_INF)
        sink = sink_ref[h]
        mx = jnp.maximum(jnp.max(s, axis=-1, keepdims=True), sink)
        p = jnp.exp(s - mx)
        den = jnp.sum(p, axis=-1, keepdims=True) + jnp.exp(sink - mx)
        p = p / den
        outs.append(jnp.dot(p.astype(BF16), vh, preferred_element_type=F32))
    o_ref[...] = jnp.concatenate(outs, axis=-1).astype(o_ref.dtype)


def _swa_prompt(p_main, sinks, bsz, t):
    nb = t // WINDOW
    kblk = EV_K // SWA_KV_W
    vblk = EV_V // SWA_KV_W

    def cur(col):
        return lambda b, n: (b * nb + n, col)

    def prev(col):
        return lambda b, n: (b * nb + jnp.maximum(n - 1, 0), col)

    return pl.pallas_call(
        _swa_prompt_kernel,
        grid=(bsz, nb),
        in_specs=[
            pl.BlockSpec(memory_space=pltpu.SMEM),
            pl.BlockSpec((WINDOW, SWA_Q_W), cur(EV_Q // SWA_Q_W)),
            pl.BlockSpec((WINDOW, SWA_KV_W), cur(kblk)),
            pl.BlockSpec((WINDOW, SWA_KV_W), prev(kblk)),
            pl.BlockSpec((WINDOW, SWA_KV_W), cur(vblk)),
            pl.BlockSpec((WINDOW, SWA_KV_W), prev(vblk)),
        ],
        out_specs=pl.BlockSpec((WINDOW, SWA_Q_W), cur(0)),
        out_shape=jax.ShapeDtypeStruct((bsz * t, SWA_Q_W), BF16),
        compiler_params=_params(("parallel", "arbitrary")),
        name="swa_prompt",
    )(sinks, p_main, p_main, p_main, p_main, p_main)


def _unit_lower_inverse(a, ri, ci):
    a0 = jnp.where((ri >> 4) == (ci >> 4), a, 0.0)
    y = -a0
    p = _bdot(a0, a0)
    for step in range(3):
        y = y + p + _bdot(y, p)
        if step < 2:
            p = _bdot(p, p)
    for sh in (4, 5, 6):
        e = jnp.where(((ri >> (sh + 1)) == (ci >> (sh + 1))) & ((ri >> sh) != (ci >> sh)), a, 0.0)
        t = e + _bdot(y, e)
        y = y - t - _bdot(t, y)
    return y


def _conv_section(ext_ref, carry_ref, sec, u_ref, w, bias, rows):
    ext_ref[0:SUBLANES, :] = carry_ref[sec]
    ext_ref[SUBLANES:SUBLANES + rows, :] = u_ref[...]
    acc = ext_ref[SUBLANES:SUBLANES + rows, :] * w[CONV_W - 1:CONV_W, :]
    for j in range(CONV_W - 1):
        off = SUBLANES - (CONV_W - 1) + j
        acc = acc + ext_ref[off:off + rows, :] * w[j:j + 1, :]
    if bias is not None:
        acc = acc + bias
    carry_ref[sec] = u_ref[rows - SUBLANES:rows, :]
    return _silu(acc)


def _gdn_prompt_kernel(uq0, uq1, uk0, uk1, uv0, uv1, z0_ref, z1_ref, tail_ref, cw_ref, pa_ref, pb_ref, nw_ref,
                       tri_ref, ones_ref,
                       o_ref, sout_ref,
                       s_ref, carry_ref, ext_ref, cq_ref, ck_ref, cv_ref, *, rows):
    i = pl.program_id(1)
    nchunk = rows // CHUNK
    hh = GDN_HEADS
    hw = GDN_HALF_W

    @pl.when(i == 0)
    def _():
        s_ref[...] = jnp.zeros_like(s_ref)
        carry_ref[...] = jnp.zeros_like(carry_ref)

    for sec, (halves, c_ref) in enumerate((((uq0, uq1), cq_ref), ((uk0, uk1), ck_ref), ((uv0, uv1), cv_ref))):
        for half, u_ref in enumerate(halves):
            ws = slice(sec * GDN_QK_W + half * hw, sec * GDN_QK_W + (half + 1) * hw)
            c_ref[:, half * hw:(half + 1) * hw] = _conv_section(ext_ref, carry_ref, 2 * sec + half, u_ref,
                                                                 cw_ref[:, ws], None, rows)

    t = tail_ref[...]
    g_all = -jnp.exp(pa_ref[...]) * _softplus(t + pb_ref[...])
    beta_all = _sigmoid(t)
    g_parts = _split3(g_all)
    gcum_all = _sel_dot(tri_ref[...], g_parts)
    gtot_all = _sel_dot(ones_ref[...], g_parts)
    gcum_t = jnp.transpose(gcum_all)

    ri = _iota2((1, CHUNK, CHUNK), 1)
    ci = _iota2((1, CHUNK, CHUNK), 2)
    incl = ri >= ci
    strict = ri > ci

    def rs(c):
        return slice(c * CHUNK, (c + 1) * CHUNK)

    def hs(h):
        return slice(h * GDN_DK, (h + 1) * GDN_DK)

    def stack(fn):
        return jnp.stack([fn(c, h) for c in range(nchunk) for h in range(hh)], axis=0)

    q = stack(lambda c, h: cq_ref[rs(c), hs(h)])
    k = stack(lambda c, h: ck_ref[rs(c), hs(h)])
    v = stack(lambda c, h: cv_ref[rs(c), hs(h)])
    gc = stack(lambda c, h: gcum_all[rs(c), h:h + 1])
    gt = stack(lambda c, h: gtot_all[rs(c), h:h + 1])
    bc = stack(lambda c, h: beta_all[rs(c), hh + h:hh + h + 1])
    gr = stack(lambda c, h: gcum_t[h:h + 1, rs(c)])

    ones_dk = jnp.ones((GDN_DK, GDN_DK), BF16)

    def sumsq(x):
        x2 = (x * x).reshape(x.shape[0] * CHUNK, GDN_DK)
        return _dot(x2, ones_dk).reshape(x.shape)

    q = q * lax.rsqrt(sumsq(q) + EPS) * (GDN_DK ** -0.5)
    k = k * lax.rsqrt(sumsq(k) + EPS)
    decay = jnp.exp(jnp.where(incl, gc - gr, NEG_INF))
    kb = k * bc
    a = jnp.where(strict, _bdot_nt(kb, k) * decay, 0.0)
    y = _unit_lower_inverse(a, ri, ci)
    eg = jnp.exp(gc)
    rhs = jnp.concatenate([v * bc, kb * eg], axis=2)
    sol = rhs + _bdot(y, rhs)
    u = sol[:, :, :GDN_DV]
    w = sol[:, :, GDN_DV:]
    wq = jnp.concatenate([w, q * eg], axis=1)
    attn = _bdot_nt(q, k) * decay
    kdt = jnp.swapaxes(k * jnp.exp(gt - gc), 1, 2)
    egt = jnp.exp(gt)

    for c in range(nchunk):
        bs = slice(c * hh, (c + 1) * hh)
        s = s_ref[...]
        ws_qs = _bdot(wq[bs], s)
        v_new = u[bs] - ws_qs[:, :CHUNK]
        o = ws_qs[:, CHUNK:] + _bdot(attn[bs], v_new)
        s_ref[...] = s * egt[bs] + _bdot(kdt[bs], v_new)
        for h in range(hh):
            z_ref = z0_ref if h < hh // 2 else z1_ref
            zc = z_ref[rs(c), hs(h % (hh // 2))]
            og = _rms_rows(o[h], nw_ref[...]) * _silu(zc)
            o_ref[rs(c), hs(h)] = og.astype(o_ref.dtype)

    @pl.when(i == pl.num_programs(1) - 1)
    def _():
        sout_ref[0] = s_ref[...]


def _chunk_masks(rows):
    r = jnp.arange(rows)
    same = (r[:, None] // CHUNK) == (r[None, :] // CHUNK)
    tri = (same & (r[:, None] >= r[None, :])).astype(BF16)
    ones = same.astype(BF16)
    return tri, ones


def _gdn_prompt(p_main, p_tail, conv_w, pa, pb, norm_w, layer, n_layers, prev, bsz, t, *, rows):
    nblk = t // rows
    hw = GDN_HALF_W
    hh = GDN_HEADS
    tri, ones = _chunk_masks(rows)

    def col(base):
        return lambda b, i: (b * nblk + i, base // hw)

    const = lambda b, i: (0, 0)
    in_specs = [pl.BlockSpec((rows, hw), col(EV_GQKV + k * hw)) for k in range(6)]
    in_specs += [pl.BlockSpec((rows, hw), col(EV_Z)), pl.BlockSpec((rows, hw), col(EV_Z + hw))]
    in_specs += [
        pl.BlockSpec((rows, LANES), lambda b, i: (b * nblk + i, 0)),
        pl.BlockSpec((None, CONV_W, GDN_CONV_CH), lambda b, i: (layer, 0, 0)),
        pl.BlockSpec((1, LANES), const),
        pl.BlockSpec((1, LANES), const),
        pl.BlockSpec((1, GDN_DV), const),
        pl.BlockSpec((rows, rows), const),
        pl.BlockSpec((rows, rows), const),
    ]
    args = [p_main] * 8 + [p_tail, conv_w, pa, pb, norm_w, tri, ones]
    extra_specs, extra_args, aliases = _alias_prev(prev, len(args), 1)
    kern = functools.partial(_gdn_prompt_kernel, rows=rows)
    if prev is not None:
        kern = _drop_ref(kern, len(args))
    return pl.pallas_call(
        kern,
        grid=(bsz, nblk),
        in_specs=in_specs + extra_specs,
        out_specs=[
            pl.BlockSpec((rows, GDN_V_W), lambda b, i: (b * nblk + i, 0)),
            pl.BlockSpec((None, 1, hh, GDN_DK, GDN_DV), lambda b, i: (layer, b, 0, 0, 0)),
        ],
        out_shape=[
            jax.ShapeDtypeStruct((bsz * t, GDN_V_W), BF16),
            jax.ShapeDtypeStruct((n_layers, bsz, hh, GDN_DK, GDN_DV), F32),
        ],
        scratch_shapes=[
            pltpu.VMEM((hh, GDN_DK, GDN_DV), F32),
            pltpu.VMEM((6, SUBLANES, hw), F32),
            pltpu.VMEM((rows + SUBLANES, hw), F32),
            pltpu.VMEM((rows, GDN_QK_W), F32),
            pltpu.VMEM((rows, GDN_QK_W), F32),
            pltpu.VMEM((rows, GDN_V_W), F32),
        ],
        input_output_aliases=aliases,
        compiler_params=_params(("parallel", "arbitrary")),
        name="gdn_prompt",
    )(*args, *extra_args)


def _ssd_prompt_kernel(z_ref, xa_ref, xb_ref, xc3_ref, tail_ref, cw_ref, cb_ref, pa_ref, pb_ref, de_ref, nw_ref,
                       ex_ref, tri_ref,
                       y_ref, hout_ref,
                       ht_ref, carry_ref, ext_ref, xc_ref, cumt_ref, dtt_ref):
    i = pl.program_id(1)
    rows = CHUNK

    @pl.when(i == 0)
    def _():
        ht_ref[...] = jnp.zeros_like(ht_ref)
        carry_ref[...] = jnp.zeros_like(carry_ref)

    for sec, u_ref in enumerate((xa_ref, xb_ref, xc3_ref)):
        ss = slice(sec * SSD_SEC_W, (sec + 1) * SSD_SEC_W)
        xc_ref[:, ss] = _conv_section(ext_ref, carry_ref, sec, u_ref, cw_ref[:, ss], cb_ref[:, ss], rows)

    dt = _softplus(tail_ref[...] + pb_ref[...])
    da = dt * (-jnp.exp(pa_ref[...]))
    cum = _sel_dot(tri_ref[...], _split3(da))
    cumt_ref[...] = jnp.transpose(cum)
    dtt_ref[...] = jnp.transpose(dt)
    last = cum[rows - 1:rows, :]
    ecum_parts = _split3(jnp.exp(cum))
    wts_parts = _split3(jnp.exp(last - cum) * dt)

    ri = _iota2((rows, rows), 0)
    ci = _iota2((rows, rows), 1)
    tri = ri >= ci
    lane = _iota2((rows, LANES), 1)
    lo_half = lane < SSM_HEAD_DIM

    for g in range(SSM_GROUPS):
        gs = slice(g * SSM_GROUP_W, (g + 1) * SSM_GROUP_W)
        bmat = xc_ref[:, SSM_D_INNER + g * SSM_STATE:SSM_D_INNER + (g + 1) * SSM_STATE]
        cmat = xc_ref[:, SSM_D_INNER + SSM_BC_W + g * SSM_STATE:SSM_D_INNER + SSM_BC_W + (g + 1) * SSM_STATE]
        xg = xc_ref[:, gs]
        cb = _dot_nt(cmat, bmat)
        exg = ex_ref[:, gs]
        ecum_e = _dot_sel(ecum_parts, exg)
        wts = _dot_sel(wts_parts, exg)
        htg = ht_ref[g]
        y = _dot(cmat, htg) * ecum_e
        pieces = []
        for pr in range(SSM_REP // 2):
            scs = []
            for half in range(2):
                h = g * SSM_REP + 2 * pr + half
                ccol = cum[:, h:h + 1]
                crow = cumt_ref[h:h + 1, :]
                drow = dtt_ref[h:h + 1, :]
                lmat = jnp.exp(jnp.where(tri, ccol - crow, NEG_INF))
                scs.append((cb * lmat * drow).astype(BF16))
            xp = xg[:, pr * LANES:(pr + 1) * LANES]
            rhs = jnp.concatenate([jnp.where(lo_half, xp, 0.0), jnp.where(lo_half, 0.0, xp)], axis=0)
            pieces.append(jnp.dot(jnp.concatenate(scs, axis=1), rhs.astype(BF16),
                                  preferred_element_type=F32))
        y = y + jnp.concatenate(pieces, axis=1) + de_ref[:, gs] * xg
        ht_ref[g] = htg * ecum_e[rows - 1:rows, :] + _dot(jnp.transpose(bmat), xg * wts)
        zg = z_ref[:, gs]
        yz = y * _silu(zg)
        y_ref[:, gs] = _rms_rows(yz, nw_ref[:, gs]).astype(y_ref.dtype)

    @pl.when(i == pl.num_programs(1) - 1)
    def _():
        for g in range(SSM_GROUPS):
            hg = jnp.transpose(ht_ref[g])
            hout_ref[0, g * SSM_REP:(g + 1) * SSM_REP] = hg.reshape(SSM_REP, SSM_HEAD_DIM, SSM_STATE)


def _head_expand():
    h = jnp.arange(LANES)[:, None]
    c = jnp.arange(SSM_D_INNER)[None, :]
    return ((c // SSM_HEAD_DIM) == h).astype(BF16)


def _ssd_prompt(p_main, p_tail, conv_w, conv_b, pa, pb, d_e, norm_w, ex, layer, n_layers, prev, bsz, t):
    rows = CHUNK
    nblk = t // rows
    tri, _ = _chunk_masks(rows)
    const = lambda b, i: (0, 0)
    row = lambda b, i: (b * nblk + i, 0)
    args = [p_main, p_main, p_main, p_main, p_tail, conv_w, conv_b, pa, pb, d_e, norm_w, ex, tri]
    extra_specs, extra_args, aliases = _alias_prev(prev, len(args), 1)
    kern = _ssd_prompt_kernel if prev is None else _drop_ref(_ssd_prompt_kernel, len(args))
    return pl.pallas_call(
        kern,
        grid=(bsz, nblk),
        input_output_aliases=aliases,
        in_specs=[
            pl.BlockSpec((rows, SSM_D_INNER), row),
            pl.BlockSpec((rows, SSD_SEC_W), lambda b, i: (b * nblk + i, OD_XBC // SSD_SEC_W)),
            pl.BlockSpec((rows, SSD_SEC_W), lambda b, i: (b * nblk + i, OD_XBC // SSD_SEC_W + 1)),
            pl.BlockSpec((rows, SSD_SEC_W), lambda b, i: (b * nblk + i, OD_XBC // SSD_SEC_W + 2)),
            pl.BlockSpec((rows, LANES), row),
            pl.BlockSpec((CONV_W, SSM_CONV_CH), const),
            pl.BlockSpec((1, SSM_CONV_CH), const),
            pl.BlockSpec((1, LANES), const),
            pl.BlockSpec((1, LANES), const),
            pl.BlockSpec((1, SSM_D_INNER), const),
            pl.BlockSpec((1, SSM_D_INNER), const),
            pl.BlockSpec((LANES, SSM_D_INNER), const),
            pl.BlockSpec((rows, rows), const),
        ] + extra_specs,
        out_specs=[
            pl.BlockSpec((rows, SSM_D_INNER), row),
            pl.BlockSpec((None, 1, SSM_HEADS, SSM_HEAD_DIM, SSM_STATE), lambda b, i: (layer, b, 0, 0, 0)),
        ],
        out_shape=[
            jax.ShapeDtypeStruct((bsz * t, SSM_D_INNER), BF16),
            jax.ShapeDtypeStruct((n_layers, bsz, SSM_HEADS, SSM_HEAD_DIM, SSM_STATE), F32),
        ],
        scratch_shapes=[
            pltpu.VMEM((SSM_GROUPS, SSM_STATE, SSM_GROUP_W), F32),
            pltpu.VMEM((3, SUBLANES, SSD_SEC_W), F32),
            pltpu.VMEM((rows + SUBLANES, SSD_SEC_W), F32),
            pltpu.VMEM((rows, SSM_CONV_CH), F32),
            pltpu.VMEM((LANES, rows), F32),
            pltpu.VMEM((LANES, rows), F32),
        ],
        compiler_params=_params(("parallel", "arbitrary")),
        name="ssd_prompt",
    )(*args, *extra_args)


def _swa_sample_group_kernel(sink_ref, q_ref, kn_ref, vn_ref, ck_ref, cv_ref, o_ref, *, bb):
    scale = SWA_HEAD_DIM ** -0.5
    rows, keys = bb * SWA_GROUP, bb * WINDOW
    ck = ck_ref[...].reshape(keys, SWA_KV_W).astype(BF16)
    cv = cv_ref[...].reshape(keys, SWA_KV_W).astype(BF16)
    kn = kn_ref[...].astype(BF16).astype(F32)
    vn = vn_ref[...].astype(BF16).astype(F32)
    ri = _iota2((rows, keys), 0)
    cj = _iota2((rows, keys), 1)
    valid = ((cj // WINDOW) == (ri // SWA_GROUP)) & ((cj % WINDOW) >= 1)
    for c in range(SWA_KV_HEADS):
        cs = slice(c * SWA_HEAD_DIM, (c + 1) * SWA_HEAD_DIM)
        qc = q_ref[c].astype(BF16)
        s = lax.dot_general(qc, ck[:, cs], (((1,), (1,)), ((), ())), preferred_element_type=F32) * scale
        s = jnp.where(valid, s, NEG_INF)
        sn = jnp.sum(qc.astype(F32) * kn[:, cs], axis=-1, keepdims=True) * scale
        sk = sink_ref[c]
        mx = jnp.maximum(jnp.maximum(jnp.max(s, axis=-1, keepdims=True), sn), sk)
        p = jnp.exp(s - mx)
        pn = jnp.exp(sn - mx)
        den = jnp.sum(p, axis=-1, keepdims=True) + pn + jnp.exp(sk - mx)
        o = jnp.dot(p.astype(BF16), cv[:, cs], preferred_element_type=F32) + pn * vn[:, cs]
        o_ref[c] = (o / den).astype(o_ref.dtype)


def _swa_sample_grouped(q, kn, vn, ck_all, cv_all, sinks, layer):
    nb = q.shape[0]
    g, kvh, hd, kvw = SWA_GROUP, SWA_KV_HEADS, SWA_HEAD_DIM, SWA_KV_W
    bb = SUBLANES if nb % SUBLANES == 0 else nb
    qg = jnp.transpose(q.reshape(nb, kvh, g, hd), (1, 0, 2, 3)).reshape(kvh, nb * g, hd)
    kn4 = jnp.repeat(kn, g, axis=0)
    vn4 = jnp.repeat(vn, g, axis=0)
    sk = jnp.tile(sinks.astype(F32).reshape(kvh, 1, g), (1, bb, 1)).reshape(kvh, bb * g, 1)
    cache = pl.BlockSpec((None, bb, WINDOW, kvw), lambda b: (layer, b, 0, 0))
    out = pl.pallas_call(
        functools.partial(_swa_sample_group_kernel, bb=bb),
        grid=(nb // bb,),
        in_specs=[
            pl.BlockSpec((kvh, bb * g, 1), lambda b: (0, 0, 0)),
            pl.BlockSpec((kvh, bb * g, hd), lambda b: (0, b, 0)),
            pl.BlockSpec((bb * g, kvw), lambda b: (b, 0)),
            pl.BlockSpec((bb * g, kvw), lambda b: (b, 0)),
            cache,
            cache,
        ],
        out_specs=pl.BlockSpec((kvh, bb * g, hd), lambda b: (0, b, 0)),
        out_shape=jax.ShapeDtypeStruct((kvh, nb * g, hd), F32),
        compiler_params=_params(("parallel",)),
        name="swa_sample",
    )(sk, qg, kn4, vn4, ck_all, cv_all)
    return jnp.transpose(out.reshape(kvh, nb, g, hd), (1, 0, 2, 3)).reshape(nb, kvh * g * hd)


def _gdn_sample_kernel(u_ref, prev_ref, w_ref, s_ref, z_ref, ab_ref, pa_ref, pb_ref, nw_ref,
                       o_ref, sout_ref, cnew_ref):
    for b in range(u_ref.shape[0]):
        _gdn_sample_row(u_ref, prev_ref, w_ref, s_ref, z_ref, ab_ref, pa_ref, pb_ref, nw_ref,
                        o_ref, sout_ref, cnew_ref, b)


def _gdn_sample_row(u_ref, prev_ref, w_ref, s_ref, z_ref, ab_ref, pa_ref, pb_ref, nw_ref,
                    o_ref, sout_ref, cnew_ref, b):
    hh = GDN_HEADS
    u = u_ref[b]
    pv = prev_ref[b]
    w = w_ref[...]
    x = u * w[CONV_W - 1]
    for j in range(CONV_W - 1):
        x = x + pv[j] * w[j]
        cnew_ref[b, j] = pv[j + 1] if j + 1 < CONV_W - 1 else u
    x = _silu(x)
    q = x[0:hh]
    k = x[hh:2 * hh]
    v = x[2 * hh:3 * hh]
    q = q * lax.rsqrt(jnp.sum(q * q, axis=-1, keepdims=True) + EPS) * (GDN_DK ** -0.5)
    k = k * lax.rsqrt(jnp.sum(k * k, axis=-1, keepdims=True) + EPS)
    ab = ab_ref[b]
    dec = jnp.exp(-jnp.exp(pa_ref[...]) * _softplus(ab[0:hh] + pb_ref[...]))
    beta = _sigmoid(ab[hh:2 * hh])
    qk = jnp.concatenate([q, k, jnp.zeros((LANES - 2 * hh, GDN_DK), F32)], axis=0)
    qkt = jnp.transpose(qk)
    outs = []
    for h in range(hh):
        s = s_ref[b, h] * dec[h:h + 1, :]
        kcol = qkt[:, hh + h:hh + h + 1]
        ks = jnp.sum(s * kcol, axis=0, keepdims=True)
        delta = beta[h:h + 1, :] * (v[h:h + 1, :] - ks)
        s = s + kcol * delta
        sout_ref[b, h] = s
        outs.append(jnp.sum(s * qkt[:, h:h + 1], axis=0, keepdims=True))
    o = jnp.concatenate(outs, axis=0)
    o_ref[b] = _rms_rows(o, nw_ref[...]) * _silu(z_ref[b])


def _gdn_sample(u, prev, w, s_all, z, ab, pa, pb, nw, layer, prev_out):
    nb = u.shape[0]
    hh = GDN_HEADS
    n_layers = s_all.shape[0]
    bb = 4 if nb % 4 == 0 else 1
    b3 = lambda b: (b, 0, 0)
    b4 = lambda b: (b, 0, 0, 0)
    c2 = lambda b: (0, 0)
    state_spec = pl.BlockSpec((None, bb, hh, GDN_DK, GDN_DV), lambda b: (layer, b, 0, 0, 0))
    args = [u, prev, w, s_all, z, ab, pa, pb, nw]
    extra_specs, extra_args, aliases = _alias_prev(prev_out, len(args), 1)
    kern = _gdn_sample_kernel if prev_out is None else _drop_ref(_gdn_sample_kernel, len(args))
    return pl.pallas_call(
        kern,
        grid=(nb // bb,),
        input_output_aliases=aliases,
        in_specs=[
            pl.BlockSpec((bb, 3 * hh, GDN_DK), b3),
            pl.BlockSpec((bb, CONV_W - 1, 3 * hh, GDN_DK), b4),
            pl.BlockSpec((CONV_W, 3 * hh, GDN_DK), lambda b: (0, 0, 0)),
            state_spec,
            pl.BlockSpec((bb, hh, GDN_DV), b3),
            pl.BlockSpec((bb, 2 * hh, LANES), b3),
            pl.BlockSpec((hh, LANES), c2),
            pl.BlockSpec((hh, LANES), c2),
            pl.BlockSpec((1, GDN_DV), c2),
        ] + extra_specs,
        out_specs=[
            pl.BlockSpec((bb, hh, GDN_DV), b3),
            state_spec,
            pl.BlockSpec((bb, CONV_W - 1, 3 * hh, GDN_DK), b4),
        ],
        out_shape=[
            jax.ShapeDtypeStruct((nb, hh, GDN_DV), F32),
            jax.ShapeDtypeStruct((n_layers, nb, hh, GDN_DK, GDN_DV), F32),
            jax.ShapeDtypeStruct((nb, CONV_W - 1, 3 * hh, GDN_DK), F32),
        ],
        compiler_params=_params(("parallel",)),
        name="gdn_sample",
    )(*args, *extra_args)


def _ssd_prep_kernel(x_ref, prev_ref, cw_ref, cb_ref, tail_ref, pa_ref, pb_ref,
                     xc_ref, dt_ref, dec_ref, cnew_ref):
    u = x_ref[...]
    w = cw_ref[...]
    acc = u * w[CONV_W - 1:CONV_W, :] + cb_ref[...]
    for j in range(CONV_W - 1):
        acc = acc + prev_ref[j] * w[j:j + 1, :]
        cnew_ref[j] = prev_ref[j + 1] if j + 1 < CONV_W - 1 else u
    xc_ref[...] = _silu(acc)
    dt = _softplus(tail_ref[...] + pb_ref[...])
    dt_ref[...] = dt
    dec_ref[...] = jnp.exp(dt * (-jnp.exp(pa_ref[...])))


def _ssd_prep_sample(p_main, prev_t, cw, cb, tail, pa, pb):
    nb = p_main.shape[0]
    sec = SSD_SEC_W
    nsec = SSM_CONV_CH // sec
    c2 = lambda s: (0, 0)
    return pl.pallas_call(
        _ssd_prep_kernel,
        grid=(nsec,),
        in_specs=[
            pl.BlockSpec((nb, sec), lambda s: (0, OD_XBC // sec + s)),
            pl.BlockSpec((CONV_W - 1, nb, sec), lambda s: (0, 0, s)),
            pl.BlockSpec((CONV_W, sec), lambda s: (0, s)),
            pl.BlockSpec((1, sec), lambda s: (0, s)),
            pl.BlockSpec((nb, LANES), c2),
            pl.BlockSpec((1, LANES), c2),
            pl.BlockSpec((1, LANES), c2),
        ],
        out_specs=[
            pl.BlockSpec((nb, sec), lambda s: (0, s)),
            pl.BlockSpec((nb, LANES), c2),
            pl.BlockSpec((nb, LANES), c2),
            pl.BlockSpec((CONV_W - 1, nb, sec), lambda s: (0, 0, s)),
        ],
        out_shape=[
            jax.ShapeDtypeStruct((nb, SSM_CONV_CH), F32),
            jax.ShapeDtypeStruct((nb, LANES), F32),
            jax.ShapeDtypeStruct((nb, LANES), F32),
            jax.ShapeDtypeStruct((CONV_W - 1, nb, SSM_CONV_CH), F32),
        ],
        compiler_params=_params(("arbitrary",)),
        name="ssd_prep_sample",
    )(p_main, prev_t, cw, cb, tail, pa, pb)


def _ssd_state_kernel(dec_ref, x_ref, dt_ref, b_ref, c_ref, z_ref, de_ref, nw_ref, ex_ref, h_ref,
                      y_ref, hout_ref, *, nb):
    g = pl.program_id(0)
    x = x_ref[...]
    dtx = x * _dot_sel(_split3(dt_ref[...]), ex_ref[...])
    zrows = LANES - nb
    xt = jnp.transpose(jnp.concatenate([dtx, jnp.zeros((zrows, SSM_GROUP_W), F32)], axis=0))
    xh, xl = _split2(xt)
    bpad = jnp.concatenate([b_ref[...], jnp.zeros((zrows, SSM_STATE), F32)], axis=0)
    cpad_t = jnp.transpose(jnp.concatenate([c_ref[...], jnp.zeros((zrows, SSM_STATE), F32)], axis=0))
    rowi = _iota2((LANES, SSM_STATE), 0)
    coli = _iota2((SSM_STATE, LANES), 1)
    yt = jnp.zeros((SSM_GROUP_W, LANES), F32)
    for b in range(nb):
        bh, bl = _split2(jnp.where(rowi == b, bpad, 0.0))
        upd = (jnp.dot(xh, bh, preferred_element_type=F32) + jnp.dot(xh, bl, preferred_element_type=F32)
               + jnp.dot(xl, bh, preferred_element_type=F32))
        parts = []
        for r in range(SSM_REP):
            hn = h_ref[b, r] * dec_ref[b, g * SSM_REP + r] + upd[r * SSM_HEAD_DIM:(r + 1) * SSM_HEAD_DIM]
            hout_ref[b, r] = hn
            parts.append(hn)
        hh_, hl_ = _split2(jnp.concatenate(parts, axis=0))
        ch, cl = _split2(jnp.where(coli == b, cpad_t, 0.0))
        yt = yt + (jnp.dot(hh_, ch, preferred_element_type=F32) + jnp.dot(hh_, cl, preferred_element_type=F32)
                   + jnp.dot(hl_, ch, preferred_element_type=F32))
    y = jnp.transpose(yt)[0:nb, :] + de_ref[...] * x
    y_ref[...] = _rms_rows(y * _silu(z_ref[...]), nw_ref[...]).astype(y_ref.dtype)


def _ssd_state_sample(dtv, dec, xc, p_main, d_e, nw, ex, h_all, layer, prev_out):
    nb = xc.shape[0]
    assert nb <= LANES
    gw = SSM_GROUP_W
    n_layers = h_all.shape[0]
    state_spec = pl.BlockSpec((None, nb, SSM_REP, SSM_HEAD_DIM, SSM_STATE), lambda g: (layer, 0, g, 0, 0))
    args = [dec, xc, dtv, xc, xc, p_main, d_e, nw, ex, h_all]
    extra_specs, extra_args, aliases = _alias_prev(prev_out, len(args), 1)
    kern = functools.partial(_ssd_state_kernel, nb=nb)
    if prev_out is not None:
        kern = _drop_ref(kern, len(args))
    return pl.pallas_call(
        kern,
        grid=(SSM_GROUPS,),
        input_output_aliases=aliases,
        in_specs=[
            pl.BlockSpec(memory_space=pltpu.SMEM),
            pl.BlockSpec((nb, gw), lambda g: (0, g)),
            pl.BlockSpec((nb, LANES), lambda g: (0, 0)),
            pl.BlockSpec((nb, SSM_STATE), lambda g: (0, SSM_D_INNER // SSM_STATE + g)),
            pl.BlockSpec((nb, SSM_STATE), lambda g: (0, (SSM_D_INNER + SSM_BC_W) // SSM_STATE + g)),
            pl.BlockSpec((nb, gw), lambda g: (0, g)),
            pl.BlockSpec((1, gw), lambda g: (0, g)),
            pl.BlockSpec((1, gw), lambda g: (0, g)),
            pl.BlockSpec((LANES, gw), lambda g: (0, g)),
            state_spec,
        ] + extra_specs,
        out_specs=[
            pl.BlockSpec((nb, gw), lambda g: (0, g)),
            state_spec,
        ],
        out_shape=[
            jax.ShapeDtypeStruct((nb, SSM_D_INNER), BF16),
            jax.ShapeDtypeStruct((n_layers, nb, SSM_HEADS, SSM_HEAD_DIM, SSM_STATE), F32),
        ],
        compiler_params=_params(("arbitrary",)),
        name="ssd_state_sample",
    )(*args, *extra_args)


def _lane_pad_row(v):
    return jnp.zeros((1, LANES), F32).at[0, :v.shape[0]].set(v.astype(F32))


def kernel(x_prompt, x_sample, cache_swa_k, cache_swa_v, state_gdn, state_gdn_conv, state_ssm, state_ssm_conv,
           norm_ffn1, norm_mix, norm_ffn2, norm_final, w_ffn_gate, w_ffn_up, w_ffn_down,
           w_in_even, w_out_even, attn_sinks, gdn_conv_w, gdn_A_log, gdn_dt_bias, gdn_norm_w,
           w_in_odd, w_out_odd, ssm_conv_w, ssm_conv_b, ssm_A_log, ssm_dt_bias, ssm_D, ssm_norm_w):
    bsz, t, d = x_prompt.shape
    nb = x_sample.shape[0]
    depth = norm_ffn1.shape[0]
    n_even, n_odd = (depth + 1) // 2, depth // 2
    mp = bsz * t
    assert x_sample.shape[1] == 1 and d == D_MODEL and t % (2 * CHUNK) == 0
    hh = GDN_HEADS

    w_even_t = jnp.swapaxes(w_in_even, 1, 2)
    w_odd_t = jnp.swapaxes(w_in_odd, 1, 2)
    ex = _head_expand()
    cache_k = cache_swa_k.reshape(n_even, nb, WINDOW, SWA_KV_W)
    cache_v = cache_swa_v.reshape(n_even, nb, WINDOW, SWA_KV_W)

    xp = x_prompt.reshape(mp, d)
    xs = x_sample.reshape(nb, d)
    gain_f = norm_final.reshape(1, d)
    ffn_tm, ffn_first_tm = _tile(mp, FFN_TM), _tile(mp, FFN_FIRST_TM)

    small = {k: [] for k in ("pk", "pv", "pgc", "psc", "sk", "sv", "sgc", "ssc")}
    p_gdn = p_ssm = s_gdn = s_ssm = None

    def ffn_pair(xp, xs, gain, l, which, final):
        gain = gain.reshape(1, d)
        yp, ys, wg, wu, wd = _ffn_first(xp, xs, gain, gain_f, w_ffn_gate, w_ffn_up, w_ffn_down, l, which,
                                        final_norm=final, tm=ffn_first_tm, tf=FFN_FIRST_TF)
        yp = _ffn_rest(xp, gain, gain_f, wg, wu, wd, yp, final_norm=final, tm=ffn_tm, tf=FFN_TF,
                       rows_done=ffn_first_tm)
        return yp, ys

    for l in range(depth):
        xp, xs = ffn_pair(xp, xs, norm_ffn1[l], l, 0, False)
        gmix = norm_mix[l].reshape(1, d)
        if l % 2 == 0:
            e = l // 2
            ptm, otm = _tile(mp, PROJ_EVEN_TM), _tile(mp, OUT_EVEN_TM)
            pm, pt, sm, st, w_main, w_tail = _proj_first(xp, xs, gmix, w_even_t, e, EV_MAIN, tm=ptm, tn=FIRST_TN)
            pm, pt = _proj_rest(xp, gmix, w_main, w_tail, pm, pt, tm=ptm, tn=PROJ_EVEN_TN)
            pa = _lane_pad_row(gdn_A_log[e])
            pb = _lane_pad_row(gdn_dt_bias[e])
            nw = gdn_norm_w[e].reshape(1, GDN_DV).astype(F32)

            kn = sm[:, EV_K:EV_K + SWA_KV_W]
            vn = sm[:, EV_V:EV_V + SWA_KV_W]
            o_a_s = _swa_sample_grouped(sm[:, EV_Q:EV_Q + SWA_Q_W], kn, vn, cache_k, cache_v, attn_sinks[e], e)
            ab = jnp.broadcast_to(st[:, :2 * hh, None], (nb, 2 * hh, LANES))
            o_b_s, s_gdn, gc_s = _gdn_sample(
                sm[:, EV_GQKV:EV_GQKV + GDN_CONV_CH].reshape(nb, 3 * hh, GDN_DK),
                state_gdn_conv[e].reshape(nb, CONV_W - 1, 3 * hh, GDN_DK),
                gdn_conv_w[e].reshape(CONV_W, 3 * hh, GDN_DK),
                state_gdn, sm[:, EV_Z:EV_Z + GDN_V_W].reshape(nb, hh, GDN_DV), ab,
                jnp.broadcast_to(gdn_A_log[e].astype(F32)[:, None], (hh, LANES)),
                jnp.broadcast_to(gdn_dt_bias[e].astype(F32)[:, None], (hh, LANES)), nw, e, s_gdn)
            small["sk"].append(kn.reshape(nb, 1, SWA_KV_HEADS, SWA_HEAD_DIM))
            small["sv"].append(vn.reshape(nb, 1, SWA_KV_HEADS, SWA_HEAD_DIM))
            small["sgc"].append(gc_s.reshape(nb, CONV_W - 1, GDN_CONV_CH))
            outs_s = [o_a_s.reshape(nb, SWA_Q_W).astype(BF16), o_b_s.reshape(nb, GDN_V_W).astype(BF16)]

            o_a_p = _swa_prompt(pm, attn_sinks[e].astype(F32), bsz, t)
            o_b_p, p_gdn = _gdn_prompt(pm, pt, gdn_conv_w, pa, pb, nw, e, n_even, p_gdn, bsz, t, rows=2 * CHUNK)
            pm3 = pm.reshape(bsz, t, EV_MAIN)
            small["pk"].append(pm3[:, t - WINDOW:, EV_K:EV_K + SWA_KV_W].reshape(bsz, WINDOW, SWA_KV_HEADS, SWA_HEAD_DIM))
            small["pv"].append(pm3[:, t - WINDOW:, EV_V:EV_V + SWA_KV_W].reshape(bsz, WINDOW, SWA_KV_HEADS, SWA_HEAD_DIM))
            small["pgc"].append(pm3[:, t - (CONV_W - 1):, EV_GQKV:EV_GQKV + GDN_CONV_CH])
            yp, xs, w_out = _outproj_first(xp, xs, [o_a_p, o_b_p], outs_s, w_out_even, e, tm=otm, tn=FIRST_TN)
            xp = _outproj_rest(xp, [o_a_p, o_b_p], w_out, yp, tm=otm, tn=OUT_EVEN_TN)
        else:
            o = l // 2
            ptm, otm = _tile(mp, PROJ_ODD_TM), _tile(mp, OUT_ODD_TM)
            pm, pt, sm, st, w_main, w_tail = _proj_first(xp, xs, gmix, w_odd_t, o, OD_MAIN, tm=ptm, tn=FIRST_TN)
            pm, pt = _proj_rest(xp, gmix, w_main, w_tail, pm, pt, tm=ptm, tn=PROJ_ODD_TN)
            pa = _lane_pad_row(ssm_A_log[o])
            pb = _lane_pad_row(ssm_dt_bias[o])
            d_e = jnp.repeat(ssm_D[o].astype(F32), SSM_HEAD_DIM).reshape(1, SSM_D_INNER)
            nw = ssm_norm_w[o].reshape(1, SSM_D_INNER).astype(F32)
            cb = ssm_conv_b[o].reshape(1, SSM_CONV_CH)

            xc, dtv, dec, cnew = _ssd_prep_sample(sm, jnp.swapaxes(state_ssm_conv[o], 0, 1), ssm_conv_w[o], cb,
                                                  st, pa, pb)
            y_s, s_ssm = _ssd_state_sample(dtv, dec, xc, sm, d_e, nw, ex, state_ssm, o, s_ssm)
            small["ssc"].append(jnp.swapaxes(cnew, 0, 1))

            y_p, p_ssm = _ssd_prompt(pm, pt, ssm_conv_w[o], cb, pa, pb, d_e, nw, ex, o, n_odd, p_ssm, bsz, t)
            pm3 = pm.reshape(bsz, t, OD_MAIN)
            small["psc"].append(pm3[:, t - (CONV_W - 1):, OD_XBC:OD_XBC + SSM_CONV_CH])
            yp, xs, w_out = _outproj_first(xp, xs, [y_p], [y_s], w_out_odd, o, tm=otm, tn=FIRST_TN)
            xp = _outproj_rest(xp, [y_p], w_out, yp, tm=otm, tn=OUT_ODD_TN)
        xp, xs = ffn_pair(xp, xs, norm_ffn2[l], l, 1, l == depth - 1)

    st_ = {k: jnp.stack(v) for k, v in small.items()}
    return (xp.reshape(bsz, t, d), xs.reshape(nb, 1, d),
            st_["pk"], st_["pv"], p_gdn, st_["pgc"], p_ssm, st_["psc"],
            st_["sk"], st_["sv"], s_gdn, st_["sgc"], s_ssm, st_["ssc"])
```

```python
import functools

import jax
import jax.numpy as jnp
from jax import lax
from jax.experimental import pallas as pl
from jax.experimental.pallas import tpu as pltpu

F32 = jnp.float32
BF16 = jnp.bfloat16

D_MODEL = 2048
SWA_HEADS = 16
SWA_KV_HEADS = 4
SWA_GROUP = SWA_HEADS // SWA_KV_HEADS
SWA_HEAD_DIM = 64
WINDOW = 128
GDN_HEADS = 8
GDN_DK = 128
GDN_DV = 128
CONV_W = 4
SSM_D_INNER = 2 * D_MODEL
SSM_HEAD_DIM = 64
SSM_HEADS = SSM_D_INNER // SSM_HEAD_DIM
SSM_GROUPS = 8
SSM_REP = SSM_HEADS // SSM_GROUPS
SSM_STATE = 128
EPS = 1e-6
NEG_INF = -1e30

SWA_Q_W = SWA_HEADS * SWA_HEAD_DIM
SWA_KV_W = SWA_KV_HEADS * SWA_HEAD_DIM
GDN_QK_W = GDN_HEADS * GDN_DK
GDN_V_W = GDN_HEADS * GDN_DV
GDN_CONV_CH = 2 * GDN_QK_W + GDN_V_W
GDN_HALF_W = GDN_QK_W // 2
SSM_BC_W = SSM_GROUPS * SSM_STATE
SSM_CONV_CH = SSM_D_INNER + 2 * SSM_BC_W
SSM_GROUP_W = SSM_REP * SSM_HEAD_DIM
SSD_SEC_W = SSM_CONV_CH // 3

EV_Q = 0
EV_K = EV_Q + SWA_Q_W
EV_V = EV_K + SWA_KV_W
EV_GQKV = EV_V + SWA_KV_W
EV_Z = EV_GQKV + GDN_CONV_CH
EV_MAIN = EV_Z + GDN_V_W
OD_XBC = SSM_D_INNER
OD_MAIN = SSM_D_INNER + SSM_CONV_CH

LANES = 128
SUBLANES = 8
CHUNK = 128
V7X_VMEM_BYTES = 64 * 1024 * 1024
VMEM_LIMIT = V7X_VMEM_BYTES - 8 * 1024 * 1024

FFN_TM, FFN_TF = 512, 512
PROJ_EVEN_TM, PROJ_EVEN_TN = 1024, EV_MAIN // 4
PROJ_ODD_TM, PROJ_ODD_TN = 1024, 1024
OUT_EVEN_TM, OUT_EVEN_TN = 512, D_MODEL
OUT_ODD_TM, OUT_ODD_TN = 1024, 1024
FFN_FIRST_TF, FIRST_TN = 256, 512
FFN_FIRST_TM = 1024
PROJ_FIRST_TM = 2048
PROJ_FIRST_VMEM_LIMIT = V7X_VMEM_BYTES - 3 * 1024 * 1024
OUT_EVEN_FIRST_TM, OUT_ODD_FIRST_TM = 1024, 1024


def _params(sem):
    return pltpu.CompilerParams(dimension_semantics=sem, vmem_limit_bytes=VMEM_LIMIT)


def _tile(m, pref):
    return pref if m % pref == 0 else m


def _sigmoid(x):
    return 1.0 / (1.0 + jnp.exp(-x))


def _silu(x):
    return x * _sigmoid(x)


def _softplus(x):
    return jnp.maximum(x, 0.0) + jnp.log(1.0 + jnp.exp(-jnp.abs(x)))


def _dot(a, b):
    return jnp.dot(a.astype(BF16), b.astype(BF16), preferred_element_type=F32)


def _dot_nt(a, b):
    return lax.dot_general(a.astype(BF16), b.astype(BF16), (((1,), (1,)), ((), ())),
                           preferred_element_type=F32)


def _bdot(a, b):
    return jnp.einsum("bij,bjk->bik", a.astype(BF16), b.astype(BF16), preferred_element_type=F32)


def _bdot_nt(a, b):
    return jnp.einsum("bid,bjd->bij", a.astype(BF16), b.astype(BF16), preferred_element_type=F32)


def _split2(a):
    hi = a.astype(BF16)
    lo = (a - hi.astype(F32)).astype(BF16)
    return hi, lo


def _split3(a):
    hi = a.astype(BF16)
    r = a - hi.astype(F32)
    mid = r.astype(BF16)
    lo = (r - mid.astype(F32)).astype(BF16)
    return hi, mid, lo


def _sel_dot(m01, parts):
    out = jnp.dot(m01, parts[0], preferred_element_type=F32)
    for p in parts[1:]:
        out = out + jnp.dot(m01, p, preferred_element_type=F32)
    return out


def _dot_sel(parts, m01):
    out = jnp.dot(parts[0], m01, preferred_element_type=F32)
    for p in parts[1:]:
        out = out + jnp.dot(p, m01, preferred_element_type=F32)
    return out


def _rms_rows(x, gain):
    ms = jnp.mean(x * x, axis=-1, keepdims=True)
    return x * lax.rsqrt(ms + EPS) * gain


def _iota2(shape, dim):
    return lax.broadcasted_iota(jnp.int32, shape, dim)


def _drop_ref(kern, pos):
    def entry(*refs):
        return kern(*refs[:pos], *refs[pos + 1:])
    return entry


def _alias_prev(prev, n_in, out_idx):
    if prev is None:
        return [], [], {}
    return [pl.BlockSpec(memory_space=pl.ANY)], [prev], {n_in: out_idx}


def _ffn_rows(h, wg, wu, wd):
    nsplit = 2 if wg.shape[1] >= 4 * LANES else 1
    half = wg.shape[1] // nsplit
    ts = []
    for s in range(nsplit):
        cs = slice(s * half, (s + 1) * half)
        a = jnp.dot(h, wg[:, cs], preferred_element_type=F32)
        b = jnp.dot(h, wu[:, cs], preferred_element_type=F32)
        ts.append((_silu(a) * b).astype(BF16))
    return jnp.dot(jnp.concatenate(ts, axis=1), wd, preferred_element_type=F32)


def _ffn_step(j, nsteps, streams, g_ref, gf_ref, weights, final_norm):
    def finish(x_ref, acc):
        y = x_ref[...] + 0.5 * acc
        return _rms_rows(y, gf_ref[...]) if final_norm else y

    @pl.when(j == 0)
    def _():
        w = weights()
        for x_ref, h_ref, o_ref in streams:
            h = _rms_rows(x_ref[...], g_ref[...]).astype(BF16)
            h_ref[...] = h
            o_ref[...] = _ffn_rows(h, *w)

    @pl.when((j > 0) & (j < nsteps - 1))
    def _():
        w = weights()
        for _, h_ref, o_ref in streams:
            o_ref[...] += _ffn_rows(h_ref[...], *w)

    @pl.when(j == nsteps - 1)
    def _():
        w = weights()
        for x_ref, h_ref, o_ref in streams:
            o_ref[...] = finish(x_ref, o_ref[...] + _ffn_rows(h_ref[...], *w))


def _ffn_first_kernel(xp_ref, xs_ref, g_ref, gf_ref, wg_ref, wu_ref, wd_ref,
                      op_ref, os_ref, wgb_ref, wub_ref, wdb_ref, hp_ref, hs_ref, *, final_norm):
    def weights():
        wg, wu, wd = wg_ref[...].astype(BF16), wu_ref[...].astype(BF16), wd_ref[...].astype(BF16)
        wgb_ref[...] = wg
        wub_ref[...] = wu
        wdb_ref[...] = wd
        return wg, wu, wd

    streams = ((xp_ref, hp_ref, op_ref), (xs_ref, hs_ref, os_ref))
    _ffn_step(pl.program_id(0), pl.num_programs(0), streams, g_ref, gf_ref, weights, final_norm)


def _ffn_first(xp, xs, gain, gain_final, wg, wu, wd, layer, which, *, final_norm, tm, tf):
    mp, d = xp.shape
    nb = xs.shape[0]
    f = wg.shape[-1]
    c2 = lambda j: (0, 0)
    return pl.pallas_call(
        functools.partial(_ffn_first_kernel, final_norm=final_norm),
        grid=(f // tf,),
        in_specs=[
            pl.BlockSpec((tm, d), c2, pipeline_mode=pl.Buffered(1)),
            pl.BlockSpec((nb, d), c2),
            pl.BlockSpec((1, d), c2),
            pl.BlockSpec((1, d), c2),
            pl.BlockSpec((None, None, d, tf), lambda j: (layer, which, 0, j)),
            pl.BlockSpec((None, None, d, tf), lambda j: (layer, which, 0, j)),
            pl.BlockSpec((None, None, tf, d), lambda j: (layer, which, j, 0)),
        ],
        out_specs=[
            pl.BlockSpec((tm, d), c2, pipeline_mode=pl.Buffered(1)),
            pl.BlockSpec((nb, d), c2),
            pl.BlockSpec((d, tf), lambda j: (0, j)),
            pl.BlockSpec((d, tf), lambda j: (0, j)),
            pl.BlockSpec((tf, d), lambda j: (j, 0)),
        ],
        out_shape=[
            jax.ShapeDtypeStruct((mp, d), F32), jax.ShapeDtypeStruct((nb, d), F32),
            jax.ShapeDtypeStruct((d, f), BF16), jax.ShapeDtypeStruct((d, f), BF16),
            jax.ShapeDtypeStruct((f, d), BF16),
        ],
        scratch_shapes=[pltpu.VMEM((tm, d), BF16), pltpu.VMEM((nb, d), BF16)],
        compiler_params=_params(("arbitrary",)),
        name="ffn_first",
    )(xp, xs, gain, gain_final, wg, wu, wd)


def _ffn_rest_kernel(x_ref, g_ref, gf_ref, wg_ref, wu_ref, wd_ref, prev_ref, o_ref, h_ref, *, final_norm):
    del prev_ref
    _ffn_step(pl.program_id(1), pl.num_programs(1), ((x_ref, h_ref, o_ref),), g_ref, gf_ref,
              lambda: (wg_ref[...], wu_ref[...], wd_ref[...]), final_norm)


def _ffn_rest(x, gain, gain_final, wg, wu, wd, prev, *, final_norm, tm, tf, rows_done):
    m, d = x.shape
    f = wg.shape[-1]
    if m == rows_done:
        return prev
    assert rows_done % tm == 0
    k0 = rows_done // tm
    return pl.pallas_call(
        functools.partial(_ffn_rest_kernel, final_norm=final_norm),
        grid=(m // tm - k0, f // tf),
        in_specs=[
            pl.BlockSpec((tm, d), lambda i, j: (i + k0, 0)),
            pl.BlockSpec((1, d), lambda i, j: (0, 0)),
            pl.BlockSpec((1, d), lambda i, j: (0, 0)),
            pl.BlockSpec((d, tf), lambda i, j: (0, j)),
            pl.BlockSpec((d, tf), lambda i, j: (0, j)),
            pl.BlockSpec((tf, d), lambda i, j: (j, 0)),
            pl.BlockSpec(memory_space=pl.ANY),
        ],
        out_specs=pl.BlockSpec((tm, d), lambda i, j: (i + k0, 0)),
        out_shape=jax.ShapeDtypeStruct((m, d), F32),
        scratch_shapes=[pltpu.VMEM((tm, d), BF16)],
        input_output_aliases={6: 0},
        compiler_params=_params(("parallel", "arbitrary")),
        name="ffn_rest",
    )(x, gain, gain_final, wg, wu, wd, prev)


def _proj_first_kernel(xp_ref, xs_ref, g_ref, w_ref, wt_ref,
                       op_ref, otp_ref, os_ref, ots_ref, wb_ref, wtb_ref, hp_ref, hs_ref):
    j = pl.program_id(0)
    streams = ((xp_ref, hp_ref, op_ref, otp_ref), (xs_ref, hs_ref, os_ref, ots_ref))

    @pl.when(j == 0)
    def _():
        wt = wt_ref[...]
        pad = jnp.zeros((LANES - wt.shape[0], wt.shape[1]), F32)
        wt = jnp.transpose(jnp.concatenate([wt, pad], axis=0)).astype(BF16)
        wtb_ref[...] = wt
        for x_ref, h_ref, _, ot_ref in streams:
            h = _rms_rows(x_ref[...], g_ref[...]).astype(BF16)
            h_ref[...] = h
            ot_ref[...] = jnp.dot(h, wt, preferred_element_type=F32)

    w = jnp.transpose(w_ref[...].astype(BF16))
    wb_ref[...] = w
    for _, h_ref, o_ref, _ in streams:
        o_ref[...] = jnp.dot(h_ref[...], w, preferred_element_type=F32)


def _proj_first(xp, xs, gain, w_t, layer, n, *, tm, tn):
    mp, d = xp.shape
    nb = xs.shape[0]
    n_tail = w_t.shape[1] - n
    assert n % n_tail == 0 and n_tail % SUBLANES == 0 and n_tail <= LANES
    c2 = lambda j: (0, 0)
    return pl.pallas_call(
        _proj_first_kernel,
        grid=(n // tn,),
        in_specs=[
            pl.BlockSpec((tm, d), c2, pipeline_mode=pl.Buffered(1)),
            pl.BlockSpec((nb, d), c2),
            pl.BlockSpec((1, d), c2),
            pl.BlockSpec((None, tn, d), lambda j: (layer, j, 0)),
            pl.BlockSpec((None, n_tail, d), lambda j: (layer, n // n_tail, 0)),
        ],
        out_specs=[
            pl.BlockSpec((tm, tn), lambda j: (0, j)),
            pl.BlockSpec((tm, LANES), c2),
            pl.BlockSpec((nb, tn), lambda j: (0, j)),
            pl.BlockSpec((nb, LANES), c2),
            pl.BlockSpec((d, tn), lambda j: (0, j)),
            pl.BlockSpec((d, LANES), c2),
        ],
        out_shape=[
            jax.ShapeDtypeStruct((mp, n), F32), jax.ShapeDtypeStruct((mp, LANES), F32),
            jax.ShapeDtypeStruct((nb, n), F32), jax.ShapeDtypeStruct((nb, LANES), F32),
            jax.ShapeDtypeStruct((d, n), BF16), jax.ShapeDtypeStruct((d, LANES), BF16),
        ],
        scratch_shapes=[pltpu.VMEM((tm, d), BF16), pltpu.VMEM((nb, d), BF16)],
        compiler_params=pltpu.CompilerParams(dimension_semantics=("arbitrary",),
                                             vmem_limit_bytes=PROJ_FIRST_VMEM_LIMIT),
        name="proj_first",
    )(xp, xs, gain, w_t, w_t)


def _proj_rest_kernel(x_ref, g_ref, w_ref, wt_ref, prev_ref, prevt_ref, o_ref, ot_ref, h_ref):
    del prev_ref, prevt_ref
    j = pl.program_id(1)

    @pl.when(j == 0)
    def _():
        h = _rms_rows(x_ref[...], g_ref[...]).astype(BF16)
        h_ref[...] = h
        ot_ref[...] = jnp.dot(h, wt_ref[...], preferred_element_type=F32)

    o_ref[...] = jnp.dot(h_ref[...], w_ref[...], preferred_element_type=F32)


def _proj_rest(x, gain, w_main, w_tail, prev_main, prev_tail, *, tm, tn, rows_done):
    m, d = x.shape
    n = w_main.shape[1]
    if m == rows_done:
        return prev_main, prev_tail
    assert rows_done % tm == 0
    k0 = rows_done // tm
    return pl.pallas_call(
        _proj_rest_kernel,
        grid=(m // tm - k0, n // tn),
        in_specs=[
            pl.BlockSpec((tm, d), lambda i, j: (i + k0, 0)),
            pl.BlockSpec((1, d), lambda i, j: (0, 0)),
            pl.BlockSpec((d, tn), lambda i, j: (0, j)),
            pl.BlockSpec((d, LANES), lambda i, j: (0, 0)),
            pl.BlockSpec(memory_space=pl.ANY),
            pl.BlockSpec(memory_space=pl.ANY),
        ],
        out_specs=[
            pl.BlockSpec((tm, tn), lambda i, j: (i + k0, j)),
            pl.BlockSpec((tm, LANES), lambda i, j: (i + k0, 0)),
        ],
        out_shape=[jax.ShapeDtypeStruct((m, n), F32), jax.ShapeDtypeStruct((m, LANES), F32)],
        scratch_shapes=[pltpu.VMEM((tm, d), BF16)],
        input_output_aliases={4: 0, 5: 1},
        compiler_params=_params(("parallel", "arbitrary")),
        name="proj_rest",
    )(x, gain, w_main, w_tail, prev_main, prev_tail)


def _outproj_first_kernel(*refs, n_in):
    xp_ref, xs_ref = refs[0], refs[1]
    op_refs = refs[2:2 + n_in]
    os_refs = refs[2 + n_in:2 + 2 * n_in]
    w_refs = refs[2 + 2 * n_in:2 + 3 * n_in]
    yp_ref, ys_ref, wb_ref = refs[2 + 3 * n_in:]
    accp = xp_ref[...]
    accs = xs_ref[...]
    for kb in range(n_in):
        w = w_refs[kb][...].astype(BF16)
        kw = w.shape[0]
        wb_ref[kb * kw:(kb + 1) * kw, :] = w
        accp = accp + jnp.dot(op_refs[kb][...], w, preferred_element_type=F32)
        accs = accs + jnp.dot(os_refs[kb][...], w, preferred_element_type=F32)
    yp_ref[...] = accp
    ys_ref[...] = accs


def _outproj_first(xp, xs, outs_p, outs_s, w_all, layer, *, tm, tn):
    mp, d = xp.shape
    nb = xs.shape[0]
    n_in = len(outs_p)
    kw = outs_p[0].shape[1]
    k_all = kw * n_in
    assert all(o.shape[1] == kw for o in outs_p + outs_s) and k_all == w_all.shape[1]
    in_specs = [pl.BlockSpec((tm, tn), lambda j: (0, j)), pl.BlockSpec((nb, tn), lambda j: (0, j))]
    in_specs += [pl.BlockSpec((tm, kw), lambda j: (0, 0), pipeline_mode=pl.Buffered(1)) for _ in range(n_in)]
    in_specs += [pl.BlockSpec((nb, kw), lambda j: (0, 0)) for _ in range(n_in)]
    in_specs += [pl.BlockSpec((None, kw, tn), lambda j, kb=kb: (layer, kb, j)) for kb in range(n_in)]
    return pl.pallas_call(
        functools.partial(_outproj_first_kernel, n_in=n_in),
        grid=(d // tn,),
        in_specs=in_specs,
        out_specs=[
            pl.BlockSpec((tm, tn), lambda j: (0, j)),
            pl.BlockSpec((nb, tn), lambda j: (0, j)),
            pl.BlockSpec((k_all, tn), lambda j: (0, j)),
        ],
        out_shape=[
            jax.ShapeDtypeStruct((mp, d), F32), jax.ShapeDtypeStruct((nb, d), F32),
            jax.ShapeDtypeStruct((k_all, d), BF16),
        ],
        compiler_params=_params(("arbitrary",)),
        name="outproj_first",
    )(xp, xs, *outs_p, *outs_s, *([w_all] * n_in))


def _outproj_rest_kernel(*refs, n_in):
    x_ref = refs[0]
    o_refs = refs[1:1 + n_in]
    w_refs = refs[1 + n_in:1 + 2 * n_in]
    out_ref = refs[2 + 2 * n_in]
    acc = x_ref[...]
    for o_ref, w_ref in zip(o_refs, w_refs):
        acc = acc + jnp.dot(o_ref[...], w_ref[...], preferred_element_type=F32)
    out_ref[...] = acc


def _outproj_rest(x, outs, w, prev, *, tm, tn, rows_done):
    m, d = x.shape
    n_in = len(outs)
    kw = outs[0].shape[1]
    if m == rows_done:
        return prev
    assert rows_done % tm == 0
    k0 = rows_done // tm
    in_specs = [pl.BlockSpec((tm, tn), lambda i, j: (i + k0, j))]
    in_specs += [pl.BlockSpec((tm, kw), lambda i, j: (i + k0, 0)) for _ in range(n_in)]
    in_specs += [pl.BlockSpec((kw, tn), lambda i, j, kb=kb: (kb, j)) for kb in range(n_in)]
    in_specs.append(pl.BlockSpec(memory_space=pl.ANY))
    return pl.pallas_call(
        functools.partial(_outproj_rest_kernel, n_in=n_in),
        grid=(m // tm - k0, d // tn),
        in_specs=in_specs,
        out_specs=pl.BlockSpec((tm, tn), lambda i, j: (i + k0, j)),
        out_shape=jax.ShapeDtypeStruct((m, d), F32),
        input_output_aliases={1 + 2 * n_in: 0},
        compiler_params=_params(("parallel", "arbitrary")),
        name="outproj_rest",
    )(x, *outs, *([w] * n_in), prev)


def _swa_prompt_kernel(sink_ref, q_ref, kc_ref, kp_ref, vc_ref, vp_ref, o_ref):
    n = pl.program_id(1)
    w = WINDOW
    qi = _iota2((w, 2 * w), 0)
    kj = _iota2((w, 2 * w), 1)
    valid = (kj > qi) & (kj <= qi + w) & ((n > 0) | (kj >= w))
    scale = SWA_HEAD_DIM ** -0.5
    kk = jnp.concatenate([kp_ref[...], kc_ref[...]], axis=0).astype(BF16)
    vv = jnp.concatenate([vp_ref[...], vc_ref[...]], axis=0).astype(BF16)
    q = q_ref[...].astype(BF16)
    outs = []
    for h in range(SWA_HEADS):
        c = h // SWA_GROUP
        qh = q[:, h * SWA_HEAD_DIM:(h + 1) * SWA_HEAD_DIM]
        kh = kk[:, c * SWA_HEAD_DIM:(c + 1) * SWA_HEAD_DIM]
        vh = vv[:, c * SWA_HEAD_DIM:(c + 1) * SWA_HEAD_DIM]
        s = lax.dot_general(qh, kh, (((1,), (1,)), ((), ())), preferred_element_type=F32) * scale
        s = jnp.where(valid, s, NEG_INF)
        sink = sink_ref[h]
        mx = jnp.maximum(jnp.max(s, axis=-1, keepdims=True), sink)
        p = jnp.exp(s - mx)
        den = jnp.sum(p, axis=-1, keepdims=True) + jnp.exp(sink - mx)
        p = p / den
        outs.append(jnp.dot(p.astype(BF16), vh, preferred_element_type=F32))
    o_ref[...] = jnp.concatenate(outs, axis=-1).astype(o_ref.dtype)


def _swa_prompt(p_main, sinks, bsz, t):
    nb = t // WINDOW
    kblk = EV_K // SWA_KV_W
    vblk = EV_V // SWA_KV_W

    def cur(col):
        return lambda b, n: (b * nb + n, col)

    def prev(col):
        return lambda b, n: (b * nb + jnp.maximum(n - 1, 0), col)

    return pl.pallas_call(
        _swa_prompt_kernel,
        grid=(bsz, nb),
        in_specs=[
            pl.BlockSpec(memory_space=pltpu.SMEM),
            pl.BlockSpec((WINDOW, SWA_Q_W), cur(EV_Q // SWA_Q_W)),
            pl.BlockSpec((WINDOW, SWA_KV_W), cur(kblk)),
            pl.BlockSpec((WINDOW, SWA_KV_W), prev(kblk)),
            pl.BlockSpec((WINDOW, SWA_KV_W), cur(vblk)),
            pl.BlockSpec((WINDOW, SWA_KV_W), prev(vblk)),
        ],
        out_specs=pl.BlockSpec((WINDOW, SWA_Q_W), cur(0)),
        out_shape=jax.ShapeDtypeStruct((bsz * t, SWA_Q_W), BF16),
        compiler_params=_params(("parallel", "arbitrary")),
        name="swa_prompt",
    )(sinks, p_main, p_main, p_main, p_main, p_main)


def _unit_lower_inverse(a, ri, ci):
    a0 = jnp.where((ri >> 4) == (ci >> 4), a, 0.0)
    y = -a0
    p = _bdot(a0, a0)
    for step in range(3):
        y = y + p + _bdot(y, p)
        if step < 2:
            p = _bdot(p, p)
    for sh in (4, 5, 6):
        e = jnp.where(((ri >> (sh + 1)) == (ci >> (sh + 1))) & ((ri >> sh) != (ci >> sh)), a, 0.0)
        t = e + _bdot(y, e)
        y = y - t - _bdot(t, y)
    return y


def _conv_section(ext_ref, carry_ref, sec, u_ref, w, bias, rows):
    ext_ref[0:SUBLANES, :] = carry_ref[sec]
    ext_ref[SUBLANES:SUBLANES + rows, :] = u_ref[...]
    acc = ext_ref[SUBLANES:SUBLANES + rows, :] * w[CONV_W - 1:CONV_W, :]
    for j in range(CONV_W - 1):
        off = SUBLANES - (CONV_W - 1) + j
        acc = acc + ext_ref[off:off + rows, :] * w[j:j + 1, :]
    if bias is not None:
        acc = acc + bias
    carry_ref[sec] = u_ref[rows - SUBLANES:rows, :]
    return _silu(acc)


def _gdn_prompt_kernel(uq0, uq1, uk0, uk1, uv0, uv1, z0_ref, z1_ref, tail_ref, cw_ref, pa_ref, pb_ref, nw_ref,
                       tri_ref, ones_ref,
                       o_ref, sout_ref,
                       s_ref, carry_ref, ext_ref, cq_ref, ck_ref, cv_ref, *, rows):
    i = pl.program_id(1)
    nchunk = rows // CHUNK
    hh = GDN_HEADS
    hw = GDN_HALF_W

    @pl.when(i == 0)
    def _():
        s_ref[...] = jnp.zeros_like(s_ref)
        carry_ref[...] = jnp.zeros_like(carry_ref)

    for sec, (halves, c_ref) in enumerate((((uq0, uq1), cq_ref), ((uk0, uk1), ck_ref), ((uv0, uv1), cv_ref))):
        for half, u_ref in enumerate(halves):
            ws = slice(sec * GDN_QK_W + half * hw, sec * GDN_QK_W + (half + 1) * hw)
            c_ref[:, half * hw:(half + 1) * hw] = _conv_section(ext_ref, carry_ref, 2 * sec + half, u_ref,
                                                                 cw_ref[:, ws], None, rows)

    t = tail_ref[...]
    g_all = -jnp.exp(pa_ref[...]) * _softplus(t + pb_ref[...])
    beta_all = _sigmoid(t)
    g_parts = _split3(g_all)
    gcum_all = _sel_dot(tri_ref[...], g_parts)
    gtot_all = _sel_dot(ones_ref[...], g_parts)
    gcum_t = jnp.transpose(gcum_all)

    ri = _iota2((1, CHUNK, CHUNK), 1)
    ci = _iota2((1, CHUNK, CHUNK), 2)
    incl = ri >= ci
    strict = ri > ci

    def rs(c):
        return slice(c * CHUNK, (c + 1) * CHUNK)

    def hs(h):
        return slice(h * GDN_DK, (h + 1) * GDN_DK)

    def stack(fn):
        return jnp.stack([fn(c, h) for c in range(nchunk) for h in range(hh)], axis=0)

    q = stack(lambda c, h: cq_ref[rs(c), hs(h)])
    k = stack(lambda c, h: ck_ref[rs(c), hs(h)])
    v = stack(lambda c, h: cv_ref[rs(c), hs(h)])
    gc = stack(lambda c, h: gcum_all[rs(c), h:h + 1])
    gt = stack(lambda c, h: gtot_all[rs(c), h:h + 1])
    bc = stack(lambda c, h: beta_all[rs(c), hh + h:hh + h + 1])
    gr = stack(lambda c, h: gcum_t[h:h + 1, rs(c)])

    ones_dk = jnp.ones((GDN_DK, GDN_DK), BF16)

    def sumsq(x):
        x2 = (x * x).reshape(x.shape[0] * CHUNK, GDN_DK)
        return _dot(x2, ones_dk).reshape(x.shape)

    q = q * lax.rsqrt(sumsq(q) + EPS) * (GDN_DK ** -0.5)
    k = k * lax.rsqrt(sumsq(k) + EPS)
    decay = jnp.exp(jnp.where(incl, gc - gr, NEG_INF))
    kb = k * bc
    a = jnp.where(strict, _bdot_nt(kb, k) * decay, 0.0)
    y = _unit_lower_inverse(a, ri, ci)
    eg = jnp.exp(gc)
    rhs = jnp.concatenate([v * bc, kb * eg], axis=2)
    sol = rhs + _bdot(y, rhs)
    u = sol[:, :, :GDN_DV]
    w = sol[:, :, GDN_DV:]
    wq = jnp.concatenate([w, q * eg], axis=1)
    attn = _bdot_nt(q, k) * decay
    kdt = jnp.swapaxes(k * jnp.exp(gt - gc), 1, 2)
    egt = jnp.exp(gt)

    for c in range(nchunk):
        bs = slice(c * hh, (c + 1) * hh)
        s = s_ref[...]
        ws_qs = _bdot(wq[bs], s)
        v_new = u[bs] - ws_qs[:, :CHUNK]
        o = ws_qs[:, CHUNK:] + _bdot(attn[bs], v_new)
        s_ref[...] = s * egt[bs] + _bdot(kdt[bs], v_new)
        for h in range(hh):
            z_ref = z0_ref if h < hh // 2 else z1_ref
            zc = z_ref[rs(c), hs(h % (hh // 2))]
            og = _rms_rows(o[h], nw_ref[...]) * _silu(zc)
            o_ref[rs(c), hs(h)] = og.astype(o_ref.dtype)

    @pl.when(i == pl.num_programs(1) - 1)
    def _():
        sout_ref[0] = s_ref[...]


def _chunk_masks(rows):
    r = jnp.arange(rows)
    same = (r[:, None] // CHUNK) == (r[None, :] // CHUNK)
    tri = (same & (r[:, None] >= r[None, :])).astype(BF16)
    ones = same.astype(BF16)
    return tri, ones


def _gdn_prompt(p_main, p_tail, conv_w, pa, pb, norm_w, layer, n_layers, prev, bsz, t, *, rows):
    nblk = t // rows
    hw = GDN_HALF_W
    hh = GDN_HEADS
    tri, ones = _chunk_masks(rows)

    def col(base):
        return lambda b, i: (b * nblk + i, base // hw)

    const = lambda b, i: (0, 0)
    in_specs = [pl.BlockSpec((rows, hw), col(EV_GQKV + k * hw)) for k in range(6)]
    in_specs += [pl.BlockSpec((rows, hw), col(EV_Z)), pl.BlockSpec((rows, hw), col(EV_Z + hw))]
    in_specs += [
        pl.BlockSpec((rows, LANES), lambda b, i: (b * nblk + i, 0)),
        pl.BlockSpec((None, CONV_W, GDN_CONV_CH), lambda b, i: (layer, 0, 0)),
        pl.BlockSpec((1, LANES), const),
        pl.BlockSpec((1, LANES), const),
        pl.BlockSpec((1, GDN_DV), const),
        pl.BlockSpec((rows, rows), const),
        pl.BlockSpec((rows, rows), const),
    ]
    args = [p_main] * 8 + [p_tail, conv_w, pa, pb, norm_w, tri, ones]
    extra_specs, extra_args, aliases = _alias_prev(prev, len(args), 1)
    kern = functools.partial(_gdn_prompt_kernel, rows=rows)
    if prev is not None:
        kern = _drop_ref(kern, len(args))
    return pl.pallas_call(
        kern,
        grid=(bsz, nblk),
        in_specs=in_specs + extra_specs,
        out_specs=[
            pl.BlockSpec((rows, GDN_V_W), lambda b, i: (b * nblk + i, 0)),
            pl.BlockSpec((None, 1, hh, GDN_DK, GDN_DV), lambda b, i: (layer, b, 0, 0, 0)),
        ],
        out_shape=[
            jax.ShapeDtypeStruct((bsz * t, GDN_V_W), BF16),
            jax.ShapeDtypeStruct((n_layers, bsz, hh, GDN_DK, GDN_DV), F32),
        ],
        scratch_shapes=[
            pltpu.VMEM((hh, GDN_DK, GDN_DV), F32),
            pltpu.VMEM((6, SUBLANES, hw), F32),
            pltpu.VMEM((rows + SUBLANES, hw), F32),
            pltpu.VMEM((rows, GDN_QK_W), F32),
            pltpu.VMEM((rows, GDN_QK_W), F32),
            pltpu.VMEM((rows, GDN_V_W), F32),
        ],
        input_output_aliases=aliases,
        compiler_params=_params(("parallel", "arbitrary")),
        name="gdn_prompt",
    )(*args, *extra_args)


def _ssd_prompt_kernel(z_ref, xa_ref, xb_ref, xc3_ref, tail_ref, cw_ref, cb_ref, pa_ref, pb_ref, de_ref, nw_ref,
                       ex_ref, tri_ref,
                       y_ref, hout_ref,
                       ht_ref, carry_ref, ext_ref, xc_ref, cumt_ref, dtt_ref):
    i = pl.program_id(1)
    rows = CHUNK

    @pl.when(i == 0)
    def _():
        ht_ref[...] = jnp.zeros_like(ht_ref)
        carry_ref[...] = jnp.zeros_like(carry_ref)

    for sec, u_ref in enumerate((xa_ref, xb_ref, xc3_ref)):
        ss = slice(sec * SSD_SEC_W, (sec + 1) * SSD_SEC_W)
        xc_ref[:, ss] = _conv_section(ext_ref, carry_ref, sec, u_ref, cw_ref[:, ss], cb_ref[:, ss], rows)

    dt = _softplus(tail_ref[...] + pb_ref[...])
    da = dt * (-jnp.exp(pa_ref[...]))
    cum = _sel_dot(tri_ref[...], _split3(da))
    cumt_ref[...] = jnp.transpose(cum)
    dtt_ref[...] = jnp.transpose(dt)
    last = cum[rows - 1:rows, :]
    ecum_parts = _split3(jnp.exp(cum))
    wts_parts = _split3(jnp.exp(last - cum) * dt)

    ri = _iota2((rows, rows), 0)
    ci = _iota2((rows, rows), 1)
    tri = ri >= ci
    lane = _iota2((rows, LANES), 1)
    lo_half = lane < SSM_HEAD_DIM

    for g in range(SSM_GROUPS):
        gs = slice(g * SSM_GROUP_W, (g + 1) * SSM_GROUP_W)
        bmat = xc_ref[:, SSM_D_INNER + g * SSM_STATE:SSM_D_INNER + (g + 1) * SSM_STATE]
        cmat = xc_ref[:, SSM_D_INNER + SSM_BC_W + g * SSM_STATE:SSM_D_INNER + SSM_BC_W + (g + 1) * SSM_STATE]
        xg = xc_ref[:, gs]
        cb = _dot_nt(cmat, bmat)
        exg = ex_ref[:, gs]
        ecum_e = _dot_sel(ecum_parts, exg)
        wts = _dot_sel(wts_parts, exg)
        htg = ht_ref[g]
        y = _dot(cmat, htg) * ecum_e
        pieces = []
        for pr in range(SSM_REP // 2):
            scs = []
            for half in range(2):
                h = g * SSM_REP + 2 * pr + half
                ccol = cum[:, h:h + 1]
                crow = cumt_ref[h:h + 1, :]
                drow = dtt_ref[h:h + 1, :]
                lmat = jnp.exp(jnp.where(tri, ccol - crow, NEG_INF))
                scs.append((cb * lmat * drow).astype(BF16))
            xp = xg[:, pr * LANES:(pr + 1) * LANES]
            rhs = jnp.concatenate([jnp.where(lo_half, xp, 0.0), jnp.where(lo_half, 0.0, xp)], axis=0)
            pieces.append(jnp.dot(jnp.concatenate(scs, axis=1), rhs.astype(BF16),
                                  preferred_element_type=F32))
        y = y + jnp.concatenate(pieces, axis=1) + de_ref[:, gs] * xg
        ht_ref[g] = htg * ecum_e[rows - 1:rows, :] + _dot(jnp.transpose(bmat), xg * wts)
        zg = z_ref[:, gs]
        yz = y * _silu(zg)
        y_ref[:, gs] = _rms_rows(yz, nw_ref[:, gs]).astype(y_ref.dtype)

    @pl.when(i == pl.num_programs(1) - 1)
    def _():
        for g in range(SSM_GROUPS):
            hg = jnp.transpose(ht_ref[g])
            hout_ref[0, g * SSM_REP:(g + 1) * SSM_REP] = hg.reshape(SSM_REP, SSM_HEAD_DIM, SSM_STATE)


def _head_expand():
    h = jnp.arange(LANES)[:, None]
    c = jnp.arange(SSM_D_INNER)[None, :]
    return ((c // SSM_HEAD_DIM) == h).astype(BF16)


def _ssd_prompt(p_main, p_tail, conv_w, conv_b, pa, pb, d_e, norm_w, ex, layer, n_layers, prev, bsz, t):
    rows = CHUNK
    nblk = t // rows
    tri, _ = _chunk_masks(rows)
    const = lambda b, i: (0, 0)
    row = lambda b, i: (b * nblk + i, 0)
    args = [p_main, p_main, p_main, p_main, p_tail, conv_w, conv_b, pa, pb, d_e, norm_w, ex, tri]
    extra_specs, extra_args, aliases = _alias_prev(prev, len(args), 1)
    kern = _ssd_prompt_kernel if prev is None else _drop_ref(_ssd_prompt_kernel, len(args))
    return pl.pallas_call(
        kern,
        grid=(bsz, nblk),
        input_output_aliases=aliases,
        in_specs=[
            pl.BlockSpec((rows, SSM_D_INNER), row),
            pl.BlockSpec((rows, SSD_SEC_W), lambda b, i: (b * nblk + i, OD_XBC // SSD_SEC_W)),
            pl.BlockSpec((rows, SSD_SEC_W), lambda b, i: (b * nblk + i, OD_XBC // SSD_SEC_W + 1)),
            pl.BlockSpec((rows, SSD_SEC_W), lambda b, i: (b * nblk + i, OD_XBC // SSD_SEC_W + 2)),
            pl.BlockSpec((rows, LANES), row),
            pl.BlockSpec((CONV_W, SSM_CONV_CH), const),
            pl.BlockSpec((1, SSM_CONV_CH), const),
            pl.BlockSpec((1, LANES), const),
            pl.BlockSpec((1, LANES), const),
            pl.BlockSpec((1, SSM_D_INNER), const),
            pl.BlockSpec((1, SSM_D_INNER), const),
            pl.BlockSpec((LANES, SSM_D_INNER), const),
            pl.BlockSpec((rows, rows), const),
        ] + extra_specs,
        out_specs=[
            pl.BlockSpec((rows, SSM_D_INNER), row),
            pl.BlockSpec((None, 1, SSM_HEADS, SSM_HEAD_DIM, SSM_STATE), lambda b, i: (layer, b, 0, 0, 0)),
        ],
        out_shape=[
            jax.ShapeDtypeStruct((bsz * t, SSM_D_INNER), BF16),
            jax.ShapeDtypeStruct((n_layers, bsz, SSM_HEADS, SSM_HEAD_DIM, SSM_STATE), F32),
        ],
        scratch_shapes=[
            pltpu.VMEM((SSM_GROUPS, SSM_STATE, SSM_GROUP_W), F32),
            pltpu.VMEM((3, SUBLANES, SSD_SEC_W), F32),
            pltpu.VMEM((rows + SUBLANES, SSD_SEC_W), F32),
            pltpu.VMEM((rows, SSM_CONV_CH), F32),
            pltpu.VMEM((LANES, rows), F32),
            pltpu.VMEM((LANES, rows), F32),
        ],
        compiler_params=_params(("parallel", "arbitrary")),
        name="ssd_prompt",
    )(*args, *extra_args)


def _swa_sample_group_kernel(sink_ref, q_ref, kn_ref, vn_ref, ck_ref, cv_ref, o_ref, *, bb):
    scale = SWA_HEAD_DIM ** -0.5
    rows, keys = bb * SWA_GROUP, bb * WINDOW
    ck = ck_ref[...].reshape(keys, SWA_KV_W).astype(BF16)
    cv = cv_ref[...].reshape(keys, SWA_KV_W).astype(BF16)
    kn = kn_ref[...].astype(BF16).astype(F32)
    vn = vn_ref[...].astype(BF16).astype(F32)
    ri = _iota2((rows, keys), 0)
    cj = _iota2((rows, keys), 1)
    valid = ((cj // WINDOW) == (ri // SWA_GROUP)) & ((cj % WINDOW) >= 1)
    for c in range(SWA_KV_HEADS):
        cs = slice(c * SWA_HEAD_DIM, (c + 1) * SWA_HEAD_DIM)
        qc = q_ref[c].astype(BF16)
        s = lax.dot_general(qc, ck[:, cs], (((1,), (1,)), ((), ())), preferred_element_type=F32) * scale
        s = jnp.where(valid, s, NEG_INF)
        sn = jnp.sum(qc.astype(F32) * kn[:, cs], axis=-1, keepdims=True) * scale
        sk = sink_ref[c]
        mx = jnp.maximum(jnp.maximum(jnp.max(s, axis=-1, keepdims=True), sn), sk)
        p = jnp.exp(s - mx)
        pn = jnp.exp(sn - mx)
        den = jnp.sum(p, axis=-1, keepdims=True) + pn + jnp.exp(sk - mx)
        o = jnp.dot(p.astype(BF16), cv[:, cs], preferred_element_type=F32) + pn * vn[:, cs]
        o_ref[c] = (o / den).astype(o_ref.dtype)


def _swa_sample_grouped(q, kn, vn, ck_all, cv_all, sinks, layer):
    nb = q.shape[0]
    g, kvh, hd, kvw = SWA_GROUP, SWA_KV_HEADS, SWA_HEAD_DIM, SWA_KV_W
    bb = SUBLANES if nb % SUBLANES == 0 else nb
    qg = jnp.transpose(q.reshape(nb, kvh, g, hd), (1, 0, 2, 3)).reshape(kvh, nb * g, hd)
    kn4 = jnp.repeat(kn, g, axis=0)
    vn4 = jnp.repeat(vn, g, axis=0)
    sk = jnp.tile(sinks.astype(F32).reshape(kvh, 1, g), (1, bb, 1)).reshape(kvh, bb * g, 1)
    cache = pl.BlockSpec((None, bb, WINDOW, kvw), lambda b: (layer, b, 0, 0))
    out = pl.pallas_call(
        functools.partial(_swa_sample_group_kernel, bb=bb),
        grid=(nb // bb,),
        in_specs=[
            pl.BlockSpec((kvh, bb * g, 1), lambda b: (0, 0, 0)),
            pl.BlockSpec((kvh, bb * g, hd), lambda b: (0, b, 0)),
            pl.BlockSpec((bb * g, kvw), lambda b: (b, 0)),
            pl.BlockSpec((bb * g, kvw), lambda b: (b, 0)),
            cache,
            cache,
        ],
        out_specs=pl.BlockSpec((kvh, bb * g, hd), lambda b: (0, b, 0)),
        out_shape=jax.ShapeDtypeStruct((kvh, nb * g, hd), F32),
        compiler_params=_params(("parallel",)),
        name="swa_sample",
    )(sk, qg, kn4, vn4, ck_all, cv_all)
    return jnp.transpose(out.reshape(kvh, nb, g, hd), (1, 0, 2, 3)).reshape(nb, kvh * g * hd)


def _gdn_sample_kernel(u_ref, prev_ref, w_ref, s_ref, z_ref, ab_ref, pa_ref, pb_ref, nw_ref,
                       o_ref, sout_ref, cnew_ref):
    for b in range(u_ref.shape[0]):
        _gdn_sample_row(u_ref, prev_ref, w_ref, s_ref, z_ref, ab_ref, pa_ref, pb_ref, nw_ref,
                        o_ref, sout_ref, cnew_ref, b)


def _gdn_sample_row(u_ref, prev_ref, w_ref, s_ref, z_ref, ab_ref, pa_ref, pb_ref, nw_ref,
                    o_ref, sout_ref, cnew_ref, b):
    hh = GDN_HEADS
    u = u_ref[b]
    pv = prev_ref[b]
    w = w_ref[...]
    x = u * w[CONV_W - 1]
    for j in range(CONV_W - 1):
        x = x + pv[j] * w[j]
        cnew_ref[b, j] = pv[j + 1] if j + 1 < CONV_W - 1 else u
    x = _silu(x)
    q = x[0:hh]
    k = x[hh:2 * hh]
    v = x[2 * hh:3 * hh]
    q = q * lax.rsqrt(jnp.sum(q * q, axis=-1, keepdims=True) + EPS) * (GDN_DK ** -0.5)
    k = k * lax.rsqrt(jnp.sum(k * k, axis=-1, keepdims=True) + EPS)
    ab = ab_ref[b]
    dec = jnp.exp(-jnp.exp(pa_ref[...]) * _softplus(ab[0:hh] + pb_ref[...]))
    beta = _sigmoid(ab[hh:2 * hh])
    qk = jnp.concatenate([q, k, jnp.zeros((LANES - 2 * hh, GDN_DK), F32)], axis=0)
    qkt = jnp.transpose(qk)
    outs = []
    for h in range(hh):
        s = s_ref[b, h] * dec[h:h + 1, :]
        kcol = qkt[:, hh + h:hh + h + 1]
        ks = jnp.sum(s * kcol, axis=0, keepdims=True)
        delta = beta[h:h + 1, :] * (v[h:h + 1, :] - ks)
        s = s + kcol * delta
        sout_ref[b, h] = s
        outs.append(jnp.sum(s * qkt[:, h:h + 1], axis=0, keepdims=True))
    o = jnp.concatenate(outs, axis=0)
    o_ref[b] = _rms_rows(o, nw_ref[...]) * _silu(z_ref[b])


def _gdn_sample(u, prev, w, s_all, z, ab, pa, pb, nw, layer, prev_out):
    nb = u.shape[0]
    hh = GDN_HEADS
    n_layers = s_all.shape[0]
    bb = 4 if nb % 4 == 0 else 1
    b3 = lambda b: (b, 0, 0)
    b4 = lambda b: (b, 0, 0, 0)
    c2 = lambda b: (0, 0)
    state_spec = pl.BlockSpec((None, bb, hh, GDN_DK, GDN_DV), lambda b: (layer, b, 0, 0, 0))
    args = [u, prev, w, s_all, z, ab, pa, pb, nw]
    extra_specs, extra_args, aliases = _alias_prev(prev_out, len(args), 1)
    kern = _gdn_sample_kernel if prev_out is None else _drop_ref(_gdn_sample_kernel, len(args))
    return pl.pallas_call(
        kern,
        grid=(nb // bb,),
        input_output_aliases=aliases,
        in_specs=[
            pl.BlockSpec((bb, 3 * hh, GDN_DK), b3),
            pl.BlockSpec((bb, CONV_W - 1, 3 * hh, GDN_DK), b4),
            pl.BlockSpec((CONV_W, 3 * hh, GDN_DK), lambda b: (0, 0, 0)),
            state_spec,
            pl.BlockSpec((bb, hh, GDN_DV), b3),
            pl.BlockSpec((bb, 2 * hh, LANES), b3),
            pl.BlockSpec((hh, LANES), c2),
            pl.BlockSpec((hh, LANES), c2),
            pl.BlockSpec((1, GDN_DV), c2),
        ] + extra_specs,
        out_specs=[
            pl.BlockSpec((bb, hh, GDN_DV), b3),
            state_spec,
            pl.BlockSpec((bb, CONV_W - 1, 3 * hh, GDN_DK), b4),
        ],
        out_shape=[
            jax.ShapeDtypeStruct((nb, hh, GDN_DV), F32),
            jax.ShapeDtypeStruct((n_layers, nb, hh, GDN_DK, GDN_DV), F32),
            jax.ShapeDtypeStruct((nb, CONV_W - 1, 3 * hh, GDN_DK), F32),
        ],
        compiler_params=_params(("parallel",)),
        name="gdn_sample",
    )(*args, *extra_args)


def _ssd_prep_kernel(x_ref, prev_ref, cw_ref, cb_ref, tail_ref, pa_ref, pb_ref,
                     xc_ref, dt_ref, dec_ref, cnew_ref):
    u = x_ref[...]
    w = cw_ref[...]
    acc = u * w[CONV_W - 1:CONV_W, :] + cb_ref[...]
    for j in range(CONV_W - 1):
        acc = acc + prev_ref[j] * w[j:j + 1, :]
        cnew_ref[j] = prev_ref[j + 1] if j + 1 < CONV_W - 1 else u
    xc_ref[...] = _silu(acc)
    dt = _softplus(tail_ref[...] + pb_ref[...])
    dt_ref[...] = dt
    dec_ref[...] = jnp.exp(dt * (-jnp.exp(pa_ref[...])))


def _ssd_prep_sample(p_main, prev_t, cw, cb, tail, pa, pb):
    nb = p_main.shape[0]
    sec = SSD_SEC_W
    nsec = SSM_CONV_CH // sec
    c2 = lambda s: (0, 0)
    return pl.pallas_call(
        _ssd_prep_kernel,
        grid=(nsec,),
        in_specs=[
            pl.BlockSpec((nb, sec), lambda s: (0, OD_XBC // sec + s)),
            pl.BlockSpec((CONV_W - 1, nb, sec), lambda s: (0, 0, s)),
            pl.BlockSpec((CONV_W, sec), lambda s: (0, s)),
            pl.BlockSpec((1, sec), lambda s: (0, s)),
            pl.BlockSpec((nb, LANES), c2),
            pl.BlockSpec((1, LANES), c2),
            pl.BlockSpec((1, LANES), c2),
        ],
        out_specs=[
            pl.BlockSpec((nb, sec), lambda s: (0, s)),
            pl.BlockSpec((nb, LANES), c2),
            pl.BlockSpec((nb, LANES), c2),
            pl.BlockSpec((CONV_W - 1, nb, sec), lambda s: (0, 0, s)),
        ],
        out_shape=[
            jax.ShapeDtypeStruct((nb, SSM_CONV_CH), F32),
            jax.ShapeDtypeStruct((nb, LANES), F32),
            jax.ShapeDtypeStruct((nb, LANES), F32),
            jax.ShapeDtypeStruct((CONV_W - 1, nb, SSM_CONV_CH), F32),
        ],
        compiler_params=_params(("arbitrary",)),
        name="ssd_prep_sample",
    )(p_main, prev_t, cw, cb, tail, pa, pb)


def _ssd_state_kernel(dec_ref, x_ref, dt_ref, b_ref, c_ref, z_ref, de_ref, nw_ref, ex_ref, h_ref,
                      y_ref, hout_ref, *, nb):
    g = pl.program_id(0)
    x = x_ref[...]
    dtx = x * _dot_sel(_split3(dt_ref[...]), ex_ref[...])
    zrows = LANES - nb
    xt = jnp.transpose(jnp.concatenate([dtx, jnp.zeros((zrows, SSM_GROUP_W), F32)], axis=0))
    xh, xl = _split2(xt)
    bpad = jnp.concatenate([b_ref[...], jnp.zeros((zrows, SSM_STATE), F32)], axis=0)
    cpad_t = jnp.transpose(jnp.concatenate([c_ref[...], jnp.zeros((zrows, SSM_STATE), F32)], axis=0))
    rowi = _iota2((LANES, SSM_STATE), 0)
    coli = _iota2((SSM_STATE, LANES), 1)
    yt = jnp.zeros((SSM_GROUP_W, LANES), F32)
    for b in range(nb):
        bh, bl = _split2(jnp.where(rowi == b, bpad, 0.0))
        upd = (jnp.dot(xh, bh, preferred_element_type=F32) + jnp.dot(xh, bl, preferred_element_type=F32)
               + jnp.dot(xl, bh, preferred_element_type=F32))
        parts = []
        for r in range(SSM_REP):
            hn = h_ref[b, r] * dec_ref[b, g * SSM_REP + r] + upd[r * SSM_HEAD_DIM:(r + 1) * SSM_HEAD_DIM]
            hout_ref[b, r] = hn
            parts.append(hn)
        hh_, hl_ = _split2(jnp.concatenate(parts, axis=0))
        ch, cl = _split2(jnp.where(coli == b, cpad_t, 0.0))
        yt = yt + (jnp.dot(hh_, ch, preferred_element_type=F32) + jnp.dot(hh_, cl, preferred_element_type=F32)
                   + jnp.dot(hl_, ch, preferred_element_type=F32))
    y = jnp.transpose(yt)[0:nb, :] + de_ref[...] * x
    y_ref[...] = _rms_rows(y * _silu(z_ref[...]), nw_ref[...]).astype(y_ref.dtype)


def _ssd_state_sample(dtv, dec, xc, p_main, d_e, nw, ex, h_all, layer, prev_out):
    nb = xc.shape[0]
    assert nb <= LANES
    gw = SSM_GROUP_W
    n_layers = h_all.shape[0]
    state_spec = pl.BlockSpec((None, nb, SSM_REP, SSM_HEAD_DIM, SSM_STATE), lambda g: (layer, 0, g, 0, 0))
    args = [dec, xc, dtv, xc, xc, p_main, d_e, nw, ex, h_all]
    extra_specs, extra_args, aliases = _alias_prev(prev_out, len(args), 1)
    kern = functools.partial(_ssd_state_kernel, nb=nb)
    if prev_out is not None:
        kern = _drop_ref(kern, len(args))
    return pl.pallas_call(
        kern,
        grid=(SSM_GROUPS,),
        input_output_aliases=aliases,
        in_specs=[
            pl.BlockSpec(memory_space=pltpu.SMEM),
            pl.BlockSpec((nb, gw), lambda g: (0, g)),
            pl.BlockSpec((nb, LANES), lambda g: (0, 0)),
            pl.BlockSpec((nb, SSM_STATE), lambda g: (0, SSM_D_INNER // SSM_STATE + g)),
            pl.BlockSpec((nb, SSM_STATE), lambda g: (0, (SSM_D_INNER + SSM_BC_W) // SSM_STATE + g)),
            pl.BlockSpec((nb, gw), lambda g: (0, g)),
            pl.BlockSpec((1, gw), lambda g: (0, g)),
            pl.BlockSpec((1, gw), lambda g: (0, g)),
            pl.BlockSpec((LANES, gw), lambda g: (0, g)),
            state_spec,
        ] + extra_specs,
        out_specs=[
            pl.BlockSpec((nb, gw), lambda g: (0, g)),
            state_spec,
        ],
        out_shape=[
            jax.ShapeDtypeStruct((nb, SSM_D_INNER), BF16),
            jax.ShapeDtypeStruct((n_layers, nb, SSM_HEADS, SSM_HEAD_DIM, SSM_STATE), F32),
        ],
        compiler_params=_params(("arbitrary",)),
        name="ssd_state_sample",
    )(*args, *extra_args)


def _lane_pad_row(v):
    return jnp.zeros((1, LANES), F32).at[0, :v.shape[0]].set(v.astype(F32))


def kernel(x_prompt, x_sample, cache_swa_k, cache_swa_v, state_gdn, state_gdn_conv, state_ssm, state_ssm_conv,
           norm_ffn1, norm_mix, norm_ffn2, norm_final, w_ffn_gate, w_ffn_up, w_ffn_down,
           w_in_even, w_out_even, attn_sinks, gdn_conv_w, gdn_A_log, gdn_dt_bias, gdn_norm_w,
           w_in_odd, w_out_odd, ssm_conv_w, ssm_conv_b, ssm_A_log, ssm_dt_bias, ssm_D, ssm_norm_w):
    bsz, t, d = x_prompt.shape
    nb = x_sample.shape[0]
    depth = norm_ffn1.shape[0]
    n_even, n_odd = (depth + 1) // 2, depth // 2
    mp = bsz * t
    assert x_sample.shape[1] == 1 and d == D_MODEL and t % (2 * CHUNK) == 0
    hh = GDN_HEADS

    w_even_t = jnp.swapaxes(w_in_even, 1, 2)
    w_odd_t = jnp.swapaxes(w_in_odd, 1, 2)
    ex = _head_expand()
    cache_k = cache_swa_k.reshape(n_even, nb, WINDOW, SWA_KV_W)
    cache_v = cache_swa_v.reshape(n_even, nb, WINDOW, SWA_KV_W)

    xp = x_prompt.reshape(mp, d)
    xs = x_sample.reshape(nb, d)
    gain_f = norm_final.reshape(1, d)
    ffn_tm, ffn_first_tm = _tile(mp, FFN_TM), _tile(mp, FFN_FIRST_TM)

    small = {k: [] for k in ("pk", "pv", "pgc", "psc", "sk", "sv", "sgc", "ssc")}
    p_gdn = p_ssm = s_gdn = s_ssm = None

    def ffn_pair(xp, xs, gain, l, which, final):
        gain = gain.reshape(1, d)
        yp, ys, wg, wu, wd = _ffn_first(xp, xs, gain, gain_f, w_ffn_gate, w_ffn_up, w_ffn_down, l, which,
                                        final_norm=final, tm=ffn_first_tm, tf=FFN_FIRST_TF)
        yp = _ffn_rest(xp, gain, gain_f, wg, wu, wd, yp, final_norm=final, tm=ffn_tm, tf=FFN_TF,
                       rows_done=ffn_first_tm)
        return yp, ys

    for l in range(depth):
        xp, xs = ffn_pair(xp, xs, norm_ffn1[l], l, 0, False)
        gmix = norm_mix[l].reshape(1, d)
        if l % 2 == 0:
            e = l // 2
            ptm, otm = _tile(mp, PROJ_EVEN_TM), _tile(mp, OUT_EVEN_TM)
            pfm, ofm = _tile(mp, PROJ_FIRST_TM), _tile(mp, OUT_EVEN_FIRST_TM)
            pm, pt, sm, st, w_main, w_tail = _proj_first(xp, xs, gmix, w_even_t, e, EV_MAIN, tm=pfm, tn=FIRST_TN)
            pm, pt = _proj_rest(xp, gmix, w_main, w_tail, pm, pt, tm=ptm, tn=PROJ_EVEN_TN, rows_done=pfm)
            pa = _lane_pad_row(gdn_A_log[e])
            pb = _lane_pad_row(gdn_dt_bias[e])
            nw = gdn_norm_w[e].reshape(1, GDN_DV).astype(F32)

            kn = sm[:, EV_K:EV_K + SWA_KV_W]
            vn = sm[:, EV_V:EV_V + SWA_KV_W]
            o_a_s = _swa_sample_grouped(sm[:, EV_Q:EV_Q + SWA_Q_W], kn, vn, cache_k, cache_v, attn_sinks[e], e)
            ab = jnp.broadcast_to(st[:, :2 * hh, None], (nb, 2 * hh, LANES))
            o_b_s, s_gdn, gc_s = _gdn_sample(
                sm[:, EV_GQKV:EV_GQKV + GDN_CONV_CH].reshape(nb, 3 * hh, GDN_DK),
                state_gdn_conv[e].reshape(nb, CONV_W - 1, 3 * hh, GDN_DK),
                gdn_conv_w[e].reshape(CONV_W, 3 * hh, GDN_DK),
                state_gdn, sm[:, EV_Z:EV_Z + GDN_V_W].reshape(nb, hh, GDN_DV), ab,
                jnp.broadcast_to(gdn_A_log[e].astype(F32)[:, None], (hh, LANES)),
                jnp.broadcast_to(gdn_dt_bias[e].astype(F32)[:, None], (hh, LANES)), nw, e, s_gdn)
            small["sk"].append(kn.reshape(nb, 1, SWA_KV_HEADS, SWA_HEAD_DIM))
            small["sv"].append(vn.reshape(nb, 1, SWA_KV_HEADS, SWA_HEAD_DIM))
            small["sgc"].append(gc_s.reshape(nb, CONV_W - 1, GDN_CONV_CH))
            outs_s = [o_a_s.reshape(nb, SWA_Q_W).astype(BF16), o_b_s.reshape(nb, GDN_V_W).astype(BF16)]

            o_a_p = _swa_prompt(pm, attn_sinks[e].astype(F32), bsz, t)
            o_b_p, p_gdn = _gdn_prompt(pm, pt, gdn_conv_w, pa, pb, nw, e, n_even, p_gdn, bsz, t, rows=2 * CHUNK)
            pm3 = pm.reshape(bsz, t, EV_MAIN)
            small["pk"].append(pm3[:, t - WINDOW:, EV_K:EV_K + SWA_KV_W].reshape(bsz, WINDOW, SWA_KV_HEADS, SWA_HEAD_DIM))
            small["pv"].append(pm3[:, t - WINDOW:, EV_V:EV_V + SWA_KV_W].reshape(bsz, WINDOW, SWA_KV_HEADS, SWA_HEAD_DIM))
            small["pgc"].append(pm3[:, t - (CONV_W - 1):, EV_GQKV:EV_GQKV + GDN_CONV_CH])
            yp, xs, w_out = _outproj_first(xp, xs, [o_a_p, o_b_p], outs_s, w_out_even, e, tm=ofm, tn=FIRST_TN)
            xp = _outproj_rest(xp, [o_a_p, o_b_p], w_out, yp, tm=otm, tn=OUT_EVEN_TN, rows_done=ofm)
        else:
            o = l // 2
            ptm, otm = _tile(mp, PROJ_ODD_TM), _tile(mp, OUT_ODD_TM)
            pfm, ofm = _tile(mp, PROJ_FIRST_TM), _tile(mp, OUT_ODD_FIRST_TM)
            pm, pt, sm, st, w_main, w_tail = _proj_first(xp, xs, gmix, w_odd_t, o, OD_MAIN, tm=pfm, tn=FIRST_TN)
            pm, pt = _proj_rest(xp, gmix, w_main, w_tail, pm, pt, tm=ptm, tn=PROJ_ODD_TN, rows_done=pfm)
            pa = _lane_pad_row(ssm_A_log[o])
            pb = _lane_pad_row(ssm_dt_bias[o])
            d_e = jnp.repeat(ssm_D[o].astype(F32), SSM_HEAD_DIM).reshape(1, SSM_D_INNER)
            nw = ssm_norm_w[o].reshape(1, SSM_D_INNER).astype(F32)
            cb = ssm_conv_b[o].reshape(1, SSM_CONV_CH)

            xc, dtv, dec, cnew = _ssd_prep_sample(sm, jnp.swapaxes(state_ssm_conv[o], 0, 1), ssm_conv_w[o], cb,
                                                  st, pa, pb)
            y_s, s_ssm = _ssd_state_sample(dtv, dec, xc, sm, d_e, nw, ex, state_ssm, o, s_ssm)
            small["ssc"].append(jnp.swapaxes(cnew, 0, 1))

            y_p, p_ssm = _ssd_prompt(pm, pt, ssm_conv_w[o], cb, pa, pb, d_e, nw, ex, o, n_odd, p_ssm, bsz, t)
            pm3 = pm.reshape(bsz, t, OD_MAIN)
            small["psc"].append(pm3[:, t - (CONV_W - 1):, OD_XBC:OD_XBC + SSM_CONV_CH])
            yp, xs, w_out = _outproj_first(xp, xs, [y_p], [y_s], w_out_odd, o, tm=ofm, tn=FIRST_TN)
            xp = _outproj_rest(xp, [y_p], w_out, yp, tm=otm, tn=OUT_ODD_TN, rows_done=ofm)
        xp, xs = ffn_pair(xp, xs, norm_ffn2[l], l, 1, l == depth - 1)

    st_ = {k: jnp.stack(v) for k, v in small.items()}
    return (xp.reshape(bsz, t, d), xs.reshape(nb, 1, d),
            st_["pk"], st_["pv"], p_gdn, st_["pgc"], p_ssm, st_["psc"],
            st_["sk"], st_["sv"], s_gdn, st_["sgc"], s_ssm, st_["ssc"])
```
